```python
import jax
import jax.numpy as jnp
from jax import lax
import numpy as np

D_MODEL = 2048
BATCH = 8
SEQ = 2048
DEPTH = 1

N_META = 16
HEAD_DIM = 128
GDN_HEADS = 8
FOX_HEADS = 8
GDN_WIDTH = GDN_HEADS * HEAD_DIM
FOX_WIDTH = FOX_HEADS * HEAD_DIM
MIX_WIDTH = GDN_WIDTH + FOX_WIDTH
CONV_WIDTH = 4
CHUNK = 64
Q_BLOCK = 128
EPS = 1e-6

IN_SPLITS = (GDN_WIDTH, GDN_WIDTH, GDN_WIDTH, GDN_WIDTH, GDN_HEADS, GDN_HEADS,
             FOX_WIDTH, FOX_WIDTH, FOX_WIDTH, FOX_WIDTH, FOX_HEADS)
IN_WIDTH = sum(IN_SPLITS)
SPLIT_POINTS = tuple(int(s) for s in np.cumsum(IN_SPLITS)[:-1])

kernel_name = 'hymba_gdn_fox_sandwich_meta'


def rmsnorm(x, w):
    xf = x.astype(jnp.float32)
    y = xf * lax.rsqrt(jnp.mean(xf * xf, axis=-1, keepdims=True) + EPS)
    return y * w.astype(jnp.float32)


def l2norm(x):
    return x * lax.rsqrt(jnp.sum(x * x, axis=-1, keepdims=True) + EPS)


def causal_depthwise_conv(x, w):
    L = x.shape[1]
    xp = jnp.pad(x, ((0, 0), (CONV_WIDTH - 1, 0), (0, 0)))
    y = xp[:, 0:L, :] * w[:, 0]
    for j in range(1, CONV_WIDTH):
        y = y + xp[:, j:j + L, :] * w[:, j]
    return y


def gdn_chunk_prep(q, k, v, beta, g):
    C = q.shape[-2]
    G = jnp.cumsum(g, axis=-1)
    causal = jnp.tril(jnp.ones((C, C), dtype=bool))
    strict = jnp.tril(jnp.ones((C, C), dtype=bool), -1)
    diff = G[..., :, None] - G[..., None, :]
    D = jnp.where(causal, jnp.exp(jnp.where(causal, diff, 0.0)), 0.0)
    kk = jnp.einsum('bhncd,bhnsd->bhncs', k, k)
    n_mat = jnp.where(strict, beta[..., :, None] * kk * D, 0.0)
    eye = jnp.eye(C, dtype=q.dtype)
    T = lax.linalg.triangular_solve(eye + n_mat, jnp.broadcast_to(eye, n_mat.shape),
                                    left_side=True, lower=True, unit_diagonal=True)
    U = jnp.einsum('bhncs,bhnsd->bhncd', T, beta[..., None] * v)
    W = jnp.einsum('bhncs,bhnsd->bhncd', T, (beta * jnp.exp(G))[..., None] * k)
    a_qk = jnp.where(causal, jnp.einsum('bhncd,bhnsd->bhncs', q, k) * D, 0.0)
    q_dec = q * jnp.exp(G)[..., None]
    k_dec = k * jnp.exp(G[..., -1:] - G)[..., None]
    chunk_decay = jnp.exp(G[..., -1])
    return (q_dec, k_dec, U, W, a_qk, chunk_decay)


def gdn_chunk_step(S, xs):
    q_dec, k_dec, U, W, a_qk, decay = xs
    v_new = U - jnp.einsum('bhcd,bhde->bhce', W, S)
    o = jnp.einsum('bhcd,bhde->bhce', q_dec, S) + jnp.einsum('bhcs,bhse->bhce', a_qk, v_new)
    S = S * decay[..., None, None] + jnp.einsum('bhcd,bhce->bhde', k_dec, v_new)
    return S, o


def gated_delta_rule(q, k, v, beta, g):
    B, H, L, d = q.shape
    n_chunks = (L - N_META) // CHUNK

    def split(t):
        meta = t[:, :, :N_META][:, :, None]
        real = t[:, :, N_META:].reshape((B, H, n_chunks, CHUNK) + t.shape[3:])
        return meta, real

    parts = [split(t) for t in (q, k, v, beta, g)]
    meta_in = gdn_chunk_prep(*[p[0] for p in parts])
    real_in = gdn_chunk_prep(*[p[1] for p in parts])
    S0 = jnp.zeros((B, H, d, d), jnp.float32)
    S, o_meta = gdn_chunk_step(S0, tuple(t[:, :, 0] for t in meta_in))
    real_xs = tuple(jnp.moveaxis(t, 2, 0) for t in real_in)
    _, o_real = lax.scan(gdn_chunk_step, S, real_xs)
    o_real = jnp.moveaxis(o_real, 0, 2).reshape(B, H, L - N_META, d)
    return jnp.concatenate([o_meta, o_real], axis=2)


def forgetting_attention(q, k, v, logf):
    L = q.shape[2]
    scale = HEAD_DIM ** -0.5
    c = jnp.cumsum(logf, axis=-1)
    starts = [0] + [N_META + i * Q_BLOCK for i in range((L - N_META) // Q_BLOCK)]
    ends = [N_META] + [N_META + (i + 1) * Q_BLOCK for i in range((L - N_META) // Q_BLOCK)]
    outs = []
    for s0, s1 in zip(starts, ends):
        qb = q[:, :, s0:s1]
        kb = k[:, :, :s1]
        vb = v[:, :, :s1]
        logits = (jnp.einsum('bhqd,bhkd->bhqk', qb, kb) * scale
                  + (c[:, :, s0:s1, None] - c[:, :, None, :s1]))
        qpos = s0 + jnp.arange(s1 - s0)
        kpos = jnp.arange(s1)
        mask = kpos[None, :] <= qpos[:, None]
        logits = jnp.where(mask, logits, -jnp.inf)
        p = jax.nn.softmax(logits, axis=-1)
        outs.append(jnp.einsum('bhqk,bhkd->bhqd', p, vb))
    return jnp.concatenate(outs, axis=2)


def hybrid_layer(h, pre_w, w_in, conv_w, a_log, dt_bias, gdn_norm_w,
                 fox_q_norm_w, fox_k_norm_w, fox_f_bias, w_out, post_w):
    B, L, _ = h.shape
    f32 = jnp.float32
    xn = rmsnorm(h, pre_w).astype(h.dtype)
    proj = xn @ w_in
    gq, gk, gv, gz, gb, ga, fq, fk, fv, fg, ff = jnp.split(proj, SPLIT_POINTS, axis=-1)

    def heads(t, n):
        return t.reshape(B, L, n, HEAD_DIM).transpose(0, 2, 1, 3)

    qkv = jax.nn.silu(causal_depthwise_conv(jnp.concatenate([gq, gk, gv], axis=-1), conv_w))
    gq, gk, gv = jnp.split(qkv.astype(f32), 3, axis=-1)
    gq = l2norm(heads(gq, GDN_HEADS)) * (HEAD_DIM ** -0.5)
    gk = l2norm(heads(gk, GDN_HEADS))
    gv = heads(gv, GDN_HEADS)
    beta = jax.nn.sigmoid(gb.astype(f32)).transpose(0, 2, 1)
    g = (-jnp.exp(a_log.astype(f32))
         * jax.nn.softplus(ga.astype(f32) + dt_bias.astype(f32))).transpose(0, 2, 1)
    o_gdn = gated_delta_rule(gq, gk, gv, beta, g)
    o_gdn = rmsnorm(o_gdn, gdn_norm_w) * jax.nn.silu(heads(gz.astype(f32), GDN_HEADS))

    fq = rmsnorm(heads(fq, FOX_HEADS), fox_q_norm_w)
    fk = rmsnorm(heads(fk, FOX_HEADS), fox_k_norm_w)
    fv = heads(fv.astype(f32), FOX_HEADS)
    logf = jax.nn.log_sigmoid(ff.astype(f32) + fox_f_bias.astype(f32)).transpose(0, 2, 1)
    o_fox = forgetting_attention(fq, fk, fv, logf) * jax.nn.silu(heads(fg.astype(f32), FOX_HEADS))

    merged = jnp.concatenate([o_gdn, o_fox], axis=1)
    merged = merged.transpose(0, 2, 1, 3).reshape(B, L, MIX_WIDTH).astype(h.dtype)
    out = merged @ w_out
    return h + rmsnorm(out, post_w).astype(h.dtype)


def setup_inputs(seed: int = 0) -> dict:
    key = jax.random.key(seed)
    ks = jax.random.split(key, 14)
    f32 = jnp.float32
    x = jax.random.normal(ks[0], (BATCH, SEQ, D_MODEL), f32)
    meta_tokens = jax.random.normal(ks[1], (N_META, D_MODEL), f32)
    pre_norm_w = 1.0 + 0.01 * jax.random.normal(ks[2], (DEPTH, D_MODEL), f32)
    w_in = jax.random.normal(ks[3], (DEPTH, D_MODEL, IN_WIDTH), f32) * (D_MODEL ** -0.5)
    conv_w = jax.random.normal(ks[4], (DEPTH, 3 * GDN_WIDTH, CONV_WIDTH), f32) * (CONV_WIDTH ** -0.5)
    a_log = jnp.log(jax.random.uniform(ks[5], (DEPTH, GDN_HEADS), f32, 1.0, 16.0))
    dt = jnp.exp(jax.random.uniform(ks[6], (DEPTH, GDN_HEADS), f32,
                                    float(np.log(1e-3)), float(np.log(1e-1))))
    dt_bias = dt + jnp.log(-jnp.expm1(-dt))
    gdn_norm_w = 1.0 + 0.01 * jax.random.normal(ks[7], (DEPTH, HEAD_DIM), f32)
    fox_q_norm_w = 1.0 + 0.01 * jax.random.normal(ks[8], (DEPTH, HEAD_DIM), f32)
    fox_k_norm_w = 1.0 + 0.01 * jax.random.normal(ks[9], (DEPTH, HEAD_DIM), f32)
    fox_f_bias = jax.random.uniform(ks[10], (DEPTH, FOX_HEADS), f32, 1.0, 4.0)
    w_out = jax.random.normal(ks[11], (DEPTH, MIX_WIDTH, D_MODEL), f32) * (MIX_WIDTH ** -0.5)
    post_norm_w = 1.0 + 0.01 * jax.random.normal(ks[12], (DEPTH, D_MODEL), f32)
    return {'x': x, 'meta_tokens': meta_tokens, 'pre_norm_w': pre_norm_w, 'w_in': w_in,
            'conv_w': conv_w, 'a_log': a_log, 'dt_bias': dt_bias, 'gdn_norm_w': gdn_norm_w,
            'fox_q_norm_w': fox_q_norm_w, 'fox_k_norm_w': fox_k_norm_w, 'fox_f_bias': fox_f_bias,
            'w_out': w_out, 'post_norm_w': post_norm_w}


def reference(x, meta_tokens, pre_norm_w, w_in, conv_w, a_log, dt_bias, gdn_norm_w,
              fox_q_norm_w, fox_k_norm_w, fox_f_bias, w_out, post_norm_w):
    B = x.shape[0]
    meta = jnp.broadcast_to(meta_tokens.astype(x.dtype)[None], (B, N_META, x.shape[-1]))
    h = jnp.concatenate([meta, x], axis=1)
    for l in range(DEPTH):
        h = hybrid_layer(h, pre_norm_w[l], w_in[l], conv_w[l], a_log[l], dt_bias[l],
                         gdn_norm_w[l], fox_q_norm_w[l], fox_k_norm_w[l], fox_f_bias[l],
                         w_out[l], post_norm_w[l])
    return h[:, N_META:]
```

```python
import functools

import jax
import jax.numpy as jnp
from jax import lax
from jax.experimental import pallas as pl
from jax.experimental.pallas import tpu as pltpu

N_META = 16
HEAD_DIM = 128
GDN_HEADS = 8
FOX_HEADS = 8
GDN_WIDTH = GDN_HEADS * HEAD_DIM
FOX_WIDTH = FOX_HEADS * HEAD_DIM
CONV_WIDTH = 4
CHUNK = 64
EPS = 1e-6

LANES = 128
MAIN_WIDTH = 4 * GDN_WIDTH + 4 * FOX_WIDTH
GATE_WIDTH = LANES
BETA_LANE, DECAY_LANE, FORGET_LANE = 0, GDN_HEADS, 2 * GDN_HEADS
GATE_ROWS = 32
META_ROWS = CHUNK
META_PAD = META_ROWS - N_META
GROUP = 256
ROW_BLOCK = 256
VMEM_LIMIT = 56 * 1024 * 1024

F32 = jnp.float32
BF16 = jnp.bfloat16
NEG_BIG = -1e30


def _dot(a, b):
    return jnp.dot(a, b, preferred_element_type=F32)


def _dot_nt(a, b):
    return lax.dot_general(a, b, (((1,), (1,)), ((), ())), preferred_element_type=F32)


def _split3(x):
    hi = x.astype(BF16)
    r1 = x - hi.astype(F32)
    mid = r1.astype(BF16)
    lo = (r1 - mid.astype(F32)).astype(BF16)
    return hi, mid, lo


def _dot_exact_rhs01(parts, m):
    return _dot(parts[0], m) + _dot(parts[1], m) + _dot(parts[2], m)


def _dot_exact_lhs01(m, parts):
    return _dot(m, parts[0]) + _dot(m, parts[1]) + _dot(m, parts[2])


def _lane_bcast(col_tile, lane):
    sel = (lax.broadcasted_iota(jnp.int32, (LANES, LANES), 0) == lane).astype(BF16)
    return _dot_exact_rhs01(_split3(col_tile), sel)


def _chunk_of(idx):
    return jnp.right_shift(idx, CHUNK.bit_length() - 1)


def _rms(x, w):
    return x * lax.rsqrt(jnp.mean(x * x, axis=-1, keepdims=True) + EPS) * w


def _silu(x):
    return x * (1.0 / (1.0 + jnp.exp(-x)))


def _softplus(x):
    return jnp.maximum(x, 0.0) + jnp.log1p(jnp.exp(-jnp.abs(x)))


def _in_proj_kernel(x_ref, nw_ref, w_ref, wg_ref, o_ref, og_ref, xn_ref):
    @pl.when(pl.program_id(1) == 0)
    def _():
        xn = _rms(x_ref[...], nw_ref[...]).astype(BF16)
        xn_ref[...] = xn
        og_ref[...] = _dot(xn, wg_ref[...])

    o_ref[...] = _dot(xn_ref[...], w_ref[...])


def _in_proj(x2d, norm_w, w_main, w_gate, tm, tn):
    m, d = x2d.shape
    n = w_main.shape[1]
    return pl.pallas_call(
        _in_proj_kernel,
        out_shape=(jax.ShapeDtypeStruct((m, n), F32),
                   jax.ShapeDtypeStruct((m, GATE_WIDTH), F32)),
        grid=(m // tm, n // tn),
        in_specs=[pl.BlockSpec((tm, d), lambda i, j: (i, 0)),
                  pl.BlockSpec((1, d), lambda i, j: (0, 0)),
                  pl.BlockSpec((d, tn), lambda i, j: (0, j)),
                  pl.BlockSpec((d, GATE_WIDTH), lambda i, j: (0, 0))],
        out_specs=(pl.BlockSpec((tm, tn), lambda i, j: (i, j)),
                   pl.BlockSpec((tm, GATE_WIDTH), lambda i, j: (i, 0))),
        scratch_shapes=[pltpu.VMEM((tm, d), BF16)],
        compiler_params=pltpu.CompilerParams(
            dimension_semantics=("parallel", "arbitrary"), vmem_limit_bytes=VMEM_LIMIT),
        name="in_proj",
    )(x2d, norm_w, w_main, w_gate)


def _gate_kernel(t_ref, add_ref, alog_ref, col_ref, row_ref, *, rows, is_meta):
    blk = min(rows, ROW_BLOCK)
    lane = lax.broadcasted_iota(jnp.int32, (blk, LANES), 1)
    ri = lax.broadcasted_iota(jnp.int32, (blk, blk), 0)
    ci = lax.broadcasted_iota(jnp.int32, (blk, blk), 1)
    tri = (ci <= ri).astype(BF16)
    tri_chunk = ((ci <= ri) & (_chunk_of(ri) == _chunk_of(ci))).astype(BF16)
    is_beta = lane < DECAY_LANE
    is_decay = (lane >= DECAY_LANE) & (lane < FORGET_LANE)
    is_forget = (lane >= FORGET_LANE) & (lane < FORGET_LANE + FOX_HEADS)

    carry = jnp.zeros((1, LANES), F32)
    for r in range(rows // blk):
        t = t_ref[r * blk:(r + 1) * blk, :]
        ta = t + add_ref[...]
        beta = 1.0 / (1.0 + jnp.exp(-t))
        g = -jnp.exp(alog_ref[...]) * _softplus(ta)
        logf = -_softplus(-ta)
        val = jnp.where(is_decay, g, jnp.where(is_forget, logf, 0.0))
        if is_meta:
            row = lax.broadcasted_iota(jnp.int32, (blk, LANES), 0)
            val = jnp.where(row >= META_PAD, val, 0.0)
            beta = jnp.where(row >= META_PAD, beta, 0.0)
        parts = _split3(val)
        cum_chunk = _dot_exact_lhs01(tri_chunk, parts)
        cum_all = _dot_exact_lhs01(tri, parts) + carry
        carry = cum_all[blk - 1:blk, :]
        if is_meta:
            cum_all = cum_all - carry
        res = jnp.where(is_beta, beta, jnp.where(is_decay, cum_chunk, cum_all))
        col_ref[r * blk:(r + 1) * blk, :] = res
        row_ref[:, r * blk:(r + 1) * blk] = res.T[:GATE_ROWS, :]


def _gate_prep(gate3d, add_vec, alog_vec, is_meta):
    b, rows, _ = gate3d.shape
    kern = functools.partial(_gate_kernel, rows=rows, is_meta=is_meta)
    return pl.pallas_call(
        kern,
        out_shape=(jax.ShapeDtypeStruct((b, rows, LANES), F32),
                   jax.ShapeDtypeStruct((b, GATE_ROWS, rows), F32)),
        grid=(b,),
        in_specs=[pl.BlockSpec((None, rows, LANES), lambda i: (i, 0, 0)),
                  pl.BlockSpec((1, LANES), lambda i: (0, 0)),
                  pl.BlockSpec((1, LANES), lambda i: (0, 0))],
        out_specs=(pl.BlockSpec((None, rows, LANES), lambda i: (i, 0, 0)),
                   pl.BlockSpec((None, GATE_ROWS, rows), lambda i: (i, 0, 0))),
        compiler_params=pltpu.CompilerParams(
            dimension_semantics=("parallel",), vmem_limit_bytes=VMEM_LIMIT),
        name="gate_prep_meta" if is_meta else "gate_prep",
    )(gate3d, add_vec, alog_vec)


def _conv_silu(pad_ref, w, r0, nrows):
    y = pad_ref[pl.ds(r0 + 8 - (CONV_WIDTH - 1), nrows), :] * w[0:1, :]
    for j in range(1, CONV_WIDTH):
        y = y + pad_ref[pl.ds(r0 + 8 - (CONV_WIDTH - 1) + j, nrows), :] * w[j:j + 1, :]
    return _silu(y)


def _l2norm(x):
    return x * lax.rsqrt(jnp.sum(x * x, axis=-1, keepdims=True) + EPS)


def _gdn_pointwise(q, k, v, beta_b, g_b):
    r = q.shape[0]
    q = _l2norm(q) * (HEAD_DIM ** -0.5)
    k = _l2norm(k)
    g3 = g_b.reshape(r // CHUNK, CHUNK, LANES)
    g_last = jnp.broadcast_to(g3[:, CHUNK - 1:CHUNK, :], g3.shape).reshape(r, LANES)
    e_g = jnp.exp(g_b)
    q_dec = q * e_g
    k_dec = k * jnp.exp(g_last - g_b)
    y = jnp.concatenate([v * beta_b, k * (beta_b * e_g)], axis=1)
    return q, k, q_dec, k_dec, y, jnp.exp(g_last)


def _gdn_group(q, k, y, beta_b, g_b, g_row):
    r = q.shape[0]
    reps = r // LANES if r >= LANES else 1
    if r >= LANES:
        g_c = jnp.concatenate([g_b] * reps, axis=1)
        b_c = jnp.concatenate([beta_b] * reps, axis=1)
    else:
        g_c = g_b[:, :r]
        b_c = beta_b[:, :r]
    ri = lax.broadcasted_iota(jnp.int32, (r, r), 0)
    ci = lax.broadcasted_iota(jnp.int32, (r, r), 1)
    same = _chunk_of(ri) == _chunk_of(ci)
    causal = same & (ci <= ri)
    strict = same & (ci < ri)
    dmat = jnp.where(causal, jnp.exp(jnp.where(causal, g_c - g_row, 0.0)), 0.0)
    kk = _dot_nt(k, k)
    a_qk = _dot_nt(q, k) * dmat
    n = jnp.where(strict, b_c * kk * dmat, 0.0)
    eye = (ri == ci).astype(F32)
    t = eye - n
    x = n.astype(BF16)
    span = 2
    while span < CHUNK:
        x2 = _dot(x, x)
        x = x2.astype(BF16)
        t = t + _dot(t.astype(BF16), x)
        span *= 2
    uw = _dot(t.astype(BF16), y)
    return uw, a_qk


def _gdn_kernel(q_ref, k_ref, v_ref, z_ref, qm_ref, km_ref, vm_ref, wq_ref, wk_ref, wv_ref,
                col_ref, grow_ref, colm_ref, growm_ref, nw_ref, o_ref,
                pad_ref, padm_ref, qs_ref, ks_ref, qd_ref, kd_ref, y_ref, bb_ref, gb_ref,
                dec_ref, os_ref, *, seq):
    h = pl.program_id(1)
    n_blocks = seq // ROW_BLOCK
    n_groups = seq // GROUP
    cpg = GROUP // CHUNK

    padm_ref[0:8, :] = jnp.zeros((8, LANES), F32)
    beta_m = _lane_bcast(colm_ref[...], BETA_LANE + h)
    g_m = _lane_bcast(colm_ref[...], DECAY_LANE + h)
    conv_m = []
    for src, w_ref in ((qm_ref, wq_ref), (km_ref, wk_ref), (vm_ref, wv_ref)):
        padm_ref[8:8 + META_ROWS, :] = src[...]
        conv_m.append(_conv_silu(padm_ref, w_ref[...], 0, META_ROWS))
    q_m, k_m, _, kd_m, y_m, _ = _gdn_pointwise(conv_m[0], conv_m[1], conv_m[2], beta_m, g_m)
    uw_m, _ = _gdn_group(q_m.astype(BF16), k_m.astype(BF16), y_m.astype(BF16), beta_m, g_m,
                         growm_ref[...])
    state0 = _dot(kd_m.T.astype(BF16), uw_m[:, :HEAD_DIM].astype(BF16))

    bb_all = None
    for idx, (src, srcm, w_ref) in enumerate(((q_ref, qm_ref, wq_ref), (k_ref, km_ref, wk_ref),
                                             (v_ref, vm_ref, wv_ref))):
        pad_ref[0:8, :] = srcm[META_ROWS - 8:META_ROWS, :]
        pad_ref[8:8 + seq, :] = src[...]
        dst = (qs_ref, ks_ref, os_ref)[idx]

        def conv_body(i, _, w_ref=w_ref, dst=dst):
            r0 = pl.multiple_of(i * ROW_BLOCK, ROW_BLOCK)
            dst[pl.ds(r0, ROW_BLOCK), :] = _conv_silu(pad_ref, w_ref[...], r0, ROW_BLOCK)
            return 0
        lax.fori_loop(0, n_blocks, conv_body, 0)

    def point_body(i, _):
        r0 = pl.multiple_of(i * ROW_BLOCK, ROW_BLOCK)
        rows = pl.ds(r0, ROW_BLOCK)
        col = col_ref[rows, :]
        beta_b = _lane_bcast(col, BETA_LANE + h)
        g_b = _lane_bcast(col, DECAY_LANE + h)
        q, k, q_dec, k_dec, y, dec = _gdn_pointwise(qs_ref[rows, :], ks_ref[rows, :],
                                                    os_ref[rows, :], beta_b, g_b)
        qs_ref[rows, :] = q
        ks_ref[rows, :] = k
        qd_ref[rows, :] = q_dec.astype(BF16)
        kd_ref[rows, :] = k_dec
        y_ref[rows, :] = y.astype(BF16)
        bb_ref[rows, :] = beta_b
        gb_ref[rows, :] = g_b
        dec_ref[rows, :] = dec
        return 0
    lax.fori_loop(0, n_blocks, point_body, 0)

    def group_body(gi, state):
        r0 = pl.multiple_of(gi * GROUP, GROUP)
        rows = pl.ds(r0, GROUP)
        uw, a_qk = _gdn_group(qs_ref[rows, :].astype(BF16), ks_ref[rows, :].astype(BF16),
                              y_ref[rows, :], bb_ref[rows, :], gb_ref[rows, :],
                              grow_ref[:, rows])
        kd_t = kd_ref[rows, :].T.astype(BF16)
        a_qk = a_qk.astype(BF16)
        q_dec = qd_ref[rows, :]
        outs = []
        for c in range(cpg):
            cr = slice(c * CHUNK, (c + 1) * CHUNK)
            s_bf = state.astype(BF16)
            v_new = uw[cr, :HEAD_DIM] - _dot(uw[cr, HEAD_DIM:].astype(BF16), s_bf)
            pieces = []
            if c > 0:
                pieces.append(jnp.zeros((c * CHUNK, HEAD_DIM), BF16))
            pieces.append(v_new.astype(BF16))
            if c < cpg - 1:
                pieces.append(jnp.zeros(((cpg - 1 - c) * CHUNK, HEAD_DIM), BF16))
            v_pad = jnp.concatenate(pieces, axis=0) if len(pieces) > 1 else pieces[0]
            outs.append(_dot(q_dec[cr, :], s_bf) + _dot(a_qk[cr, :], v_pad))
            dec = dec_ref[pl.ds(r0 + c * CHUNK, 1), :]
            state = state * dec + _dot(kd_t, v_pad)
        os_ref[rows, :] = jnp.concatenate(outs, axis=0)
        return state
    lax.fori_loop(0, n_groups, group_body, state0)

    def out_body(i, _):
        r0 = pl.multiple_of(i * ROW_BLOCK, ROW_BLOCK)
        rows = pl.ds(r0, ROW_BLOCK)
        o_ref[rows, :] = (_rms(os_ref[rows, :], nw_ref[...]) * _silu(z_ref[rows, :])).astype(BF16)
        return 0
    lax.fori_loop(0, n_blocks, out_body, 0)


def _gdn(proj, proj_m, conv_wt, col, row4, col_m, row_m, norm_w):
    b, seq, _ = proj.shape
    hb = GDN_HEADS

    def head_block(off):
        return pl.BlockSpec((None, seq, HEAD_DIM), lambda i, j, off=off: (i, 0, off + j))

    def meta_block(off):
        return pl.BlockSpec((META_ROWS, HEAD_DIM), lambda i, j, off=off: (0, off + j))

    def conv_block(off):
        return pl.BlockSpec((CONV_WIDTH, HEAD_DIM), lambda i, j, off=off: (0, off + j))

    big = lambda dt: pltpu.VMEM((seq, HEAD_DIM), dt)
    return pl.pallas_call(
        functools.partial(_gdn_kernel, seq=seq),
        out_shape=jax.ShapeDtypeStruct((b, seq, GDN_WIDTH), BF16),
        grid=(b, GDN_HEADS),
        in_specs=[head_block(0), head_block(hb), head_block(2 * hb), head_block(3 * hb),
                  meta_block(0), meta_block(hb), meta_block(2 * hb),
                  conv_block(0), conv_block(hb), conv_block(2 * hb),
                  pl.BlockSpec((None, seq, LANES), lambda i, j: (i, 0, 0)),
                  pl.BlockSpec((None, None, 1, seq), lambda i, j: (i, DECAY_LANE + j, 0, 0)),
                  pl.BlockSpec((None, META_ROWS, LANES), lambda i, j: (0, 0, 0)),
                  pl.BlockSpec((None, None, 1, META_ROWS), lambda i, j: (0, DECAY_LANE + j, 0, 0)),
                  pl.BlockSpec((1, HEAD_DIM), lambda i, j: (0, 0))],
        out_specs=pl.BlockSpec((None, seq, HEAD_DIM), lambda i, j: (i, 0, j)),
        scratch_shapes=[pltpu.VMEM((seq + 8, HEAD_DIM), F32),
                        pltpu.VMEM((META_ROWS + 8, HEAD_DIM), F32),
                        big(F32), big(F32), big(BF16), big(F32),
                        pltpu.VMEM((seq, 2 * HEAD_DIM), BF16),
                        big(F32), big(F32), big(F32), big(F32)],
        compiler_params=pltpu.CompilerParams(
            dimension_semantics=("parallel", "arbitrary"), vmem_limit_bytes=VMEM_LIMIT),
        name="gdn",
    )(proj, proj, proj, proj, proj_m, proj_m, proj_m, conv_wt, conv_wt, conv_wt,
      col, row4, col_m, row_m, norm_w)


def _fox_kernel(q_ref, k_ref, v_ref, g_ref, km_ref, vm_ref, col_ref, crow_ref, crowm_ref,
                qw_ref, kw_ref, o_ref, qn_ref, kn_ref, vb_ref, cq_ref, *, seq, tq, tk):
    h = pl.program_id(1)
    n_blocks = seq // ROW_BLOCK
    scale = HEAD_DIM ** -0.5

    def pro_body(i, _):
        r0 = pl.multiple_of(i * ROW_BLOCK, ROW_BLOCK)
        rows = pl.ds(r0, ROW_BLOCK)
        qn_ref[rows, :] = (_rms(q_ref[rows, :], qw_ref[...]) * scale).astype(BF16)
        kn_ref[rows, :] = _rms(k_ref[rows, :], kw_ref[...]).astype(BF16)
        vb_ref[rows, :] = v_ref[rows, :].astype(BF16)
        cq_ref[rows, :] = _lane_bcast(col_ref[rows, :], FORGET_LANE + h)
        return 0
    lax.fori_loop(0, n_blocks, pro_body, 0)

    kn_m = _rms(km_ref[...], kw_ref[...]).astype(BF16)
    v_m = vm_ref[...].astype(BF16)
    mcol = lax.broadcasted_iota(jnp.int32, (1, META_ROWS), 1)
    bias_m = jnp.where(mcol >= META_PAD, -crowm_ref[...], NEG_BIG)

    qi_ = lax.broadcasted_iota(jnp.int32, (tq, tk), 0)
    ki_ = lax.broadcasted_iota(jnp.int32, (tq, tk), 1)
    reps = tk // LANES

    def q_body(qi, _):
        q0 = pl.multiple_of(qi * tq, tq)
        q = qn_ref[pl.ds(q0, tq), :]
        cq = cq_ref[pl.ds(q0, tq), :]
        cq_t = jnp.concatenate([cq] * reps, axis=1) if reps > 1 else cq

        s = _dot_nt(q, kn_m) + cq[:, :META_ROWS] + bias_m
        m = jnp.max(s, axis=-1, keepdims=True)
        p = jnp.exp(s - m)
        l = jnp.sum(p, axis=-1, keepdims=True)
        acc = _dot(p.astype(BF16), v_m)

        def kv_step(k0, carry, masked):
            m, l, acc = carry
            kb = kn_ref[pl.ds(k0, tk), :]
            s = _dot_nt(q, kb) + (cq_t - crow_ref[:, pl.ds(k0, tk)])
            if masked:
                s = jnp.where(k0 + ki_ <= q0 + qi_, s, NEG_BIG)
            m_new = jnp.maximum(m, jnp.max(s, axis=-1, keepdims=True))
            alpha = jnp.exp(m - m_new)
            p = jnp.exp(s - m_new)
            l = alpha * l + jnp.sum(p, axis=-1, keepdims=True)
            acc = alpha * acc + _dot(p.astype(BF16), vb_ref[pl.ds(k0, tk), :])
            return m_new, l, acc

        def kv_body(j, carry):
            return kv_step(pl.multiple_of(j * tk, tk), carry, False)
        n_full = (qi * tq) // tk
        carry = lax.fori_loop(0, n_full, kv_body, (m, l, acc))
        for d in range(tq // tk):
            carry = kv_step(pl.multiple_of(q0 + d * tk, tk), carry, True)
        m, l, acc = carry
        out = acc * (1.0 / l) * _silu(g_ref[pl.ds(q0, tq), :])
        o_ref[pl.ds(q0, tq), :] = out.astype(BF16)
        return 0
    lax.fori_loop(0, seq // tq, q_body, 0)


def _fox(proj, proj_m, col, row4, row_m, q_w, k_w, tq, tk):
    b, seq, _ = proj.shape
    base = 4 * GDN_HEADS
    hb = FOX_HEADS

    def head_block(off):
        return pl.BlockSpec((None, seq, HEAD_DIM), lambda i, j, off=off: (i, 0, off + j))

    def meta_block(off):
        return pl.BlockSpec((META_ROWS, HEAD_DIM), lambda i, j, off=off: (0, off + j))

    return pl.pallas_call(
        functools.partial(_fox_kernel, seq=seq, tq=tq, tk=tk),
        out_shape=jax.ShapeDtypeStruct((b, seq, FOX_WIDTH), BF16),
        grid=(b, FOX_HEADS),
        in_specs=[head_block(base), head_block(base + hb), head_block(base + 2 * hb),
                  head_block(base + 3 * hb),
                  meta_block(base + hb), meta_block(base + 2 * hb),
                  pl.BlockSpec((None, seq, LANES), lambda i, j: (i, 0, 0)),
                  pl.BlockSpec((None, None, 1, seq), lambda i, j: (i, FORGET_LANE + j, 0, 0)),
                  pl.BlockSpec((None, None, 1, META_ROWS), lambda i, j: (0, FORGET_LANE + j, 0, 0)),
                  pl.BlockSpec((1, HEAD_DIM), lambda i, j: (0, 0)),
                  pl.BlockSpec((1, HEAD_DIM), lambda i, j: (0, 0))],
        out_specs=pl.BlockSpec((None, seq, HEAD_DIM), lambda i, j: (i, 0, j)),
        scratch_shapes=[pltpu.VMEM((seq, HEAD_DIM), BF16), pltpu.VMEM((seq, HEAD_DIM), BF16),
                        pltpu.VMEM((seq, HEAD_DIM), BF16), pltpu.VMEM((seq, HEAD_DIM), F32)],
        compiler_params=pltpu.CompilerParams(
            dimension_semantics=("parallel", "arbitrary"), vmem_limit_bytes=VMEM_LIMIT),
        name="fox",
    )(proj, proj, proj, proj, proj_m, proj_m, col, row4, row_m, q_w, k_w)


def _out_proj_kernel(mg_ref, mf_ref, wg_ref, wf_ref, pw_ref, x_ref, o_ref):
    out = _dot(mg_ref[...], wg_ref[...]) + _dot(mf_ref[...], wf_ref[...])
    o_ref[...] = x_ref[...] + _rms(out, pw_ref[...])


def _out_proj(mg, mf, w_g, w_f, post_w, x2d, tm):
    m, d = x2d.shape
    return pl.pallas_call(
        _out_proj_kernel,
        out_shape=jax.ShapeDtypeStruct((m, d), F32),
        grid=(m // tm,),
        in_specs=[pl.BlockSpec((tm, GDN_WIDTH), lambda i: (i, 0)),
                  pl.BlockSpec((tm, FOX_WIDTH), lambda i: (i, 0)),
                  pl.BlockSpec((GDN_WIDTH, d), lambda i: (0, 0)),
                  pl.BlockSpec((FOX_WIDTH, d), lambda i: (0, 0)),
                  pl.BlockSpec((1, d), lambda i: (0, 0)),
                  pl.BlockSpec((tm, d), lambda i: (i, 0))],
        out_specs=pl.BlockSpec((tm, d), lambda i: (i, 0)),
        compiler_params=pltpu.CompilerParams(
            dimension_semantics=("parallel",), vmem_limit_bytes=VMEM_LIMIT),
        name="out_proj",
    )(mg, mf, w_g, w_f, post_w, x2d)


def _tile(total, want):
    t = min(total, want)
    while total % t:
        t //= 2
    return t


def _layer(x, meta_pad, pre_w, w_in, conv_w, a_log, dt_bias, gdn_norm_w, fox_q_w, fox_k_w,
           fox_f_bias, w_out, post_w):
    b, seq, d = x.shape
    gw, fw = GDN_WIDTH, FOX_WIDTH
    o_gb = 4 * gw
    o_f = o_gb + 2 * GDN_HEADS
    o_ff = o_f + 4 * fw
    w_main = jnp.concatenate([w_in[:, :o_gb], w_in[:, o_f:o_ff]], axis=1).astype(BF16)
    w_gate = jnp.concatenate(
        [w_in[:, o_gb:o_f], w_in[:, o_ff:],
         jnp.zeros((d, GATE_WIDTH - 2 * GDN_HEADS - FOX_HEADS), w_in.dtype)], axis=1).astype(BF16)
    zpad = jnp.zeros((GATE_WIDTH - FORGET_LANE - FOX_HEADS,), F32)
    add_vec = jnp.concatenate([jnp.zeros((GDN_HEADS,), F32), dt_bias, fox_f_bias, zpad])[None]
    alog_vec = jnp.concatenate([jnp.zeros((GDN_HEADS,), F32), a_log,
                                jnp.zeros((FOX_HEADS,), F32), zpad])[None]

    x2d = x.reshape(b * seq, d)
    pre_w2 = pre_w[None]
    proj, gate = _in_proj(x2d, pre_w2, w_main, w_gate, _tile(b * seq, 1024), 1024)
    proj_m, gate_m = _in_proj(meta_pad, pre_w2, w_main, w_gate, META_ROWS, 1024)
    proj = proj.reshape(b, seq, MAIN_WIDTH)

    col, row = _gate_prep(gate.reshape(b, seq, GATE_WIDTH), add_vec, alog_vec, False)
    col_m, row_m = _gate_prep(gate_m[None], add_vec, alog_vec, True)
    row4 = row.reshape(b, GATE_ROWS, 1, seq)
    row_m4 = row_m.reshape(1, GATE_ROWS, 1, META_ROWS)

    o_gdn = _gdn(proj, proj_m, conv_w.T, col, row4, col_m, row_m4, gdn_norm_w[None])
    o_fox = _fox(proj, proj_m, col, row4, row_m4, fox_q_w[None], fox_k_w[None], 256, 256)

    w_out_b = w_out.astype(BF16)
    out = _out_proj(o_gdn.reshape(b * seq, gw), o_fox.reshape(b * seq, fw),
                    w_out_b[:gw], w_out_b[gw:], post_w[None], x2d, _tile(b * seq, 512))
    return out.reshape(b, seq, d)


def kernel(x, meta_tokens, pre_norm_w, w_in, conv_w, a_log, dt_bias, gdn_norm_w, fox_q_norm_w,
           fox_k_norm_w, fox_f_bias, w_out, post_norm_w):
    assert pre_norm_w.shape[0] == 1, "single-layer stack"
    meta_pad = jnp.concatenate(
        [jnp.zeros((META_PAD, x.shape[-1]), x.dtype), meta_tokens.astype(x.dtype)], axis=0)
    return _layer(x, meta_pad, pre_norm_w[0], w_in[0], conv_w[0], a_log[0], dt_bias[0],
                  gdn_norm_w[0], fox_q_norm_w[0], fox_k_norm_w[0], fox_f_bias[0], w_out[0],
                  post_norm_w[0])
```

```python
import functools
import math

import jax
import jax.numpy as jnp
from jax import lax
from jax.experimental import pallas as pl
from jax.experimental.pallas import tpu as pltpu

N_META = 16
HEAD_DIM = 128
GDN_HEADS = 8
FOX_HEADS = 8
GDN_WIDTH = GDN_HEADS * HEAD_DIM
FOX_WIDTH = FOX_HEADS * HEAD_DIM
CONV_WIDTH = 4
CHUNK = 64
EPS = 1e-6

LANES = 128
SUBLANES = 8
MXU_DIM = 256
MAIN_WIDTH = 4 * GDN_WIDTH + 4 * FOX_WIDTH
GATE_WIDTH = LANES
BETA_LANE, DECAY_LANE, FORGET_LANE = 0, GDN_HEADS, 2 * GDN_HEADS
GATE_ROWS = 32
META_ROWS = CHUNK
META_PAD = META_ROWS - N_META
GROUP = MXU_DIM
ROW_BLOCK = 256
HEAD_PAIR = 2
REC_PAIRS = 2
REC_HEADS = REC_PAIRS * HEAD_PAIR
PREP_UNROLL = 2
FOX_TK = MXU_DIM
FOX_PIECE = 2 * MXU_DIM
VMEM_LIMIT = 56 * 1024 * 1024

F32 = jnp.float32
BF16 = jnp.bfloat16
NEG_BIG = -1e30
LOG2E = math.log2(math.e)


def _dot(a, b):
    return jnp.dot(a, b, preferred_element_type=F32)


def _dot_nt(a, b):
    return lax.dot_general(a, b, (((1,), (1,)), ((), ())), preferred_element_type=F32)


def _split3(x):
    hi = x.astype(BF16)
    r1 = x - hi.astype(F32)
    mid = r1.astype(BF16)
    lo = (r1 - mid.astype(F32)).astype(BF16)
    return hi, mid, lo


def _dot_exact_rhs01(parts, m):
    return _dot(parts[0], m) + _dot(parts[1], m) + _dot(parts[2], m)


def _dot_exact_lhs01(m, parts):
    return _dot(m, parts[0]) + _dot(m, parts[1]) + _dot(m, parts[2])


def _lane_bcast(col_tile, lane):
    sel = (lax.broadcasted_iota(jnp.int32, (LANES, LANES), 0) == lane).astype(BF16)
    return _dot_exact_rhs01(_split3(col_tile), sel)


def _chunk_of(idx):
    return jnp.right_shift(idx, CHUNK.bit_length() - 1)


def _rms(x, w):
    return x * lax.rsqrt(jnp.mean(x * x, axis=-1, keepdims=True) + EPS) * w


def _silu(x):
    return x * (1.0 / (1.0 + jnp.exp(-x)))


def _softplus(x):
    return jnp.maximum(x, 0.0) + jnp.log1p(jnp.exp(-jnp.abs(x)))


def _in_proj_kernel(x_ref, nw_ref, w_ref, wg_ref, o_ref, og_ref, xn_ref):
    @pl.when(pl.program_id(1) == 0)
    def _():
        xn = _rms(x_ref[...], nw_ref[...]).astype(BF16)
        xn_ref[...] = xn
        og_ref[...] = _dot(xn, wg_ref[...])

    o_ref[...] = _dot(xn_ref[...], w_ref[...])


def _in_proj(x2d, norm_w, w_main, w_gate, tm, tn):
    m, d = x2d.shape
    n = w_main.shape[1]
    return pl.pallas_call(
        _in_proj_kernel,
        out_shape=(jax.ShapeDtypeStruct((m, n), F32),
                   jax.ShapeDtypeStruct((m, GATE_WIDTH), F32)),
        grid=(m // tm, n // tn),
        in_specs=[pl.BlockSpec((tm, d), lambda i, j: (i, 0)),
                  pl.BlockSpec((1, d), lambda i, j: (0, 0)),
                  pl.BlockSpec((d, tn), lambda i, j: (0, j)),
                  pl.BlockSpec((d, GATE_WIDTH), lambda i, j: (0, 0))],
        out_specs=(pl.BlockSpec((tm, tn), lambda i, j: (i, j)),
                   pl.BlockSpec((tm, GATE_WIDTH), lambda i, j: (i, 0))),
        scratch_shapes=[pltpu.VMEM((tm, d), BF16)],
        compiler_params=pltpu.CompilerParams(
            dimension_semantics=("parallel", "arbitrary"), vmem_limit_bytes=VMEM_LIMIT),
        name="in_proj",
    )(x2d, norm_w, w_main, w_gate)


def _gate_kernel(t_ref, add_ref, alog_ref, col_ref, row_ref, *, rows, is_meta):
    blk = min(rows, ROW_BLOCK)
    lane = lax.broadcasted_iota(jnp.int32, (blk, LANES), 1)
    ri = lax.broadcasted_iota(jnp.int32, (blk, blk), 0)
    ci = lax.broadcasted_iota(jnp.int32, (blk, blk), 1)
    tri = (ci <= ri).astype(BF16)
    tri_chunk = ((ci <= ri) & (_chunk_of(ri) == _chunk_of(ci))).astype(BF16)
    is_beta = lane < DECAY_LANE
    is_decay = (lane >= DECAY_LANE) & (lane < FORGET_LANE)
    is_forget = (lane >= FORGET_LANE) & (lane < FORGET_LANE + FOX_HEADS)

    carry = jnp.zeros((1, LANES), F32)
    for r in range(rows // blk):
        t = t_ref[r * blk:(r + 1) * blk, :]
        ta = t + add_ref[...]
        beta = 1.0 / (1.0 + jnp.exp(-t))
        g = -jnp.exp(alog_ref[...]) * _softplus(ta)
        logf = -_softplus(-ta)
        val = jnp.where(is_decay, g, jnp.where(is_forget, logf, 0.0))
        if is_meta:
            row = lax.broadcasted_iota(jnp.int32, (blk, LANES), 0)
            val = jnp.where(row >= META_PAD, val, 0.0)
            beta = jnp.where(row >= META_PAD, beta, 0.0)
        parts = _split3(val)
        cum_chunk = _dot_exact_lhs01(tri_chunk, parts)
        cum_all = _dot_exact_lhs01(tri, parts) + carry
        carry = cum_all[blk - 1:blk, :]
        if is_meta:
            cum_all = cum_all - carry
        res = jnp.where(is_beta, beta, jnp.where(is_decay, cum_chunk, cum_all))
        col_ref[r * blk:(r + 1) * blk, :] = res
        row_ref[:, r * blk:(r + 1) * blk] = res.T[:GATE_ROWS, :]


def _gate_prep(gate3d, add_vec, alog_vec, is_meta):
    b, rows, _ = gate3d.shape
    kern = functools.partial(_gate_kernel, rows=rows, is_meta=is_meta)
    return pl.pallas_call(
        kern,
        out_shape=(jax.ShapeDtypeStruct((b, rows, LANES), F32),
                   jax.ShapeDtypeStruct((b, GATE_ROWS, rows), F32)),
        grid=(b,),
        in_specs=[pl.BlockSpec((None, rows, LANES), lambda i: (i, 0, 0)),
                  pl.BlockSpec((1, LANES), lambda i: (0, 0)),
                  pl.BlockSpec((1, LANES), lambda i: (0, 0))],
        out_specs=(pl.BlockSpec((None, rows, LANES), lambda i: (i, 0, 0)),
                   pl.BlockSpec((None, GATE_ROWS, rows), lambda i: (i, 0, 0))),
        compiler_params=pltpu.CompilerParams(
            dimension_semantics=("parallel",), vmem_limit_bytes=VMEM_LIMIT),
        name="gate_prep_meta" if is_meta else "gate_prep",
    )(gate3d, add_vec, alog_vec)


def _conv_silu(load_rows, w):
    y = load_rows(0) * w[0:1, :]
    for j in range(1, CONV_WIDTH):
        y = y + load_rows(j) * w[j:j + 1, :]
    return _silu(y)


def _l2norm(x):
    return x * lax.rsqrt(jnp.sum(x * x, axis=-1, keepdims=True) + EPS)


def _gdn_pointwise(q, k, v, beta_b, g_b):
    r = q.shape[0]
    q = _l2norm(q) * (HEAD_DIM ** -0.5)
    k = _l2norm(k)
    g3 = g_b.reshape(r // CHUNK, CHUNK, LANES)
    g_last = jnp.broadcast_to(g3[:, CHUNK - 1:CHUNK, :], g3.shape).reshape(r, LANES)
    e_g = jnp.exp(g_b)
    q_dec = q * e_g
    k_dec = k * jnp.exp(g_last - g_b)
    y = jnp.concatenate([v * beta_b, k * (beta_b * e_g)], axis=1)
    return q, k, q_dec, k_dec, y, jnp.exp(g_last)


def _gdn_groups(probs):
    r = probs[0][0].shape[0]
    ri = lax.broadcasted_iota(jnp.int32, (r, r), 0)
    ci = lax.broadcasted_iota(jnp.int32, (r, r), 1)
    same = _chunk_of(ri) == _chunk_of(ci)
    causal = same & (ci <= ri)
    strict = same & (ci < ri)

    def widen(t):
        return jnp.concatenate([t] * (r // LANES), axis=1) if r >= LANES else t[:, :r]

    kks = [_dot_nt(k, k) for _, k, _, _, _, _ in probs]
    qks = [_dot_nt(q, k) for q, k, _, _, _, _ in probs]
    dmats = [jnp.where(causal, jnp.exp(jnp.where(causal, widen(g_b) - g_row, 0.0)), 0.0)
             for _, _, _, _, g_b, g_row in probs]
    xs = [jnp.where(strict, widen(p[3]) * kk * d, 0.0).astype(BF16)
          for p, kk, d in zip(probs, kks, dmats)]
    a_qks = [qk * d for qk, d in zip(qks, dmats)]
    zs = [p[2].astype(F32) - _dot(x, p[2]) for p, x in zip(probs, xs)]
    span = 2
    while span < CHUNK:
        xs = [_dot(x, x).astype(BF16) for x in xs]
        zs = [z + _dot(x, z.astype(BF16)) for x, z in zip(xs, zs)]
        span *= 2
    return list(zip(zs, a_qks))


def _gdn_state0_kernel(km_ref, vm_ref, wk_ref, wv_ref, colm_ref, growm_ref, s_ref, pad_ref):
    h = pl.program_id(0)
    pad_ref[0:SUBLANES, :] = jnp.zeros((SUBLANES, LANES), F32)
    beta_m = _lane_bcast(colm_ref[...], BETA_LANE + h)
    g_m = _lane_bcast(colm_ref[...], DECAY_LANE + h)
    conv = []
    for src, w_ref in ((km_ref, wk_ref), (vm_ref, wv_ref)):
        pad_ref[SUBLANES:SUBLANES + META_ROWS, :] = src[...]
        conv.append(_conv_silu(
            lambda j: pad_ref[SUBLANES - (CONV_WIDTH - 1) + j:
                              SUBLANES - (CONV_WIDTH - 1) + j + META_ROWS, :], w_ref[...]))
    _, k_m, _, kd_m, y_m, _ = _gdn_pointwise(conv[0], conv[0], conv[1], beta_m, g_m)
    k_bf = k_m.astype(BF16)
    (uw_m, _), = _gdn_groups([(k_bf, k_bf, y_m.astype(BF16), beta_m, g_m, growm_ref[...])])
    s_ref[...] = _dot(kd_m.T.astype(BF16), uw_m[:, :HEAD_DIM].astype(BF16))


def _gdn_state0(proj_m, conv_wt, col_m, row_m4):
    hb = GDN_HEADS

    def meta_block(off):
        return pl.BlockSpec((META_ROWS, HEAD_DIM), lambda j, off=off: (0, off + j))

    def conv_block(off):
        return pl.BlockSpec((CONV_WIDTH, HEAD_DIM), lambda j, off=off: (0, off + j))

    return pl.pallas_call(
        _gdn_state0_kernel,
        out_shape=jax.ShapeDtypeStruct((GDN_HEADS, HEAD_DIM, HEAD_DIM), F32),
        grid=(GDN_HEADS,),
        in_specs=[meta_block(hb), meta_block(2 * hb), conv_block(hb), conv_block(2 * hb),
                  pl.BlockSpec((None, META_ROWS, LANES), lambda j: (0, 0, 0)),
                  pl.BlockSpec((None, None, 1, META_ROWS), lambda j: (0, DECAY_LANE + j, 0, 0))],
        out_specs=pl.BlockSpec((None, HEAD_DIM, HEAD_DIM), lambda j: (j, 0, 0)),
        scratch_shapes=[pltpu.VMEM((META_ROWS + SUBLANES, HEAD_DIM), F32)],
        compiler_params=pltpu.CompilerParams(
            dimension_semantics=("parallel",), vmem_limit_bytes=VMEM_LIMIT),
        name="gdn_state0",
    )(proj_m, proj_m, conv_wt, conv_wt, col_m, row_m4)


def _gdn_kernel(*refs, seq):
    n_qkv = 3 * HEAD_PAIR
    src_refs = [refs[t * HEAD_PAIR:(t + 1) * HEAD_PAIR] for t in range(3)]
    z_ref = refs[n_qkv]
    meta_refs = [refs[n_qkv + 1 + t * HEAD_PAIR:n_qkv + 1 + (t + 1) * HEAD_PAIR] for t in range(3)]
    tap_refs = [refs[2 * n_qkv + 1 + t * HEAD_PAIR:2 * n_qkv + 1 + (t + 1) * HEAD_PAIR]
                for t in range(3)]
    (col_ref, grow_ref, s0_ref, nw_ref, o_ref, pad_ref, qs_ref, ks_ref, y_ref, bb_ref, gb_ref,
     qd_ref, kdt_ref, dec_ref, u_ref, w_ref, aqk_ref, st_ref) = refs[3 * n_qkv + 1:]
    pair = pl.program_id(1)
    slot = pair % REC_PAIRS
    n_blocks = seq // ROW_BLOCK
    n_groups = seq // GROUP
    cpg = GROUP // CHUNK
    hist = CONV_WIDTH - 1

    def pointwise_block(i, convs):
        rows = pl.ds(pl.multiple_of(i * ROW_BLOCK, ROW_BLOCK), ROW_BLOCK)
        col = col_ref[rows, :]
        for hh in range(HEAD_PAIR):
            hs = slot * HEAD_PAIR + hh
            beta_b = _lane_bcast(col, BETA_LANE + pair * HEAD_PAIR + hh)
            g_b = _lane_bcast(col, DECAY_LANE + pair * HEAD_PAIR + hh)
            q, k, q_dec, k_dec, y, dec = _gdn_pointwise(convs[0][hh], convs[1][hh], convs[2][hh],
                                                        beta_b, g_b)
            qs_ref[hh, rows, :] = q.astype(BF16)
            ks_ref[hh, rows, :] = k.astype(BF16)
            y_ref[hh, rows, :] = y.astype(BF16)
            bb_ref[hh, rows, :] = beta_b
            gb_ref[hh, rows, :] = g_b
            qd_ref[hs, rows, :] = q_dec.astype(BF16)
            kdt_ref[hs, :, rows] = k_dec.T.astype(BF16)
            for c in range(ROW_BLOCK // CHUNK):
                dec_ref[hs, pl.ds(i * (ROW_BLOCK // CHUNK) + c, 1), :] = dec[c * CHUNK:c * CHUNK + 1, :]

    first = []
    for t in range(3):
        per_head = []
        for hh in range(HEAD_PAIR):
            pad_ref[0:SUBLANES, :] = meta_refs[t][hh][META_ROWS - SUBLANES:META_ROWS, :]
            pad_ref[SUBLANES:SUBLANES + ROW_BLOCK, :] = src_refs[t][hh][0:ROW_BLOCK, :]
            per_head.append(_conv_silu(
                lambda j: pad_ref[SUBLANES - hist + j:SUBLANES - hist + j + ROW_BLOCK, :],
                tap_refs[t][hh][...]))
        first.append(per_head)
    pointwise_block(0, first)

    def point_body(i, _):
        r0 = pl.multiple_of(i * ROW_BLOCK, ROW_BLOCK)
        convs = [[_conv_silu(lambda j, src=src_refs[t][hh]: src[pl.ds(r0 - hist + j, ROW_BLOCK), :],
                             tap_refs[t][hh][...])
                  for hh in range(HEAD_PAIR)] for t in range(3)]
        pointwise_block(i, convs)
        return 0
    lax.fori_loop(1, n_blocks, point_body, 0)

    def prep_body(gi, _):
        keys = [(hh, pl.ds(pl.multiple_of((gi * PREP_UNROLL + u) * GROUP, GROUP), GROUP))
                for u in range(PREP_UNROLL) for hh in range(HEAD_PAIR)]
        probs = [(qs_ref[hh, rows, :], ks_ref[hh, rows, :], y_ref[hh, rows, :],
                  bb_ref[hh, rows, :], gb_ref[hh, rows, :], grow_ref[hh, :, rows])
                 for hh, rows in keys]
        for (hh, rows), (uw, a_qk) in zip(keys, _gdn_groups(probs)):
            hs = slot * HEAD_PAIR + hh
            u_ref[hs, rows, :] = uw[:, :HEAD_DIM]
            w_ref[hs, rows, :] = uw[:, HEAD_DIM:].astype(BF16)
            aqk_ref[hs, rows, :] = a_qk.astype(BF16)
        return 0
    lax.fori_loop(0, n_groups // PREP_UNROLL, prep_body, 0)

    @pl.when(slot == REC_PAIRS - 1)
    def _():
        st_ref[...] = s0_ref[...]

        def rec_body(gi, _):
            r0 = pl.multiple_of(gi * GROUP, GROUP)
            rows = pl.ds(r0, GROUP)
            heads = range(REC_HEADS)
            kd_t = [kdt_ref[h, :, rows] for h in heads]
            outs = [[] for _ in heads]
            for c in range(cpg):
                crow = pl.ds(r0 + c * CHUNK, CHUNK)
                states = [st_ref[h] for h in heads]
                s_bf = [s.astype(BF16) for s in states]
                ws = [_dot(jnp.concatenate([w_ref[h, crow, :], qd_ref[h, crow, :]], axis=0), s_bf[h])
                      for h in heads]
                v_new = [u_ref[h, crow, :] - ws[h][:CHUNK, :] for h in heads]
                zero = lambda n: jnp.zeros((n * CHUNK, HEAD_DIM), BF16)
                v_pad = [jnp.concatenate(([zero(c)] if c else []) + [v.astype(BF16)]
                                         + ([zero(cpg - 1 - c)] if c < cpg - 1 else []), axis=0)
                         for v in v_new]
                upd = [_dot(jnp.concatenate([aqk_ref[h, crow, :], kd_t[h]], axis=0), v_pad[h])
                       for h in heads]
                for h in heads:
                    dec = dec_ref[h, pl.ds(gi * cpg + c, 1), :]
                    st_ref[h] = states[h] * dec + upd[h][CHUNK:, :]
                    outs[h].append(ws[h][CHUNK:, :] + upd[h][:CHUNK, :])
            o = jnp.concatenate([_rms(jnp.concatenate(outs[h], axis=0), nw_ref[...]) for h in heads],
                                axis=1)
            o_ref[rows, :] = (o * _silu(z_ref[rows, :])).astype(BF16)
            return 0
        lax.fori_loop(0, n_groups, rec_body, 0)


def _gdn(proj, proj_m, conv_wt, col, row4, state0, norm_w):
    b, seq, _ = proj.shape
    n_pairs = GDN_HEADS // HEAD_PAIR
    rec_width = REC_HEADS * HEAD_DIM

    def head_specs(shape, imap):
        return [pl.BlockSpec(shape, functools.partial(imap, t * GDN_HEADS + hh))
                for t in range(3) for hh in range(HEAD_PAIR)]

    src_specs = head_specs((None, seq, HEAD_DIM), lambda off, i, j: (i, 0, off + HEAD_PAIR * j))
    meta_specs = head_specs((META_ROWS, HEAD_DIM), lambda off, i, j: (0, off + HEAD_PAIR * j))
    tap_specs = head_specs((CONV_WIDTH, HEAD_DIM), lambda off, i, j: (0, off + HEAD_PAIR * j))
    n_qkv = 3 * HEAD_PAIR
    z_block0 = 3 * GDN_WIDTH // rec_width

    per_pair = lambda width, dt: pltpu.VMEM((HEAD_PAIR, seq, width), dt)
    per_rec = lambda width, dt: pltpu.VMEM((REC_HEADS, seq, width), dt)
    return pl.pallas_call(
        functools.partial(_gdn_kernel, seq=seq),
        out_shape=jax.ShapeDtypeStruct((b, seq, GDN_WIDTH), BF16),
        grid=(b, n_pairs),
        in_specs=src_specs
        + [pl.BlockSpec((None, seq, rec_width), lambda i, j: (i, 0, z_block0 + j // REC_PAIRS))]
        + meta_specs + tap_specs
        + [pl.BlockSpec((None, seq, LANES), lambda i, j: (i, 0, 0)),
           pl.BlockSpec((None, HEAD_PAIR, 1, seq),
                        lambda i, j: (i, DECAY_LANE // HEAD_PAIR + j, 0, 0)),
           pl.BlockSpec((REC_HEADS, HEAD_DIM, HEAD_DIM), lambda i, j: (j // REC_PAIRS, 0, 0)),
           pl.BlockSpec((1, HEAD_DIM), lambda i, j: (0, 0))],
        out_specs=pl.BlockSpec((None, seq, rec_width), lambda i, j: (i, 0, j // REC_PAIRS)),
        scratch_shapes=[pltpu.VMEM((ROW_BLOCK + SUBLANES, HEAD_DIM), F32),
                        per_pair(HEAD_DIM, BF16), per_pair(HEAD_DIM, BF16),
                        per_pair(2 * HEAD_DIM, BF16),
                        per_pair(HEAD_DIM, F32), per_pair(HEAD_DIM, F32),
                        per_rec(HEAD_DIM, BF16),
                        pltpu.VMEM((REC_HEADS, HEAD_DIM, seq), BF16),
                        pltpu.VMEM((REC_HEADS, seq // CHUNK, HEAD_DIM), F32),
                        per_rec(HEAD_DIM, F32), per_rec(HEAD_DIM, BF16), per_rec(GROUP, BF16),
                        pltpu.VMEM((REC_HEADS, HEAD_DIM, HEAD_DIM), F32)],
        compiler_params=pltpu.CompilerParams(
            dimension_semantics=("parallel", "arbitrary"), vmem_limit_bytes=VMEM_LIMIT),
        name="gdn",
    )(*([proj] * n_qkv), proj, *([proj_m] * n_qkv), *([conv_wt] * n_qkv),
      col, row4, state0, norm_w)


def _fox_kernel(q_ref, k_ref, v_ref, g_ref, km_ref, vm_ref, col_ref, colm_ref, crow_ref,
                qw_ref, kw_ref, o_ref, qt_ref, ka_ref, kam_ref, vt_ref, vtm_ref,
                m_ref, l_ref, acc_ref, *, seq):
    h = pl.program_id(1)
    n_blocks = seq // ROW_BLOCK
    aug_r = lax.broadcasted_iota(jnp.int32, (LANES, ROW_BLOCK), 0)

    def key_aug(ck, valid=None):
        hi, mid, lo = _split3(ck)
        lane = lax.broadcasted_iota(jnp.int32, ck.shape, 1)
        neg_hi = -hi.astype(F32)
        if valid is not None:
            neg_hi = jnp.where(valid, neg_hi, NEG_BIG)
        blk = jnp.where(lane < 3, 1.0,
                        jnp.where(lane == 3, neg_hi,
                                  jnp.where(lane == 4, -mid.astype(F32),
                                            jnp.where(lane == 5, -lo.astype(F32), 0.0))))
        return blk.astype(BF16)

    def pro_body(i, _):
        r0 = pl.multiple_of(i * ROW_BLOCK, ROW_BLOCK)
        rows = pl.ds(r0, ROW_BLOCK)
        qn = _rms(q_ref[rows, :], qw_ref[...]) * (HEAD_DIM ** -0.5 * LOG2E)
        qt_ref[0:HEAD_DIM, rows] = qn.T.astype(BF16)
        hi, mid, lo = _split3(crow_ref[:, rows] * LOG2E)
        aug = jnp.where(aug_r == 0, hi.astype(F32),
                        jnp.where(aug_r == 1, mid.astype(F32),
                                  jnp.where(aug_r == 2, lo.astype(F32),
                                            jnp.where(aug_r < 6, 1.0, 0.0))))
        qt_ref[HEAD_DIM:2 * HEAD_DIM, rows] = aug.astype(BF16)
        ka_ref[rows, 0:HEAD_DIM] = _rms(k_ref[rows, :], kw_ref[...]).astype(BF16)
        ck = _lane_bcast(col_ref[rows, :], FORGET_LANE + h) * LOG2E
        ka_ref[rows, HEAD_DIM:2 * HEAD_DIM] = key_aug(ck)
        vt_ref[:, rows] = v_ref[rows, :].T.astype(BF16)
        return 0
    lax.fori_loop(0, n_blocks, pro_body, 0)

    kam_ref[:, 0:HEAD_DIM] = _rms(km_ref[...], kw_ref[...]).astype(BF16)
    ck_m = _lane_bcast(colm_ref[...], FORGET_LANE + h) * LOG2E
    mrow = lax.broadcasted_iota(jnp.int32, (META_ROWS, LANES), 0)
    kam_ref[:, HEAD_DIM:2 * HEAD_DIM] = key_aug(ck_m, mrow >= META_PAD)
    vtm_ref[...] = vm_ref[...].T.astype(BF16)

    kidx = lax.broadcasted_iota(jnp.int32, (FOX_TK, FOX_TK), 0)
    qidx = lax.broadcasted_iota(jnp.int32, (FOX_TK, FOX_TK), 1)
    diag_ok = kidx <= qidx

    def key_step(k_aug, v_t, lane0, first, diag):
        pieces = [slice(p0, min(p0 + FOX_PIECE, seq)) for p0 in range(lane0, seq, FOX_PIECE)]
        ss = [_dot(k_aug, qt_ref[:, ln]) for ln in pieces]
        if diag:
            masked = jnp.where(diag_ok, ss[0][:, :FOX_TK], NEG_BIG)
            ss[0] = (jnp.concatenate([masked, ss[0][:, FOX_TK:]], axis=1)
                     if ss[0].shape[1] > FOX_TK else masked)
        tops = [jnp.max(s, axis=0, keepdims=True) for s in ss]
        if first:
            m_new = tops
        else:
            m_old = [m_ref[:, ln] for ln in pieces]
            m_new = [jnp.maximum(a, b) for a, b in zip(m_old, tops)]
            alpha = [jnp.exp2(a - b) for a, b in zip(m_old, m_new)]
        ps = [jnp.exp2(s - m) for s, m in zip(ss, m_new)]
        sums = [jnp.sum(p, axis=0, keepdims=True) for p in ps]
        pvs = [_dot(v_t, p.astype(BF16)) for p in ps]
        for i, ln in enumerate(pieces):
            m_ref[:, ln] = m_new[i]
            if first:
                l_ref[:, ln] = sums[i]
                acc_ref[:, ln] = pvs[i]
            else:
                l_ref[:, ln] = alpha[i] * l_ref[:, ln] + sums[i]
                acc_ref[:, ln] = alpha[i] * acc_ref[:, ln] + pvs[i]

    key_step(kam_ref[...], vtm_ref[...], 0, True, False)
    for kj in range(seq // FOX_TK):
        k0 = kj * FOX_TK
        key_step(ka_ref[k0:k0 + FOX_TK, :], vt_ref[:, k0:k0 + FOX_TK], k0, False, True)

    def out_body(i, _):
        r0 = pl.multiple_of(i * ROW_BLOCK, ROW_BLOCK)
        rows = pl.ds(r0, ROW_BLOCK)
        out_t = acc_ref[:, rows] * (1.0 / l_ref[:, rows])
        o_ref[rows, :] = (out_t.T * _silu(g_ref[rows, :])).astype(BF16)
        return 0
    lax.fori_loop(0, n_blocks, out_body, 0)


def _fox(proj, proj_m, col, col_m, row4, q_w, k_w):
    b, seq, _ = proj.shape
    base = 4 * GDN_HEADS
    hb = FOX_HEADS

    def head_block(off):
        return pl.BlockSpec((None, seq, HEAD_DIM), lambda i, j, off=off: (i, 0, off + j))

    def meta_block(off):
        return pl.BlockSpec((META_ROWS, HEAD_DIM), lambda i, j, off=off: (0, off + j))

    return pl.pallas_call(
        functools.partial(_fox_kernel, seq=seq),
        out_shape=jax.ShapeDtypeStruct((b, seq, FOX_WIDTH), BF16),
        grid=(b, FOX_HEADS),
        in_specs=[head_block(base), head_block(base + hb), head_block(base + 2 * hb),
                  head_block(base + 3 * hb),
                  meta_block(base + hb), meta_block(base + 2 * hb),
                  pl.BlockSpec((None, seq, LANES), lambda i, j: (i, 0, 0)),
                  pl.BlockSpec((None, META_ROWS, LANES), lambda i, j: (0, 0, 0)),
                  pl.BlockSpec((None, None, 1, seq), lambda i, j: (i, FORGET_LANE + j, 0, 0)),
                  pl.BlockSpec((1, HEAD_DIM), lambda i, j: (0, 0)),
                  pl.BlockSpec((1, HEAD_DIM), lambda i, j: (0, 0))],
        out_specs=pl.BlockSpec((None, seq, HEAD_DIM), lambda i, j: (i, 0, j)),
        scratch_shapes=[pltpu.VMEM((2 * HEAD_DIM, seq), BF16),
                        pltpu.VMEM((seq, 2 * HEAD_DIM), BF16),
                        pltpu.VMEM((META_ROWS, 2 * HEAD_DIM), BF16),
                        pltpu.VMEM((HEAD_DIM, seq), BF16),
                        pltpu.VMEM((HEAD_DIM, META_ROWS), BF16),
                        pltpu.VMEM((1, seq), F32), pltpu.VMEM((1, seq), F32),
                        pltpu.VMEM((HEAD_DIM, seq), F32)],
        compiler_params=pltpu.CompilerParams(
            dimension_semantics=("parallel", "arbitrary"), vmem_limit_bytes=VMEM_LIMIT),
        name="fox",
    )(proj, proj, proj, proj, proj_m, proj_m, col, col_m, row4, q_w, k_w)


def _out_proj_kernel(mg_ref, mf_ref, wg_ref, wf_ref, pw_ref, x_ref, o_ref):
    out = _dot(mg_ref[...], wg_ref[...]) + _dot(mf_ref[...], wf_ref[...])
    o_ref[...] = x_ref[...] + _rms(out, pw_ref[...])


def _out_proj(mg, mf, w_g, w_f, post_w, x2d, tm):
    m, d = x2d.shape
    return pl.pallas_call(
        _out_proj_kernel,
        out_shape=jax.ShapeDtypeStruct((m, d), F32),
        grid=(m // tm,),
        in_specs=[pl.BlockSpec((tm, GDN_WIDTH), lambda i: (i, 0)),
                  pl.BlockSpec((tm, FOX_WIDTH), lambda i: (i, 0)),
                  pl.BlockSpec((GDN_WIDTH, d), lambda i: (0, 0)),
                  pl.BlockSpec((FOX_WIDTH, d), lambda i: (0, 0)),
                  pl.BlockSpec((1, d), lambda i: (0, 0)),
                  pl.BlockSpec((tm, d), lambda i: (i, 0))],
        out_specs=pl.BlockSpec((tm, d), lambda i: (i, 0)),
        compiler_params=pltpu.CompilerParams(
            dimension_semantics=("parallel",), vmem_limit_bytes=VMEM_LIMIT),
        name="out_proj",
    )(mg, mf, w_g, w_f, post_w, x2d)


def _tile(total, want):
    t = min(total, want)
    while total % t:
        t //= 2
    return t


def _layer(x, meta_pad, pre_w, w_in, conv_w, a_log, dt_bias, gdn_norm_w, fox_q_w, fox_k_w,
           fox_f_bias, w_out, post_w):
    b, seq, d = x.shape
    assert seq % (GROUP * PREP_UNROLL) == 0 and seq % FOX_TK == 0 and seq % ROW_BLOCK == 0
    gw, fw = GDN_WIDTH, FOX_WIDTH
    o_gb = 4 * gw
    o_f = o_gb + 2 * GDN_HEADS
    o_ff = o_f + 4 * fw
    w_main = jnp.concatenate([w_in[:, :o_gb], w_in[:, o_f:o_ff]], axis=1).astype(BF16)
    w_gate = jnp.concatenate(
        [w_in[:, o_gb:o_f], w_in[:, o_ff:],
         jnp.zeros((d, GATE_WIDTH - 2 * GDN_HEADS - FOX_HEADS), w_in.dtype)], axis=1).astype(BF16)
    zpad = jnp.zeros((GATE_WIDTH - FORGET_LANE - FOX_HEADS,), F32)
    add_vec = jnp.concatenate([jnp.zeros((GDN_HEADS,), F32), dt_bias, fox_f_bias, zpad])[None]
    alog_vec = jnp.concatenate([jnp.zeros((GDN_HEADS,), F32), a_log,
                                jnp.zeros((FOX_HEADS,), F32), zpad])[None]

    x2d = x.reshape(b * seq, d)
    pre_w2 = pre_w[None]
    proj, gate = _in_proj(x2d, pre_w2, w_main, w_gate, _tile(b * seq, 1024), 1024)
    proj_m, gate_m = _in_proj(meta_pad, pre_w2, w_main, w_gate, META_ROWS, 1024)
    proj = proj.reshape(b, seq, MAIN_WIDTH)

    col, row = _gate_prep(gate.reshape(b, seq, GATE_WIDTH), add_vec, alog_vec, False)
    col_m, row_m = _gate_prep(gate_m[None], add_vec, alog_vec, True)
    row4 = row.reshape(b, GATE_ROWS, 1, seq)
    row_m4 = row_m.reshape(1, GATE_ROWS, 1, META_ROWS)

    conv_wt = conv_w.T
    state0 = _gdn_state0(proj_m, conv_wt, col_m, row_m4)
    o_gdn = _gdn(proj, proj_m, conv_wt, col, row4, state0, gdn_norm_w[None])
    o_fox = _fox(proj, proj_m, col, col_m, row4, fox_q_w[None], fox_k_w[None])

    w_out_b = w_out.astype(BF16)
    out = _out_proj(o_gdn.reshape(b * seq, gw), o_fox.reshape(b * seq, fw),
                    w_out_b[:gw], w_out_b[gw:], post_w[None], x2d, _tile(b * seq, 512))
    return out.reshape(b, seq, d)


def kernel(x, meta_tokens, pre_norm_w, w_in, conv_w, a_log, dt_bias, gdn_norm_w, fox_q_norm_w,
           fox_k_norm_w, fox_f_bias, w_out, post_norm_w):
    assert pre_norm_w.shape[0] == 1, "single-layer stack"
    meta_pad = jnp.concatenate(
        [jnp.zeros((META_PAD, x.shape[-1]), x.dtype), meta_tokens.astype(x.dtype)], axis=0)
    return _layer(x, meta_pad, pre_norm_w[0], w_in[0], conv_w[0], a_log[0], dt_bias[0],
                  gdn_norm_w[0], fox_q_norm_w[0], fox_k_norm_w[0], fox_f_bias[0], w_out[0],
                  post_norm_w[0])
```

```python
import functools
import math

import jax
import jax.numpy as jnp
from jax import lax
from jax.experimental import pallas as pl
from jax.experimental.pallas import tpu as pltpu

N_META = 16
HEAD_DIM = 128
GDN_HEADS = 8
FOX_HEADS = 8
GDN_WIDTH = GDN_HEADS * HEAD_DIM
FOX_WIDTH = FOX_HEADS * HEAD_DIM
CONV_WIDTH = 4
CHUNK = 64
EPS = 1e-6

LANES = 128
SUBLANES = 8
MXU_DIM = 256
MAIN_WIDTH = 4 * GDN_WIDTH + 4 * FOX_WIDTH
GATE_WIDTH = LANES
BETA_LANE, DECAY_LANE, FORGET_LANE = 0, GDN_HEADS, 2 * GDN_HEADS
GATE_ROWS = 32
META_ROWS = CHUNK
META_PAD = META_ROWS - N_META
GROUP = MXU_DIM
ROW_BLOCK = 256
HIST_ROWS = 2 * SUBLANES
IN_PROJ_TM, IN_PROJ_TN = 1024, 2048
OUT_PROJ_TM = 512
HEAD_PAIR = 2
REC_PAIRS = 2
REC_HEADS = REC_PAIRS * HEAD_PAIR
PREP_UNROLL = 2
FOX_TK = MXU_DIM
FOX_PIECE = 2 * MXU_DIM
VMEM_LIMIT = 56 * 1024 * 1024

F32 = jnp.float32
BF16 = jnp.bfloat16
NEG_BIG = -1e30
LOG2E = math.log2(math.e)


def _dot(a, b):
    return jnp.dot(a, b, preferred_element_type=F32)


def _dot_nt(a, b):
    return lax.dot_general(a, b, (((1,), (1,)), ((), ())), preferred_element_type=F32)


def _split3(x):
    hi = x.astype(BF16)
    r1 = x - hi.astype(F32)
    mid = r1.astype(BF16)
    lo = (r1 - mid.astype(F32)).astype(BF16)
    return hi, mid, lo


def _dot_exact_rhs01(parts, m):
    return _dot(parts[0], m) + _dot(parts[1], m) + _dot(parts[2], m)


def _dot_exact_lhs01(m, parts):
    return _dot(m, parts[0]) + _dot(m, parts[1]) + _dot(m, parts[2])


def _lane_bcast(col_tile, lane):
    sel = (lax.broadcasted_iota(jnp.int32, (LANES, LANES), 0) == lane).astype(BF16)
    return _dot_exact_rhs01(_split3(col_tile), sel)


def _chunk_of(idx):
    return jnp.right_shift(idx, CHUNK.bit_length() - 1)


def _rms(x, w):
    return x * lax.rsqrt(jnp.mean(x * x, axis=-1, keepdims=True) + EPS) * w


def _silu(x):
    return x * (1.0 / (1.0 + jnp.exp(-x)))


def _softplus(x):
    return jnp.maximum(x, 0.0) + jnp.log1p(jnp.exp(-jnp.abs(x)))


def _in_proj_kernel(x_ref, nw_ref, w_ref, wg_ref, o_ref, og_ref, xn_ref):
    @pl.when(pl.program_id(1) == 0)
    def _():
        xn = _rms(x_ref[...], nw_ref[...]).astype(BF16)
        xn_ref[...] = xn
        og_ref[...] = _dot(xn, wg_ref[...])

    o_ref[...] = _dot(xn_ref[...], w_ref[...]).astype(o_ref.dtype)


def _in_proj(x2d, norm_w, w_main, w_gate, tm, tn):
    m, d = x2d.shape
    n = w_main.shape[1]
    return pl.pallas_call(
        _in_proj_kernel,
        out_shape=(jax.ShapeDtypeStruct((m, n), BF16),
                   jax.ShapeDtypeStruct((m, GATE_WIDTH), F32)),
        grid=(m // tm, n // tn),
        in_specs=[pl.BlockSpec((tm, d), lambda i, j: (i, 0)),
                  pl.BlockSpec((1, d), lambda i, j: (0, 0)),
                  pl.BlockSpec((d, tn), lambda i, j: (0, j)),
                  pl.BlockSpec((d, GATE_WIDTH), lambda i, j: (0, 0))],
        out_specs=(pl.BlockSpec((tm, tn), lambda i, j: (i, j)),
                   pl.BlockSpec((tm, GATE_WIDTH), lambda i, j: (i, 0))),
        scratch_shapes=[pltpu.VMEM((tm, d), BF16)],
        compiler_params=pltpu.CompilerParams(
            dimension_semantics=("parallel", "arbitrary"), vmem_limit_bytes=VMEM_LIMIT),
        name="in_proj",
    )(x2d, norm_w, w_main, w_gate)


def _gate_kernel(t_ref, add_ref, alog_ref, col_ref, row_ref, *, rows, is_meta):
    blk = min(rows, ROW_BLOCK)
    lane = lax.broadcasted_iota(jnp.int32, (blk, LANES), 1)
    ri = lax.broadcasted_iota(jnp.int32, (blk, blk), 0)
    ci = lax.broadcasted_iota(jnp.int32, (blk, blk), 1)
    tri = (ci <= ri).astype(BF16)
    tri_chunk = ((ci <= ri) & (_chunk_of(ri) == _chunk_of(ci))).astype(BF16)
    is_beta = lane < DECAY_LANE
    is_decay = (lane >= DECAY_LANE) & (lane < FORGET_LANE)
    is_forget = (lane >= FORGET_LANE) & (lane < FORGET_LANE + FOX_HEADS)

    carry = jnp.zeros((1, LANES), F32)
    for r in range(rows // blk):
        t = t_ref[r * blk:(r + 1) * blk, :]
        ta = t + add_ref[...]
        beta = 1.0 / (1.0 + jnp.exp(-t))
        g = -jnp.exp(alog_ref[...]) * _softplus(ta)
        logf = -_softplus(-ta)
        val = jnp.where(is_decay, g, jnp.where(is_forget, logf, 0.0))
        if is_meta:
            row = lax.broadcasted_iota(jnp.int32, (blk, LANES), 0)
            val = jnp.where(row >= META_PAD, val, 0.0)
            beta = jnp.where(row >= META_PAD, beta, 0.0)
        parts = _split3(val)
        cum_chunk = _dot_exact_lhs01(tri_chunk, parts)
        cum_all = _dot_exact_lhs01(tri, parts) + carry
        carry = cum_all[blk - 1:blk, :]
        if is_meta:
            cum_all = cum_all - carry
        res = jnp.where(is_beta, beta, jnp.where(is_decay, cum_chunk, cum_all))
        col_ref[r * blk:(r + 1) * blk, :] = res
        row_ref[:, r * blk:(r + 1) * blk] = res.T[:GATE_ROWS, :]


def _gate_prep(gate3d, add_vec, alog_vec, is_meta):
    b, rows, _ = gate3d.shape
    kern = functools.partial(_gate_kernel, rows=rows, is_meta=is_meta)
    return pl.pallas_call(
        kern,
        out_shape=(jax.ShapeDtypeStruct((b, rows, LANES), F32),
                   jax.ShapeDtypeStruct((b, GATE_ROWS, rows), F32)),
        grid=(b,),
        in_specs=[pl.BlockSpec((None, rows, LANES), lambda i: (i, 0, 0)),
                  pl.BlockSpec((1, LANES), lambda i: (0, 0)),
                  pl.BlockSpec((1, LANES), lambda i: (0, 0))],
        out_specs=(pl.BlockSpec((None, rows, LANES), lambda i: (i, 0, 0)),
                   pl.BlockSpec((None, GATE_ROWS, rows), lambda i: (i, 0, 0))),
        compiler_params=pltpu.CompilerParams(
            dimension_semantics=("parallel",), vmem_limit_bytes=VMEM_LIMIT),
        name="gate_prep_meta" if is_meta else "gate_prep",
    )(gate3d, add_vec, alog_vec)


def _conv_silu(load_rows, w):
    y = load_rows(0) * w[0:1, :]
    for j in range(1, CONV_WIDTH):
        y = y + load_rows(j) * w[j:j + 1, :]
    return _silu(y)


def _l2norm(x):
    return x * lax.rsqrt(jnp.sum(x * x, axis=-1, keepdims=True) + EPS)


def _gdn_pointwise(q, k, v, beta_b, g_b):
    r = q.shape[0]
    q = _l2norm(q) * (HEAD_DIM ** -0.5)
    k = _l2norm(k)
    g3 = g_b.reshape(r // CHUNK, CHUNK, LANES)
    g_last = jnp.broadcast_to(g3[:, CHUNK - 1:CHUNK, :], g3.shape).reshape(r, LANES)
    e_g = jnp.exp(g_b)
    q_dec = q * e_g
    k_dec = k * jnp.exp(g_last - g_b)
    y = jnp.concatenate([v * beta_b, k * (beta_b * e_g)], axis=1)
    return q, k, q_dec, k_dec, y, jnp.exp(g_last)


def _gdn_groups(probs):
    r = probs[0][0].shape[0]
    ri = lax.broadcasted_iota(jnp.int32, (r, r), 0)
    ci = lax.broadcasted_iota(jnp.int32, (r, r), 1)
    same = _chunk_of(ri) == _chunk_of(ci)
    causal = same & (ci <= ri)
    strict = same & (ci < ri)

    def widen(t):
        return jnp.concatenate([t] * (r // LANES), axis=1) if r >= LANES else t[:, :r]

    kks = [_dot_nt(k, k) for _, k, _, _, _, _ in probs]
    qks = [_dot_nt(q, k) for q, k, _, _, _, _ in probs]
    dmats = [jnp.where(causal, jnp.exp(jnp.where(causal, widen(g_b) - g_row, 0.0)), 0.0)
             for _, _, _, _, g_b, g_row in probs]
    xs = [jnp.where(strict, widen(p[3]) * kk * d, 0.0).astype(BF16)
          for p, kk, d in zip(probs, kks, dmats)]
    a_qks = [qk * d for qk, d in zip(qks, dmats)]
    zs = [p[2].astype(F32) - _dot(x, p[2]) for p, x in zip(probs, xs)]
    span = 2
    while span < CHUNK:
        xs = [_dot(x, x).astype(BF16) for x in xs]
        zs = [z + _dot(x, z.astype(BF16)) for x, z in zip(xs, zs)]
        span *= 2
    return list(zip(zs, a_qks))


def _gdn_state0_kernel(km_ref, vm_ref, wk_ref, wv_ref, colm_ref, growm_ref, s_ref, pad_ref):
    h = pl.program_id(0)
    pad_ref[0:SUBLANES, :] = jnp.zeros((SUBLANES, LANES), F32)
    beta_m = _lane_bcast(colm_ref[...], BETA_LANE + h)
    g_m = _lane_bcast(colm_ref[...], DECAY_LANE + h)
    conv = []
    for src, w_ref in ((km_ref, wk_ref), (vm_ref, wv_ref)):
        pad_ref[SUBLANES:SUBLANES + META_ROWS, :] = src[...].astype(F32)
        conv.append(_conv_silu(
            lambda j: pad_ref[SUBLANES - (CONV_WIDTH - 1) + j:
                              SUBLANES - (CONV_WIDTH - 1) + j + META_ROWS, :], w_ref[...]))
    _, k_m, _, kd_m, y_m, _ = _gdn_pointwise(conv[0], conv[0], conv[1], beta_m, g_m)
    k_bf = k_m.astype(BF16)
    (uw_m, _), = _gdn_groups([(k_bf, k_bf, y_m.astype(BF16), beta_m, g_m, growm_ref[...])])
    s_ref[...] = _dot(kd_m.T.astype(BF16), uw_m[:, :HEAD_DIM].astype(BF16))


def _gdn_state0(proj_m, conv_wt, col_m, row_m4):
    hb = GDN_HEADS

    def meta_block(off):
        return pl.BlockSpec((META_ROWS, HEAD_DIM), lambda j, off=off: (0, off + j))

    def conv_block(off):
        return pl.BlockSpec((CONV_WIDTH, HEAD_DIM), lambda j, off=off: (0, off + j))

    return pl.pallas_call(
        _gdn_state0_kernel,
        out_shape=jax.ShapeDtypeStruct((GDN_HEADS, HEAD_DIM, HEAD_DIM), F32),
        grid=(GDN_HEADS,),
        in_specs=[meta_block(hb), meta_block(2 * hb), conv_block(hb), conv_block(2 * hb),
                  pl.BlockSpec((None, META_ROWS, LANES), lambda j: (0, 0, 0)),
                  pl.BlockSpec((None, None, 1, META_ROWS), lambda j: (0, DECAY_LANE + j, 0, 0))],
        out_specs=pl.BlockSpec((None, HEAD_DIM, HEAD_DIM), lambda j: (j, 0, 0)),
        scratch_shapes=[pltpu.VMEM((META_ROWS + SUBLANES, HEAD_DIM), F32)],
        compiler_params=pltpu.CompilerParams(
            dimension_semantics=("parallel",), vmem_limit_bytes=VMEM_LIMIT),
        name="gdn_state0",
    )(proj_m, proj_m, conv_wt, conv_wt, col_m, row_m4)


def _gdn_kernel(*refs, seq):
    n_qkv = 3 * HEAD_PAIR
    src_refs = [refs[t * HEAD_PAIR:(t + 1) * HEAD_PAIR] for t in range(3)]
    z_ref = refs[n_qkv]
    meta_refs = [refs[n_qkv + 1 + t * HEAD_PAIR:n_qkv + 1 + (t + 1) * HEAD_PAIR] for t in range(3)]
    tap_refs = [refs[2 * n_qkv + 1 + t * HEAD_PAIR:2 * n_qkv + 1 + (t + 1) * HEAD_PAIR]
                for t in range(3)]
    (col_ref, grow_ref, s0_ref, nw_ref, o_ref, pad_ref, qs_ref, ks_ref, y_ref, bb_ref, gb_ref,
     qd_ref, kdt_ref, dec_ref, u_ref, w_ref, aqk_ref, st_ref) = refs[3 * n_qkv + 1:]
    pair = pl.program_id(1)
    slot = pair % REC_PAIRS
    n_blocks = seq // ROW_BLOCK
    n_groups = seq // GROUP
    cpg = GROUP // CHUNK
    hist = CONV_WIDTH - 1

    def pointwise_block(i, convs):
        rows = pl.ds(pl.multiple_of(i * ROW_BLOCK, ROW_BLOCK), ROW_BLOCK)
        col = col_ref[rows, :]
        for hh in range(HEAD_PAIR):
            hs = slot * HEAD_PAIR + hh
            beta_b = _lane_bcast(col, BETA_LANE + pair * HEAD_PAIR + hh)
            g_b = _lane_bcast(col, DECAY_LANE + pair * HEAD_PAIR + hh)
            q, k, q_dec, k_dec, y, dec = _gdn_pointwise(convs[0][hh], convs[1][hh], convs[2][hh],
                                                        beta_b, g_b)
            qs_ref[hh, rows, :] = q.astype(BF16)
            ks_ref[hh, rows, :] = k.astype(BF16)
            y_ref[hh, rows, :] = y.astype(BF16)
            bb_ref[hh, rows, :] = beta_b
            gb_ref[hh, rows, :] = g_b
            qd_ref[hs, rows, :] = q_dec.astype(BF16)
            kdt_ref[hs, :, rows] = k_dec.T.astype(BF16)
            for c in range(ROW_BLOCK // CHUNK):
                dec_ref[hs, pl.ds(i * (ROW_BLOCK // CHUNK) + c, 1), :] = dec[c * CHUNK:c * CHUNK + 1, :]

    def convs_of(history, r0):
        out = []
        for t in range(3):
            per_head = []
            for hh in range(HEAD_PAIR):
                win = pad_ref.at[t * HEAD_PAIR + hh]
                win[0:HIST_ROWS, :] = history(t, hh).astype(F32)
                win[HIST_ROWS:HIST_ROWS + ROW_BLOCK, :] = (
                    src_refs[t][hh][pl.ds(r0, ROW_BLOCK), :].astype(F32))
                per_head.append(_conv_silu(
                    lambda j, win=win: win[HIST_ROWS - hist + j:HIST_ROWS - hist + j + ROW_BLOCK, :],
                    tap_refs[t][hh][...]))
            out.append(per_head)
        return out

    pointwise_block(0, convs_of(
        lambda t, hh: meta_refs[t][hh][META_ROWS - HIST_ROWS:META_ROWS, :], 0))

    def point_body(i, _):
        r0 = pl.multiple_of(i * ROW_BLOCK, ROW_BLOCK)
        prev = pl.multiple_of(r0 - HIST_ROWS, HIST_ROWS)
        pointwise_block(i, convs_of(lambda t, hh: src_refs[t][hh][pl.ds(prev, HIST_ROWS), :], r0))
        return 0
    lax.fori_loop(1, n_blocks, point_body, 0)

    def prep_body(gi, _):
        keys = [(hh, pl.ds(pl.multiple_of((gi * PREP_UNROLL + u) * GROUP, GROUP), GROUP))
                for u in range(PREP_UNROLL) for hh in range(HEAD_PAIR)]
        probs = [(qs_ref[hh, rows, :], ks_ref[hh, rows, :], y_ref[hh, rows, :],
                  bb_ref[hh, rows, :], gb_ref[hh, rows, :], grow_ref[hh, :, rows])
                 for hh, rows in keys]
        for (hh, rows), (uw, a_qk) in zip(keys, _gdn_groups(probs)):
            hs = slot * HEAD_PAIR + hh
            u_ref[hs, rows, :] = uw[:, :HEAD_DIM]
            w_ref[hs, rows, :] = uw[:, HEAD_DIM:].astype(BF16)
            aqk_ref[hs, rows, :] = a_qk.astype(BF16)
        return 0
    lax.fori_loop(0, n_groups // PREP_UNROLL, prep_body, 0)

    @pl.when(slot == REC_PAIRS - 1)
    def _():
        st_ref[...] = s0_ref[...]

        def rec_body(gi, _):
            r0 = pl.multiple_of(gi * GROUP, GROUP)
            rows = pl.ds(r0, GROUP)
            heads = range(REC_HEADS)
            kd_t = [kdt_ref[h, :, rows] for h in heads]
            outs = [[] for _ in heads]
            for c in range(cpg):
                crow = pl.ds(r0 + c * CHUNK, CHUNK)
                states = [st_ref[h] for h in heads]
                s_bf = [s.astype(BF16) for s in states]
                ws = [_dot(jnp.concatenate([w_ref[h, crow, :], qd_ref[h, crow, :]], axis=0), s_bf[h])
                      for h in heads]
                v_new = [u_ref[h, crow, :] - ws[h][:CHUNK, :] for h in heads]
                zero = lambda n: jnp.zeros((n * CHUNK, HEAD_DIM), BF16)
                v_pad = [jnp.concatenate(([zero(c)] if c else []) + [v.astype(BF16)]
                                         + ([zero(cpg - 1 - c)] if c < cpg - 1 else []), axis=0)
                         for v in v_new]
                upd = [_dot(jnp.concatenate([aqk_ref[h, crow, :], kd_t[h]], axis=0), v_pad[h])
                       for h in heads]
                for h in heads:
                    dec = dec_ref[h, pl.ds(gi * cpg + c, 1), :]
                    st_ref[h] = states[h] * dec + upd[h][CHUNK:, :]
                    outs[h].append(ws[h][CHUNK:, :] + upd[h][:CHUNK, :])
            o = jnp.concatenate([_rms(jnp.concatenate(outs[h], axis=0), nw_ref[...]) for h in heads],
                                axis=1)
            o_ref[rows, :] = (o * _silu(z_ref[rows, :].astype(F32))).astype(BF16)
            return 0
        lax.fori_loop(0, n_groups, rec_body, 0)


def _gdn(proj, proj_m, conv_wt, col, row4, state0, norm_w):
    b, seq, _ = proj.shape
    n_pairs = GDN_HEADS // HEAD_PAIR
    rec_width = REC_HEADS * HEAD_DIM

    def head_specs(shape, imap):
        return [pl.BlockSpec(shape, functools.partial(imap, t * GDN_HEADS + hh))
                for t in range(3) for hh in range(HEAD_PAIR)]

    src_specs = head_specs((None, seq, HEAD_DIM), lambda off, i, j: (i, 0, off + HEAD_PAIR * j))
    meta_specs = head_specs((META_ROWS, HEAD_DIM), lambda off, i, j: (0, off + HEAD_PAIR * j))
    tap_specs = head_specs((CONV_WIDTH, HEAD_DIM), lambda off, i, j: (0, off + HEAD_PAIR * j))
    n_qkv = 3 * HEAD_PAIR
    z_block0 = 3 * GDN_WIDTH // rec_width

    per_pair = lambda width, dt: pltpu.VMEM((HEAD_PAIR, seq, width), dt)
    per_rec = lambda width, dt: pltpu.VMEM((REC_HEADS, seq, width), dt)
    return pl.pallas_call(
        functools.partial(_gdn_kernel, seq=seq),
        out_shape=jax.ShapeDtypeStruct((b, seq, GDN_WIDTH), BF16),
        grid=(b, n_pairs),
        in_specs=src_specs
        + [pl.BlockSpec((None, seq, rec_width), lambda i, j: (i, 0, z_block0 + j // REC_PAIRS))]
        + meta_specs + tap_specs
        + [pl.BlockSpec((None, seq, LANES), lambda i, j: (i, 0, 0)),
           pl.BlockSpec((None, HEAD_PAIR, 1, seq),
                        lambda i, j: (i, DECAY_LANE // HEAD_PAIR + j, 0, 0)),
           pl.BlockSpec((REC_HEADS, HEAD_DIM, HEAD_DIM), lambda i, j: (j // REC_PAIRS, 0, 0)),
           pl.BlockSpec((1, HEAD_DIM), lambda i, j: (0, 0))],
        out_specs=pl.BlockSpec((None, seq, rec_width), lambda i, j: (i, 0, j // REC_PAIRS)),
        scratch_shapes=[pltpu.VMEM((n_qkv, HIST_ROWS + ROW_BLOCK, HEAD_DIM), F32),
                        per_pair(HEAD_DIM, BF16), per_pair(HEAD_DIM, BF16),
                        per_pair(2 * HEAD_DIM, BF16),
                        per_pair(HEAD_DIM, F32), per_pair(HEAD_DIM, F32),
                        per_rec(HEAD_DIM, BF16),
                        pltpu.VMEM((REC_HEADS, HEAD_DIM, seq), BF16),
                        pltpu.VMEM((REC_HEADS, seq // CHUNK, HEAD_DIM), F32),
                        per_rec(HEAD_DIM, F32), per_rec(HEAD_DIM, BF16), per_rec(GROUP, BF16),
                        pltpu.VMEM((REC_HEADS, HEAD_DIM, HEAD_DIM), F32)],
        compiler_params=pltpu.CompilerParams(
            dimension_semantics=("parallel", "arbitrary"), vmem_limit_bytes=VMEM_LIMIT),
        name="gdn",
    )(*([proj] * n_qkv), proj, *([proj_m] * n_qkv), *([conv_wt] * n_qkv),
      col, row4, state0, norm_w)


def _fox_kernel(q_ref, k_ref, v_ref, g_ref, km_ref, vm_ref, col_ref, colm_ref, crow_ref,
                qw_ref, kw_ref, o_ref, qt_ref, ka_ref, kam_ref, vt_ref, vtm_ref,
                m_ref, l_ref, acc_ref, *, seq):
    h = pl.program_id(1)
    n_blocks = seq // ROW_BLOCK
    aug_r = lax.broadcasted_iota(jnp.int32, (LANES, ROW_BLOCK), 0)

    def key_aug(ck, valid=None):
        hi, mid, lo = _split3(ck)
        lane = lax.broadcasted_iota(jnp.int32, ck.shape, 1)
        neg_hi = -hi.astype(F32)
        if valid is not None:
            neg_hi = jnp.where(valid, neg_hi, NEG_BIG)
        blk = jnp.where(lane < 3, 1.0,
                        jnp.where(lane == 3, neg_hi,
                                  jnp.where(lane == 4, -mid.astype(F32),
                                            jnp.where(lane == 5, -lo.astype(F32), 0.0))))
        return blk.astype(BF16)

    def pro_body(i, _):
        r0 = pl.multiple_of(i * ROW_BLOCK, ROW_BLOCK)
        rows = pl.ds(r0, ROW_BLOCK)
        qn = _rms(q_ref[rows, :].astype(F32), qw_ref[...]) * (HEAD_DIM ** -0.5 * LOG2E)
        qt_ref[0:HEAD_DIM, rows] = qn.T.astype(BF16)
        hi, mid, lo = _split3(crow_ref[:, rows] * LOG2E)
        aug = jnp.where(aug_r == 0, hi.astype(F32),
                        jnp.where(aug_r == 1, mid.astype(F32),
                                  jnp.where(aug_r == 2, lo.astype(F32),
                                            jnp.where(aug_r < 6, 1.0, 0.0))))
        qt_ref[HEAD_DIM:2 * HEAD_DIM, rows] = aug.astype(BF16)
        ka_ref[rows, 0:HEAD_DIM] = _rms(k_ref[rows, :].astype(F32), kw_ref[...]).astype(BF16)
        ck = _lane_bcast(col_ref[rows, :], FORGET_LANE + h) * LOG2E
        ka_ref[rows, HEAD_DIM:2 * HEAD_DIM] = key_aug(ck)
        vt_ref[:, rows] = v_ref[rows, :].astype(F32).T.astype(BF16)
        return 0
    lax.fori_loop(0, n_blocks, pro_body, 0)

    kam_ref[:, 0:HEAD_DIM] = _rms(km_ref[...].astype(F32), kw_ref[...]).astype(BF16)
    ck_m = _lane_bcast(colm_ref[...], FORGET_LANE + h) * LOG2E
    mrow = lax.broadcasted_iota(jnp.int32, (META_ROWS, LANES), 0)
    kam_ref[:, HEAD_DIM:2 * HEAD_DIM] = key_aug(ck_m, mrow >= META_PAD)
    vtm_ref[...] = vm_ref[...].astype(F32).T.astype(BF16)

    kidx = lax.broadcasted_iota(jnp.int32, (FOX_TK, FOX_TK), 0)
    qidx = lax.broadcasted_iota(jnp.int32, (FOX_TK, FOX_TK), 1)
    diag_ok = kidx <= qidx

    items = []
    for kj in range(-1, seq // FOX_TK):
        lane0 = max(kj, 0) * FOX_TK
        for p0 in range(lane0, seq, FOX_PIECE):
            items.append((kj, slice(p0, min(p0 + FOX_PIECE, seq)), p0 == lane0))

    def scores(item):
        kj, ln, leads = item
        k_aug = kam_ref[...] if kj < 0 else ka_ref[kj * FOX_TK:(kj + 1) * FOX_TK, :]
        s = _dot(k_aug, qt_ref[:, ln])
        if kj >= 0 and leads:
            masked = jnp.where(diag_ok, s[:, :FOX_TK], NEG_BIG)
            s = jnp.concatenate([masked, s[:, FOX_TK:]], axis=1) if s.shape[1] > FOX_TK else masked
        return s

    def softmax_stats(item, s):
        kj, ln, _ = item
        top = jnp.max(s, axis=0, keepdims=True)
        if kj < 0:
            m_new, alpha = top, None
        else:
            m_old = m_ref[:, ln]
            m_new = jnp.maximum(m_old, top)
            alpha = jnp.exp2(m_old - m_new)
        p = jnp.exp2(s - m_new)
        psum = jnp.sum(p, axis=0, keepdims=True)
        m_ref[:, ln] = m_new
        l_ref[:, ln] = psum if kj < 0 else alpha * l_ref[:, ln] + psum
        return p.astype(BF16), alpha

    def values(item, p, alpha):
        kj, ln, _ = item
        v_t = vtm_ref[...] if kj < 0 else vt_ref[:, kj * FOX_TK:(kj + 1) * FOX_TK]
        pv = _dot(v_t, p)
        acc_ref[:, ln] = pv if kj < 0 else alpha * acc_ref[:, ln] + pv

    s_cur = scores(items[0])
    p_cur = None
    for t in range(len(items) + 1):
        s_next = scores(items[t + 1]) if t + 1 < len(items) else None
        p_next = softmax_stats(items[t], s_cur) if t < len(items) else None
        if p_cur is not None:
            values(items[t - 1], *p_cur)
        s_cur, p_cur = s_next, p_next

    def out_body(i, _):
        r0 = pl.multiple_of(i * ROW_BLOCK, ROW_BLOCK)
        rows = pl.ds(r0, ROW_BLOCK)
        out_t = acc_ref[:, rows] * (1.0 / l_ref[:, rows])
        o_ref[rows, :] = (out_t.T * _silu(g_ref[rows, :].astype(F32))).astype(BF16)
        return 0
    lax.fori_loop(0, n_blocks, out_body, 0)


def _fox(proj, proj_m, col, col_m, row4, q_w, k_w):
    b, seq, _ = proj.shape
    base = 4 * GDN_HEADS
    hb = FOX_HEADS

    def head_block(off):
        return pl.BlockSpec((None, seq, HEAD_DIM), lambda i, j, off=off: (i, 0, off + j))

    def meta_block(off):
        return pl.BlockSpec((META_ROWS, HEAD_DIM), lambda i, j, off=off: (0, off + j))

    return pl.pallas_call(
        functools.partial(_fox_kernel, seq=seq),
        out_shape=jax.ShapeDtypeStruct((b, seq, FOX_WIDTH), BF16),
        grid=(b, FOX_HEADS),
        in_specs=[head_block(base), head_block(base + hb), head_block(base + 2 * hb),
                  head_block(base + 3 * hb),
                  meta_block(base + hb), meta_block(base + 2 * hb),
                  pl.BlockSpec((None, seq, LANES), lambda i, j: (i, 0, 0)),
                  pl.BlockSpec((None, META_ROWS, LANES), lambda i, j: (0, 0, 0)),
                  pl.BlockSpec((None, None, 1, seq), lambda i, j: (i, FORGET_LANE + j, 0, 0)),
                  pl.BlockSpec((1, HEAD_DIM), lambda i, j: (0, 0)),
                  pl.BlockSpec((1, HEAD_DIM), lambda i, j: (0, 0))],
        out_specs=pl.BlockSpec((None, seq, HEAD_DIM), lambda i, j: (i, 0, j)),
        scratch_shapes=[pltpu.VMEM((2 * HEAD_DIM, seq), BF16),
                        pltpu.VMEM((seq, 2 * HEAD_DIM), BF16),
                        pltpu.VMEM((META_ROWS, 2 * HEAD_DIM), BF16),
                        pltpu.VMEM((HEAD_DIM, seq), BF16),
                        pltpu.VMEM((HEAD_DIM, META_ROWS), BF16),
                        pltpu.VMEM((1, seq), F32), pltpu.VMEM((1, seq), F32),
                        pltpu.VMEM((HEAD_DIM, seq), F32)],
        compiler_params=pltpu.CompilerParams(
            dimension_semantics=("parallel", "arbitrary"), vmem_limit_bytes=VMEM_LIMIT),
        name="fox",
    )(proj, proj, proj, proj, proj_m, proj_m, col, col_m, row4, q_w, k_w)


def _out_proj_kernel(mg_ref, mf_ref, wg_ref, wf_ref, pw_ref, x_ref, o_ref):
    out = _dot(mg_ref[...], wg_ref[...]) + _dot(mf_ref[...], wf_ref[...])
    o_ref[...] = x_ref[...] + _rms(out, pw_ref[...])


def _out_proj(mg, mf, w_g, w_f, post_w, x2d, tm):
    m, d = x2d.shape
    return pl.pallas_call(
        _out_proj_kernel,
        out_shape=jax.ShapeDtypeStruct((m, d), F32),
        grid=(m // tm,),
        in_specs=[pl.BlockSpec((tm, GDN_WIDTH), lambda i: (i, 0)),
                  pl.BlockSpec((tm, FOX_WIDTH), lambda i: (i, 0)),
                  pl.BlockSpec((GDN_WIDTH, d), lambda i: (0, 0)),
                  pl.BlockSpec((FOX_WIDTH, d), lambda i: (0, 0)),
                  pl.BlockSpec((1, d), lambda i: (0, 0)),
                  pl.BlockSpec((tm, d), lambda i: (i, 0))],
        out_specs=pl.BlockSpec((tm, d), lambda i: (i, 0)),
        compiler_params=pltpu.CompilerParams(
            dimension_semantics=("parallel",), vmem_limit_bytes=VMEM_LIMIT),
        name="out_proj",
    )(mg, mf, w_g, w_f, post_w, x2d)


def _tile(total, want):
    t = min(total, want)
    while total % t:
        t //= 2
    return t


def _layer(x, meta_pad, pre_w, w_in, conv_w, a_log, dt_bias, gdn_norm_w, fox_q_w, fox_k_w,
           fox_f_bias, w_out, post_w):
    b, seq, d = x.shape
    assert seq % (GROUP * PREP_UNROLL) == 0 and seq % FOX_TK == 0 and seq % ROW_BLOCK == 0
    gw, fw = GDN_WIDTH, FOX_WIDTH
    o_gb = 4 * gw
    o_f = o_gb + 2 * GDN_HEADS
    o_ff = o_f + 4 * fw
    w_main = jnp.concatenate([w_in[:, :o_gb], w_in[:, o_f:o_ff]], axis=1).astype(BF16)
    w_gate = jnp.concatenate(
        [w_in[:, o_gb:o_f], w_in[:, o_ff:],
         jnp.zeros((d, GATE_WIDTH - 2 * GDN_HEADS - FOX_HEADS), w_in.dtype)], axis=1).astype(BF16)
    zpad = jnp.zeros((GATE_WIDTH - FORGET_LANE - FOX_HEADS,), F32)
    add_vec = jnp.concatenate([jnp.zeros((GDN_HEADS,), F32), dt_bias, fox_f_bias, zpad])[None]
    alog_vec = jnp.concatenate([jnp.zeros((GDN_HEADS,), F32), a_log,
                                jnp.zeros((FOX_HEADS,), F32), zpad])[None]

    x2d = x.reshape(b * seq, d)
    pre_w2 = pre_w[None]
    proj, gate = _in_proj(x2d, pre_w2, w_main, w_gate, _tile(b * seq, IN_PROJ_TM), IN_PROJ_TN)
    proj_m, gate_m = _in_proj(meta_pad, pre_w2, w_main, w_gate, META_ROWS, IN_PROJ_TN)
    proj = proj.reshape(b, seq, MAIN_WIDTH)

    col, row = _gate_prep(gate.reshape(b, seq, GATE_WIDTH), add_vec, alog_vec, False)
    col_m, row_m = _gate_prep(gate_m[None], add_vec, alog_vec, True)
    row4 = row.reshape(b, GATE_ROWS, 1, seq)
    row_m4 = row_m.reshape(1, GATE_ROWS, 1, META_ROWS)

    conv_wt = conv_w.T
    state0 = _gdn_state0(proj_m, conv_wt, col_m, row_m4)
    o_gdn = _gdn(proj, proj_m, conv_wt, col, row4, state0, gdn_norm_w[None])
    o_fox = _fox(proj, proj_m, col, col_m, row4, fox_q_w[None], fox_k_w[None])

    w_out_b = w_out.astype(BF16)
    out = _out_proj(o_gdn.reshape(b * seq, gw), o_fox.reshape(b * seq, fw),
                    w_out_b[:gw], w_out_b[gw:], post_w[None], x2d, _tile(b * seq, OUT_PROJ_TM))
    return out.reshape(b, seq, d)


def kernel(x, meta_tokens, pre_norm_w, w_in, conv_w, a_log, dt_bias, gdn_norm_w, fox_q_norm_w,
           fox_k_norm_w, fox_f_bias, w_out, post_norm_w):
    assert pre_norm_w.shape[0] == 1, "single-layer stack"
    meta_pad = jnp.concatenate(
        [jnp.zeros((META_PAD, x.shape[-1]), x.dtype), meta_tokens.astype(x.dtype)], axis=0)
    return _layer(x, meta_pad, pre_norm_w[0], w_in[0], conv_w[0], a_log[0], dt_bias[0],
                  gdn_norm_w[0], fox_q_norm_w[0], fox_k_norm_w[0], fox_f_bias[0], w_out[0],
                  post_norm_w[0])
```

```python
import functools
import math

import jax
import jax.numpy as jnp
from jax import lax
from jax.experimental import pallas as pl
from jax.experimental.pallas import tpu as pltpu

N_META = 16
HEAD_DIM = 128
GDN_HEADS = 8
FOX_HEADS = 8
GDN_WIDTH = GDN_HEADS * HEAD_DIM
FOX_WIDTH = FOX_HEADS * HEAD_DIM
CONV_WIDTH = 4
CHUNK = 64
EPS = 1e-6

LANES = 128
SUBLANES = 8
MXU_DIM = 256
MAIN_WIDTH = 4 * GDN_WIDTH + 4 * FOX_WIDTH
GATE_WIDTH = LANES
BETA_LANE, DECAY_LANE, FORGET_LANE = 0, GDN_HEADS, 2 * GDN_HEADS
GATE_ROWS = 32
META_ROWS = CHUNK
META_PAD = META_ROWS - N_META
GROUP = MXU_DIM
ROW_BLOCK = 256
HIST_ROWS = 2 * SUBLANES
IN_PROJ_TM, IN_PROJ_TN = 1024, 2048
OUT_PROJ_TM = 512
HEAD_PAIR = 2
REC_PAIRS = 2
REC_HEADS = REC_PAIRS * HEAD_PAIR
PREP_UNROLL = 2
FOX_TK = MXU_DIM
FOX_PIECE = 2 * MXU_DIM
VMEM_LIMIT = 56 * 1024 * 1024

F32 = jnp.float32
BF16 = jnp.bfloat16
NEG_BIG = -1e30
LOG2E = math.log2(math.e)


def _dot(a, b):
    return jnp.dot(a, b, preferred_element_type=F32)


def _dot_nt(a, b):
    return lax.dot_general(a, b, (((1,), (1,)), ((), ())), preferred_element_type=F32)


def _split3(x):
    hi = x.astype(BF16)
    r1 = x - hi.astype(F32)
    mid = r1.astype(BF16)
    lo = (r1 - mid.astype(F32)).astype(BF16)
    return hi, mid, lo


def _dot_exact_rhs01(parts, m):
    return _dot(parts[0], m) + _dot(parts[1], m) + _dot(parts[2], m)


def _dot_exact_lhs01(m, parts):
    return _dot(m, parts[0]) + _dot(m, parts[1]) + _dot(m, parts[2])


def _lane_bcast(col_tile, lane):
    sel = (lax.broadcasted_iota(jnp.int32, (LANES, LANES), 0) == lane).astype(BF16)
    return _dot_exact_rhs01(_split3(col_tile), sel)


def _chunk_of(idx):
    return jnp.right_shift(idx, CHUNK.bit_length() - 1)


def _rms(x, w):
    return x * lax.rsqrt(jnp.mean(x * x, axis=-1, keepdims=True) + EPS) * w


def _silu(x):
    return x * (1.0 / (1.0 + jnp.exp(-x)))


def _softplus(x):
    return jnp.maximum(x, 0.0) + jnp.log1p(jnp.exp(-jnp.abs(x)))


def _in_proj_kernel(x_ref, nw_ref, w_ref, wg_ref, o_ref, og_ref, xn_ref):
    @pl.when(pl.program_id(1) == 0)
    def _():
        xn = _rms(x_ref[...], nw_ref[...]).astype(BF16)
        xn_ref[...] = xn
        og_ref[...] = _dot(xn, wg_ref[...])

    o_ref[...] = _dot(xn_ref[...], w_ref[...]).astype(o_ref.dtype)


def _in_proj(x2d, norm_w, w_cat, tm, tn):
    m, d = x2d.shape
    n = MAIN_WIDTH
    return pl.pallas_call(
        _in_proj_kernel,
        out_shape=(jax.ShapeDtypeStruct((m, n), BF16),
                   jax.ShapeDtypeStruct((m, GATE_WIDTH), F32)),
        grid=(m // tm, n // tn),
        in_specs=[pl.BlockSpec((tm, d), lambda i, j: (i, 0)),
                  pl.BlockSpec((1, d), lambda i, j: (0, 0)),
                  pl.BlockSpec((d, tn), lambda i, j: (0, j)),
                  pl.BlockSpec((d, GATE_WIDTH), lambda i, j: (0, MAIN_WIDTH // GATE_WIDTH))],
        out_specs=(pl.BlockSpec((tm, tn), lambda i, j: (i, j)),
                   pl.BlockSpec((tm, GATE_WIDTH), lambda i, j: (i, 0))),
        scratch_shapes=[pltpu.VMEM((tm, d), BF16)],
        compiler_params=pltpu.CompilerParams(
            dimension_semantics=("parallel", "arbitrary"), vmem_limit_bytes=VMEM_LIMIT),
        name="in_proj",
    )(x2d, norm_w, w_cat, w_cat)


def _gate_kernel(t_ref, add_ref, alog_ref, col_ref, row_ref, *, rows, is_meta):
    blk = min(rows, ROW_BLOCK)
    lane = lax.broadcasted_iota(jnp.int32, (blk, LANES), 1)
    ri = lax.broadcasted_iota(jnp.int32, (blk, blk), 0)
    ci = lax.broadcasted_iota(jnp.int32, (blk, blk), 1)
    tri = (ci <= ri).astype(BF16)
    tri_chunk = ((ci <= ri) & (_chunk_of(ri) == _chunk_of(ci))).astype(BF16)
    is_beta = lane < DECAY_LANE
    is_decay = (lane >= DECAY_LANE) & (lane < FORGET_LANE)
    is_forget = (lane >= FORGET_LANE) & (lane < FORGET_LANE + FOX_HEADS)

    carry = jnp.zeros((1, LANES), F32)
    for r in range(rows // blk):
        t = t_ref[r * blk:(r + 1) * blk, :]
        ta = t + add_ref[...]
        beta = 1.0 / (1.0 + jnp.exp(-t))
        g = -jnp.exp(alog_ref[...]) * _softplus(ta)
        logf = -_softplus(-ta)
        val = jnp.where(is_decay, g, jnp.where(is_forget, logf, 0.0))
        if is_meta:
            row = lax.broadcasted_iota(jnp.int32, (blk, LANES), 0)
            val = jnp.where(row >= META_PAD, val, 0.0)
            beta = jnp.where(row >= META_PAD, beta, 0.0)
        parts = _split3(val)
        cum_chunk = _dot_exact_lhs01(tri_chunk, parts)
        cum_all = _dot_exact_lhs01(tri, parts) + carry
        carry = cum_all[blk - 1:blk, :]
        if is_meta:
            cum_all = cum_all - carry
        res = jnp.where(is_beta, beta, jnp.where(is_decay, cum_chunk, cum_all))
        col_ref[r * blk:(r + 1) * blk, :] = res
        row_ref[:, r * blk:(r + 1) * blk] = res.T[:GATE_ROWS, :]


def _gate_prep(gate3d, add_vec, alog_vec, is_meta):
    b, rows, _ = gate3d.shape
    kern = functools.partial(_gate_kernel, rows=rows, is_meta=is_meta)
    return pl.pallas_call(
        kern,
        out_shape=(jax.ShapeDtypeStruct((b, rows, LANES), F32),
                   jax.ShapeDtypeStruct((b, GATE_ROWS, rows), F32)),
        grid=(b,),
        in_specs=[pl.BlockSpec((None, rows, LANES), lambda i: (i, 0, 0)),
                  pl.BlockSpec((1, LANES), lambda i: (0, 0)),
                  pl.BlockSpec((1, LANES), lambda i: (0, 0))],
        out_specs=(pl.BlockSpec((None, rows, LANES), lambda i: (i, 0, 0)),
                   pl.BlockSpec((None, GATE_ROWS, rows), lambda i: (i, 0, 0))),
        compiler_params=pltpu.CompilerParams(
            dimension_semantics=("parallel",), vmem_limit_bytes=VMEM_LIMIT),
        name="gate_prep_meta" if is_meta else "gate_prep",
    )(gate3d, add_vec, alog_vec)


def _conv_silu(load_rows, w):
    y = load_rows(0) * w[0:1, :]
    for j in range(1, CONV_WIDTH):
        y = y + load_rows(j) * w[j:j + 1, :]
    return _silu(y)


def _l2norm(x):
    return x * lax.rsqrt(jnp.sum(x * x, axis=-1, keepdims=True) + EPS)


def _gdn_pointwise(q, k, v, beta_b, g_b):
    r = q.shape[0]
    q = _l2norm(q) * (HEAD_DIM ** -0.5)
    k = _l2norm(k)
    g3 = g_b.reshape(r // CHUNK, CHUNK, LANES)
    g_last = jnp.broadcast_to(g3[:, CHUNK - 1:CHUNK, :], g3.shape).reshape(r, LANES)
    e_g = jnp.exp(g_b)
    q_dec = q * e_g
    k_dec = k * jnp.exp(g_last - g_b)
    y = jnp.concatenate([v * beta_b, k * (beta_b * e_g)], axis=1)
    return q, k, q_dec, k_dec, y, jnp.exp(g_last)


def _gdn_groups(probs, fillers=()):
    r = probs[0][0].shape[0]
    ri = lax.broadcasted_iota(jnp.int32, (r, r), 0)
    ci = lax.broadcasted_iota(jnp.int32, (r, r), 1)
    same = _chunk_of(ri) == _chunk_of(ci)
    causal = same & (ci <= ri)
    strict = same & (ci < ri)

    def widen(t):
        return jnp.concatenate([t] * (r // LANES), axis=1) if r >= LANES else t[:, :r]

    kks = [_dot_nt(k, k) for _, k, _, _, _, _ in probs]
    qks = [_dot_nt(q, k) for q, k, _, _, _, _ in probs]
    dmats = [jnp.where(causal, jnp.exp(jnp.where(causal, widen(g_b) - g_row, 0.0)), 0.0)
             for _, _, _, _, g_b, g_row in probs]
    xs = [jnp.where(strict, widen(p[3]) * kk * d, 0.0).astype(BF16)
          for p, kk, d in zip(probs, kks, dmats)]
    a_qks = [qk * d for qk, d in zip(qks, dmats)]
    zs = [p[2].astype(F32) - _dot(x, p[2]) for p, x in zip(probs, xs)]
    fillers = list(fillers)
    span = 2
    while span < CHUNK:
        xs = [_dot(x, x).astype(BF16) for x in xs]
        zs = [z + _dot(x, z.astype(BF16)) for x, z in zip(xs, zs)]
        if fillers:
            fillers.pop(0)()
        span *= 2
    for thunk in fillers:
        thunk()
    return list(zip(zs, a_qks))


def _gdn_state0_kernel(km_ref, vm_ref, wk_ref, wv_ref, colm_ref, growm_ref, s_ref, pad_ref):
    h = pl.program_id(0)
    pad_ref[0:SUBLANES, :] = jnp.zeros((SUBLANES, LANES), F32)
    beta_m = _lane_bcast(colm_ref[...], BETA_LANE + h)
    g_m = _lane_bcast(colm_ref[...], DECAY_LANE + h)
    conv = []
    for src, w_ref in ((km_ref, wk_ref), (vm_ref, wv_ref)):
        pad_ref[SUBLANES:SUBLANES + META_ROWS, :] = src[...].astype(F32)
        conv.append(_conv_silu(
            lambda j: pad_ref[SUBLANES - (CONV_WIDTH - 1) + j:
                              SUBLANES - (CONV_WIDTH - 1) + j + META_ROWS, :], w_ref[...]))
    _, k_m, _, kd_m, y_m, _ = _gdn_pointwise(conv[0], conv[0], conv[1], beta_m, g_m)
    k_bf = k_m.astype(BF16)
    (uw_m, _), = _gdn_groups([(k_bf, k_bf, y_m.astype(BF16), beta_m, g_m, growm_ref[...])])
    s_ref[...] = _dot(kd_m.T.astype(BF16), uw_m[:, :HEAD_DIM].astype(BF16))


def _gdn_state0(proj_m, conv_wt, col_m, row_m4):
    hb = GDN_HEADS

    def meta_block(off):
        return pl.BlockSpec((META_ROWS, HEAD_DIM), lambda j, off=off: (0, off + j))

    def conv_block(off):
        return pl.BlockSpec((CONV_WIDTH, HEAD_DIM), lambda j, off=off: (0, off + j))

    return pl.pallas_call(
        _gdn_state0_kernel,
        out_shape=jax.ShapeDtypeStruct((GDN_HEADS, HEAD_DIM, HEAD_DIM), F32),
        grid=(GDN_HEADS,),
        in_specs=[meta_block(hb), meta_block(2 * hb), conv_block(hb), conv_block(2 * hb),
                  pl.BlockSpec((None, META_ROWS, LANES), lambda j: (0, 0, 0)),
                  pl.BlockSpec((None, None, 1, META_ROWS), lambda j: (0, DECAY_LANE + j, 0, 0))],
        out_specs=pl.BlockSpec((None, HEAD_DIM, HEAD_DIM), lambda j: (j, 0, 0)),
        scratch_shapes=[pltpu.VMEM((META_ROWS + SUBLANES, HEAD_DIM), F32)],
        compiler_params=pltpu.CompilerParams(
            dimension_semantics=("parallel",), vmem_limit_bytes=VMEM_LIMIT),
        name="gdn_state0",
    )(proj_m, proj_m, conv_wt, conv_wt, col_m, row_m4)


def _gdn_kernel(*refs, seq):
    n_qkv = 3 * HEAD_PAIR
    src_refs = [refs[t * HEAD_PAIR:(t + 1) * HEAD_PAIR] for t in range(3)]
    z_ref = refs[n_qkv]
    meta_refs = [refs[n_qkv + 1 + t * HEAD_PAIR:n_qkv + 1 + (t + 1) * HEAD_PAIR] for t in range(3)]
    tap_refs = [refs[2 * n_qkv + 1 + t * HEAD_PAIR:2 * n_qkv + 1 + (t + 1) * HEAD_PAIR]
                for t in range(3)]
    (col_ref, grow_ref, s0_ref, nw_ref, o_ref, pad_ref, qs_ref, ks_ref, y_ref, bb_ref, gb_ref,
     qd_ref, kdt_ref, dec_ref, u_ref, w_ref, aqk_ref, st_ref) = refs[3 * n_qkv + 1:]
    pair = pl.program_id(1)
    slot = pair % REC_PAIRS
    n_blocks = seq // ROW_BLOCK
    n_groups = seq // GROUP
    cpg = GROUP // CHUNK
    hist = CONV_WIDTH - 1

    def pointwise_head(i, bank, hh):
        r0 = pl.multiple_of(i * ROW_BLOCK, ROW_BLOCK)
        rows = pl.ds(r0, ROW_BLOCK)
        convs = []
        for t in range(3):
            win = pad_ref.at[(bank * 3 + t) * HEAD_PAIR + hh]
            if isinstance(i, int) and i == 0:
                past = meta_refs[t][hh][META_ROWS - HIST_ROWS:META_ROWS, :]
            else:
                past = src_refs[t][hh][pl.ds(pl.multiple_of(r0 - HIST_ROWS, HIST_ROWS), HIST_ROWS), :]
            win[0:HIST_ROWS, :] = past.astype(F32)
            win[HIST_ROWS:HIST_ROWS + ROW_BLOCK, :] = src_refs[t][hh][rows, :].astype(F32)
            convs.append(_conv_silu(
                lambda j, win=win: win[HIST_ROWS - hist + j:HIST_ROWS - hist + j + ROW_BLOCK, :],
                tap_refs[t][hh][...]))
        hs = slot * HEAD_PAIR + hh
        col = col_ref[rows, :]
        beta_b = _lane_bcast(col, BETA_LANE + pair * HEAD_PAIR + hh)
        g_b = _lane_bcast(col, DECAY_LANE + pair * HEAD_PAIR + hh)
        q, k, q_dec, k_dec, y, dec = _gdn_pointwise(convs[0], convs[1], convs[2], beta_b, g_b)
        qs_ref[hh, rows, :] = q.astype(BF16)
        ks_ref[hh, rows, :] = k.astype(BF16)
        y_ref[hh, rows, :] = y.astype(BF16)
        bb_ref[hh, rows, :] = beta_b
        gb_ref[hh, rows, :] = g_b
        qd_ref[hs, rows, :] = q_dec.astype(BF16)
        kdt_ref[hs, :, rows] = k_dec.T.astype(BF16)
        for c in range(ROW_BLOCK // CHUNK):
            dec_ref[hs, pl.ds(i * (ROW_BLOCK // CHUNK) + c, 1), :] = dec[c * CHUNK:c * CHUNK + 1, :]

    def pointwise_thunks(i, bank):
        return [functools.partial(pointwise_head, i, bank, hh) for hh in range(HEAD_PAIR)]

    for i in range(PREP_UNROLL):
        for thunk in pointwise_thunks(i, i):
            thunk()

    def prep_groups(gi, fillers):
        keys = [(hh, pl.ds(pl.multiple_of((gi * PREP_UNROLL + u) * GROUP, GROUP), GROUP))
                for u in range(PREP_UNROLL) for hh in range(HEAD_PAIR)]
        probs = [(qs_ref[hh, rows, :], ks_ref[hh, rows, :], y_ref[hh, rows, :],
                  bb_ref[hh, rows, :], gb_ref[hh, rows, :], grow_ref[hh, :, rows])
                 for hh, rows in keys]
        for (hh, rows), (uw, a_qk) in zip(keys, _gdn_groups(probs, fillers)):
            hs = slot * HEAD_PAIR + hh
            u_ref[hs, rows, :] = uw[:, :HEAD_DIM]
            w_ref[hs, rows, :] = uw[:, HEAD_DIM:].astype(BF16)
            aqk_ref[hs, rows, :] = a_qk.astype(BF16)

    def prep_body(gi, _):
        prep_groups(gi, [th for u in range(PREP_UNROLL)
                         for th in pointwise_thunks((gi + 1) * PREP_UNROLL + u, u)])
        return 0
    n_trips = n_groups // PREP_UNROLL
    lax.fori_loop(0, n_trips - 1, prep_body, 0)
    prep_groups(n_trips - 1, [])

    @pl.when(slot == REC_PAIRS - 1)
    def _():
        st_ref[...] = s0_ref[...]

        def rec_body(gi, _):
            r0 = pl.multiple_of(gi * GROUP, GROUP)
            rows = pl.ds(r0, GROUP)
            heads = range(REC_HEADS)
            kd_t = [kdt_ref[h, :, rows] for h in heads]
            outs = [[] for _ in heads]
            for c in range(cpg):
                crow = pl.ds(r0 + c * CHUNK, CHUNK)
                states = [st_ref[h] for h in heads]
                s_bf = [s.astype(BF16) for s in states]
                ws = [_dot(jnp.concatenate([w_ref[h, crow, :], qd_ref[h, crow, :]], axis=0), s_bf[h])
                      for h in heads]
                v_new = [u_ref[h, crow, :] - ws[h][:CHUNK, :] for h in heads]
                zero = lambda n: jnp.zeros((n * CHUNK, HEAD_DIM), BF16)
                v_pad = [jnp.concatenate(([zero(c)] if c else []) + [v.astype(BF16)]
                                         + ([zero(cpg - 1 - c)] if c < cpg - 1 else []), axis=0)
                         for v in v_new]
                upd = [_dot(jnp.concatenate([aqk_ref[h, crow, :], kd_t[h]], axis=0), v_pad[h])
                       for h in heads]
                for h in heads:
                    dec = dec_ref[h, pl.ds(gi * cpg + c, 1), :]
                    st_ref[h] = states[h] * dec + upd[h][CHUNK:, :]
                    outs[h].append(ws[h][CHUNK:, :] + upd[h][:CHUNK, :])
            o = jnp.concatenate([_rms(jnp.concatenate(outs[h], axis=0), nw_ref[...]) for h in heads],
                                axis=1)
            o_ref[rows, :] = (o * _silu(z_ref[rows, :].astype(F32))).astype(BF16)
            return 0
        lax.fori_loop(0, n_groups, rec_body, 0)


def _gdn(proj, proj_m, conv_wt, col, row4, state0, norm_w):
    b, seq, _ = proj.shape
    n_pairs = GDN_HEADS // HEAD_PAIR
    rec_width = REC_HEADS * HEAD_DIM

    def head_specs(shape, imap):
        return [pl.BlockSpec(shape, functools.partial(imap, t * GDN_HEADS + hh))
                for t in range(3) for hh in range(HEAD_PAIR)]

    src_specs = head_specs((None, seq, HEAD_DIM), lambda off, i, j: (i, 0, off + HEAD_PAIR * j))
    meta_specs = head_specs((META_ROWS, HEAD_DIM), lambda off, i, j: (0, off + HEAD_PAIR * j))
    tap_specs = head_specs((CONV_WIDTH, HEAD_DIM), lambda off, i, j: (0, off + HEAD_PAIR * j))
    n_qkv = 3 * HEAD_PAIR
    z_block0 = 3 * GDN_WIDTH // rec_width

    per_pair = lambda width, dt: pltpu.VMEM((HEAD_PAIR, seq, width), dt)
    per_rec = lambda width, dt: pltpu.VMEM((REC_HEADS, seq, width), dt)
    return pl.pallas_call(
        functools.partial(_gdn_kernel, seq=seq),
        out_shape=jax.ShapeDtypeStruct((b, seq, GDN_WIDTH), BF16),
        grid=(b, n_pairs),
        in_specs=src_specs
        + [pl.BlockSpec((None, seq, rec_width), lambda i, j: (i, 0, z_block0 + j // REC_PAIRS))]
        + meta_specs + tap_specs
        + [pl.BlockSpec((None, seq, LANES), lambda i, j: (i, 0, 0)),
           pl.BlockSpec((None, HEAD_PAIR, 1, seq),
                        lambda i, j: (i, DECAY_LANE // HEAD_PAIR + j, 0, 0)),
           pl.BlockSpec((REC_HEADS, HEAD_DIM, HEAD_DIM), lambda i, j: (j // REC_PAIRS, 0, 0)),
           pl.BlockSpec((1, HEAD_DIM), lambda i, j: (0, 0))],
        out_specs=pl.BlockSpec((None, seq, rec_width), lambda i, j: (i, 0, j // REC_PAIRS)),
        scratch_shapes=[pltpu.VMEM((PREP_UNROLL * n_qkv, HIST_ROWS + ROW_BLOCK, HEAD_DIM), F32),
                        per_pair(HEAD_DIM, BF16), per_pair(HEAD_DIM, BF16),
                        per_pair(2 * HEAD_DIM, BF16),
                        per_pair(HEAD_DIM, F32), per_pair(HEAD_DIM, F32),
                        per_rec(HEAD_DIM, BF16),
                        pltpu.VMEM((REC_HEADS, HEAD_DIM, seq), BF16),
                        pltpu.VMEM((REC_HEADS, seq // CHUNK, HEAD_DIM), F32),
                        per_rec(HEAD_DIM, F32), per_rec(HEAD_DIM, BF16), per_rec(GROUP, BF16),
                        pltpu.VMEM((REC_HEADS, HEAD_DIM, HEAD_DIM), F32)],
        compiler_params=pltpu.CompilerParams(
            dimension_semantics=("parallel", "arbitrary"), vmem_limit_bytes=VMEM_LIMIT),
        name="gdn",
    )(*([proj] * n_qkv), proj, *([proj_m] * n_qkv), *([conv_wt] * n_qkv),
      col, row4, state0, norm_w)


def _fox_kernel(q_ref, k_ref, v_ref, g_ref, km_ref, vm_ref, col_ref, colm_ref, crow_ref,
                qw_ref, kw_ref, o_ref, qt_ref, ka_ref, kam_ref, vt_ref, vtm_ref,
                m_ref, l_ref, acc_ref, *, seq):
    h = pl.program_id(1)
    n_blocks = seq // ROW_BLOCK
    aug_r = lax.broadcasted_iota(jnp.int32, (LANES, ROW_BLOCK), 0)

    def key_aug(ck, valid=None):
        hi, mid, lo = _split3(ck)
        lane = lax.broadcasted_iota(jnp.int32, ck.shape, 1)
        neg_hi = -hi.astype(F32)
        if valid is not None:
            neg_hi = jnp.where(valid, neg_hi, NEG_BIG)
        blk = jnp.where(lane < 3, 1.0,
                        jnp.where(lane == 3, neg_hi,
                                  jnp.where(lane == 4, -mid.astype(F32),
                                            jnp.where(lane == 5, -lo.astype(F32), 0.0))))
        return blk.astype(BF16)

    def pro_body(i, _):
        r0 = pl.multiple_of(i * ROW_BLOCK, ROW_BLOCK)
        rows = pl.ds(r0, ROW_BLOCK)
        qn = _rms(q_ref[rows, :].astype(F32), qw_ref[...]) * (HEAD_DIM ** -0.5 * LOG2E)
        qt_ref[0:HEAD_DIM, rows] = qn.T.astype(BF16)
        hi, mid, lo = _split3(crow_ref[:, rows] * LOG2E)
        aug = jnp.where(aug_r == 0, hi.astype(F32),
                        jnp.where(aug_r == 1, mid.astype(F32),
                                  jnp.where(aug_r == 2, lo.astype(F32),
                                            jnp.where(aug_r < 6, 1.0, 0.0))))
        qt_ref[HEAD_DIM:2 * HEAD_DIM, rows] = aug.astype(BF16)
        ka_ref[rows, 0:HEAD_DIM] = _rms(k_ref[rows, :].astype(F32), kw_ref[...]).astype(BF16)
        ck = _lane_bcast(col_ref[rows, :], FORGET_LANE + h) * LOG2E
        ka_ref[rows, HEAD_DIM:2 * HEAD_DIM] = key_aug(ck)
        vt_ref[:, rows] = v_ref[rows, :].astype(F32).T.astype(BF16)
        return 0
    lax.fori_loop(0, n_blocks, pro_body, 0)

    kam_ref[:, 0:HEAD_DIM] = _rms(km_ref[...].astype(F32), kw_ref[...]).astype(BF16)
    ck_m = _lane_bcast(colm_ref[...], FORGET_LANE + h) * LOG2E
    mrow = lax.broadcasted_iota(jnp.int32, (META_ROWS, LANES), 0)
    kam_ref[:, HEAD_DIM:2 * HEAD_DIM] = key_aug(ck_m, mrow >= META_PAD)
    vtm_ref[...] = vm_ref[...].astype(F32).T.astype(BF16)

    kidx = lax.broadcasted_iota(jnp.int32, (FOX_TK, FOX_TK), 0)
    qidx = lax.broadcasted_iota(jnp.int32, (FOX_TK, FOX_TK), 1)
    diag_ok = kidx <= qidx

    items = []
    for kj in range(-1, seq // FOX_TK):
        lane0 = max(kj, 0) * FOX_TK
        for p0 in range(lane0, seq, FOX_PIECE):
            items.append((kj, slice(p0, min(p0 + FOX_PIECE, seq)), p0 == lane0))

    def scores(item):
        kj, ln, leads = item
        k_aug = kam_ref[...] if kj < 0 else ka_ref[kj * FOX_TK:(kj + 1) * FOX_TK, :]
        s = _dot(k_aug, qt_ref[:, ln])
        if kj >= 0 and leads:
            masked = jnp.where(diag_ok, s[:, :FOX_TK], NEG_BIG)
            s = jnp.concatenate([masked, s[:, FOX_TK:]], axis=1) if s.shape[1] > FOX_TK else masked
        return s

    def softmax_stats(item, s):
        kj, ln, _ = item
        top = jnp.max(s, axis=0, keepdims=True)
        if kj < 0:
            m_new, alpha = top, None
        else:
            m_old = m_ref[:, ln]
            m_new = jnp.maximum(m_old, top)
            alpha = jnp.exp2(m_old - m_new)
        p = jnp.exp2(s - m_new)
        psum = jnp.sum(p, axis=0, keepdims=True)
        m_ref[:, ln] = m_new
        l_ref[:, ln] = psum if kj < 0 else alpha * l_ref[:, ln] + psum
        return p.astype(BF16), alpha

    def values(item, p, alpha):
        kj, ln, _ = item
        v_t = vtm_ref[...] if kj < 0 else vt_ref[:, kj * FOX_TK:(kj + 1) * FOX_TK]
        pv = _dot(v_t, p)
        acc_ref[:, ln] = pv if kj < 0 else alpha * acc_ref[:, ln] + pv

    s_cur = scores(items[0])
    p_cur = None
    for t in range(len(items) + 1):
        s_next = scores(items[t + 1]) if t + 1 < len(items) else None
        p_next = softmax_stats(items[t], s_cur) if t < len(items) else None
        if p_cur is not None:
            values(items[t - 1], *p_cur)
        s_cur, p_cur = s_next, p_next

    def out_body(i, _):
        r0 = pl.multiple_of(i * ROW_BLOCK, ROW_BLOCK)
        rows = pl.ds(r0, ROW_BLOCK)
        out_t = acc_ref[:, rows] * (1.0 / l_ref[:, rows])
        o_ref[rows, :] = (out_t.T * _silu(g_ref[rows, :].astype(F32))).astype(BF16)
        return 0
    lax.fori_loop(0, n_blocks, out_body, 0)


def _fox(proj, proj_m, col, col_m, row4, q_w, k_w):
    b, seq, _ = proj.shape
    base = 4 * GDN_HEADS
    hb = FOX_HEADS

    def head_block(off):
        return pl.BlockSpec((None, seq, HEAD_DIM), lambda i, j, off=off: (i, 0, off + j))

    def meta_block(off):
        return pl.BlockSpec((META_ROWS, HEAD_DIM), lambda i, j, off=off: (0, off + j))

    return pl.pallas_call(
        functools.partial(_fox_kernel, seq=seq),
        out_shape=jax.ShapeDtypeStruct((b, seq, FOX_WIDTH), BF16),
        grid=(b, FOX_HEADS),
        in_specs=[head_block(base), head_block(base + hb), head_block(base + 2 * hb),
                  head_block(base + 3 * hb),
                  meta_block(base + hb), meta_block(base + 2 * hb),
                  pl.BlockSpec((None, seq, LANES), lambda i, j: (i, 0, 0)),
                  pl.BlockSpec((None, META_ROWS, LANES), lambda i, j: (0, 0, 0)),
                  pl.BlockSpec((None, None, 1, seq), lambda i, j: (i, FORGET_LANE + j, 0, 0)),
                  pl.BlockSpec((1, HEAD_DIM), lambda i, j: (0, 0)),
                  pl.BlockSpec((1, HEAD_DIM), lambda i, j: (0, 0))],
        out_specs=pl.BlockSpec((None, seq, HEAD_DIM), lambda i, j: (i, 0, j)),
        scratch_shapes=[pltpu.VMEM((2 * HEAD_DIM, seq), BF16),
                        pltpu.VMEM((seq, 2 * HEAD_DIM), BF16),
                        pltpu.VMEM((META_ROWS, 2 * HEAD_DIM), BF16),
                        pltpu.VMEM((HEAD_DIM, seq), BF16),
                        pltpu.VMEM((HEAD_DIM, META_ROWS), BF16),
                        pltpu.VMEM((1, seq), F32), pltpu.VMEM((1, seq), F32),
                        pltpu.VMEM((HEAD_DIM, seq), F32)],
        compiler_params=pltpu.CompilerParams(
            dimension_semantics=("parallel", "arbitrary"), vmem_limit_bytes=VMEM_LIMIT),
        name="fox",
    )(proj, proj, proj, proj, proj_m, proj_m, col, col_m, row4, q_w, k_w)


def _out_proj_kernel(mg_ref, mf_ref, wg_ref, wf_ref, pw_ref, x_ref, o_ref):
    out = _dot(mg_ref[...], wg_ref[...]) + _dot(mf_ref[...], wf_ref[...])
    o_ref[...] = x_ref[...] + _rms(out, pw_ref[...])


def _out_proj(mg, mf, w_g, w_f, post_w, x2d, tm):
    m, d = x2d.shape
    return pl.pallas_call(
        _out_proj_kernel,
        out_shape=jax.ShapeDtypeStruct((m, d), F32),
        grid=(m // tm,),
        in_specs=[pl.BlockSpec((tm, GDN_WIDTH), lambda i: (i, 0)),
                  pl.BlockSpec((tm, FOX_WIDTH), lambda i: (i, 0)),
                  pl.BlockSpec((GDN_WIDTH, d), lambda i: (0, 0)),
                  pl.BlockSpec((FOX_WIDTH, d), lambda i: (0, 0)),
                  pl.BlockSpec((1, d), lambda i: (0, 0)),
                  pl.BlockSpec((tm, d), lambda i: (i, 0))],
        out_specs=pl.BlockSpec((tm, d), lambda i: (i, 0)),
        compiler_params=pltpu.CompilerParams(
            dimension_semantics=("parallel",), vmem_limit_bytes=VMEM_LIMIT),
        name="out_proj",
    )(mg, mf, w_g, w_f, post_w, x2d)


def _tile(total, want):
    t = min(total, want)
    while total % t:
        t //= 2
    return t


def _layer(x, meta_pad, pre_w, w_in, conv_w, a_log, dt_bias, gdn_norm_w, fox_q_w, fox_k_w,
           fox_f_bias, w_out, post_w):
    b, seq, d = x.shape
    assert seq % (GROUP * PREP_UNROLL) == 0 and seq % FOX_TK == 0 and seq % ROW_BLOCK == 0
    gw, fw = GDN_WIDTH, FOX_WIDTH
    o_gb = 4 * gw
    o_f = o_gb + 2 * GDN_HEADS
    o_ff = o_f + 4 * fw
    w_cat = jnp.concatenate(
        [w_in[:, :o_gb], w_in[:, o_f:o_ff], w_in[:, o_gb:o_f], w_in[:, o_ff:],
         jnp.zeros((d, GATE_WIDTH - 2 * GDN_HEADS - FOX_HEADS), w_in.dtype)], axis=1).astype(BF16)
    zpad = jnp.zeros((GATE_WIDTH - FORGET_LANE - FOX_HEADS,), F32)
    add_vec = jnp.concatenate([jnp.zeros((GDN_HEADS,), F32), dt_bias, fox_f_bias, zpad])[None]
    alog_vec = jnp.concatenate([jnp.zeros((GDN_HEADS,), F32), a_log,
                                jnp.zeros((FOX_HEADS,), F32), zpad])[None]

    x2d = x.reshape(b * seq, d)
    pre_w2 = pre_w[None]
    proj, gate = _in_proj(x2d, pre_w2, w_cat, _tile(b * seq, IN_PROJ_TM), IN_PROJ_TN)
    proj_m, gate_m = _in_proj(meta_pad, pre_w2, w_cat, META_ROWS, IN_PROJ_TN)
    proj = proj.reshape(b, seq, MAIN_WIDTH)

    col, row = _gate_prep(gate.reshape(b, seq, GATE_WIDTH), add_vec, alog_vec, False)
    col_m, row_m = _gate_prep(gate_m[None], add_vec, alog_vec, True)
    row4 = row.reshape(b, GATE_ROWS, 1, seq)
    row_m4 = row_m.reshape(1, GATE_ROWS, 1, META_ROWS)

    conv_wt = conv_w.T
    state0 = _gdn_state0(proj_m, conv_wt, col_m, row_m4)
    o_gdn = _gdn(proj, proj_m, conv_wt, col, row4, state0, gdn_norm_w[None])
    o_fox = _fox(proj, proj_m, col, col_m, row4, fox_q_w[None], fox_k_w[None])

    w_out_b = w_out.astype(BF16)
    out = _out_proj(o_gdn.reshape(b * seq, gw), o_fox.reshape(b * seq, fw),
                    w_out_b[:gw], w_out_b[gw:], post_w[None], x2d, _tile(b * seq, OUT_PROJ_TM))
    return out.reshape(b, seq, d)


def kernel(x, meta_tokens, pre_norm_w, w_in, conv_w, a_log, dt_bias, gdn_norm_w, fox_q_norm_w,
           fox_k_norm_w, fox_f_bias, w_out, post_norm_w):
    assert pre_norm_w.shape[0] == 1, "single-layer stack"
    meta_pad = jnp.concatenate(
        [jnp.zeros((META_PAD, x.shape[-1]), x.dtype), meta_tokens.astype(x.dtype)], axis=0)
    return _layer(x, meta_pad, pre_norm_w[0], w_in[0], conv_w[0], a_log[0], dt_bias[0],
                  gdn_norm_w[0], fox_q_norm_w[0], fox_k_norm_w[0], fox_f_bias[0], w_out[0],
                  post_norm_w[0])
```

```python
import functools
import math

import jax
import jax.numpy as jnp
from jax import lax
from jax.experimental import pallas as pl
from jax.experimental.pallas import tpu as pltpu

N_META = 16
HEAD_DIM = 128
GDN_HEADS = 8
FOX_HEADS = 8
GDN_WIDTH = GDN_HEADS * HEAD_DIM
FOX_WIDTH = FOX_HEADS * HEAD_DIM
CONV_WIDTH = 4
CHUNK = 64
EPS = 1e-6

LANES = 128
SUBLANES = 8
MXU_DIM = 256
MAIN_WIDTH = 4 * GDN_WIDTH + 4 * FOX_WIDTH
GATE_WIDTH = LANES
BETA_LANE, DECAY_LANE, FORGET_LANE = 0, GDN_HEADS, 2 * GDN_HEADS
GATE_ROWS = 32
META_ROWS = CHUNK
META_PAD = META_ROWS - N_META
GROUP = MXU_DIM
ROW_BLOCK = 256
HIST_ROWS = 2 * SUBLANES
IN_PROJ_TM, IN_PROJ_TN = 1024, 2048
OUT_PROJ_TM = 512
HEAD_PAIR = 2
REC_PAIRS = 2
REC_HEADS = REC_PAIRS * HEAD_PAIR
PREP_UNROLL = 2
FOX_TK = MXU_DIM
FOX_PIECE = 2 * MXU_DIM
VMEM_LIMIT = 56 * 1024 * 1024

F32 = jnp.float32
BF16 = jnp.bfloat16
NEG_BIG = -1e30
LOG2E = math.log2(math.e)


def _dot(a, b):
    return jnp.dot(a, b, preferred_element_type=F32)


def _dot_nt(a, b):
    return lax.dot_general(a, b, (((1,), (1,)), ((), ())), preferred_element_type=F32)


def _split3(x):
    hi = x.astype(BF16)
    r1 = x - hi.astype(F32)
    mid = r1.astype(BF16)
    lo = (r1 - mid.astype(F32)).astype(BF16)
    return hi, mid, lo


def _dot_exact_rhs01(parts, m):
    return _dot(parts[0], m) + _dot(parts[1], m) + _dot(parts[2], m)


def _dot_exact_lhs01(m, parts):
    return _dot(m, parts[0]) + _dot(m, parts[1]) + _dot(m, parts[2])


def _lane_bcast(col_tile, lane):
    sel = (lax.broadcasted_iota(jnp.int32, (LANES, LANES), 0) == lane).astype(BF16)
    return _dot_exact_rhs01(_split3(col_tile), sel)


def _chunk_of(idx):
    return jnp.right_shift(idx, CHUNK.bit_length() - 1)


def _rms(x, w):
    return x * lax.rsqrt(jnp.mean(x * x, axis=-1, keepdims=True) + EPS) * w


def _silu(x):
    return x * (1.0 / (1.0 + jnp.exp(-x)))


def _softplus(x):
    return jnp.maximum(x, 0.0) + jnp.log1p(jnp.exp(-jnp.abs(x)))


def _in_proj_kernel(x_ref, nw_ref, w_ref, wg_ref, o_ref, og_ref, xn_ref):
    @pl.when(pl.program_id(1) == 0)
    def _():
        xn = _rms(x_ref[...], nw_ref[...]).astype(BF16)
        xn_ref[...] = xn
        og_ref[...] = _dot_nt(xn, wg_ref[...])

    o_ref[...] = _dot_nt(xn_ref[...], w_ref[...]).astype(o_ref.dtype)


def _in_proj(x2d, norm_w, w_cat, tm, tn):
    m, d = x2d.shape
    n = MAIN_WIDTH
    return pl.pallas_call(
        _in_proj_kernel,
        out_shape=(jax.ShapeDtypeStruct((m, n), BF16),
                   jax.ShapeDtypeStruct((m, GATE_WIDTH), F32)),
        grid=(m // tm, n // tn),
        in_specs=[pl.BlockSpec((tm, d), lambda i, j: (i, 0)),
                  pl.BlockSpec((1, d), lambda i, j: (0, 0)),
                  pl.BlockSpec((tn, d), lambda i, j: (j, 0)),
                  pl.BlockSpec((GATE_WIDTH, d), lambda i, j: (MAIN_WIDTH // GATE_WIDTH, 0))],
        out_specs=(pl.BlockSpec((tm, tn), lambda i, j: (i, j)),
                   pl.BlockSpec((tm, GATE_WIDTH), lambda i, j: (i, 0))),
        scratch_shapes=[pltpu.VMEM((tm, d), BF16)],
        compiler_params=pltpu.CompilerParams(
            dimension_semantics=("parallel", "arbitrary"), vmem_limit_bytes=VMEM_LIMIT),
        name="in_proj",
    )(x2d, norm_w, w_cat, w_cat)


def _gate_kernel(t_ref, add_ref, alog_ref, col_ref, row_ref, *, rows, is_meta):
    blk = min(rows, ROW_BLOCK)
    lane = lax.broadcasted_iota(jnp.int32, (blk, LANES), 1)
    ri = lax.broadcasted_iota(jnp.int32, (blk, blk), 0)
    ci = lax.broadcasted_iota(jnp.int32, (blk, blk), 1)
    tri = (ci <= ri).astype(BF16)
    tri_chunk = ((ci <= ri) & (_chunk_of(ri) == _chunk_of(ci))).astype(BF16)
    is_beta = lane < DECAY_LANE
    is_decay = (lane >= DECAY_LANE) & (lane < FORGET_LANE)
    is_forget = (lane >= FORGET_LANE) & (lane < FORGET_LANE + FOX_HEADS)

    carry = jnp.zeros((1, LANES), F32)
    for r in range(rows // blk):
        t = t_ref[r * blk:(r + 1) * blk, :]
        ta = t + add_ref[...]
        beta = 1.0 / (1.0 + jnp.exp(-t))
        g = -jnp.exp(alog_ref[...]) * _softplus(ta)
        logf = -_softplus(-ta)
        val = jnp.where(is_decay, g, jnp.where(is_forget, logf, 0.0))
        if is_meta:
            row = lax.broadcasted_iota(jnp.int32, (blk, LANES), 0)
            val = jnp.where(row >= META_PAD, val, 0.0)
            beta = jnp.where(row >= META_PAD, beta, 0.0)
        parts = _split3(val)
        cum_chunk = _dot_exact_lhs01(tri_chunk, parts)
        cum_all = _dot_exact_lhs01(tri, parts) + carry
        carry = cum_all[blk - 1:blk, :]
        if is_meta:
            cum_all = cum_all - carry
        res = jnp.where(is_beta, beta, jnp.where(is_decay, cum_chunk, cum_all))
        col_ref[r * blk:(r + 1) * blk, :] = res
        row_ref[:, r * blk:(r + 1) * blk] = res.T[:GATE_ROWS, :]


def _gate_prep(gate3d, add_vec, alog_vec, is_meta):
    b, rows, _ = gate3d.shape
    kern = functools.partial(_gate_kernel, rows=rows, is_meta=is_meta)
    return pl.pallas_call(
        kern,
        out_shape=(jax.ShapeDtypeStruct((b, rows, LANES), F32),
                   jax.ShapeDtypeStruct((b, GATE_ROWS, rows), F32)),
        grid=(b,),
        in_specs=[pl.BlockSpec((None, rows, LANES), lambda i: (i, 0, 0)),
                  pl.BlockSpec((1, LANES), lambda i: (0, 0)),
                  pl.BlockSpec((1, LANES), lambda i: (0, 0))],
        out_specs=(pl.BlockSpec((None, rows, LANES), lambda i: (i, 0, 0)),
                   pl.BlockSpec((None, GATE_ROWS, rows), lambda i: (i, 0, 0))),
        compiler_params=pltpu.CompilerParams(
            dimension_semantics=("parallel",), vmem_limit_bytes=VMEM_LIMIT),
        name="gate_prep_meta" if is_meta else "gate_prep",
    )(gate3d, add_vec, alog_vec)


def _conv_silu(load_rows, w):
    y = load_rows(0) * w[0:1, :]
    for j in range(1, CONV_WIDTH):
        y = y + load_rows(j) * w[j:j + 1, :]
    return _silu(y)


def _l2norm(x):
    return x * lax.rsqrt(jnp.sum(x * x, axis=-1, keepdims=True) + EPS)


def _gdn_pointwise(q, k, v, beta_b, g_b):
    r = q.shape[0]
    q = _l2norm(q) * (HEAD_DIM ** -0.5)
    k = _l2norm(k)
    g3 = g_b.reshape(r // CHUNK, CHUNK, LANES)
    g_last = jnp.broadcast_to(g3[:, CHUNK - 1:CHUNK, :], g3.shape).reshape(r, LANES)
    e_g = jnp.exp(g_b)
    q_dec = q * e_g
    k_dec = k * jnp.exp(g_last - g_b)
    y = jnp.concatenate([v * beta_b, k * (beta_b * e_g)], axis=1)
    return q, k, q_dec, k_dec, y, jnp.exp(g_last)


def _gdn_groups(probs, fillers=()):
    r = probs[0][0].shape[0]
    ri = lax.broadcasted_iota(jnp.int32, (r, r), 0)
    ci = lax.broadcasted_iota(jnp.int32, (r, r), 1)
    same = _chunk_of(ri) == _chunk_of(ci)
    causal = same & (ci <= ri)
    strict = same & (ci < ri)

    def widen(t):
        return jnp.concatenate([t] * (r // LANES), axis=1) if r >= LANES else t[:, :r]

    kks = [_dot_nt(k, k) for _, k, _, _, _, _ in probs]
    qks = [_dot_nt(q, k) for q, k, _, _, _, _ in probs]
    dmats = [jnp.where(causal, jnp.exp(jnp.where(causal, widen(g_b) - g_row, 0.0)), 0.0)
             for _, _, _, _, g_b, g_row in probs]
    xs = [jnp.where(strict, widen(p[3]) * kk * d, 0.0).astype(BF16)
          for p, kk, d in zip(probs, kks, dmats)]
    a_qks = [qk * d for qk, d in zip(qks, dmats)]
    zs = [p[2].astype(F32) - _dot(x, p[2]) for p, x in zip(probs, xs)]
    fillers = list(fillers)
    span = 2
    while span < CHUNK:
        xs = [_dot(x, x).astype(BF16) for x in xs]
        zs = [z + _dot(x, z.astype(BF16)) for x, z in zip(xs, zs)]
        if fillers:
            fillers.pop(0)()
        span *= 2
    for thunk in fillers:
        thunk()
    return list(zip(zs, a_qks))


def _gdn_state0_kernel(km_ref, vm_ref, wk_ref, wv_ref, colm_ref, growm_ref, s_ref, pad_ref):
    h = pl.program_id(0)
    pad_ref[0:SUBLANES, :] = jnp.zeros((SUBLANES, LANES), F32)
    beta_m = _lane_bcast(colm_ref[...], BETA_LANE + h)
    g_m = _lane_bcast(colm_ref[...], DECAY_LANE + h)
    conv = []
    for src, w_ref in ((km_ref, wk_ref), (vm_ref, wv_ref)):
        pad_ref[SUBLANES:SUBLANES + META_ROWS, :] = src[...].astype(F32)
        conv.append(_conv_silu(
            lambda j: pad_ref[SUBLANES - (CONV_WIDTH - 1) + j:
                              SUBLANES - (CONV_WIDTH - 1) + j + META_ROWS, :], w_ref[...]))
    _, k_m, _, kd_m, y_m, _ = _gdn_pointwise(conv[0], conv[0], conv[1], beta_m, g_m)
    k_bf = k_m.astype(BF16)
    (uw_m, _), = _gdn_groups([(k_bf, k_bf, y_m.astype(BF16), beta_m, g_m, growm_ref[...])])
    s_ref[...] = _dot(kd_m.T.astype(BF16), uw_m[:, :HEAD_DIM].astype(BF16))


def _gdn_state0(proj_m, conv_wt, col_m, row_m4):
    hb = GDN_HEADS

    def meta_block(off):
        return pl.BlockSpec((META_ROWS, HEAD_DIM), lambda j, off=off: (0, off + j))

    def conv_block(off):
        return pl.BlockSpec((CONV_WIDTH, HEAD_DIM), lambda j, off=off: (0, off + j))

    return pl.pallas_call(
        _gdn_state0_kernel,
        out_shape=jax.ShapeDtypeStruct((GDN_HEADS, HEAD_DIM, HEAD_DIM), F32),
        grid=(GDN_HEADS,),
        in_specs=[meta_block(hb), meta_block(2 * hb), conv_block(hb), conv_block(2 * hb),
                  pl.BlockSpec((None, META_ROWS, LANES), lambda j: (0, 0, 0)),
                  pl.BlockSpec((None, None, 1, META_ROWS), lambda j: (0, DECAY_LANE + j, 0, 0))],
        out_specs=pl.BlockSpec((None, HEAD_DIM, HEAD_DIM), lambda j: (j, 0, 0)),
        scratch_shapes=[pltpu.VMEM((META_ROWS + SUBLANES, HEAD_DIM), F32)],
        compiler_params=pltpu.CompilerParams(
            dimension_semantics=("parallel",), vmem_limit_bytes=VMEM_LIMIT),
        name="gdn_state0",
    )(proj_m, proj_m, conv_wt, conv_wt, col_m, row_m4)


def _gdn_kernel(*refs, seq):
    n_qkv = 3 * HEAD_PAIR
    src_refs = [refs[t * HEAD_PAIR:(t + 1) * HEAD_PAIR] for t in range(3)]
    z_ref = refs[n_qkv]
    meta_refs = [refs[n_qkv + 1 + t * HEAD_PAIR:n_qkv + 1 + (t + 1) * HEAD_PAIR] for t in range(3)]
    tap_refs = [refs[2 * n_qkv + 1 + t * HEAD_PAIR:2 * n_qkv + 1 + (t + 1) * HEAD_PAIR]
                for t in range(3)]
    (brow_ref, grow_ref, s0_ref, nw_ref, o_ref, pad_ref, qs_ref, ks_ref, y_ref, bb_ref, gb_ref,
     qd_ref, kdt_ref, dec_ref, u_ref, w_ref, aqk_ref, st_ref) = refs[3 * n_qkv + 1:]
    pair = pl.program_id(1)
    slot = pair % REC_PAIRS
    n_blocks = seq // ROW_BLOCK
    n_groups = seq // GROUP
    cpg = GROUP // CHUNK
    hist = CONV_WIDTH - 1

    def pointwise_head(i, bank, hh):
        r0 = pl.multiple_of(i * ROW_BLOCK, ROW_BLOCK)
        rows = pl.ds(r0, ROW_BLOCK)
        convs = []
        for t in range(3):
            win = pad_ref.at[(bank * 3 + t) * HEAD_PAIR + hh]
            if isinstance(i, int) and i == 0:
                past = meta_refs[t][hh][META_ROWS - HIST_ROWS:META_ROWS, :]
            else:
                past = src_refs[t][hh][pl.ds(pl.multiple_of(r0 - HIST_ROWS, HIST_ROWS), HIST_ROWS), :]
            win[0:HIST_ROWS, :] = past.astype(F32)
            win[HIST_ROWS:HIST_ROWS + ROW_BLOCK, :] = src_refs[t][hh][rows, :].astype(F32)
            convs.append(_conv_silu(
                lambda j, win=win: win[HIST_ROWS - hist + j:HIST_ROWS - hist + j + ROW_BLOCK, :],
                tap_refs[t][hh][...]))
        hs = slot * HEAD_PAIR + hh
        beta_b = jnp.broadcast_to(brow_ref[hh, :, rows], (LANES, ROW_BLOCK)).T
        g_b = jnp.broadcast_to(grow_ref[hh, :, rows], (LANES, ROW_BLOCK)).T
        q, k, q_dec, k_dec, y, dec = _gdn_pointwise(convs[0], convs[1], convs[2], beta_b, g_b)
        qs_ref[hh, rows, :] = q.astype(BF16)
        ks_ref[hh, rows, :] = k.astype(BF16)
        y_ref[hh, rows, :] = y.astype(BF16)
        bb_ref[hh, rows, :] = beta_b
        gb_ref[hh, rows, :] = g_b
        qd_ref[hs, rows, :] = q_dec.astype(BF16)
        kdt_ref[hs, :, rows] = k_dec.T.astype(BF16)
        for c in range(ROW_BLOCK // CHUNK):
            dec_ref[hs, pl.ds(i * (ROW_BLOCK // CHUNK) + c, 1), :] = dec[c * CHUNK:c * CHUNK + 1, :]

    def pointwise_thunks(i, bank):
        return [functools.partial(pointwise_head, i, bank, hh) for hh in range(HEAD_PAIR)]

    for i in range(PREP_UNROLL):
        for thunk in pointwise_thunks(i, i):
            thunk()

    def prep_groups(gi, fillers):
        keys = [(hh, pl.ds(pl.multiple_of((gi * PREP_UNROLL + u) * GROUP, GROUP), GROUP))
                for u in range(PREP_UNROLL) for hh in range(HEAD_PAIR)]
        probs = [(qs_ref[hh, rows, :], ks_ref[hh, rows, :], y_ref[hh, rows, :],
                  bb_ref[hh, rows, :], gb_ref[hh, rows, :], grow_ref[hh, :, rows])
                 for hh, rows in keys]
        for (hh, rows), (uw, a_qk) in zip(keys, _gdn_groups(probs, fillers)):
            hs = slot * HEAD_PAIR + hh
            u_ref[hs, rows, :] = uw[:, :HEAD_DIM]
            w_ref[hs, rows, :] = uw[:, HEAD_DIM:].astype(BF16)
            aqk_ref[hs, rows, :] = a_qk.astype(BF16)

    def prep_body(gi, _):
        prep_groups(gi, [th for u in range(PREP_UNROLL)
                         for th in pointwise_thunks((gi + 1) * PREP_UNROLL + u, u)])
        return 0
    n_trips = n_groups // PREP_UNROLL
    lax.fori_loop(0, n_trips - 1, prep_body, 0)
    prep_groups(n_trips - 1, [])

    @pl.when(slot == REC_PAIRS - 1)
    def _():
        st_ref[...] = s0_ref[...]

        def rec_body(gi, _):
            r0 = pl.multiple_of(gi * GROUP, GROUP)
            rows = pl.ds(r0, GROUP)
            heads = range(REC_HEADS)
            kd_t = [kdt_ref[h, :, rows] for h in heads]
            outs = [[] for _ in heads]
            for c in range(cpg):
                crow = pl.ds(r0 + c * CHUNK, CHUNK)
                states = [st_ref[h] for h in heads]
                s_bf = [s.astype(BF16) for s in states]
                ws = [_dot(jnp.concatenate([w_ref[h, crow, :], qd_ref[h, crow, :]], axis=0), s_bf[h])
                      for h in heads]
                v_new = [u_ref[h, crow, :] - ws[h][:CHUNK, :] for h in heads]
                zero = lambda n: jnp.zeros((n * CHUNK, HEAD_DIM), BF16)
                v_pad = [jnp.concatenate(([zero(c)] if c else []) + [v.astype(BF16)]
                                         + ([zero(cpg - 1 - c)] if c < cpg - 1 else []), axis=0)
                         for v in v_new]
                upd = [_dot(jnp.concatenate([aqk_ref[h, crow, :], kd_t[h]], axis=0), v_pad[h])
                       for h in heads]
                for h in heads:
                    dec = dec_ref[h, pl.ds(gi * cpg + c, 1), :]
                    st_ref[h] = states[h] * dec + upd[h][CHUNK:, :]
                    outs[h].append(ws[h][CHUNK:, :] + upd[h][:CHUNK, :])
            o = jnp.concatenate([_rms(jnp.concatenate(outs[h], axis=0), nw_ref[...]) for h in heads],
                                axis=1)
            o_ref[rows, :] = (o * _silu(z_ref[rows, :].astype(F32))).astype(BF16)
            return 0
        lax.fori_loop(0, n_groups, rec_body, 0)


def _gdn(proj, proj_m, conv_wt, row4, state0, norm_w):
    b, seq, _ = proj.shape
    n_pairs = GDN_HEADS // HEAD_PAIR
    rec_width = REC_HEADS * HEAD_DIM

    def head_specs(shape, imap):
        return [pl.BlockSpec(shape, functools.partial(imap, t * GDN_HEADS + hh))
                for t in range(3) for hh in range(HEAD_PAIR)]

    src_specs = head_specs((None, seq, HEAD_DIM), lambda off, i, j: (i, 0, off + HEAD_PAIR * j))
    meta_specs = head_specs((META_ROWS, HEAD_DIM), lambda off, i, j: (0, off + HEAD_PAIR * j))
    tap_specs = head_specs((CONV_WIDTH, HEAD_DIM), lambda off, i, j: (0, off + HEAD_PAIR * j))
    n_qkv = 3 * HEAD_PAIR
    z_block0 = 3 * GDN_WIDTH // rec_width

    per_pair = lambda width, dt: pltpu.VMEM((HEAD_PAIR, seq, width), dt)
    per_rec = lambda width, dt: pltpu.VMEM((REC_HEADS, seq, width), dt)
    return pl.pallas_call(
        functools.partial(_gdn_kernel, seq=seq),
        out_shape=jax.ShapeDtypeStruct((b, seq, GDN_WIDTH), BF16),
        grid=(b, n_pairs),
        in_specs=src_specs
        + [pl.BlockSpec((None, seq, rec_width), lambda i, j: (i, 0, z_block0 + j // REC_PAIRS))]
        + meta_specs + tap_specs
        + [pl.BlockSpec((None, HEAD_PAIR, 1, seq),
                        lambda i, j: (i, BETA_LANE // HEAD_PAIR + j, 0, 0)),
           pl.BlockSpec((None, HEAD_PAIR, 1, seq),
                        lambda i, j: (i, DECAY_LANE // HEAD_PAIR + j, 0, 0)),
           pl.BlockSpec((REC_HEADS, HEAD_DIM, HEAD_DIM), lambda i, j: (j // REC_PAIRS, 0, 0)),
           pl.BlockSpec((1, HEAD_DIM), lambda i, j: (0, 0))],
        out_specs=pl.BlockSpec((None, seq, rec_width), lambda i, j: (i, 0, j // REC_PAIRS)),
        scratch_shapes=[pltpu.VMEM((PREP_UNROLL * n_qkv, HIST_ROWS + ROW_BLOCK, HEAD_DIM), F32),
                        per_pair(HEAD_DIM, BF16), per_pair(HEAD_DIM, BF16),
                        per_pair(2 * HEAD_DIM, BF16),
                        per_pair(HEAD_DIM, F32), per_pair(HEAD_DIM, F32),
                        per_rec(HEAD_DIM, BF16),
                        pltpu.VMEM((REC_HEADS, HEAD_DIM, seq), BF16),
                        pltpu.VMEM((REC_HEADS, seq // CHUNK, HEAD_DIM), F32),
                        per_rec(HEAD_DIM, F32), per_rec(HEAD_DIM, BF16), per_rec(GROUP, BF16),
                        pltpu.VMEM((REC_HEADS, HEAD_DIM, HEAD_DIM), F32)],
        compiler_params=pltpu.CompilerParams(
            dimension_semantics=("parallel", "arbitrary"), vmem_limit_bytes=VMEM_LIMIT),
        name="gdn",
    )(*([proj] * n_qkv), proj, *([proj_m] * n_qkv), *([conv_wt] * n_qkv),
      row4, row4, state0, norm_w)


def _fox_kernel(q_ref, k_ref, v_ref, g_ref, km_ref, vm_ref, colm_ref, crow_ref,
                qw_ref, kw_ref, o_ref, qt_ref, ka_ref, kam_ref, vt_ref, vtm_ref,
                m_ref, l_ref, acc_ref, *, seq):
    h = pl.program_id(1)
    n_blocks = seq // ROW_BLOCK
    aug_r = lax.broadcasted_iota(jnp.int32, (LANES, ROW_BLOCK), 0)

    def key_aug(ck, valid=None):
        hi, mid, lo = _split3(ck)
        lane = lax.broadcasted_iota(jnp.int32, ck.shape, 1)
        neg_hi = -hi.astype(F32)
        if valid is not None:
            neg_hi = jnp.where(valid, neg_hi, NEG_BIG)
        blk = jnp.where(lane < 3, 1.0,
                        jnp.where(lane == 3, neg_hi,
                                  jnp.where(lane == 4, -mid.astype(F32),
                                            jnp.where(lane == 5, -lo.astype(F32), 0.0))))
        return blk.astype(BF16)

    def pro_body(i, _):
        r0 = pl.multiple_of(i * ROW_BLOCK, ROW_BLOCK)
        rows = pl.ds(r0, ROW_BLOCK)
        qn = _rms(q_ref[rows, :].astype(F32), qw_ref[...]) * (HEAD_DIM ** -0.5 * LOG2E)
        qt_ref[0:HEAD_DIM, rows] = qn.T.astype(BF16)
        hi, mid, lo = _split3(crow_ref[:, rows] * LOG2E)
        aug = jnp.where(aug_r == 0, hi.astype(F32),
                        jnp.where(aug_r == 1, mid.astype(F32),
                                  jnp.where(aug_r == 2, lo.astype(F32),
                                            jnp.where(aug_r < 6, 1.0, 0.0))))
        qt_ref[HEAD_DIM:2 * HEAD_DIM, rows] = aug.astype(BF16)
        ka_ref[rows, 0:HEAD_DIM] = _rms(k_ref[rows, :].astype(F32), kw_ref[...]).astype(BF16)
        aug_k = jnp.where(aug_r < 3, 1.0,
                          jnp.where(aug_r == 3, -hi.astype(F32),
                                    jnp.where(aug_r == 4, -mid.astype(F32),
                                              jnp.where(aug_r == 5, -lo.astype(F32), 0.0))))
        ka_ref[rows, HEAD_DIM:2 * HEAD_DIM] = aug_k.T.astype(BF16)
        vt_ref[:, rows] = v_ref[rows, :].astype(F32).T.astype(BF16)
        return 0
    lax.fori_loop(0, n_blocks, pro_body, 0)

    kam_ref[:, 0:HEAD_DIM] = _rms(km_ref[...].astype(F32), kw_ref[...]).astype(BF16)
    ck_m = _lane_bcast(colm_ref[...], FORGET_LANE + h) * LOG2E
    mrow = lax.broadcasted_iota(jnp.int32, (META_ROWS, LANES), 0)
    kam_ref[:, HEAD_DIM:2 * HEAD_DIM] = key_aug(ck_m, mrow >= META_PAD)
    vtm_ref[...] = vm_ref[...].astype(F32).T.astype(BF16)

    kidx = lax.broadcasted_iota(jnp.int32, (FOX_TK, FOX_TK), 0)
    qidx = lax.broadcasted_iota(jnp.int32, (FOX_TK, FOX_TK), 1)
    diag_ok = kidx <= qidx

    items = []
    for kj in range(-1, seq // FOX_TK):
        lane0 = max(kj, 0) * FOX_TK
        for p0 in range(lane0, seq, FOX_PIECE):
            items.append((kj, slice(p0, min(p0 + FOX_PIECE, seq)), p0 == lane0))

    def scores(item):
        kj, ln, leads = item
        k_aug = kam_ref[...] if kj < 0 else ka_ref[kj * FOX_TK:(kj + 1) * FOX_TK, :]
        s = _dot(k_aug, qt_ref[:, ln])
        if kj >= 0 and leads:
            masked = jnp.where(diag_ok, s[:, :FOX_TK], NEG_BIG)
            s = jnp.concatenate([masked, s[:, FOX_TK:]], axis=1) if s.shape[1] > FOX_TK else masked
        return s

    def softmax_stats(item, s):
        kj, ln, _ = item
        top = jnp.max(s, axis=0, keepdims=True)
        if kj < 0:
            m_new, alpha = top, None
        else:
            m_old = m_ref[:, ln]
            m_new = jnp.maximum(m_old, top)
            alpha = jnp.exp2(m_old - m_new)
        p = jnp.exp2(s - m_new)
        psum = jnp.sum(p, axis=0, keepdims=True)
        m_ref[:, ln] = m_new
        l_ref[:, ln] = psum if kj < 0 else alpha * l_ref[:, ln] + psum
        return p.astype(BF16), alpha

    def values(item, p, alpha):
        kj, ln, _ = item
        v_t = vtm_ref[...] if kj < 0 else vt_ref[:, kj * FOX_TK:(kj + 1) * FOX_TK]
        pv = _dot(v_t, p)
        acc_ref[:, ln] = pv if kj < 0 else alpha * acc_ref[:, ln] + pv

    s_cur = scores(items[0])
    p_cur = None
    for t in range(len(items) + 1):
        s_next = scores(items[t + 1]) if t + 1 < len(items) else None
        p_next = softmax_stats(items[t], s_cur) if t < len(items) else None
        if p_cur is not None:
            values(items[t - 1], *p_cur)
        s_cur, p_cur = s_next, p_next

    def out_body(i, _):
        r0 = pl.multiple_of(i * ROW_BLOCK, ROW_BLOCK)
        rows = pl.ds(r0, ROW_BLOCK)
        out_t = acc_ref[:, rows] * (1.0 / l_ref[:, rows])
        o_ref[rows, :] = (out_t.T * _silu(g_ref[rows, :].astype(F32))).astype(BF16)
        return 0
    lax.fori_loop(0, n_blocks, out_body, 0)


def _fox(proj, proj_m, col_m, row4, q_w, k_w):
    b, seq, _ = proj.shape
    base = 4 * GDN_HEADS
    hb = FOX_HEADS

    def head_block(off):
        return pl.BlockSpec((None, seq, HEAD_DIM), lambda i, j, off=off: (i, 0, off + j))

    def meta_block(off):
        return pl.BlockSpec((META_ROWS, HEAD_DIM), lambda i, j, off=off: (0, off + j))

    return pl.pallas_call(
        functools.partial(_fox_kernel, seq=seq),
        out_shape=jax.ShapeDtypeStruct((b, seq, FOX_WIDTH), BF16),
        grid=(b, FOX_HEADS),
        in_specs=[head_block(base), head_block(base + hb), head_block(base + 2 * hb),
                  head_block(base + 3 * hb),
                  meta_block(base + hb), meta_block(base + 2 * hb),
                  pl.BlockSpec((None, META_ROWS, LANES), lambda i, j: (0, 0, 0)),
                  pl.BlockSpec((None, None, 1, seq), lambda i, j: (i, FORGET_LANE + j, 0, 0)),
                  pl.BlockSpec((1, HEAD_DIM), lambda i, j: (0, 0)),
                  pl.BlockSpec((1, HEAD_DIM), lambda i, j: (0, 0))],
        out_specs=pl.BlockSpec((None, seq, HEAD_DIM), lambda i, j: (i, 0, j)),
        scratch_shapes=[pltpu.VMEM((2 * HEAD_DIM, seq), BF16),
                        pltpu.VMEM((seq, 2 * HEAD_DIM), BF16),
                        pltpu.VMEM((META_ROWS, 2 * HEAD_DIM), BF16),
                        pltpu.VMEM((HEAD_DIM, seq), BF16),
                        pltpu.VMEM((HEAD_DIM, META_ROWS), BF16),
                        pltpu.VMEM((1, seq), F32), pltpu.VMEM((1, seq), F32),
                        pltpu.VMEM((HEAD_DIM, seq), F32)],
        compiler_params=pltpu.CompilerParams(
            dimension_semantics=("parallel", "arbitrary"), vmem_limit_bytes=VMEM_LIMIT),
        name="fox",
    )(proj, proj, proj, proj, proj_m, proj_m, col_m, row4, q_w, k_w)


def _out_proj_kernel(mg_ref, mf_ref, wg_ref, wf_ref, pw_ref, x_ref, o_ref):
    out = _dot(mg_ref[...], wg_ref[...]) + _dot(mf_ref[...], wf_ref[...])
    o_ref[...] = x_ref[...] + _rms(out, pw_ref[...])


def _out_proj(mg, mf, w_g, w_f, post_w, x2d, tm):
    m, d = x2d.shape
    return pl.pallas_call(
        _out_proj_kernel,
        out_shape=jax.ShapeDtypeStruct((m, d), F32),
        grid=(m // tm,),
        in_specs=[pl.BlockSpec((tm, GDN_WIDTH), lambda i: (i, 0)),
                  pl.BlockSpec((tm, FOX_WIDTH), lambda i: (i, 0)),
                  pl.BlockSpec((GDN_WIDTH, d), lambda i: (0, 0)),
                  pl.BlockSpec((FOX_WIDTH, d), lambda i: (0, 0)),
                  pl.BlockSpec((1, d), lambda i: (0, 0)),
                  pl.BlockSpec((tm, d), lambda i: (i, 0))],
        out_specs=pl.BlockSpec((tm, d), lambda i: (i, 0)),
        compiler_params=pltpu.CompilerParams(
            dimension_semantics=("parallel",), vmem_limit_bytes=VMEM_LIMIT),
        name="out_proj",
    )(mg, mf, w_g, w_f, post_w, x2d)


def _tile(total, want):
    t = min(total, want)
    while total % t:
        t //= 2
    return t


def _layer(x, meta_pad, pre_w, w_in, conv_w, a_log, dt_bias, gdn_norm_w, fox_q_w, fox_k_w,
           fox_f_bias, w_out, post_w):
    b, seq, d = x.shape
    assert seq % (GROUP * PREP_UNROLL) == 0 and seq % FOX_TK == 0 and seq % ROW_BLOCK == 0
    gw, fw = GDN_WIDTH, FOX_WIDTH
    o_gb = 4 * gw
    o_f = o_gb + 2 * GDN_HEADS
    o_ff = o_f + 4 * fw
    w_t = w_in.T
    w_cat = jnp.concatenate(
        [w_t[:o_gb], w_t[o_f:o_ff], w_t[o_gb:o_f], w_t[o_ff:],
         jnp.zeros((GATE_WIDTH - 2 * GDN_HEADS - FOX_HEADS, d), w_in.dtype)], axis=0).astype(BF16)
    zpad = jnp.zeros((GATE_WIDTH - FORGET_LANE - FOX_HEADS,), F32)
    add_vec = jnp.concatenate([jnp.zeros((GDN_HEADS,), F32), dt_bias, fox_f_bias, zpad])[None]
    alog_vec = jnp.concatenate([jnp.zeros((GDN_HEADS,), F32), a_log,
                                jnp.zeros((FOX_HEADS,), F32), zpad])[None]

    x2d = x.reshape(b * seq, d)
    pre_w2 = pre_w[None]
    proj, gate = _in_proj(x2d, pre_w2, w_cat, _tile(b * seq, IN_PROJ_TM), IN_PROJ_TN)
    proj_m, gate_m = _in_proj(meta_pad, pre_w2, w_cat, META_ROWS, IN_PROJ_TN)
    proj = proj.reshape(b, seq, MAIN_WIDTH)

    col, row = _gate_prep(gate.reshape(b, seq, GATE_WIDTH), add_vec, alog_vec, False)
    col_m, row_m = _gate_prep(gate_m[None], add_vec, alog_vec, True)
    row4 = row.reshape(b, GATE_ROWS, 1, seq)
    row_m4 = row_m.reshape(1, GATE_ROWS, 1, META_ROWS)

    conv_wt = conv_w.T
    state0 = _gdn_state0(proj_m, conv_wt, col_m, row_m4)
    o_gdn = _gdn(proj, proj_m, conv_wt, row4, state0, gdn_norm_w[None])
    o_fox = _fox(proj, proj_m, col_m, row4, fox_q_w[None], fox_k_w[None])

    w_out_b = w_out.astype(BF16)
    out = _out_proj(o_gdn.reshape(b * seq, gw), o_fox.reshape(b * seq, fw),
                    w_out_b[:gw], w_out_b[gw:], post_w[None], x2d, _tile(b * seq, OUT_PROJ_TM))
    return out.reshape(b, seq, d)


def kernel(x, meta_tokens, pre_norm_w, w_in, conv_w, a_log, dt_bias, gdn_norm_w, fox_q_norm_w,
           fox_k_norm_w, fox_f_bias, w_out, post_norm_w):
    assert pre_norm_w.shape[0] == 1, "single-layer stack"
    meta_pad = jnp.concatenate(
        [jnp.zeros((META_PAD, x.shape[-1]), x.dtype), meta_tokens.astype(x.dtype)], axis=0)
    return _layer(x, meta_pad, pre_norm_w[0], w_in[0], conv_w[0], a_log[0], dt_bias[0],
                  gdn_norm_w[0], fox_q_norm_w[0], fox_k_norm_w[0], fox_f_bias[0], w_out[0],
                  post_norm_w[0])
```

```python
import functools
import math

import jax
import jax.numpy as jnp
from jax import lax
from jax.experimental import pallas as pl
from jax.experimental.pallas import tpu as pltpu

N_META = 16
HEAD_DIM = 128
GDN_HEADS = 8
FOX_HEADS = 8
GDN_WIDTH = GDN_HEADS * HEAD_DIM
FOX_WIDTH = FOX_HEADS * HEAD_DIM
CONV_WIDTH = 4
CHUNK = 64
EPS = 1e-6

LANES = 128
SUBLANES = 8
MXU_DIM = 256
MAIN_WIDTH = 4 * GDN_WIDTH + 4 * FOX_WIDTH
GATE_WIDTH = LANES
BETA_LANE, DECAY_LANE, FORGET_LANE = 0, GDN_HEADS, 2 * GDN_HEADS
GATE_ROWS = 32
META_ROWS = CHUNK
META_PAD = META_ROWS - N_META
GROUP = MXU_DIM
ROW_BLOCK = 256
HIST_ROWS = 2 * SUBLANES
IN_PROJ_TM, IN_PROJ_TN = 1024, 2048
OUT_PROJ_TM = 512
HEAD_PAIR = 2
REC_PAIRS = 2
REC_HEADS = REC_PAIRS * HEAD_PAIR
PREP_UNROLL = 2
FOX_TK = MXU_DIM
FOX_PIECE = 2 * MXU_DIM
VMEM_LIMIT = 56 * 1024 * 1024

F32 = jnp.float32
BF16 = jnp.bfloat16
NEG_BIG = -1e30
LOG2E = math.log2(math.e)


def _dot(a, b):
    return jnp.dot(a, b, preferred_element_type=F32)


def _dot_nt(a, b):
    return lax.dot_general(a, b, (((1,), (1,)), ((), ())), preferred_element_type=F32)


def _split3(x):
    hi = x.astype(BF16)
    r1 = x - hi.astype(F32)
    mid = r1.astype(BF16)
    lo = (r1 - mid.astype(F32)).astype(BF16)
    return hi, mid, lo


def _dot_exact_rhs01(parts, m):
    return _dot(parts[0], m) + _dot(parts[1], m) + _dot(parts[2], m)


def _dot_exact_lhs01(m, parts):
    return _dot(m, parts[0]) + _dot(m, parts[1]) + _dot(m, parts[2])


def _lane_bcast(col_tile, lane):
    sel = (lax.broadcasted_iota(jnp.int32, (LANES, LANES), 0) == lane).astype(BF16)
    return _dot_exact_rhs01(_split3(col_tile), sel)


def _chunk_of(idx):
    return jnp.right_shift(idx, CHUNK.bit_length() - 1)


def _rms(x, w):
    return x * lax.rsqrt(jnp.mean(x * x, axis=-1, keepdims=True) + EPS) * w


def _silu(x):
    return x * (1.0 / (1.0 + jnp.exp(-x)))


def _softplus(x):
    return jnp.maximum(x, 0.0) + jnp.log1p(jnp.exp(-jnp.abs(x)))


def _in_proj_kernel(x_ref, nw_ref, w_ref, wg_ref, o_ref, og_ref, xn_ref):
    @pl.when(pl.program_id(1) == 0)
    def _():
        xn = _rms(x_ref[...], nw_ref[...]).astype(BF16)
        xn_ref[...] = xn
        og_ref[...] = _dot_nt(xn, wg_ref[...])

    res = _dot_nt(xn_ref[...], w_ref[...])
    for hd in range(o_ref.shape[0]):
        o_ref[hd] = res[:, hd * HEAD_DIM:(hd + 1) * HEAD_DIM].astype(o_ref.dtype)


def _in_proj(x2d, norm_w, w_cat, tm, tn):
    m, d = x2d.shape
    n = MAIN_WIDTH
    return pl.pallas_call(
        _in_proj_kernel,
        out_shape=(jax.ShapeDtypeStruct((n // HEAD_DIM, m, HEAD_DIM), BF16),
                   jax.ShapeDtypeStruct((m, GATE_WIDTH), F32)),
        grid=(m // tm, n // tn),
        in_specs=[pl.BlockSpec((tm, d), lambda i, j: (i, 0)),
                  pl.BlockSpec((1, d), lambda i, j: (0, 0)),
                  pl.BlockSpec((tn, d), lambda i, j: (j, 0)),
                  pl.BlockSpec((GATE_WIDTH, d), lambda i, j: (MAIN_WIDTH // GATE_WIDTH, 0))],
        out_specs=(pl.BlockSpec((tn // HEAD_DIM, tm, HEAD_DIM), lambda i, j: (j, i, 0)),
                   pl.BlockSpec((tm, GATE_WIDTH), lambda i, j: (i, 0))),
        scratch_shapes=[pltpu.VMEM((tm, d), BF16)],
        compiler_params=pltpu.CompilerParams(
            dimension_semantics=("parallel", "arbitrary"), vmem_limit_bytes=VMEM_LIMIT),
        name="in_proj",
    )(x2d, norm_w, w_cat, w_cat)


def _gate_kernel(t_ref, add_ref, alog_ref, col_ref, row_ref, *, rows, is_meta):
    blk = min(rows, ROW_BLOCK)
    lane = lax.broadcasted_iota(jnp.int32, (blk, LANES), 1)
    ri = lax.broadcasted_iota(jnp.int32, (blk, blk), 0)
    ci = lax.broadcasted_iota(jnp.int32, (blk, blk), 1)
    tri = (ci <= ri).astype(BF16)
    tri_chunk = ((ci <= ri) & (_chunk_of(ri) == _chunk_of(ci))).astype(BF16)
    is_beta = lane < DECAY_LANE
    is_decay = (lane >= DECAY_LANE) & (lane < FORGET_LANE)
    is_forget = (lane >= FORGET_LANE) & (lane < FORGET_LANE + FOX_HEADS)

    carry = jnp.zeros((1, LANES), F32)
    for r in range(rows // blk):
        t = t_ref[r * blk:(r + 1) * blk, :]
        ta = t + add_ref[...]
        beta = 1.0 / (1.0 + jnp.exp(-t))
        g = -jnp.exp(alog_ref[...]) * _softplus(ta)
        logf = -_softplus(-ta)
        val = jnp.where(is_decay, g, jnp.where(is_forget, logf, 0.0))
        if is_meta:
            row = lax.broadcasted_iota(jnp.int32, (blk, LANES), 0)
            val = jnp.where(row >= META_PAD, val, 0.0)
            beta = jnp.where(row >= META_PAD, beta, 0.0)
        parts = _split3(val)
        cum_chunk = _dot_exact_lhs01(tri_chunk, parts)
        cum_all = _dot_exact_lhs01(tri, parts) + carry
        carry = cum_all[blk - 1:blk, :]
        if is_meta:
            cum_all = cum_all - carry
        res = jnp.where(is_beta, beta, jnp.where(is_decay, cum_chunk, cum_all))
        col_ref[r * blk:(r + 1) * blk, :] = res
        row_ref[:, r * blk:(r + 1) * blk] = res.T[:GATE_ROWS, :]


def _gate_prep(gate3d, add_vec, alog_vec, is_meta):
    b, rows, _ = gate3d.shape
    kern = functools.partial(_gate_kernel, rows=rows, is_meta=is_meta)
    return pl.pallas_call(
        kern,
        out_shape=(jax.ShapeDtypeStruct((b, rows, LANES), F32),
                   jax.ShapeDtypeStruct((b, GATE_ROWS, rows), F32)),
        grid=(b,),
        in_specs=[pl.BlockSpec((None, rows, LANES), lambda i: (i, 0, 0)),
                  pl.BlockSpec((1, LANES), lambda i: (0, 0)),
                  pl.BlockSpec((1, LANES), lambda i: (0, 0))],
        out_specs=(pl.BlockSpec((None, rows, LANES), lambda i: (i, 0, 0)),
                   pl.BlockSpec((None, GATE_ROWS, rows), lambda i: (i, 0, 0))),
        compiler_params=pltpu.CompilerParams(
            dimension_semantics=("parallel",), vmem_limit_bytes=VMEM_LIMIT),
        name="gate_prep_meta" if is_meta else "gate_prep",
    )(gate3d, add_vec, alog_vec)


def _conv_silu(load_rows, w):
    y = load_rows(0) * w[0:1, :]
    for j in range(1, CONV_WIDTH):
        y = y + load_rows(j) * w[j:j + 1, :]
    return _silu(y)


def _l2norm(x):
    return x * lax.rsqrt(jnp.sum(x * x, axis=-1, keepdims=True) + EPS)


def _gdn_pointwise(q, k, v, beta_b, g_b):
    r = q.shape[0]
    q = _l2norm(q) * (HEAD_DIM ** -0.5)
    k = _l2norm(k)
    g3 = g_b.reshape(r // CHUNK, CHUNK, LANES)
    g_last = jnp.broadcast_to(g3[:, CHUNK - 1:CHUNK, :], g3.shape).reshape(r, LANES)
    e_g = jnp.exp(g_b)
    q_dec = q * e_g
    k_dec = k * jnp.exp(g_last - g_b)
    y = jnp.concatenate([v * beta_b, k * (beta_b * e_g)], axis=1)
    return q, k, q_dec, k_dec, y, jnp.exp(g_last)


def _gdn_groups(probs, fillers=()):
    r = probs[0][0].shape[0]
    ri = lax.broadcasted_iota(jnp.int32, (r, r), 0)
    ci = lax.broadcasted_iota(jnp.int32, (r, r), 1)
    same = _chunk_of(ri) == _chunk_of(ci)
    causal = same & (ci <= ri)
    strict = same & (ci < ri)

    def widen(t):
        return jnp.concatenate([t] * (r // LANES), axis=1) if r >= LANES else t[:, :r]

    kks = [_dot_nt(k, k) for _, k, _, _, _, _ in probs]
    qks = [_dot_nt(q, k) for q, k, _, _, _, _ in probs]
    dmats = [jnp.where(causal, jnp.exp(jnp.where(causal, widen(g_b) - g_row, 0.0)), 0.0)
             for _, _, _, _, g_b, g_row in probs]
    xs = [jnp.where(strict, widen(p[3]) * kk * d, 0.0).astype(BF16)
          for p, kk, d in zip(probs, kks, dmats)]
    a_qks = [qk * d for qk, d in zip(qks, dmats)]
    zs = [p[2].astype(F32) - _dot(x, p[2]) for p, x in zip(probs, xs)]
    fillers = list(fillers)
    span = 2
    while span < CHUNK:
        xs = [_dot(x, x).astype(BF16) for x in xs]
        zs = [z + _dot(x, z.astype(BF16)) for x, z in zip(xs, zs)]
        if fillers:
            fillers.pop(0)()
        span *= 2
    for thunk in fillers:
        thunk()
    return list(zip(zs, a_qks))


def _gdn_state0_kernel(km_ref, vm_ref, wk_ref, wv_ref, colm_ref, growm_ref, s_ref, pad_ref):
    h = pl.program_id(0)
    pad_ref[0:SUBLANES, :] = jnp.zeros((SUBLANES, LANES), F32)
    beta_m = _lane_bcast(colm_ref[...], BETA_LANE + h)
    g_m = _lane_bcast(colm_ref[...], DECAY_LANE + h)
    conv = []
    for src, w_ref in ((km_ref, wk_ref), (vm_ref, wv_ref)):
        pad_ref[SUBLANES:SUBLANES + META_ROWS, :] = src[...].astype(F32)
        conv.append(_conv_silu(
            lambda j: pad_ref[SUBLANES - (CONV_WIDTH - 1) + j:
                              SUBLANES - (CONV_WIDTH - 1) + j + META_ROWS, :], w_ref[...]))
    _, k_m, _, kd_m, y_m, _ = _gdn_pointwise(conv[0], conv[0], conv[1], beta_m, g_m)
    k_bf = k_m.astype(BF16)
    (uw_m, _), = _gdn_groups([(k_bf, k_bf, y_m.astype(BF16), beta_m, g_m, growm_ref[...])])
    s_ref[...] = _dot(kd_m.T.astype(BF16), uw_m[:, :HEAD_DIM].astype(BF16))


def _gdn_state0(proj_m, conv_wt, col_m, row_m4):
    hb = GDN_HEADS

    def meta_block(off):
        return pl.BlockSpec((None, META_ROWS, HEAD_DIM), lambda j, off=off: (off + j, 0, 0))

    def conv_block(off):
        return pl.BlockSpec((CONV_WIDTH, HEAD_DIM), lambda j, off=off: (0, off + j))

    return pl.pallas_call(
        _gdn_state0_kernel,
        out_shape=jax.ShapeDtypeStruct((GDN_HEADS, HEAD_DIM, HEAD_DIM), F32),
        grid=(GDN_HEADS,),
        in_specs=[meta_block(hb), meta_block(2 * hb), conv_block(hb), conv_block(2 * hb),
                  pl.BlockSpec((None, META_ROWS, LANES), lambda j: (0, 0, 0)),
                  pl.BlockSpec((None, None, 1, META_ROWS), lambda j: (0, DECAY_LANE + j, 0, 0))],
        out_specs=pl.BlockSpec((None, HEAD_DIM, HEAD_DIM), lambda j: (j, 0, 0)),
        scratch_shapes=[pltpu.VMEM((META_ROWS + SUBLANES, HEAD_DIM), F32)],
        compiler_params=pltpu.CompilerParams(
            dimension_semantics=("parallel",), vmem_limit_bytes=VMEM_LIMIT),
        name="gdn_state0",
    )(proj_m, proj_m, conv_wt, conv_wt, col_m, row_m4)


def _gdn_kernel(*refs, seq):
    n_qkv = 3 * HEAD_PAIR
    src_refs = [refs[t * HEAD_PAIR:(t + 1) * HEAD_PAIR] for t in range(3)]
    z_ref = refs[n_qkv]
    meta_refs = [refs[n_qkv + 1 + t * HEAD_PAIR:n_qkv + 1 + (t + 1) * HEAD_PAIR] for t in range(3)]
    tap_refs = [refs[2 * n_qkv + 1 + t * HEAD_PAIR:2 * n_qkv + 1 + (t + 1) * HEAD_PAIR]
                for t in range(3)]
    (brow_ref, grow_ref, s0_ref, nw_ref, o_ref, pad_ref, qs_ref, ks_ref, y_ref, bb_ref, gb_ref,
     qd_ref, kdt_ref, dec_ref, u_ref, w_ref, aqk_ref, st_ref) = refs[3 * n_qkv + 1:]
    pair = pl.program_id(1)
    slot = pair % REC_PAIRS
    n_blocks = seq // ROW_BLOCK
    n_groups = seq // GROUP
    cpg = GROUP // CHUNK
    hist = CONV_WIDTH - 1

    def pointwise_head(i, bank, hh):
        r0 = pl.multiple_of(i * ROW_BLOCK, ROW_BLOCK)
        rows = pl.ds(r0, ROW_BLOCK)
        convs = []
        for t in range(3):
            win = pad_ref.at[(bank * 3 + t) * HEAD_PAIR + hh]
            if isinstance(i, int) and i == 0:
                past = meta_refs[t][hh][META_ROWS - HIST_ROWS:META_ROWS, :]
            else:
                past = src_refs[t][hh][pl.ds(pl.multiple_of(r0 - HIST_ROWS, HIST_ROWS), HIST_ROWS), :]
            win[0:HIST_ROWS, :] = past.astype(F32)
            win[HIST_ROWS:HIST_ROWS + ROW_BLOCK, :] = src_refs[t][hh][rows, :].astype(F32)
            convs.append(_conv_silu(
                lambda j, win=win: win[HIST_ROWS - hist + j:HIST_ROWS - hist + j + ROW_BLOCK, :],
                tap_refs[t][hh][...]))
        hs = slot * HEAD_PAIR + hh
        beta_b = jnp.broadcast_to(brow_ref[hh, :, rows], (LANES, ROW_BLOCK)).T
        g_b = jnp.broadcast_to(grow_ref[hh, :, rows], (LANES, ROW_BLOCK)).T
        q, k, q_dec, k_dec, y, dec = _gdn_pointwise(convs[0], convs[1], convs[2], beta_b, g_b)
        qs_ref[hh, rows, :] = q.astype(BF16)
        ks_ref[hh, rows, :] = k.astype(BF16)
        y_ref[hh, rows, :] = y.astype(BF16)
        bb_ref[hh, rows, :] = beta_b
        gb_ref[hh, rows, :] = g_b
        qd_ref[hs, rows, :] = q_dec.astype(BF16)
        kdt_ref[hs, :, rows] = k_dec.T.astype(BF16)
        for c in range(ROW_BLOCK // CHUNK):
            dec_ref[hs, pl.ds(i * (ROW_BLOCK // CHUNK) + c, 1), :] = dec[c * CHUNK:c * CHUNK + 1, :]

    def pointwise_thunks(i, bank):
        return [functools.partial(pointwise_head, i, bank, hh) for hh in range(HEAD_PAIR)]

    for i in range(PREP_UNROLL):
        for thunk in pointwise_thunks(i, i):
            thunk()

    def prep_groups(gi, fillers):
        keys = [(hh, pl.ds(pl.multiple_of((gi * PREP_UNROLL + u) * GROUP, GROUP), GROUP))
                for u in range(PREP_UNROLL) for hh in range(HEAD_PAIR)]
        probs = [(qs_ref[hh, rows, :], ks_ref[hh, rows, :], y_ref[hh, rows, :],
                  bb_ref[hh, rows, :], gb_ref[hh, rows, :], grow_ref[hh, :, rows])
                 for hh, rows in keys]
        for (hh, rows), (uw, a_qk) in zip(keys, _gdn_groups(probs, fillers)):
            hs = slot * HEAD_PAIR + hh
            u_ref[hs, rows, :] = uw[:, :HEAD_DIM]
            w_ref[hs, rows, :] = uw[:, HEAD_DIM:].astype(BF16)
            aqk_ref[hs, rows, :] = a_qk.astype(BF16)

    def prep_body(gi, _):
        prep_groups(gi, [th for u in range(PREP_UNROLL)
                         for th in pointwise_thunks((gi + 1) * PREP_UNROLL + u, u)])
        return 0
    n_trips = n_groups // PREP_UNROLL
    lax.fori_loop(0, n_trips - 1, prep_body, 0)
    prep_groups(n_trips - 1, [])

    @pl.when(slot == REC_PAIRS - 1)
    def _():
        st_ref[...] = s0_ref[...]

        def rec_body(gi, _):
            r0 = pl.multiple_of(gi * GROUP, GROUP)
            rows = pl.ds(r0, GROUP)
            heads = range(REC_HEADS)
            kd_t = [kdt_ref[h, :, rows] for h in heads]
            outs = [[] for _ in heads]
            for c in range(cpg):
                crow = pl.ds(r0 + c * CHUNK, CHUNK)
                states = [st_ref[h] for h in heads]
                s_bf = [s.astype(BF16) for s in states]
                ws = [_dot(jnp.concatenate([w_ref[h, crow, :], qd_ref[h, crow, :]], axis=0), s_bf[h])
                      for h in heads]
                v_new = [u_ref[h, crow, :] - ws[h][:CHUNK, :] for h in heads]
                zero = lambda n: jnp.zeros((n * CHUNK, HEAD_DIM), BF16)
                v_pad = [jnp.concatenate(([zero(c)] if c else []) + [v.astype(BF16)]
                                         + ([zero(cpg - 1 - c)] if c < cpg - 1 else []), axis=0)
                         for v in v_new]
                upd = [_dot(jnp.concatenate([aqk_ref[h, crow, :], kd_t[h]], axis=0), v_pad[h])
                       for h in heads]
                for h in heads:
                    dec = dec_ref[h, pl.ds(gi * cpg + c, 1), :]
                    st_ref[h] = states[h] * dec + upd[h][CHUNK:, :]
                    outs[h].append(ws[h][CHUNK:, :] + upd[h][:CHUNK, :])
            for h in heads:
                o = _rms(jnp.concatenate(outs[h], axis=0), nw_ref[...])
                o_ref[h, rows, :] = (o * _silu(z_ref[h, rows, :].astype(F32))).astype(BF16)
            return 0
        lax.fori_loop(0, n_groups, rec_body, 0)


def _gdn(proj, proj_m, conv_wt, row4, state0, norm_w):
    _, b, seq, _ = proj.shape
    n_pairs = GDN_HEADS // HEAD_PAIR

    def head_specs(shape, imap):
        return [pl.BlockSpec(shape, functools.partial(imap, t * GDN_HEADS + hh))
                for t in range(3) for hh in range(HEAD_PAIR)]

    src_specs = head_specs((None, None, seq, HEAD_DIM),
                           lambda off, i, j: (off + HEAD_PAIR * j, i, 0, 0))
    meta_specs = head_specs((None, META_ROWS, HEAD_DIM), lambda off, i, j: (off + HEAD_PAIR * j, 0, 0))
    tap_specs = head_specs((CONV_WIDTH, HEAD_DIM), lambda off, i, j: (0, off + HEAD_PAIR * j))
    n_qkv = 3 * HEAD_PAIR
    z_block0 = 3 * GDN_HEADS // REC_HEADS

    per_pair = lambda width, dt: pltpu.VMEM((HEAD_PAIR, seq, width), dt)
    per_rec = lambda width, dt: pltpu.VMEM((REC_HEADS, seq, width), dt)
    return pl.pallas_call(
        functools.partial(_gdn_kernel, seq=seq),
        out_shape=jax.ShapeDtypeStruct((GDN_HEADS, b, seq, HEAD_DIM), BF16),
        grid=(b, n_pairs),
        in_specs=src_specs
        + [pl.BlockSpec((REC_HEADS, None, seq, HEAD_DIM),
                        lambda i, j: (z_block0 + j // REC_PAIRS, i, 0, 0))]
        + meta_specs + tap_specs
        + [pl.BlockSpec((None, HEAD_PAIR, 1, seq),
                        lambda i, j: (i, BETA_LANE // HEAD_PAIR + j, 0, 0)),
           pl.BlockSpec((None, HEAD_PAIR, 1, seq),
                        lambda i, j: (i, DECAY_LANE // HEAD_PAIR + j, 0, 0)),
           pl.BlockSpec((REC_HEADS, HEAD_DIM, HEAD_DIM), lambda i, j: (j // REC_PAIRS, 0, 0)),
           pl.BlockSpec((1, HEAD_DIM), lambda i, j: (0, 0))],
        out_specs=pl.BlockSpec((REC_HEADS, None, seq, HEAD_DIM),
                               lambda i, j: (j // REC_PAIRS, i, 0, 0)),
        scratch_shapes=[pltpu.VMEM((PREP_UNROLL * n_qkv, HIST_ROWS + ROW_BLOCK, HEAD_DIM), F32),
                        per_pair(HEAD_DIM, BF16), per_pair(HEAD_DIM, BF16),
                        per_pair(2 * HEAD_DIM, BF16),
                        per_pair(HEAD_DIM, F32), per_pair(HEAD_DIM, F32),
                        per_rec(HEAD_DIM, BF16),
                        pltpu.VMEM((REC_HEADS, HEAD_DIM, seq), BF16),
                        pltpu.VMEM((REC_HEADS, seq // CHUNK, HEAD_DIM), F32),
                        per_rec(HEAD_DIM, F32), per_rec(HEAD_DIM, BF16), per_rec(GROUP, BF16),
                        pltpu.VMEM((REC_HEADS, HEAD_DIM, HEAD_DIM), F32)],
        compiler_params=pltpu.CompilerParams(
            dimension_semantics=("parallel", "arbitrary"), vmem_limit_bytes=VMEM_LIMIT),
        name="gdn",
    )(*([proj] * n_qkv), proj, *([proj_m] * n_qkv), *([conv_wt] * n_qkv),
      row4, row4, state0, norm_w)


def _fox_kernel(q_ref, k_ref, v_ref, g_ref, km_ref, vm_ref, colm_ref, crow_ref,
                qw_ref, kw_ref, o_ref, qt_ref, ka_ref, kam_ref, vt_ref, vtm_ref,
                m_ref, l_ref, acc_ref, *, seq):
    h = pl.program_id(1)
    n_blocks = seq // ROW_BLOCK
    aug_r = lax.broadcasted_iota(jnp.int32, (LANES, ROW_BLOCK), 0)

    def key_aug(ck, valid=None):
        hi, mid, lo = _split3(ck)
        lane = lax.broadcasted_iota(jnp.int32, ck.shape, 1)
        neg_hi = -hi.astype(F32)
        if valid is not None:
            neg_hi = jnp.where(valid, neg_hi, NEG_BIG)
        blk = jnp.where(lane < 3, 1.0,
                        jnp.where(lane == 3, neg_hi,
                                  jnp.where(lane == 4, -mid.astype(F32),
                                            jnp.where(lane == 5, -lo.astype(F32), 0.0))))
        return blk.astype(BF16)

    def pro_body(i, _):
        r0 = pl.multiple_of(i * ROW_BLOCK, ROW_BLOCK)
        rows = pl.ds(r0, ROW_BLOCK)
        qn = _rms(q_ref[rows, :].astype(F32), qw_ref[...]) * (HEAD_DIM ** -0.5 * LOG2E)
        qt_ref[0:HEAD_DIM, rows] = qn.T.astype(BF16)
        hi, mid, lo = _split3(crow_ref[:, rows] * LOG2E)
        aug = jnp.where(aug_r == 0, hi.astype(F32),
                        jnp.where(aug_r == 1, mid.astype(F32),
                                  jnp.where(aug_r == 2, lo.astype(F32),
                                            jnp.where(aug_r < 6, 1.0, 0.0))))
        qt_ref[HEAD_DIM:2 * HEAD_DIM, rows] = aug.astype(BF16)
        ka_ref[rows, 0:HEAD_DIM] = _rms(k_ref[rows, :].astype(F32), kw_ref[...]).astype(BF16)
        aug_k = jnp.where(aug_r < 3, 1.0,
                          jnp.where(aug_r == 3, -hi.astype(F32),
                                    jnp.where(aug_r == 4, -mid.astype(F32),
                                              jnp.where(aug_r == 5, -lo.astype(F32), 0.0))))
        ka_ref[rows, HEAD_DIM:2 * HEAD_DIM] = aug_k.T.astype(BF16)
        vt_ref[:, rows] = v_ref[rows, :].astype(F32).T.astype(BF16)
        return 0
    lax.fori_loop(0, n_blocks, pro_body, 0)

    kam_ref[:, 0:HEAD_DIM] = _rms(km_ref[...].astype(F32), kw_ref[...]).astype(BF16)
    ck_m = _lane_bcast(colm_ref[...], FORGET_LANE + h) * LOG2E
    mrow = lax.broadcasted_iota(jnp.int32, (META_ROWS, LANES), 0)
    kam_ref[:, HEAD_DIM:2 * HEAD_DIM] = key_aug(ck_m, mrow >= META_PAD)
    vtm_ref[...] = vm_ref[...].astype(F32).T.astype(BF16)

    kidx = lax.broadcasted_iota(jnp.int32, (FOX_TK, FOX_TK), 0)
    qidx = lax.broadcasted_iota(jnp.int32, (FOX_TK, FOX_TK), 1)
    diag_ok = kidx <= qidx

    items = []
    for kj in range(-1, seq // FOX_TK):
        lane0 = max(kj, 0) * FOX_TK
        for p0 in range(lane0, seq, FOX_PIECE):
            items.append((kj, slice(p0, min(p0 + FOX_PIECE, seq)), p0 == lane0))

    def scores(item):
        kj, ln, leads = item
        k_aug = kam_ref[...] if kj < 0 else ka_ref[kj * FOX_TK:(kj + 1) * FOX_TK, :]
        s = _dot(k_aug, qt_ref[:, ln])
        if kj >= 0 and leads:
            masked = jnp.where(diag_ok, s[:, :FOX_TK], NEG_BIG)
            s = jnp.concatenate([masked, s[:, FOX_TK:]], axis=1) if s.shape[1] > FOX_TK else masked
        return s

    def softmax_stats(item, s):
        kj, ln, _ = item
        top = jnp.max(s, axis=0, keepdims=True)
        if kj < 0:
            m_new, alpha = top, None
        else:
            m_old = m_ref[:, ln]
            m_new = jnp.maximum(m_old, top)
            alpha = jnp.exp2(m_old - m_new)
        p = jnp.exp2(s - m_new)
        psum = jnp.sum(p, axis=0, keepdims=True)
        m_ref[:, ln] = m_new
        l_ref[:, ln] = psum if kj < 0 else alpha * l_ref[:, ln] + psum
        return p.astype(BF16), alpha

    def values(item, p, alpha):
        kj, ln, _ = item
        v_t = vtm_ref[...] if kj < 0 else vt_ref[:, kj * FOX_TK:(kj + 1) * FOX_TK]
        pv = _dot(v_t, p)
        acc_ref[:, ln] = pv if kj < 0 else alpha * acc_ref[:, ln] + pv

    s_cur = scores(items[0])
    p_cur = None
    for t in range(len(items) + 1):
        s_next = scores(items[t + 1]) if t + 1 < len(items) else None
        p_next = softmax_stats(items[t], s_cur) if t < len(items) else None
        if p_cur is not None:
            values(items[t - 1], *p_cur)
        s_cur, p_cur = s_next, p_next

    def out_body(i, _):
        r0 = pl.multiple_of(i * ROW_BLOCK, ROW_BLOCK)
        rows = pl.ds(r0, ROW_BLOCK)
        out_t = acc_ref[:, rows] * (1.0 / l_ref[:, rows])
        o_ref[rows, :] = (out_t.T * _silu(g_ref[rows, :].astype(F32))).astype(BF16)
        return 0
    lax.fori_loop(0, n_blocks, out_body, 0)


def _fox(proj, proj_m, col_m, row4, q_w, k_w):
    _, b, seq, _ = proj.shape
    base = 4 * GDN_HEADS
    hb = FOX_HEADS

    def head_block(off):
        return pl.BlockSpec((None, None, seq, HEAD_DIM), lambda i, j, off=off: (off + j, i, 0, 0))

    def meta_block(off):
        return pl.BlockSpec((None, META_ROWS, HEAD_DIM), lambda i, j, off=off: (off + j, 0, 0))

    return pl.pallas_call(
        functools.partial(_fox_kernel, seq=seq),
        out_shape=jax.ShapeDtypeStruct((FOX_HEADS, b, seq, HEAD_DIM), BF16),
        grid=(b, FOX_HEADS),
        in_specs=[head_block(base), head_block(base + hb), head_block(base + 2 * hb),
                  head_block(base + 3 * hb),
                  meta_block(base + hb), meta_block(base + 2 * hb),
                  pl.BlockSpec((None, META_ROWS, LANES), lambda i, j: (0, 0, 0)),
                  pl.BlockSpec((None, None, 1, seq), lambda i, j: (i, FORGET_LANE + j, 0, 0)),
                  pl.BlockSpec((1, HEAD_DIM), lambda i, j: (0, 0)),
                  pl.BlockSpec((1, HEAD_DIM), lambda i, j: (0, 0))],
        out_specs=pl.BlockSpec((None, None, seq, HEAD_DIM), lambda i, j: (j, i, 0, 0)),
        scratch_shapes=[pltpu.VMEM((2 * HEAD_DIM, seq), BF16),
                        pltpu.VMEM((seq, 2 * HEAD_DIM), BF16),
                        pltpu.VMEM((META_ROWS, 2 * HEAD_DIM), BF16),
                        pltpu.VMEM((HEAD_DIM, seq), BF16),
                        pltpu.VMEM((HEAD_DIM, META_ROWS), BF16),
                        pltpu.VMEM((1, seq), F32), pltpu.VMEM((1, seq), F32),
                        pltpu.VMEM((HEAD_DIM, seq), F32)],
        compiler_params=pltpu.CompilerParams(
            dimension_semantics=("parallel", "arbitrary"), vmem_limit_bytes=VMEM_LIMIT),
        name="fox",
    )(proj, proj, proj, proj, proj_m, proj_m, col_m, row4, q_w, k_w)


def _out_proj_kernel(mg_ref, mf_ref, wg_ref, wf_ref, pw_ref, x_ref, o_ref):
    def rows_of(m_ref):
        return jnp.concatenate([m_ref[h] for h in range(m_ref.shape[0])], axis=1)
    out = _dot(rows_of(mg_ref), wg_ref[...]) + _dot(rows_of(mf_ref), wf_ref[...])
    o_ref[...] = x_ref[...] + _rms(out, pw_ref[...])


def _out_proj(mg, mf, w_g, w_f, post_w, x2d, tm):
    m, d = x2d.shape
    return pl.pallas_call(
        _out_proj_kernel,
        out_shape=jax.ShapeDtypeStruct((m, d), F32),
        grid=(m // tm,),
        in_specs=[pl.BlockSpec((GDN_HEADS, tm, HEAD_DIM), lambda i: (0, i, 0)),
                  pl.BlockSpec((FOX_HEADS, tm, HEAD_DIM), lambda i: (0, i, 0)),
                  pl.BlockSpec((GDN_WIDTH, d), lambda i: (0, 0)),
                  pl.BlockSpec((FOX_WIDTH, d), lambda i: (0, 0)),
                  pl.BlockSpec((1, d), lambda i: (0, 0)),
                  pl.BlockSpec((tm, d), lambda i: (i, 0))],
        out_specs=pl.BlockSpec((tm, d), lambda i: (i, 0)),
        compiler_params=pltpu.CompilerParams(
            dimension_semantics=("parallel",), vmem_limit_bytes=VMEM_LIMIT),
        name="out_proj",
    )(mg, mf, w_g, w_f, post_w, x2d)


def _tile(total, want):
    t = min(total, want)
    while total % t:
        t //= 2
    return t


def _layer(x, meta_pad, pre_w, w_in, conv_w, a_log, dt_bias, gdn_norm_w, fox_q_w, fox_k_w,
           fox_f_bias, w_out, post_w):
    b, seq, d = x.shape
    assert seq % (GROUP * PREP_UNROLL) == 0 and seq % FOX_TK == 0 and seq % ROW_BLOCK == 0
    gw, fw = GDN_WIDTH, FOX_WIDTH
    o_gb = 4 * gw
    o_f = o_gb + 2 * GDN_HEADS
    o_ff = o_f + 4 * fw
    w_t = w_in.T
    w_cat = jnp.concatenate(
        [w_t[:o_gb], w_t[o_f:o_ff], w_t[o_gb:o_f], w_t[o_ff:],
         jnp.zeros((GATE_WIDTH - 2 * GDN_HEADS - FOX_HEADS, d), w_in.dtype)], axis=0).astype(BF16)
    zpad = jnp.zeros((GATE_WIDTH - FORGET_LANE - FOX_HEADS,), F32)
    add_vec = jnp.concatenate([jnp.zeros((GDN_HEADS,), F32), dt_bias, fox_f_bias, zpad])[None]
    alog_vec = jnp.concatenate([jnp.zeros((GDN_HEADS,), F32), a_log,
                                jnp.zeros((FOX_HEADS,), F32), zpad])[None]

    x2d = x.reshape(b * seq, d)
    pre_w2 = pre_w[None]
    proj, gate = _in_proj(x2d, pre_w2, w_cat, _tile(b * seq, IN_PROJ_TM), IN_PROJ_TN)
    proj_m, gate_m = _in_proj(meta_pad, pre_w2, w_cat, META_ROWS, IN_PROJ_TN)
    proj = proj.reshape(MAIN_WIDTH // HEAD_DIM, b, seq, HEAD_DIM)

    col, row = _gate_prep(gate.reshape(b, seq, GATE_WIDTH), add_vec, alog_vec, False)
    col_m, row_m = _gate_prep(gate_m[None], add_vec, alog_vec, True)
    row4 = row.reshape(b, GATE_ROWS, 1, seq)
    row_m4 = row_m.reshape(1, GATE_ROWS, 1, META_ROWS)

    conv_wt = conv_w.T
    state0 = _gdn_state0(proj_m, conv_wt, col_m, row_m4)
    o_gdn = _gdn(proj, proj_m, conv_wt, row4, state0, gdn_norm_w[None])
    o_fox = _fox(proj, proj_m, col_m, row4, fox_q_w[None], fox_k_w[None])

    w_out_b = w_out.astype(BF16)
    out = _out_proj(o_gdn.reshape(GDN_HEADS, b * seq, HEAD_DIM),
                    o_fox.reshape(FOX_HEADS, b * seq, HEAD_DIM),
                    w_out_b[:gw], w_out_b[gw:], post_w[None], x2d, _tile(b * seq, OUT_PROJ_TM))
    return out.reshape(b, seq, d)


def kernel(x, meta_tokens, pre_norm_w, w_in, conv_w, a_log, dt_bias, gdn_norm_w, fox_q_norm_w,
           fox_k_norm_w, fox_f_bias, w_out, post_norm_w):
    assert pre_norm_w.shape[0] == 1, "single-layer stack"
    meta_pad = jnp.concatenate(
        [jnp.zeros((META_PAD, x.shape[-1]), x.dtype), meta_tokens.astype(x.dtype)], axis=0)
    return _layer(x, meta_pad, pre_norm_w[0], w_in[0], conv_w[0], a_log[0], dt_bias[0],
                  gdn_norm_w[0], fox_q_norm_w[0], fox_k_norm_w[0], fox_f_bias[0], w_out[0],
                  post_norm_w[0])
```

```python
import functools
import math

import jax
import jax.numpy as jnp
from jax import lax
from jax.experimental import pallas as pl
from jax.experimental.pallas import tpu as pltpu

N_META = 16
HEAD_DIM = 128
GDN_HEADS = 8
FOX_HEADS = 8
GDN_WIDTH = GDN_HEADS * HEAD_DIM
FOX_WIDTH = FOX_HEADS * HEAD_DIM
CONV_WIDTH = 4
CHUNK = 64
EPS = 1e-6

LANES = 128
SUBLANES = 8
MXU_DIM = 256
MAIN_WIDTH = 4 * GDN_WIDTH + 4 * FOX_WIDTH
GATE_WIDTH = LANES
BETA_LANE, DECAY_LANE, FORGET_LANE = 0, GDN_HEADS, 2 * GDN_HEADS
GATE_ROWS = 32
META_ROWS = CHUNK
META_PAD = META_ROWS - N_META
GROUP = MXU_DIM
ROW_BLOCK = 256
HIST_ROWS = 2 * SUBLANES
IN_PROJ_TM, IN_PROJ_TN = 1024, 2048
OUT_PROJ_TM = 512
HEAD_PAIR = 2
REC_PAIRS = 2
REC_HEADS = REC_PAIRS * HEAD_PAIR
PREP_UNROLL = 1
DEC_ROWS = SUBLANES
BLOCK_UNROLL = 4
FOX_TK = MXU_DIM
FOX_PIECE = 2 * MXU_DIM
VMEM_LIMIT = 56 * 1024 * 1024

F32 = jnp.float32
BF16 = jnp.bfloat16
NEG_BIG = -1e30
LOG2E = math.log2(math.e)


def _dot(a, b):
    return jnp.dot(a, b, preferred_element_type=F32)


def _dot_nt(a, b):
    return lax.dot_general(a, b, (((1,), (1,)), ((), ())), preferred_element_type=F32)


def _split3(x):
    hi = x.astype(BF16)
    r1 = x - hi.astype(F32)
    mid = r1.astype(BF16)
    lo = (r1 - mid.astype(F32)).astype(BF16)
    return hi, mid, lo


def _dot_exact_rhs01(parts, m):
    return _dot(parts[0], m) + _dot(parts[1], m) + _dot(parts[2], m)


def _dot_exact_lhs01(m, parts):
    return _dot(m, parts[0]) + _dot(m, parts[1]) + _dot(m, parts[2])


def _lane_bcast(col_tile, lane):
    sel = (lax.broadcasted_iota(jnp.int32, (LANES, LANES), 0) == lane).astype(BF16)
    return _dot_exact_rhs01(_split3(col_tile), sel)


def _chunk_of(idx):
    return jnp.right_shift(idx, CHUNK.bit_length() - 1)


def _rms(x, w):
    return x * lax.rsqrt(jnp.mean(x * x, axis=-1, keepdims=True) + EPS) * w


def _silu(x):
    return x * (1.0 / (1.0 + jnp.exp(-x)))


def _softplus(x):
    return jnp.maximum(x, 0.0) + jnp.log1p(jnp.exp(-jnp.abs(x)))


def _in_proj_kernel(x_ref, nw_ref, w_ref, wg_ref, o_ref, og_ref, xn_ref):
    @pl.when(pl.program_id(1) == 0)
    def _():
        xn = _rms(x_ref[...], nw_ref[...]).astype(BF16)
        xn_ref[...] = xn
        og_ref[...] = _dot_nt(xn, wg_ref[...])

    res = _dot_nt(xn_ref[...], w_ref[...])
    for hd in range(o_ref.shape[0]):
        o_ref[hd] = res[:, hd * HEAD_DIM:(hd + 1) * HEAD_DIM].astype(o_ref.dtype)


def _in_proj(x2d, norm_w, w_cat, tm, tn):
    m, d = x2d.shape
    n = MAIN_WIDTH
    return pl.pallas_call(
        _in_proj_kernel,
        out_shape=(jax.ShapeDtypeStruct((n // HEAD_DIM, m, HEAD_DIM), BF16),
                   jax.ShapeDtypeStruct((m, GATE_WIDTH), F32)),
        grid=(m // tm, n // tn),
        in_specs=[pl.BlockSpec((tm, d), lambda i, j: (i, 0)),
                  pl.BlockSpec((1, d), lambda i, j: (0, 0)),
                  pl.BlockSpec((tn, d), lambda i, j: (j, 0)),
                  pl.BlockSpec((GATE_WIDTH, d), lambda i, j: (MAIN_WIDTH // GATE_WIDTH, 0))],
        out_specs=(pl.BlockSpec((tn // HEAD_DIM, tm, HEAD_DIM), lambda i, j: (j, i, 0)),
                   pl.BlockSpec((tm, GATE_WIDTH), lambda i, j: (i, 0))),
        scratch_shapes=[pltpu.VMEM((tm, d), BF16)],
        compiler_params=pltpu.CompilerParams(
            dimension_semantics=("parallel", "arbitrary"), vmem_limit_bytes=VMEM_LIMIT),
        name="in_proj",
    )(x2d, norm_w, w_cat, w_cat)


def _gate_kernel(t_ref, add_ref, alog_ref, col_ref, row_ref, *, rows, is_meta):
    blk = min(rows, ROW_BLOCK)
    lane = lax.broadcasted_iota(jnp.int32, (blk, LANES), 1)
    ri = lax.broadcasted_iota(jnp.int32, (blk, blk), 0)
    ci = lax.broadcasted_iota(jnp.int32, (blk, blk), 1)
    tri = (ci <= ri).astype(BF16)
    tri_chunk = ((ci <= ri) & (_chunk_of(ri) == _chunk_of(ci))).astype(BF16)
    is_beta = lane < DECAY_LANE
    is_decay = (lane >= DECAY_LANE) & (lane < FORGET_LANE)
    is_forget = (lane >= FORGET_LANE) & (lane < FORGET_LANE + FOX_HEADS)

    carry = jnp.zeros((1, LANES), F32)
    for r in range(rows // blk):
        t = t_ref[r * blk:(r + 1) * blk, :]
        ta = t + add_ref[...]
        beta = 1.0 / (1.0 + jnp.exp(-t))
        g = -jnp.exp(alog_ref[...]) * _softplus(ta)
        logf = -_softplus(-ta)
        val = jnp.where(is_decay, g, jnp.where(is_forget, logf, 0.0))
        if is_meta:
            row = lax.broadcasted_iota(jnp.int32, (blk, LANES), 0)
            val = jnp.where(row >= META_PAD, val, 0.0)
            beta = jnp.where(row >= META_PAD, beta, 0.0)
        parts = _split3(val)
        cum_chunk = _dot_exact_lhs01(tri_chunk, parts)
        cum_all = _dot_exact_lhs01(tri, parts) + carry
        carry = cum_all[blk - 1:blk, :]
        if is_meta:
            cum_all = cum_all - carry
        res = jnp.where(is_beta, beta, jnp.where(is_decay, cum_chunk, cum_all))
        col_ref[r * blk:(r + 1) * blk, :] = res
        row_ref[:, r * blk:(r + 1) * blk] = res.T[:GATE_ROWS, :]


def _gate_prep(gate3d, add_vec, alog_vec, is_meta):
    b, rows, _ = gate3d.shape
    kern = functools.partial(_gate_kernel, rows=rows, is_meta=is_meta)
    return pl.pallas_call(
        kern,
        out_shape=(jax.ShapeDtypeStruct((b, rows, LANES), F32),
                   jax.ShapeDtypeStruct((b, GATE_ROWS, rows), F32)),
        grid=(b,),
        in_specs=[pl.BlockSpec((None, rows, LANES), lambda i: (i, 0, 0)),
                  pl.BlockSpec((1, LANES), lambda i: (0, 0)),
                  pl.BlockSpec((1, LANES), lambda i: (0, 0))],
        out_specs=(pl.BlockSpec((None, rows, LANES), lambda i: (i, 0, 0)),
                   pl.BlockSpec((None, GATE_ROWS, rows), lambda i: (i, 0, 0))),
        compiler_params=pltpu.CompilerParams(
            dimension_semantics=("parallel",), vmem_limit_bytes=VMEM_LIMIT),
        name="gate_prep_meta" if is_meta else "gate_prep",
    )(gate3d, add_vec, alog_vec)


def _conv_silu(load_rows, w):
    y = load_rows(0) * w[0:1, :]
    for j in range(1, CONV_WIDTH):
        y = y + load_rows(j) * w[j:j + 1, :]
    return _silu(y)


def _l2norm(x):
    return x * lax.rsqrt(jnp.sum(x * x, axis=-1, keepdims=True) + EPS)


def _gdn_pointwise(q, k, v, beta_b, g_b):
    r = q.shape[0]
    q = _l2norm(q) * (HEAD_DIM ** -0.5)
    k = _l2norm(k)
    g3 = g_b.reshape(r // CHUNK, CHUNK, LANES)
    g_last = jnp.broadcast_to(g3[:, CHUNK - 1:CHUNK, :], g3.shape).reshape(r, LANES)
    e_g = jnp.exp(g_b)
    q_dec = q * e_g
    k_dec = k * jnp.exp(g_last - g_b)
    y = jnp.concatenate([v * beta_b, k * (beta_b * e_g)], axis=1)
    return q, k, q_dec, k_dec, y, jnp.exp(g_last)


def _gdn_groups(probs, fillers=()):
    r = probs[0][0].shape[0]
    ri = lax.broadcasted_iota(jnp.int32, (r, r), 0)
    ci = lax.broadcasted_iota(jnp.int32, (r, r), 1)
    same = _chunk_of(ri) == _chunk_of(ci)
    causal = same & (ci <= ri)
    strict = same & (ci < ri)

    def widen(t):
        return jnp.concatenate([t] * (r // LANES), axis=1) if r >= LANES else t[:, :r]

    fillers = list(fillers)
    n_ticks = 2 * (CHUNK.bit_length() - 1)
    per_tick = -(-len(fillers) // n_ticks)

    def tick():
        for thunk in fillers[:per_tick]:
            thunk()
        del fillers[:per_tick]

    kks = [_dot_nt(k, k) for _, k, _, _, _, _ in probs]
    qks = [_dot_nt(q, k) for q, k, _, _, _, _ in probs]
    tick()
    dmats = [jnp.where(causal, jnp.exp(jnp.where(causal, widen(g_b) - g_row, 0.0)), 0.0)
             for _, _, _, _, g_b, g_row in probs]
    xs = [jnp.where(strict, widen(p[3]) * kk * d, 0.0).astype(BF16)
          for p, kk, d in zip(probs, kks, dmats)]
    a_qks = [qk * d for qk, d in zip(qks, dmats)]
    zs = [p[2].astype(F32) - _dot(x, p[2]) for p, x in zip(probs, xs)]
    tick()
    span = 2
    while span < CHUNK:
        xs = [_dot(x, x).astype(BF16) for x in xs]
        tick()
        zs = [z + _dot(x, z.astype(BF16)) for x, z in zip(xs, zs)]
        tick()
        span *= 2
    return list(zip(zs, a_qks))


def _gdn_state0_kernel(km_ref, vm_ref, wk_ref, wv_ref, colm_ref, growm_ref, s_ref, pad_ref):
    h = pl.program_id(0)
    pad_ref[0:SUBLANES, :] = jnp.zeros((SUBLANES, LANES), F32)
    beta_m = _lane_bcast(colm_ref[...], BETA_LANE + h)
    g_m = _lane_bcast(colm_ref[...], DECAY_LANE + h)
    conv = []
    for src, w_ref in ((km_ref, wk_ref), (vm_ref, wv_ref)):
        pad_ref[SUBLANES:SUBLANES + META_ROWS, :] = src[...].astype(F32)
        conv.append(_conv_silu(
            lambda j: pad_ref[SUBLANES - (CONV_WIDTH - 1) + j:
                              SUBLANES - (CONV_WIDTH - 1) + j + META_ROWS, :], w_ref[...]))
    _, k_m, _, kd_m, y_m, _ = _gdn_pointwise(conv[0], conv[0], conv[1], beta_m, g_m)
    k_bf = k_m.astype(BF16)
    (uw_m, _), = _gdn_groups([(k_bf, k_bf, y_m.astype(BF16), beta_m, g_m, growm_ref[...])])
    s_ref[...] = _dot(kd_m.T.astype(BF16), uw_m[:, :HEAD_DIM].astype(BF16))


def _gdn_state0(proj_m, conv_wt, col_m, row_m4):
    hb = GDN_HEADS

    def meta_block(off):
        return pl.BlockSpec((None, META_ROWS, HEAD_DIM), lambda j, off=off: (off + j, 0, 0))

    def conv_block(off):
        return pl.BlockSpec((CONV_WIDTH, HEAD_DIM), lambda j, off=off: (0, off + j))

    return pl.pallas_call(
        _gdn_state0_kernel,
        out_shape=jax.ShapeDtypeStruct((GDN_HEADS, HEAD_DIM, HEAD_DIM), F32),
        grid=(GDN_HEADS,),
        in_specs=[meta_block(hb), meta_block(2 * hb), conv_block(hb), conv_block(2 * hb),
                  pl.BlockSpec((None, META_ROWS, LANES), lambda j: (0, 0, 0)),
                  pl.BlockSpec((None, None, 1, META_ROWS), lambda j: (0, DECAY_LANE + j, 0, 0))],
        out_specs=pl.BlockSpec((None, HEAD_DIM, HEAD_DIM), lambda j: (j, 0, 0)),
        scratch_shapes=[pltpu.VMEM((META_ROWS + SUBLANES, HEAD_DIM), F32)],
        compiler_params=pltpu.CompilerParams(
            dimension_semantics=("parallel",), vmem_limit_bytes=VMEM_LIMIT),
        name="gdn_state0",
    )(proj_m, proj_m, conv_wt, conv_wt, col_m, row_m4)


def _gdn_kernel(*refs, seq):
    n_qkv = 3 * HEAD_PAIR
    src_refs = [refs[t * HEAD_PAIR:(t + 1) * HEAD_PAIR] for t in range(3)]
    z_ref = refs[n_qkv]
    meta_refs = [refs[n_qkv + 1 + t * HEAD_PAIR:n_qkv + 1 + (t + 1) * HEAD_PAIR] for t in range(3)]
    tap_refs = [refs[2 * n_qkv + 1 + t * HEAD_PAIR:2 * n_qkv + 1 + (t + 1) * HEAD_PAIR]
                for t in range(3)]
    (brow_ref, grow_ref, s0_ref, nw_ref, o_ref, pad_ref, qs_ref, ks_ref, y_ref, bb_ref, gb_ref,
     qd_ref, kdt_ref, dec_ref, u_ref, w_ref, aqk_ref, st_ref,
     qd_stage, kdt_stage, dec_stage) = refs[3 * n_qkv + 1:]
    pair = pl.program_id(1)
    slot = pair % REC_PAIRS
    n_blocks = seq // ROW_BLOCK
    n_groups = seq // GROUP
    cpg = GROUP // CHUNK
    hist = CONV_WIDTH - 1

    def pointwise_head(i, bank, hh):
        r0 = pl.multiple_of(i * ROW_BLOCK, ROW_BLOCK)
        rows = pl.ds(r0, ROW_BLOCK)
        convs = []
        for t in range(3):
            win = pad_ref.at[(bank * 3 + t) * HEAD_PAIR + hh]
            if isinstance(i, int) and i == 0:
                past = meta_refs[t][hh][META_ROWS - HIST_ROWS:META_ROWS, :]
            else:
                past = src_refs[t][hh][pl.ds(pl.multiple_of(r0 - HIST_ROWS, HIST_ROWS), HIST_ROWS), :]
            win[0:HIST_ROWS, :] = past.astype(F32)
            win[HIST_ROWS:HIST_ROWS + ROW_BLOCK, :] = src_refs[t][hh][rows, :].astype(F32)
            convs.append(_conv_silu(
                lambda j, win=win: win[HIST_ROWS - hist + j:HIST_ROWS - hist + j + ROW_BLOCK, :],
                tap_refs[t][hh][...]))
        beta_b = jnp.broadcast_to(brow_ref[hh, :, rows], (LANES, ROW_BLOCK)).T
        g_b = jnp.broadcast_to(grow_ref[hh, :, rows], (LANES, ROW_BLOCK)).T
        q, k, q_dec, k_dec, y, dec = _gdn_pointwise(convs[0], convs[1], convs[2], beta_b, g_b)
        qs_ref[hh, rows, :] = q.astype(BF16)
        ks_ref[hh, rows, :] = k.astype(BF16)
        y_ref[hh, rows, :] = y.astype(BF16)
        bb_ref[hh, rows, :] = beta_b
        gb_ref[hh, rows, :] = g_b
        qd_stage[hh, rows, :] = q_dec.astype(BF16)
        kdt_stage[hh, :, rows] = k_dec.T.astype(BF16)
        n_chunks = ROW_BLOCK // CHUNK
        dec_rows = [dec[c * CHUNK:c * CHUNK + 1, :] for c in range(n_chunks)]
        dec_rows.append(jnp.zeros((DEC_ROWS - n_chunks, LANES), F32))
        dec_stage[hh, pl.ds(pl.multiple_of(i * DEC_ROWS, DEC_ROWS), DEC_ROWS), :] = (
            jnp.concatenate(dec_rows, axis=0))

    def pointwise_thunks(i, bank):
        return [functools.partial(pointwise_head, i, bank, hh) for hh in range(HEAD_PAIR)]

    for i in range(PREP_UNROLL):
        for thunk in pointwise_thunks(i, i):
            thunk()

    def prep_groups(gi, fillers):
        groups = [gi * PREP_UNROLL + u for u in range(PREP_UNROLL) for _ in range(HEAD_PAIR)]
        keys = [(hh, pl.ds(pl.multiple_of((gi * PREP_UNROLL + u) * GROUP, GROUP), GROUP))
                for u in range(PREP_UNROLL) for hh in range(HEAD_PAIR)]
        probs = [(qs_ref[hh, rows, :], ks_ref[hh, rows, :], y_ref[hh, rows, :],
                  bb_ref[hh, rows, :], gb_ref[hh, rows, :], grow_ref[hh, :, rows])
                 for hh, rows in keys]
        for g, (hh, rows), (uw, a_qk) in zip(groups, keys, _gdn_groups(probs, fillers)):
            hs = slot * HEAD_PAIR + hh
            u_ref[hs, rows, :] = uw[:, :HEAD_DIM]
            w_ref[hs, rows, :] = uw[:, HEAD_DIM:].astype(BF16)
            aqk_ref[hs, rows, :] = a_qk.astype(BF16)
            drows = pl.ds(pl.multiple_of(g * DEC_ROWS, DEC_ROWS), DEC_ROWS)
            qd_ref[hs, rows, :] = qd_stage[hh, rows, :]
            kdt_ref[hs, :, rows] = kdt_stage[hh, :, rows]
            dec_ref[hs, drows, :] = dec_stage[hh, drows, :]

    def next_pointwise(gi):
        return [th for u in range(PREP_UNROLL)
                for th in pointwise_thunks((gi + 1) * PREP_UNROLL + u, u)]

    def rec_thunks(gi):
        r0 = pl.multiple_of(gi * GROUP, GROUP)
        rows = pl.ds(r0, GROUP)
        heads = range(REC_HEADS)
        outs = [[] for _ in heads]

        held = {}

        def read_state(c):
            crow = pl.ds(r0 + c * CHUNK, CHUNK)
            states = [st_ref[h] for h in heads]
            s_bf = [s.astype(BF16) for s in states]
            ws = [_dot(jnp.concatenate([w_ref[h, crow, :], qd_ref[h, crow, :]], axis=0), s_bf[h])
                  for h in heads]
            held[c] = (states, ws)

        def write_state(c):
            crow = pl.ds(r0 + c * CHUNK, CHUNK)
            states, ws = held.pop(c)
            v_new = [u_ref[h, crow, :] - ws[h][:CHUNK, :] for h in heads]
            zero = lambda n: jnp.zeros((n * CHUNK, HEAD_DIM), BF16)
            v_pad = [jnp.concatenate(([zero(c)] if c else []) + [v.astype(BF16)]
                                     + ([zero(cpg - 1 - c)] if c < cpg - 1 else []), axis=0)
                     for v in v_new]
            upd = [_dot(jnp.concatenate([aqk_ref[h, crow, :], kdt_ref[h, :, rows]], axis=0),
                        v_pad[h]) for h in heads]
            for h in heads:
                dec = dec_ref[h, pl.ds(gi * DEC_ROWS + c, 1), :]
                st_ref[h] = states[h] * dec + upd[h][CHUNK:, :]
                outs[h].append(ws[h][CHUNK:, :] + upd[h][:CHUNK, :])
            if c == cpg - 1:
                for h in heads:
                    o = _rms(jnp.concatenate(outs[h], axis=0), nw_ref[...])
                    o_ref[h, rows, :] = (o * _silu(z_ref[h, rows, :].astype(F32))).astype(BF16)

        return [functools.partial(half, c) for c in range(cpg) for half in (read_state, write_state)]

    def interleave(a, b):
        if not a or not b:
            return list(a) + list(b)
        out, j = [], 0
        for i, th in enumerate(a):
            out.append(th)
            while j < len(b) and (j + 1) * len(a) <= (i + 1) * len(b):
                out.append(b[j])
                j += 1
        return out + list(b[j:])

    n_trips = n_groups // PREP_UNROLL

    @pl.when(slot != REC_PAIRS - 1)
    def _():
        def prep_body(gi, _):
            prep_groups(gi, next_pointwise(gi))
            return 0
        lax.fori_loop(0, n_trips - 1, prep_body, 0)
        prep_groups(n_trips - 1, [])

    @pl.when(slot == REC_PAIRS - 1)
    def _():
        st_ref[...] = s0_ref[...]

        def rec_of_trip(t):
            return [th for u in range(PREP_UNROLL) for th in rec_thunks(t * PREP_UNROLL + u)]

        prep_groups(0, next_pointwise(0))

        def prep_rec_body(gi, _):
            prep_groups(gi, interleave(rec_of_trip(gi - 1), next_pointwise(gi)))
            return 0
        lax.fori_loop(1, n_trips - 1, prep_rec_body, 0)
        prep_groups(n_trips - 1, rec_of_trip(n_trips - 2))
        for thunk in rec_of_trip(n_trips - 1):
            thunk()


def _gdn(proj, proj_m, conv_wt, row4, state0, norm_w):
    _, b, seq, _ = proj.shape
    n_pairs = GDN_HEADS // HEAD_PAIR

    def head_specs(shape, imap):
        return [pl.BlockSpec(shape, functools.partial(imap, t * GDN_HEADS + hh))
                for t in range(3) for hh in range(HEAD_PAIR)]

    src_specs = head_specs((None, None, seq, HEAD_DIM),
                           lambda off, i, j: (off + HEAD_PAIR * j, i, 0, 0))
    meta_specs = head_specs((None, META_ROWS, HEAD_DIM), lambda off, i, j: (off + HEAD_PAIR * j, 0, 0))
    tap_specs = head_specs((CONV_WIDTH, HEAD_DIM), lambda off, i, j: (0, off + HEAD_PAIR * j))
    n_qkv = 3 * HEAD_PAIR
    z_block0 = 3 * GDN_HEADS // REC_HEADS

    per_pair = lambda width, dt: pltpu.VMEM((HEAD_PAIR, seq, width), dt)
    per_rec = lambda width, dt: pltpu.VMEM((REC_HEADS, seq, width), dt)
    return pl.pallas_call(
        functools.partial(_gdn_kernel, seq=seq),
        out_shape=jax.ShapeDtypeStruct((GDN_HEADS, b, seq, HEAD_DIM), BF16),
        grid=(b, n_pairs),
        in_specs=src_specs
        + [pl.BlockSpec((REC_HEADS, None, seq, HEAD_DIM),
                        lambda i, j: (z_block0 + j // REC_PAIRS, i, 0, 0))]
        + meta_specs + tap_specs
        + [pl.BlockSpec((None, HEAD_PAIR, 1, seq),
                        lambda i, j: (i, BETA_LANE // HEAD_PAIR + j, 0, 0)),
           pl.BlockSpec((None, HEAD_PAIR, 1, seq),
                        lambda i, j: (i, DECAY_LANE // HEAD_PAIR + j, 0, 0)),
           pl.BlockSpec((REC_HEADS, HEAD_DIM, HEAD_DIM), lambda i, j: (j // REC_PAIRS, 0, 0)),
           pl.BlockSpec((1, HEAD_DIM), lambda i, j: (0, 0))],
        out_specs=pl.BlockSpec((REC_HEADS, None, seq, HEAD_DIM),
                               lambda i, j: (j // REC_PAIRS, i, 0, 0)),
        scratch_shapes=[pltpu.VMEM((PREP_UNROLL * n_qkv, HIST_ROWS + ROW_BLOCK, HEAD_DIM), F32),
                        per_pair(HEAD_DIM, BF16), per_pair(HEAD_DIM, BF16),
                        per_pair(2 * HEAD_DIM, BF16),
                        per_pair(HEAD_DIM, F32), per_pair(HEAD_DIM, F32),
                        per_rec(HEAD_DIM, BF16),
                        pltpu.VMEM((REC_HEADS, HEAD_DIM, seq), BF16),
                        pltpu.VMEM((REC_HEADS, seq // GROUP * DEC_ROWS, HEAD_DIM), F32),
                        per_rec(HEAD_DIM, F32), per_rec(HEAD_DIM, BF16), per_rec(GROUP, BF16),
                        pltpu.VMEM((REC_HEADS, HEAD_DIM, HEAD_DIM), F32),
                        per_pair(HEAD_DIM, BF16),
                        pltpu.VMEM((HEAD_PAIR, HEAD_DIM, seq), BF16),
                        pltpu.VMEM((HEAD_PAIR, seq // GROUP * DEC_ROWS, HEAD_DIM), F32)],
        compiler_params=pltpu.CompilerParams(
            dimension_semantics=("parallel", "arbitrary"), vmem_limit_bytes=VMEM_LIMIT),
        name="gdn",
    )(*([proj] * n_qkv), proj, *([proj_m] * n_qkv), *([conv_wt] * n_qkv),
      row4, row4, state0, norm_w)


def _fox_kernel(q_ref, k_ref, v_ref, g_ref, km_ref, vm_ref, colm_ref, crow_ref,
                qw_ref, kw_ref, o_ref, qt_ref, ka_ref, kam_ref, vt_ref, vtm_ref,
                m_ref, l_ref, acc_ref, *, seq):
    h = pl.program_id(1)
    n_blocks = seq // ROW_BLOCK
    aug_r = lax.broadcasted_iota(jnp.int32, (LANES, ROW_BLOCK), 0)

    def key_aug(ck, valid=None):
        hi, mid, lo = _split3(ck)
        lane = lax.broadcasted_iota(jnp.int32, ck.shape, 1)
        neg_hi = -hi.astype(F32)
        if valid is not None:
            neg_hi = jnp.where(valid, neg_hi, NEG_BIG)
        blk = jnp.where(lane < 3, 1.0,
                        jnp.where(lane == 3, neg_hi,
                                  jnp.where(lane == 4, -mid.astype(F32),
                                            jnp.where(lane == 5, -lo.astype(F32), 0.0))))
        return blk.astype(BF16)

    def pro_body(i, _):
        r0 = pl.multiple_of(i * ROW_BLOCK, ROW_BLOCK)
        rows = pl.ds(r0, ROW_BLOCK)
        qn = _rms(q_ref[rows, :].astype(F32), qw_ref[...]) * (HEAD_DIM ** -0.5 * LOG2E)
        qt_ref[0:HEAD_DIM, rows] = qn.T.astype(BF16)
        hi, mid, lo = _split3(crow_ref[:, rows] * LOG2E)
        aug = jnp.where(aug_r == 0, hi.astype(F32),
                        jnp.where(aug_r == 1, mid.astype(F32),
                                  jnp.where(aug_r == 2, lo.astype(F32),
                                            jnp.where(aug_r < 6, 1.0, 0.0))))
        qt_ref[HEAD_DIM:2 * HEAD_DIM, rows] = aug.astype(BF16)
        ka_ref[rows, 0:HEAD_DIM] = _rms(k_ref[rows, :].astype(F32), kw_ref[...]).astype(BF16)
        aug_k = jnp.where(aug_r < 3, 1.0,
                          jnp.where(aug_r == 3, -hi.astype(F32),
                                    jnp.where(aug_r == 4, -mid.astype(F32),
                                              jnp.where(aug_r == 5, -lo.astype(F32), 0.0))))
        ka_ref[rows, HEAD_DIM:2 * HEAD_DIM] = aug_k.T.astype(BF16)
        vt_ref[:, rows] = v_ref[rows, :].astype(F32).T.astype(BF16)
        return 0
    lax.fori_loop(0, n_blocks, pro_body, 0, unroll=BLOCK_UNROLL)

    kam_ref[:, 0:HEAD_DIM] = _rms(km_ref[...].astype(F32), kw_ref[...]).astype(BF16)
    ck_m = _lane_bcast(colm_ref[...], FORGET_LANE + h) * LOG2E
    mrow = lax.broadcasted_iota(jnp.int32, (META_ROWS, LANES), 0)
    kam_ref[:, HEAD_DIM:2 * HEAD_DIM] = key_aug(ck_m, mrow >= META_PAD)
    vtm_ref[...] = vm_ref[...].astype(F32).T.astype(BF16)

    kidx = lax.broadcasted_iota(jnp.int32, (FOX_TK, FOX_TK), 0)
    qidx = lax.broadcasted_iota(jnp.int32, (FOX_TK, FOX_TK), 1)
    diag_ok = kidx <= qidx

    items = []
    for kj in range(-1, seq // FOX_TK):
        lane0 = max(kj, 0) * FOX_TK
        for p0 in range(lane0, seq, FOX_PIECE):
            items.append((kj, slice(p0, min(p0 + FOX_PIECE, seq)), p0 == lane0))

    def scores(item):
        kj, ln, leads = item
        k_aug = kam_ref[...] if kj < 0 else ka_ref[kj * FOX_TK:(kj + 1) * FOX_TK, :]
        s = _dot(k_aug, qt_ref[:, ln])
        if kj >= 0 and leads:
            masked = jnp.where(diag_ok, s[:, :FOX_TK], NEG_BIG)
            s = jnp.concatenate([masked, s[:, FOX_TK:]], axis=1) if s.shape[1] > FOX_TK else masked
        return s

    def softmax_stats(item, s):
        kj, ln, _ = item
        top = jnp.max(s, axis=0, keepdims=True)
        if kj < 0:
            m_new, alpha = top, None
        else:
            m_old = m_ref[:, ln]
            m_new = jnp.maximum(m_old, top)
            alpha = jnp.exp2(m_old - m_new)
        p = jnp.exp2(s - m_new)
        psum = jnp.sum(p, axis=0, keepdims=True)
        m_ref[:, ln] = m_new
        l_ref[:, ln] = psum if kj < 0 else alpha * l_ref[:, ln] + psum
        return p.astype(BF16), alpha

    def values(item, p, alpha):
        kj, ln, _ = item
        v_t = vtm_ref[...] if kj < 0 else vt_ref[:, kj * FOX_TK:(kj + 1) * FOX_TK]
        pv = _dot(v_t, p)
        acc_ref[:, ln] = pv if kj < 0 else alpha * acc_ref[:, ln] + pv

    s_cur = scores(items[0])
    p_cur = None
    for t in range(len(items) + 1):
        s_next = scores(items[t + 1]) if t + 1 < len(items) else None
        p_next = softmax_stats(items[t], s_cur) if t < len(items) else None
        if p_cur is not None:
            values(items[t - 1], *p_cur)
        s_cur, p_cur = s_next, p_next

    def out_body(i, _):
        r0 = pl.multiple_of(i * ROW_BLOCK, ROW_BLOCK)
        rows = pl.ds(r0, ROW_BLOCK)
        out_t = acc_ref[:, rows] * (1.0 / l_ref[:, rows])
        o_ref[rows, :] = (out_t.T * _silu(g_ref[rows, :].astype(F32))).astype(BF16)
        return 0
    lax.fori_loop(0, n_blocks, out_body, 0, unroll=BLOCK_UNROLL)


def _fox(proj, proj_m, col_m, row4, q_w, k_w):
    _, b, seq, _ = proj.shape
    base = 4 * GDN_HEADS
    hb = FOX_HEADS

    def head_block(off):
        return pl.BlockSpec((None, None, seq, HEAD_DIM), lambda i, j, off=off: (off + j, i, 0, 0))

    def meta_block(off):
        return pl.BlockSpec((None, META_ROWS, HEAD_DIM), lambda i, j, off=off: (off + j, 0, 0))

    return pl.pallas_call(
        functools.partial(_fox_kernel, seq=seq),
        out_shape=jax.ShapeDtypeStruct((FOX_HEADS, b, seq, HEAD_DIM), BF16),
        grid=(b, FOX_HEADS),
        in_specs=[head_block(base), head_block(base + hb), head_block(base + 2 * hb),
                  head_block(base + 3 * hb),
                  meta_block(base + hb), meta_block(base + 2 * hb),
                  pl.BlockSpec((None, META_ROWS, LANES), lambda i, j: (0, 0, 0)),
                  pl.BlockSpec((None, None, 1, seq), lambda i, j: (i, FORGET_LANE + j, 0, 0)),
                  pl.BlockSpec((1, HEAD_DIM), lambda i, j: (0, 0)),
                  pl.BlockSpec((1, HEAD_DIM), lambda i, j: (0, 0))],
        out_specs=pl.BlockSpec((None, None, seq, HEAD_DIM), lambda i, j: (j, i, 0, 0)),
        scratch_shapes=[pltpu.VMEM((2 * HEAD_DIM, seq), BF16),
                        pltpu.VMEM((seq, 2 * HEAD_DIM), BF16),
                        pltpu.VMEM((META_ROWS, 2 * HEAD_DIM), BF16),
                        pltpu.VMEM((HEAD_DIM, seq), BF16),
                        pltpu.VMEM((HEAD_DIM, META_ROWS), BF16),
                        pltpu.VMEM((1, seq), F32), pltpu.VMEM((1, seq), F32),
                        pltpu.VMEM((HEAD_DIM, seq), F32)],
        compiler_params=pltpu.CompilerParams(
            dimension_semantics=("parallel", "arbitrary"), vmem_limit_bytes=VMEM_LIMIT),
        name="fox",
    )(proj, proj, proj, proj, proj_m, proj_m, col_m, row4, q_w, k_w)


def _out_proj_kernel(mg_ref, mf_ref, wg_ref, wf_ref, pw_ref, x_ref, o_ref):
    def rows_of(m_ref):
        return jnp.concatenate([m_ref[h] for h in range(m_ref.shape[0])], axis=1)
    out = _dot(rows_of(mg_ref), wg_ref[...]) + _dot(rows_of(mf_ref), wf_ref[...])
    o_ref[...] = x_ref[...] + _rms(out, pw_ref[...])


def _out_proj(mg, mf, w_g, w_f, post_w, x2d, tm):
    m, d = x2d.shape
    return pl.pallas_call(
        _out_proj_kernel,
        out_shape=jax.ShapeDtypeStruct((m, d), F32),
        grid=(m // tm,),
        in_specs=[pl.BlockSpec((GDN_HEADS, tm, HEAD_DIM), lambda i: (0, i, 0)),
                  pl.BlockSpec((FOX_HEADS, tm, HEAD_DIM), lambda i: (0, i, 0)),
                  pl.BlockSpec((GDN_WIDTH, d), lambda i: (0, 0)),
                  pl.BlockSpec((FOX_WIDTH, d), lambda i: (0, 0)),
                  pl.BlockSpec((1, d), lambda i: (0, 0)),
                  pl.BlockSpec((tm, d), lambda i: (i, 0))],
        out_specs=pl.BlockSpec((tm, d), lambda i: (i, 0)),
        compiler_params=pltpu.CompilerParams(
            dimension_semantics=("parallel",), vmem_limit_bytes=VMEM_LIMIT),
        name="out_proj",
    )(mg, mf, w_g, w_f, post_w, x2d)


def _tile(total, want):
    t = min(total, want)
    while total % t:
        t //= 2
    return t


def _layer(x, meta_pad, pre_w, w_in, conv_w, a_log, dt_bias, gdn_norm_w, fox_q_w, fox_k_w,
           fox_f_bias, w_out, post_w):
    b, seq, d = x.shape
    assert seq % (GROUP * PREP_UNROLL) == 0 and seq % FOX_TK == 0 and seq % ROW_BLOCK == 0
    gw, fw = GDN_WIDTH, FOX_WIDTH
    o_gb = 4 * gw
    o_f = o_gb + 2 * GDN_HEADS
    o_ff = o_f + 4 * fw
    w_t = w_in.T
    w_cat = jnp.concatenate(
        [w_t[:o_gb], w_t[o_f:o_ff], w_t[o_gb:o_f], w_t[o_ff:],
         jnp.zeros((GATE_WIDTH - 2 * GDN_HEADS - FOX_HEADS, d), w_in.dtype)], axis=0).astype(BF16)
    zpad = jnp.zeros((GATE_WIDTH - FORGET_LANE - FOX_HEADS,), F32)
    add_vec = jnp.concatenate([jnp.zeros((GDN_HEADS,), F32), dt_bias, fox_f_bias, zpad])[None]
    alog_vec = jnp.concatenate([jnp.zeros((GDN_HEADS,), F32), a_log,
                                jnp.zeros((FOX_HEADS,), F32), zpad])[None]

    x2d = x.reshape(b * seq, d)
    pre_w2 = pre_w[None]
    proj, gate = _in_proj(x2d, pre_w2, w_cat, _tile(b * seq, IN_PROJ_TM), IN_PROJ_TN)
    proj_m, gate_m = _in_proj(meta_pad, pre_w2, w_cat, META_ROWS, IN_PROJ_TN)
    proj = proj.reshape(MAIN_WIDTH // HEAD_DIM, b, seq, HEAD_DIM)

    col, row = _gate_prep(gate.reshape(b, seq, GATE_WIDTH), add_vec, alog_vec, False)
    col_m, row_m = _gate_prep(gate_m[None], add_vec, alog_vec, True)
    row4 = row.reshape(b, GATE_ROWS, 1, seq)
    row_m4 = row_m.reshape(1, GATE_ROWS, 1, META_ROWS)

    conv_wt = conv_w.T
    state0 = _gdn_state0(proj_m, conv_wt, col_m, row_m4)
    o_gdn = _gdn(proj, proj_m, conv_wt, row4, state0, gdn_norm_w[None])
    o_fox = _fox(proj, proj_m, col_m, row4, fox_q_w[None], fox_k_w[None])

    w_out_b = w_out.astype(BF16)
    out = _out_proj(o_gdn.reshape(GDN_HEADS, b * seq, HEAD_DIM),
                    o_fox.reshape(FOX_HEADS, b * seq, HEAD_DIM),
                    w_out_b[:gw], w_out_b[gw:], post_w[None], x2d, _tile(b * seq, OUT_PROJ_TM))
    return out.reshape(b, seq, d)


def kernel(x, meta_tokens, pre_norm_w, w_in, conv_w, a_log, dt_bias, gdn_norm_w, fox_q_norm_w,
           fox_k_norm_w, fox_f_bias, w_out, post_norm_w):
    assert pre_norm_w.shape[0] == 1, "single-layer stack"
    meta_pad = jnp.concatenate(
        [jnp.zeros((META_PAD, x.shape[-1]), x.dtype), meta_tokens.astype(x.dtype)], axis=0)
    return _layer(x, meta_pad, pre_norm_w[0], w_in[0], conv_w[0], a_log[0], dt_bias[0],
                  gdn_norm_w[0], fox_q_norm_w[0], fox_k_norm_w[0], fox_f_bias[0], w_out[0],
                  post_norm_w[0])
```

```python
import functools
import math

import jax
import jax.numpy as jnp
from jax import lax
from jax.experimental import pallas as pl
from jax.experimental.pallas import tpu as pltpu

N_META = 16
HEAD_DIM = 128
GDN_HEADS = 8
FOX_HEADS = 8
GDN_WIDTH = GDN_HEADS * HEAD_DIM
FOX_WIDTH = FOX_HEADS * HEAD_DIM
CONV_WIDTH = 4
CHUNK = 64
EPS = 1e-6

LANES = 128
SUBLANES = 8
MXU_DIM = 256
MAIN_WIDTH = 4 * GDN_WIDTH + 4 * FOX_WIDTH
GATE_WIDTH = LANES
BETA_LANE, DECAY_LANE, FORGET_LANE = 0, GDN_HEADS, 2 * GDN_HEADS
GATE_ROWS = 32
META_ROWS = CHUNK
META_PAD = META_ROWS - N_META
GROUP = MXU_DIM
ROW_BLOCK = 256
HIST_ROWS = 2 * SUBLANES
IN_PROJ_TM, IN_PROJ_TN = 1024, 2048
OUT_PROJ_TM = 512
HEAD_PAIR = 2
REC_PAIRS = 2
REC_HEADS = REC_PAIRS * HEAD_PAIR
PREP_UNROLL = 2
DEC_ROWS = SUBLANES
BLOCK_UNROLL = 4
FOX_GROUP = 4
FOX_TK = MXU_DIM
FOX_PIECE = 2 * MXU_DIM
VMEM_LIMIT = 56 * 1024 * 1024

F32 = jnp.float32
BF16 = jnp.bfloat16
NEG_BIG = -1e30
LOG2E = math.log2(math.e)


def _dot(a, b):
    return jnp.dot(a, b, preferred_element_type=F32)


def _dot_nt(a, b):
    return lax.dot_general(a, b, (((1,), (1,)), ((), ())), preferred_element_type=F32)


def _split3(x):
    hi = x.astype(BF16)
    r1 = x - hi.astype(F32)
    mid = r1.astype(BF16)
    lo = (r1 - mid.astype(F32)).astype(BF16)
    return hi, mid, lo


def _dot_exact_rhs01(parts, m):
    return _dot(parts[0], m) + _dot(parts[1], m) + _dot(parts[2], m)


def _dot_exact_lhs01(m, parts):
    return _dot(m, parts[0]) + _dot(m, parts[1]) + _dot(m, parts[2])


def _lane_bcast(col_tile, lane):
    sel = (lax.broadcasted_iota(jnp.int32, (LANES, LANES), 0) == lane).astype(BF16)
    return _dot_exact_rhs01(_split3(col_tile), sel)


def _chunk_of(idx):
    return jnp.right_shift(idx, CHUNK.bit_length() - 1)


def _rms(x, w):
    return x * lax.rsqrt(jnp.mean(x * x, axis=-1, keepdims=True) + EPS) * w


def _silu(x):
    return x * (1.0 / (1.0 + jnp.exp(-x)))


def _softplus(x):
    return jnp.maximum(x, 0.0) + jnp.log1p(jnp.exp(-jnp.abs(x)))


def _in_proj_kernel(x_ref, nw_ref, w_ref, wg_ref, o_ref, og_ref, xn_ref):
    @pl.when(pl.program_id(1) == 0)
    def _():
        xn = _rms(x_ref[...], nw_ref[...]).astype(BF16)
        xn_ref[...] = xn
        og_ref[...] = _dot_nt(xn, wg_ref[...])

    res = _dot_nt(xn_ref[...], w_ref[...])
    for hd in range(o_ref.shape[0]):
        o_ref[hd] = res[:, hd * HEAD_DIM:(hd + 1) * HEAD_DIM].astype(o_ref.dtype)


def _in_proj(x2d, norm_w, w_cat, tm, tn):
    m, d = x2d.shape
    n = MAIN_WIDTH
    return pl.pallas_call(
        _in_proj_kernel,
        out_shape=(jax.ShapeDtypeStruct((n // HEAD_DIM, m, HEAD_DIM), BF16),
                   jax.ShapeDtypeStruct((m, GATE_WIDTH), F32)),
        grid=(m // tm, n // tn),
        in_specs=[pl.BlockSpec((tm, d), lambda i, j: (i, 0)),
                  pl.BlockSpec((1, d), lambda i, j: (0, 0)),
                  pl.BlockSpec((tn, d), lambda i, j: (j, 0)),
                  pl.BlockSpec((GATE_WIDTH, d), lambda i, j: (MAIN_WIDTH // GATE_WIDTH, 0))],
        out_specs=(pl.BlockSpec((tn // HEAD_DIM, tm, HEAD_DIM), lambda i, j: (j, i, 0)),
                   pl.BlockSpec((tm, GATE_WIDTH), lambda i, j: (i, 0))),
        scratch_shapes=[pltpu.VMEM((tm, d), BF16)],
        compiler_params=pltpu.CompilerParams(
            dimension_semantics=("parallel", "arbitrary"), vmem_limit_bytes=VMEM_LIMIT),
        name="in_proj",
    )(x2d, norm_w, w_cat, w_cat)


def _gate_kernel(t_ref, add_ref, alog_ref, col_ref, row_ref, *, rows, is_meta):
    blk = min(rows, ROW_BLOCK)
    lane = lax.broadcasted_iota(jnp.int32, (blk, LANES), 1)
    ri = lax.broadcasted_iota(jnp.int32, (blk, blk), 0)
    ci = lax.broadcasted_iota(jnp.int32, (blk, blk), 1)
    tri = (ci <= ri).astype(BF16)
    tri_chunk = ((ci <= ri) & (_chunk_of(ri) == _chunk_of(ci))).astype(BF16)
    is_beta = lane < DECAY_LANE
    is_decay = (lane >= DECAY_LANE) & (lane < FORGET_LANE)
    is_forget = (lane >= FORGET_LANE) & (lane < FORGET_LANE + FOX_HEADS)

    carry = jnp.zeros((1, LANES), F32)
    for r in range(rows // blk):
        t = t_ref[r * blk:(r + 1) * blk, :]
        ta = t + add_ref[...]
        beta = 1.0 / (1.0 + jnp.exp(-t))
        g = -jnp.exp(alog_ref[...]) * _softplus(ta)
        logf = -_softplus(-ta)
        val = jnp.where(is_decay, g, jnp.where(is_forget, logf, 0.0))
        if is_meta:
            row = lax.broadcasted_iota(jnp.int32, (blk, LANES), 0)
            val = jnp.where(row >= META_PAD, val, 0.0)
            beta = jnp.where(row >= META_PAD, beta, 0.0)
        parts = _split3(val)
        cum_chunk = _dot_exact_lhs01(tri_chunk, parts)
        cum_all = _dot_exact_lhs01(tri, parts) + carry
        carry = cum_all[blk - 1:blk, :]
        if is_meta:
            cum_all = cum_all - carry
        res = jnp.where(is_beta, beta, jnp.where(is_decay, cum_chunk, cum_all))
        col_ref[r * blk:(r + 1) * blk, :] = res
        row_ref[:, r * blk:(r + 1) * blk] = res.T[:GATE_ROWS, :]


def _gate_prep(gate3d, add_vec, alog_vec, is_meta):
    b, rows, _ = gate3d.shape
    kern = functools.partial(_gate_kernel, rows=rows, is_meta=is_meta)
    return pl.pallas_call(
        kern,
        out_shape=(jax.ShapeDtypeStruct((b, rows, LANES), F32),
                   jax.ShapeDtypeStruct((b, GATE_ROWS, rows), F32)),
        grid=(b,),
        in_specs=[pl.BlockSpec((None, rows, LANES), lambda i: (i, 0, 0)),
                  pl.BlockSpec((1, LANES), lambda i: (0, 0)),
                  pl.BlockSpec((1, LANES), lambda i: (0, 0))],
        out_specs=(pl.BlockSpec((None, rows, LANES), lambda i: (i, 0, 0)),
                   pl.BlockSpec((None, GATE_ROWS, rows), lambda i: (i, 0, 0))),
        compiler_params=pltpu.CompilerParams(
            dimension_semantics=("parallel",), vmem_limit_bytes=VMEM_LIMIT),
        name="gate_prep_meta" if is_meta else "gate_prep",
    )(gate3d, add_vec, alog_vec)


def _conv_silu(load_rows, w):
    y = load_rows(0) * w[0:1, :]
    for j in range(1, CONV_WIDTH):
        y = y + load_rows(j) * w[j:j + 1, :]
    return _silu(y)


def _l2norm(x):
    return x * lax.rsqrt(jnp.sum(x * x, axis=-1, keepdims=True) + EPS)


def _gdn_pointwise(q, k, v, beta_b, g_b):
    r = q.shape[0]
    q = _l2norm(q) * (HEAD_DIM ** -0.5)
    k = _l2norm(k)
    g3 = g_b.reshape(r // CHUNK, CHUNK, LANES)
    g_last = jnp.broadcast_to(g3[:, CHUNK - 1:CHUNK, :], g3.shape).reshape(r, LANES)
    e_g = jnp.exp(g_b)
    q_dec = q * e_g
    k_dec = k * jnp.exp(g_last - g_b)
    y = jnp.concatenate([v * beta_b, k * (beta_b * e_g)], axis=1)
    return q, k, q_dec, k_dec, y, jnp.exp(g_last)


def _gdn_groups(probs, fillers=()):
    r = probs[0][0].shape[0]
    ri = lax.broadcasted_iota(jnp.int32, (r, r), 0)
    ci = lax.broadcasted_iota(jnp.int32, (r, r), 1)
    same = _chunk_of(ri) == _chunk_of(ci)
    causal = same & (ci <= ri)
    strict = same & (ci < ri)

    def widen(t):
        return jnp.concatenate([t] * (r // LANES), axis=1) if r >= LANES else t[:, :r]

    fillers = list(fillers)
    n_ticks = CHUNK.bit_length()
    per_tick = -(-len(fillers) // n_ticks)

    def tick():
        for thunk in fillers[:per_tick]:
            thunk()
        del fillers[:per_tick]

    kks = [_dot_nt(k, k) for _, k, _, _, _, _ in probs]
    qks = [_dot_nt(q, k) for q, k, _, _, _, _ in probs]
    tick()
    dmats = [jnp.where(causal, jnp.exp(jnp.where(causal, widen(g_b) - g_row, 0.0)), 0.0)
             for _, _, _, _, g_b, g_row in probs]
    xs = [jnp.where(strict, widen(p[3]) * kk * d, 0.0).astype(BF16)
          for p, kk, d in zip(probs, kks, dmats)]
    a_qks = [qk * d for qk, d in zip(qks, dmats)]
    zs = [p[2].astype(F32) - _dot(x, p[2]) for p, x in zip(probs, xs)]
    span = 2
    while span < CHUNK:
        xs = [_dot(x, x).astype(BF16) for x in xs]
        tick()
        zs = [z + _dot(x, z.astype(BF16)) for x, z in zip(xs, zs)]
        span *= 2
    tick()
    assert not fillers
    return list(zip(zs, a_qks))


def _gdn_state0_kernel(km_ref, vm_ref, wk_ref, wv_ref, colm_ref, growm_ref, s_ref, pad_ref):
    h = pl.program_id(0)
    pad_ref[0:SUBLANES, :] = jnp.zeros((SUBLANES, LANES), F32)
    beta_m = _lane_bcast(colm_ref[...], BETA_LANE + h)
    g_m = _lane_bcast(colm_ref[...], DECAY_LANE + h)
    conv = []
    for src, w_ref in ((km_ref, wk_ref), (vm_ref, wv_ref)):
        pad_ref[SUBLANES:SUBLANES + META_ROWS, :] = src[...].astype(F32)
        conv.append(_conv_silu(
            lambda j: pad_ref[SUBLANES - (CONV_WIDTH - 1) + j:
                              SUBLANES - (CONV_WIDTH - 1) + j + META_ROWS, :], w_ref[...]))
    _, k_m, _, kd_m, y_m, _ = _gdn_pointwise(conv[0], conv[0], conv[1], beta_m, g_m)
    k_bf = k_m.astype(BF16)
    (uw_m, _), = _gdn_groups([(k_bf, k_bf, y_m.astype(BF16), beta_m, g_m, growm_ref[...])])
    s_ref[...] = _dot(kd_m.T.astype(BF16), uw_m[:, :HEAD_DIM].astype(BF16))


def _gdn_state0(proj_m, conv_wt, col_m, row_m4):
    hb = GDN_HEADS

    def meta_block(off):
        return pl.BlockSpec((None, META_ROWS, HEAD_DIM), lambda j, off=off: (off + j, 0, 0))

    def conv_block(off):
        return pl.BlockSpec((CONV_WIDTH, HEAD_DIM), lambda j, off=off: (0, off + j))

    return pl.pallas_call(
        _gdn_state0_kernel,
        out_shape=jax.ShapeDtypeStruct((GDN_HEADS, HEAD_DIM, HEAD_DIM), F32),
        grid=(GDN_HEADS,),
        in_specs=[meta_block(hb), meta_block(2 * hb), conv_block(hb), conv_block(2 * hb),
                  pl.BlockSpec((None, META_ROWS, LANES), lambda j: (0, 0, 0)),
                  pl.BlockSpec((None, None, 1, META_ROWS), lambda j: (0, DECAY_LANE + j, 0, 0))],
        out_specs=pl.BlockSpec((None, HEAD_DIM, HEAD_DIM), lambda j: (j, 0, 0)),
        scratch_shapes=[pltpu.VMEM((META_ROWS + SUBLANES, HEAD_DIM), F32)],
        compiler_params=pltpu.CompilerParams(
            dimension_semantics=("parallel",), vmem_limit_bytes=VMEM_LIMIT),
        name="gdn_state0",
    )(proj_m, proj_m, conv_wt, conv_wt, col_m, row_m4)


def _gdn_kernel(*refs, seq):
    n_qkv = 3 * HEAD_PAIR
    src_refs = [refs[t * HEAD_PAIR:(t + 1) * HEAD_PAIR] for t in range(3)]
    z_ref = refs[n_qkv]
    meta_refs = [refs[n_qkv + 1 + t * HEAD_PAIR:n_qkv + 1 + (t + 1) * HEAD_PAIR] for t in range(3)]
    tap_refs = [refs[2 * n_qkv + 1 + t * HEAD_PAIR:2 * n_qkv + 1 + (t + 1) * HEAD_PAIR]
                for t in range(3)]
    (brow_ref, grow_ref, s0_ref, nw_ref, o_ref, pad_ref, qs_ref, ks_ref, y_ref, bb_ref, gb_ref,
     qd_ref, kdt_ref, dec_ref, u_ref, w_ref, aqk_ref, st_ref,
     qd_stage, kdt_stage, dec_stage) = refs[3 * n_qkv + 1:]
    pair = pl.program_id(1)
    slot = pair % REC_PAIRS
    n_blocks = seq // ROW_BLOCK
    n_groups = seq // GROUP
    cpg = GROUP // CHUNK
    hist = CONV_WIDTH - 1

    def pointwise_head(i, bank, hh):
        r0 = pl.multiple_of(i * ROW_BLOCK, ROW_BLOCK)
        rows = pl.ds(r0, ROW_BLOCK)
        convs = []
        for t in range(3):
            win = pad_ref.at[(bank * 3 + t) * HEAD_PAIR + hh]
            if isinstance(i, int) and i == 0:
                past = meta_refs[t][hh][META_ROWS - HIST_ROWS:META_ROWS, :]
            else:
                past = src_refs[t][hh][pl.ds(pl.multiple_of(r0 - HIST_ROWS, HIST_ROWS), HIST_ROWS), :]
            win[0:HIST_ROWS, :] = past.astype(F32)
            win[HIST_ROWS:HIST_ROWS + ROW_BLOCK, :] = src_refs[t][hh][rows, :].astype(F32)
            convs.append(_conv_silu(
                lambda j, win=win: win[HIST_ROWS - hist + j:HIST_ROWS - hist + j + ROW_BLOCK, :],
                tap_refs[t][hh][...]))
        beta_b = jnp.broadcast_to(brow_ref[hh, :, rows], (LANES, ROW_BLOCK)).T
        g_b = jnp.broadcast_to(grow_ref[hh, :, rows], (LANES, ROW_BLOCK)).T
        q, k, q_dec, k_dec, y, dec = _gdn_pointwise(convs[0], convs[1], convs[2], beta_b, g_b)
        qs_ref[hh, rows, :] = q.astype(BF16)
        ks_ref[hh, rows, :] = k.astype(BF16)
        y_ref[hh, rows, :] = y.astype(BF16)
        bb_ref[hh, rows, :] = beta_b
        gb_ref[hh, rows, :] = g_b
        qd_stage[hh, rows, :] = q_dec.astype(BF16)
        kdt_stage[hh, :, rows] = k_dec.T.astype(BF16)
        n_chunks = ROW_BLOCK // CHUNK
        dec_rows = [dec[c * CHUNK:c * CHUNK + 1, :] for c in range(n_chunks)]
        dec_rows.append(jnp.zeros((DEC_ROWS - n_chunks, LANES), F32))
        dec_stage[hh, pl.ds(pl.multiple_of(i * DEC_ROWS, DEC_ROWS), DEC_ROWS), :] = (
            jnp.concatenate(dec_rows, axis=0))

    def pointwise_thunks(i, bank):
        return [functools.partial(pointwise_head, i, bank, hh) for hh in range(HEAD_PAIR)]

    for i in range(PREP_UNROLL):
        for thunk in pointwise_thunks(i, i):
            thunk()

    def prep_groups(gi, fillers):
        groups = [gi * PREP_UNROLL + u for u in range(PREP_UNROLL) for _ in range(HEAD_PAIR)]
        keys = [(hh, pl.ds(pl.multiple_of((gi * PREP_UNROLL + u) * GROUP, GROUP), GROUP))
                for u in range(PREP_UNROLL) for hh in range(HEAD_PAIR)]
        probs = [(qs_ref[hh, rows, :], ks_ref[hh, rows, :], y_ref[hh, rows, :],
                  bb_ref[hh, rows, :], gb_ref[hh, rows, :], grow_ref[hh, :, rows])
                 for hh, rows in keys]
        for g, (hh, rows), (uw, a_qk) in zip(groups, keys, _gdn_groups(probs, fillers)):
            hs = slot * HEAD_PAIR + hh
            u_ref[hs, rows, :] = uw[:, :HEAD_DIM]
            w_ref[hs, rows, :] = uw[:, HEAD_DIM:].astype(BF16)
            aqk_ref[hs, rows, :] = a_qk.astype(BF16)
            drows = pl.ds(pl.multiple_of(g * DEC_ROWS, DEC_ROWS), DEC_ROWS)
            qd_ref[hs, rows, :] = qd_stage[hh, rows, :]
            kdt_ref[hs, :, rows] = kdt_stage[hh, :, rows]
            dec_ref[hs, drows, :] = dec_stage[hh, drows, :]

    def next_pointwise(gi):
        return [th for u in range(PREP_UNROLL)
                for th in pointwise_thunks((gi + 1) * PREP_UNROLL + u, u)]

    def rec_thunks(gi):
        r0 = pl.multiple_of(gi * GROUP, GROUP)
        rows = pl.ds(r0, GROUP)
        heads = range(REC_HEADS)
        outs = [[] for _ in heads]

        held = {}

        def read_state(c):
            crow = pl.ds(r0 + c * CHUNK, CHUNK)
            states = [st_ref[h] for h in heads]
            s_bf = [s.astype(BF16) for s in states]
            ws = [_dot(jnp.concatenate([w_ref[h, crow, :], qd_ref[h, crow, :]], axis=0), s_bf[h])
                  for h in heads]
            held[c] = (states, ws)

        def write_state(c):
            crow = pl.ds(r0 + c * CHUNK, CHUNK)
            states, ws = held.pop(c)
            v_new = [u_ref[h, crow, :] - ws[h][:CHUNK, :] for h in heads]
            zero = lambda n: jnp.zeros((n * CHUNK, HEAD_DIM), BF16)
            v_pad = [jnp.concatenate(([zero(c)] if c else []) + [v.astype(BF16)]
                                     + ([zero(cpg - 1 - c)] if c < cpg - 1 else []), axis=0)
                     for v in v_new]
            upd = [_dot(jnp.concatenate([aqk_ref[h, crow, :], kdt_ref[h, :, rows]], axis=0),
                        v_pad[h]) for h in heads]
            for h in heads:
                dec = dec_ref[h, pl.ds(gi * DEC_ROWS + c, 1), :]
                st_ref[h] = states[h] * dec + upd[h][CHUNK:, :]
                outs[h].append(ws[h][CHUNK:, :] + upd[h][:CHUNK, :])
            if c == cpg - 1:
                for h in heads:
                    o = _rms(jnp.concatenate(outs[h], axis=0), nw_ref[...])
                    o_ref[h, rows, :] = (o * _silu(z_ref[h, rows, :].astype(F32))).astype(BF16)

        return [functools.partial(half, c) for c in range(cpg) for half in (read_state, write_state)]

    def interleave(a, b):
        if not a or not b:
            return list(a) + list(b)
        out, j = [], 0
        for i, th in enumerate(a):
            out.append(th)
            while j < len(b) and (j + 1) * len(a) <= (i + 1) * len(b):
                out.append(b[j])
                j += 1
        return out + list(b[j:])

    n_trips = n_groups // PREP_UNROLL

    @pl.when(slot != REC_PAIRS - 1)
    def _():
        def prep_body(gi, _):
            prep_groups(gi, next_pointwise(gi))
            return 0
        lax.fori_loop(0, n_trips - 1, prep_body, 0)
        prep_groups(n_trips - 1, [])

    @pl.when(slot == REC_PAIRS - 1)
    def _():
        st_ref[...] = s0_ref[...]

        def rec_of_trip(t):
            return [th for u in range(PREP_UNROLL) for th in rec_thunks(t * PREP_UNROLL + u)]

        prep_groups(0, next_pointwise(0))

        def prep_rec_body(gi, _):
            prep_groups(gi, interleave(rec_of_trip(gi - 1), next_pointwise(gi)))
            return 0
        lax.fori_loop(1, n_trips - 1, prep_rec_body, 0)
        prep_groups(n_trips - 1, rec_of_trip(n_trips - 2))
        for thunk in rec_of_trip(n_trips - 1):
            thunk()


def _gdn(proj, proj_m, conv_wt, row4, state0, norm_w):
    _, b, seq, _ = proj.shape
    n_pairs = GDN_HEADS // HEAD_PAIR

    def head_specs(shape, imap):
        return [pl.BlockSpec(shape, functools.partial(imap, t * GDN_HEADS + hh))
                for t in range(3) for hh in range(HEAD_PAIR)]

    src_specs = head_specs((None, None, seq, HEAD_DIM),
                           lambda off, i, j: (off + HEAD_PAIR * j, i, 0, 0))
    meta_specs = head_specs((None, META_ROWS, HEAD_DIM), lambda off, i, j: (off + HEAD_PAIR * j, 0, 0))
    tap_specs = head_specs((CONV_WIDTH, HEAD_DIM), lambda off, i, j: (0, off + HEAD_PAIR * j))
    n_qkv = 3 * HEAD_PAIR
    z_block0 = 3 * GDN_HEADS // REC_HEADS

    per_pair = lambda width, dt: pltpu.VMEM((HEAD_PAIR, seq, width), dt)
    per_rec = lambda width, dt: pltpu.VMEM((REC_HEADS, seq, width), dt)
    return pl.pallas_call(
        functools.partial(_gdn_kernel, seq=seq),
        out_shape=jax.ShapeDtypeStruct((GDN_HEADS, b, seq, HEAD_DIM), BF16),
        grid=(b, n_pairs),
        in_specs=src_specs
        + [pl.BlockSpec((REC_HEADS, None, seq, HEAD_DIM),
                        lambda i, j: (z_block0 + j // REC_PAIRS, i, 0, 0))]
        + meta_specs + tap_specs
        + [pl.BlockSpec((None, HEAD_PAIR, 1, seq),
                        lambda i, j: (i, BETA_LANE // HEAD_PAIR + j, 0, 0)),
           pl.BlockSpec((None, HEAD_PAIR, 1, seq),
                        lambda i, j: (i, DECAY_LANE // HEAD_PAIR + j, 0, 0)),
           pl.BlockSpec((REC_HEADS, HEAD_DIM, HEAD_DIM), lambda i, j: (j // REC_PAIRS, 0, 0)),
           pl.BlockSpec((1, HEAD_DIM), lambda i, j: (0, 0))],
        out_specs=pl.BlockSpec((REC_HEADS, None, seq, HEAD_DIM),
                               lambda i, j: (j // REC_PAIRS, i, 0, 0)),
        scratch_shapes=[pltpu.VMEM((PREP_UNROLL * n_qkv, HIST_ROWS + ROW_BLOCK, HEAD_DIM), F32),
                        per_pair(HEAD_DIM, BF16), per_pair(HEAD_DIM, BF16),
                        per_pair(2 * HEAD_DIM, BF16),
                        per_pair(HEAD_DIM, F32), per_pair(HEAD_DIM, F32),
                        per_rec(HEAD_DIM, BF16),
                        pltpu.VMEM((REC_HEADS, HEAD_DIM, seq), BF16),
                        pltpu.VMEM((REC_HEADS, seq // GROUP * DEC_ROWS, HEAD_DIM), F32),
                        per_rec(HEAD_DIM, F32), per_rec(HEAD_DIM, BF16), per_rec(GROUP, BF16),
                        pltpu.VMEM((REC_HEADS, HEAD_DIM, HEAD_DIM), F32),
                        per_pair(HEAD_DIM, BF16),
                        pltpu.VMEM((HEAD_PAIR, HEAD_DIM, seq), BF16),
                        pltpu.VMEM((HEAD_PAIR, seq // GROUP * DEC_ROWS, HEAD_DIM), F32)],
        compiler_params=pltpu.CompilerParams(
            dimension_semantics=("parallel", "arbitrary"), vmem_limit_bytes=VMEM_LIMIT),
        name="gdn",
    )(*([proj] * n_qkv), proj, *([proj_m] * n_qkv), *([conv_wt] * n_qkv),
      row4, row4, state0, norm_w)


def _fox_kernel(q_ref, k_ref, v_ref, g_ref, km_ref, vm_ref, colm_ref, crow_ref,
                qw_ref, kw_ref, o_ref, qt_ref, ka_ref, kam_ref, vt_ref, vtm_ref,
                m_ref, l_ref, acc_ref, *, seq):
    n_blocks = seq // ROW_BLOCK
    aug_r = lax.broadcasted_iota(jnp.int32, (LANES, ROW_BLOCK), 0)

    def key_aug(ck, valid=None):
        hi, mid, lo = _split3(ck)
        lane = lax.broadcasted_iota(jnp.int32, ck.shape, 1)
        neg_hi = -hi.astype(F32)
        if valid is not None:
            neg_hi = jnp.where(valid, neg_hi, NEG_BIG)
        blk = jnp.where(lane < 3, 1.0,
                        jnp.where(lane == 3, neg_hi,
                                  jnp.where(lane == 4, -mid.astype(F32),
                                            jnp.where(lane == 5, -lo.astype(F32), 0.0))))
        return blk.astype(BF16)

    def pro_body(i, _):
        r0 = pl.multiple_of(i * ROW_BLOCK, ROW_BLOCK)
        rows = pl.ds(r0, ROW_BLOCK)
        for hd in range(FOX_GROUP):
            qn = _rms(q_ref[hd, rows, :].astype(F32), qw_ref[...]) * (HEAD_DIM ** -0.5 * LOG2E)
            qt_ref[hd, 0:HEAD_DIM, rows] = qn.T.astype(BF16)
            hi, mid, lo = _split3(crow_ref[hd, :, rows] * LOG2E)
            aug = jnp.where(aug_r == 0, hi.astype(F32),
                            jnp.where(aug_r == 1, mid.astype(F32),
                                      jnp.where(aug_r == 2, lo.astype(F32),
                                                jnp.where(aug_r < 6, 1.0, 0.0))))
            qt_ref[hd, HEAD_DIM:2 * HEAD_DIM, rows] = aug.astype(BF16)
            ka_ref[hd, rows, 0:HEAD_DIM] = _rms(k_ref[hd, rows, :].astype(F32),
                                                kw_ref[...]).astype(BF16)
            aug_k = jnp.where(aug_r < 3, 1.0,
                              jnp.where(aug_r == 3, -hi.astype(F32),
                                        jnp.where(aug_r == 4, -mid.astype(F32),
                                                  jnp.where(aug_r == 5, -lo.astype(F32), 0.0))))
            ka_ref[hd, rows, HEAD_DIM:2 * HEAD_DIM] = aug_k.T.astype(BF16)
            vt_ref[hd, :, rows] = v_ref[hd, rows, :].astype(F32).T.astype(BF16)
        return 0
    lax.fori_loop(0, n_blocks, pro_body, 0, unroll=BLOCK_UNROLL // FOX_GROUP)

    mrow = lax.broadcasted_iota(jnp.int32, (META_ROWS, LANES), 0)
    for hd in range(FOX_GROUP):
        head = pl.program_id(1) * FOX_GROUP + hd
        kam_ref[hd, :, 0:HEAD_DIM] = _rms(km_ref[hd].astype(F32), kw_ref[...]).astype(BF16)
        ck_m = _lane_bcast(colm_ref[...], FORGET_LANE + head) * LOG2E
        kam_ref[hd, :, HEAD_DIM:2 * HEAD_DIM] = key_aug(ck_m, mrow >= META_PAD)
        vtm_ref[hd] = vm_ref[hd].astype(F32).T.astype(BF16)

    kidx = lax.broadcasted_iota(jnp.int32, (FOX_TK, FOX_TK), 0)
    qidx = lax.broadcasted_iota(jnp.int32, (FOX_TK, FOX_TK), 1)
    diag_ok = kidx <= qidx

    items = []
    for kj in range(-1, seq // FOX_TK):
        lane0 = max(kj, 0) * FOX_TK
        for p0 in range(lane0, seq, FOX_PIECE):
            for hd in range(FOX_GROUP):
                items.append((hd, kj, slice(p0, min(p0 + FOX_PIECE, seq)), p0 == lane0))

    def scores(item):
        hd, kj, ln, leads = item
        k_aug = kam_ref[hd] if kj < 0 else ka_ref[hd, kj * FOX_TK:(kj + 1) * FOX_TK, :]
        s = _dot(k_aug, qt_ref[hd, :, ln])
        if kj >= 0 and leads:
            masked = jnp.where(diag_ok, s[:, :FOX_TK], NEG_BIG)
            s = jnp.concatenate([masked, s[:, FOX_TK:]], axis=1) if s.shape[1] > FOX_TK else masked
        return s

    def softmax_stats(item, s):
        hd, kj, ln, _ = item
        top = jnp.max(s, axis=0, keepdims=True)
        if kj < 0:
            m_new, alpha = top, None
        else:
            m_old = m_ref[hd, :, ln]
            m_new = jnp.maximum(m_old, top)
            alpha = jnp.exp2(m_old - m_new)
        p = jnp.exp2(s - m_new)
        psum = jnp.sum(p, axis=0, keepdims=True)
        m_ref[hd, :, ln] = m_new
        l_ref[hd, :, ln] = psum if kj < 0 else alpha * l_ref[hd, :, ln] + psum
        return p.astype(BF16), alpha

    def values(item, p, alpha):
        hd, kj, ln, _ = item
        v_t = vtm_ref[hd] if kj < 0 else vt_ref[hd, :, kj * FOX_TK:(kj + 1) * FOX_TK]
        pv = _dot(v_t, p)
        acc_ref[hd, :, ln] = pv if kj < 0 else alpha * acc_ref[hd, :, ln] + pv

    s_cur = scores(items[0])
    p_cur = None
    for t in range(len(items) + 1):
        s_next = scores(items[t + 1]) if t + 1 < len(items) else None
        p_next = softmax_stats(items[t], s_cur) if t < len(items) else None
        if p_cur is not None:
            values(items[t - 1], *p_cur)
        s_cur, p_cur = s_next, p_next

    def out_body(i, _):
        r0 = pl.multiple_of(i * ROW_BLOCK, ROW_BLOCK)
        rows = pl.ds(r0, ROW_BLOCK)
        for hd in range(FOX_GROUP):
            out_t = acc_ref[hd, :, rows] * (1.0 / l_ref[hd, :, rows])
            o_ref[hd, rows, :] = (out_t.T * _silu(g_ref[hd, rows, :].astype(F32))).astype(BF16)
        return 0
    lax.fori_loop(0, n_blocks, out_body, 0, unroll=BLOCK_UNROLL // FOX_GROUP)


def _fox(proj, proj_m, col_m, row4, q_w, k_w):
    _, b, seq, _ = proj.shape
    base = 4 * GDN_HEADS // FOX_GROUP
    hb = FOX_HEADS // FOX_GROUP

    def head_block(off):
        return pl.BlockSpec((FOX_GROUP, None, seq, HEAD_DIM),
                            lambda i, j, off=off: (off + j, i, 0, 0))

    def meta_block(off):
        return pl.BlockSpec((FOX_GROUP, META_ROWS, HEAD_DIM), lambda i, j, off=off: (off + j, 0, 0))

    per_head = lambda shape, dt: pltpu.VMEM((FOX_GROUP,) + shape, dt)
    return pl.pallas_call(
        functools.partial(_fox_kernel, seq=seq),
        out_shape=jax.ShapeDtypeStruct((FOX_HEADS, b, seq, HEAD_DIM), BF16),
        grid=(b, FOX_HEADS // FOX_GROUP),
        in_specs=[head_block(base), head_block(base + hb), head_block(base + 2 * hb),
                  head_block(base + 3 * hb),
                  meta_block(base + hb), meta_block(base + 2 * hb),
                  pl.BlockSpec((None, META_ROWS, LANES), lambda i, j: (0, 0, 0)),
                  pl.BlockSpec((None, FOX_GROUP, 1, seq),
                               lambda i, j: (i, FORGET_LANE // FOX_GROUP + j, 0, 0)),
                  pl.BlockSpec((1, HEAD_DIM), lambda i, j: (0, 0)),
                  pl.BlockSpec((1, HEAD_DIM), lambda i, j: (0, 0))],
        out_specs=pl.BlockSpec((FOX_GROUP, None, seq, HEAD_DIM), lambda i, j: (j, i, 0, 0)),
        scratch_shapes=[per_head((2 * HEAD_DIM, seq), BF16),
                        per_head((seq, 2 * HEAD_DIM), BF16),
                        per_head((META_ROWS, 2 * HEAD_DIM), BF16),
                        per_head((HEAD_DIM, seq), BF16),
                        per_head((HEAD_DIM, META_ROWS), BF16),
                        per_head((1, seq), F32), per_head((1, seq), F32),
                        per_head((HEAD_DIM, seq), F32)],
        compiler_params=pltpu.CompilerParams(
            dimension_semantics=("parallel", "arbitrary"), vmem_limit_bytes=VMEM_LIMIT),
        name="fox",
    )(proj, proj, proj, proj, proj_m, proj_m, col_m, row4, q_w, k_w)


def _out_proj_kernel(mg_ref, mf_ref, wg_ref, wf_ref, pw_ref, x_ref, o_ref):
    def rows_of(m_ref):
        return jnp.concatenate([m_ref[h] for h in range(m_ref.shape[0])], axis=1)
    out = _dot(rows_of(mg_ref), wg_ref[...]) + _dot(rows_of(mf_ref), wf_ref[...])
    o_ref[...] = x_ref[...] + _rms(out, pw_ref[...])


def _out_proj(mg, mf, w_g, w_f, post_w, x2d, tm):
    m, d = x2d.shape
    return pl.pallas_call(
        _out_proj_kernel,
        out_shape=jax.ShapeDtypeStruct((m, d), F32),
        grid=(m // tm,),
        in_specs=[pl.BlockSpec((GDN_HEADS, tm, HEAD_DIM), lambda i: (0, i, 0)),
                  pl.BlockSpec((FOX_HEADS, tm, HEAD_DIM), lambda i: (0, i, 0)),
                  pl.BlockSpec((GDN_WIDTH, d), lambda i: (0, 0)),
                  pl.BlockSpec((FOX_WIDTH, d), lambda i: (0, 0)),
                  pl.BlockSpec((1, d), lambda i: (0, 0)),
                  pl.BlockSpec((tm, d), lambda i: (i, 0))],
        out_specs=pl.BlockSpec((tm, d), lambda i: (i, 0)),
        compiler_params=pltpu.CompilerParams(
            dimension_semantics=("parallel",), vmem_limit_bytes=VMEM_LIMIT),
        name="out_proj",
    )(mg, mf, w_g, w_f, post_w, x2d)


def _tile(total, want):
    t = min(total, want)
    while total % t:
        t //= 2
    return t


def _layer(x, meta_pad, pre_w, w_in, conv_w, a_log, dt_bias, gdn_norm_w, fox_q_w, fox_k_w,
           fox_f_bias, w_out, post_w):
    b, seq, d = x.shape
    assert seq % (GROUP * PREP_UNROLL) == 0 and seq % FOX_TK == 0 and seq % ROW_BLOCK == 0
    gw, fw = GDN_WIDTH, FOX_WIDTH
    o_gb = 4 * gw
    o_f = o_gb + 2 * GDN_HEADS
    o_ff = o_f + 4 * fw
    w_t = w_in.T
    w_cat = jnp.concatenate(
        [w_t[:o_gb], w_t[o_f:o_ff], w_t[o_gb:o_f], w_t[o_ff:],
         jnp.zeros((GATE_WIDTH - 2 * GDN_HEADS - FOX_HEADS, d), w_in.dtype)], axis=0).astype(BF16)
    zpad = jnp.zeros((GATE_WIDTH - FORGET_LANE - FOX_HEADS,), F32)
    add_vec = jnp.concatenate([jnp.zeros((GDN_HEADS,), F32), dt_bias, fox_f_bias, zpad])[None]
    alog_vec = jnp.concatenate([jnp.zeros((GDN_HEADS,), F32), a_log,
                                jnp.zeros((FOX_HEADS,), F32), zpad])[None]

    x2d = x.reshape(b * seq, d)
    pre_w2 = pre_w[None]
    proj, gate = _in_proj(x2d, pre_w2, w_cat, _tile(b * seq, IN_PROJ_TM), IN_PROJ_TN)
    proj_m, gate_m = _in_proj(meta_pad, pre_w2, w_cat, META_ROWS, IN_PROJ_TN)
    proj = proj.reshape(MAIN_WIDTH // HEAD_DIM, b, seq, HEAD_DIM)

    col, row = _gate_prep(gate.reshape(b, seq, GATE_WIDTH), add_vec, alog_vec, False)
    col_m, row_m = _gate_prep(gate_m[None], add_vec, alog_vec, True)
    row4 = row.reshape(b, GATE_ROWS, 1, seq)
    row_m4 = row_m.reshape(1, GATE_ROWS, 1, META_ROWS)

    conv_wt = conv_w.T
    state0 = _gdn_state0(proj_m, conv_wt, col_m, row_m4)
    o_gdn = _gdn(proj, proj_m, conv_wt, row4, state0, gdn_norm_w[None])
    o_fox = _fox(proj, proj_m, col_m, row4, fox_q_w[None], fox_k_w[None])

    w_out_b = w_out.astype(BF16)
    out = _out_proj(o_gdn.reshape(GDN_HEADS, b * seq, HEAD_DIM),
                    o_fox.reshape(FOX_HEADS, b * seq, HEAD_DIM),
                    w_out_b[:gw], w_out_b[gw:], post_w[None], x2d, _tile(b * seq, OUT_PROJ_TM))
    return out.reshape(b, seq, d)


def kernel(x, meta_tokens, pre_norm_w, w_in, conv_w, a_log, dt_bias, gdn_norm_w, fox_q_norm_w,
           fox_k_norm_w, fox_f_bias, w_out, post_norm_w):
    assert pre_norm_w.shape[0] == 1, "single-layer stack"
    meta_pad = jnp.concatenate(
        [jnp.zeros((META_PAD, x.shape[-1]), x.dtype), meta_tokens.astype(x.dtype)], axis=0)
    return _layer(x, meta_pad, pre_norm_w[0], w_in[0], conv_w[0], a_log[0], dt_bias[0],
                  gdn_norm_w[0], fox_q_norm_w[0], fox_k_norm_w[0], fox_f_bias[0], w_out[0],
                  post_norm_w[0])
```

```python
import functools
import math

import jax
import jax.numpy as jnp
from jax import lax
from jax.experimental import pallas as pl
from jax.experimental.pallas import tpu as pltpu

N_META = 16
HEAD_DIM = 128
GDN_HEADS = 8
FOX_HEADS = 8
GDN_WIDTH = GDN_HEADS * HEAD_DIM
FOX_WIDTH = FOX_HEADS * HEAD_DIM
CONV_WIDTH = 4
CHUNK = 64
EPS = 1e-6

LANES = 128
SUBLANES = 8
MXU_DIM = 256
MAIN_WIDTH = 4 * GDN_WIDTH + 4 * FOX_WIDTH
GATE_WIDTH = LANES
BETA_LANE, DECAY_LANE, FORGET_LANE = 0, GDN_HEADS, 2 * GDN_HEADS
GATE_ROWS = 32
META_ROWS = CHUNK
META_PAD = META_ROWS - N_META
GROUP = MXU_DIM
ROW_BLOCK = 256
HIST_ROWS = 2 * SUBLANES
IN_PROJ_TM, IN_PROJ_TN = 1024, 2048
OUT_PROJ_TM = 512
HEAD_PAIR = 2
REC_PAIRS = 2
REC_HEADS = REC_PAIRS * HEAD_PAIR
PREP_UNROLL = 2
DEC_ROWS = SUBLANES
BLOCK_UNROLL = 4
FOX_GROUP = 4
FOX_TK = MXU_DIM
FOX_PIECE = 2 * MXU_DIM
VMEM_LIMIT = 56 * 1024 * 1024

F32 = jnp.float32
BF16 = jnp.bfloat16
NEG_BIG = -1e30
LOG2E = math.log2(math.e)


def _dot(a, b):
    return jnp.dot(a, b, preferred_element_type=F32)


def _dot_nt(a, b):
    return lax.dot_general(a, b, (((1,), (1,)), ((), ())), preferred_element_type=F32)


def _split3(x):
    hi = x.astype(BF16)
    r1 = x - hi.astype(F32)
    mid = r1.astype(BF16)
    lo = (r1 - mid.astype(F32)).astype(BF16)
    return hi, mid, lo


def _dot_exact_rhs01(parts, m):
    return _dot(parts[0], m) + _dot(parts[1], m) + _dot(parts[2], m)


def _dot_exact_lhs01(m, parts):
    return _dot(m, parts[0]) + _dot(m, parts[1]) + _dot(m, parts[2])


def _lane_bcast(col_tile, lane):
    sel = (lax.broadcasted_iota(jnp.int32, (LANES, LANES), 0) == lane).astype(BF16)
    return _dot_exact_rhs01(_split3(col_tile), sel)


def _chunk_of(idx):
    return jnp.right_shift(idx, CHUNK.bit_length() - 1)


def _rms(x, w):
    return x * lax.rsqrt(jnp.mean(x * x, axis=-1, keepdims=True) + EPS) * w


def _silu(x):
    return x * (1.0 / (1.0 + jnp.exp(-x)))


def _softplus(x):
    return jnp.maximum(x, 0.0) + jnp.log1p(jnp.exp(-jnp.abs(x)))


def _in_proj_kernel(x_ref, meta_ref, nw_ref, w_ref, wg_ref, o_ref, og_ref, om_ref, ogm_ref,
                    xn_ref, xm_ref):
    i, j = pl.program_id(0), pl.program_id(1)

    @pl.when(j == 0)
    def _():
        xn = _rms(x_ref[...], nw_ref[...]).astype(BF16)
        xn_ref[...] = xn
        og_ref[...] = _dot_nt(xn, wg_ref[...])

        @pl.when(i == 0)
        def _():
            xm = _rms(meta_ref[...], nw_ref[...]).astype(BF16)
            xm_ref[...] = xm
            ogm_ref[...] = _dot_nt(xm, wg_ref[...])

        @pl.when(i != 0)
        def _():
            ogm_ref[...] = jnp.zeros(ogm_ref.shape, ogm_ref.dtype)

    def slabs(dst, res):
        for hd in range(dst.shape[0]):
            dst[hd] = res[:, hd * HEAD_DIM:(hd + 1) * HEAD_DIM].astype(dst.dtype)
    slabs(o_ref, _dot_nt(xn_ref[...], w_ref[...]))

    @pl.when(i == 0)
    def _():
        slabs(om_ref, _dot_nt(xm_ref[...], w_ref[...]))

    @pl.when(i != 0)
    def _():
        om_ref[...] = jnp.zeros(om_ref.shape, om_ref.dtype)


def _in_proj(x2d, meta_pad, norm_w, w_cat, tm, tn):
    m, d = x2d.shape
    n = MAIN_WIDTH
    row_blocks = m // tm
    proj, gate, proj_m, gate_m = pl.pallas_call(
        _in_proj_kernel,
        out_shape=(jax.ShapeDtypeStruct((n // HEAD_DIM, m, HEAD_DIM), BF16),
                   jax.ShapeDtypeStruct((m, GATE_WIDTH), F32),
                   jax.ShapeDtypeStruct((n // HEAD_DIM, row_blocks * META_ROWS, HEAD_DIM), BF16),
                   jax.ShapeDtypeStruct((row_blocks * META_ROWS, GATE_WIDTH), F32)),
        grid=(m // tm, n // tn),
        in_specs=[pl.BlockSpec((tm, d), lambda i, j: (i, 0)),
                  pl.BlockSpec((META_ROWS, d), lambda i, j: (0, 0)),
                  pl.BlockSpec((1, d), lambda i, j: (0, 0)),
                  pl.BlockSpec((tn, d), lambda i, j: (j, 0)),
                  pl.BlockSpec((GATE_WIDTH, d), lambda i, j: (MAIN_WIDTH // GATE_WIDTH, 0))],
        out_specs=(pl.BlockSpec((tn // HEAD_DIM, tm, HEAD_DIM), lambda i, j: (j, i, 0)),
                   pl.BlockSpec((tm, GATE_WIDTH), lambda i, j: (i, 0)),
                   pl.BlockSpec((tn // HEAD_DIM, META_ROWS, HEAD_DIM), lambda i, j: (j, i, 0)),
                   pl.BlockSpec((META_ROWS, GATE_WIDTH), lambda i, j: (i, 0))),
        scratch_shapes=[pltpu.VMEM((tm, d), BF16), pltpu.VMEM((META_ROWS, d), BF16)],
        compiler_params=pltpu.CompilerParams(
            dimension_semantics=("arbitrary", "arbitrary"), vmem_limit_bytes=VMEM_LIMIT),
        name="in_proj",
    )(x2d, meta_pad, norm_w, w_cat, w_cat)
    return proj, gate, proj_m[:, :META_ROWS], gate_m[:META_ROWS]


def _gate_kernel(t_ref, add_ref, alog_ref, col_ref, row_ref, *, rows, is_meta):
    blk = min(rows, ROW_BLOCK)
    lane = lax.broadcasted_iota(jnp.int32, (blk, LANES), 1)
    ri = lax.broadcasted_iota(jnp.int32, (blk, blk), 0)
    ci = lax.broadcasted_iota(jnp.int32, (blk, blk), 1)
    tri = (ci <= ri).astype(BF16)
    tri_chunk = ((ci <= ri) & (_chunk_of(ri) == _chunk_of(ci))).astype(BF16)
    is_beta = lane < DECAY_LANE
    is_decay = (lane >= DECAY_LANE) & (lane < FORGET_LANE)
    is_forget = (lane >= FORGET_LANE) & (lane < FORGET_LANE + FOX_HEADS)

    carry = jnp.zeros((1, LANES), F32)
    for r in range(rows // blk):
        t = t_ref[r * blk:(r + 1) * blk, :]
        ta = t + add_ref[...]
        beta = 1.0 / (1.0 + jnp.exp(-t))
        g = -jnp.exp(alog_ref[...]) * _softplus(ta)
        logf = -_softplus(-ta)
        val = jnp.where(is_decay, g, jnp.where(is_forget, logf, 0.0))
        if is_meta:
            row = lax.broadcasted_iota(jnp.int32, (blk, LANES), 0)
            val = jnp.where(row >= META_PAD, val, 0.0)
            beta = jnp.where(row >= META_PAD, beta, 0.0)
        parts = _split3(val)
        cum_chunk = _dot_exact_lhs01(tri_chunk, parts)
        cum_all = _dot_exact_lhs01(tri, parts) + carry
        carry = cum_all[blk - 1:blk, :]
        if is_meta:
            cum_all = cum_all - carry
        res = jnp.where(is_beta, beta, jnp.where(is_decay, cum_chunk, cum_all))
        col_ref[r * blk:(r + 1) * blk, :] = res
        row_ref[:, r * blk:(r + 1) * blk] = res.T[:GATE_ROWS, :]


def _gate_prep(gate3d, add_vec, alog_vec, is_meta):
    b, rows, _ = gate3d.shape
    kern = functools.partial(_gate_kernel, rows=rows, is_meta=is_meta)
    return pl.pallas_call(
        kern,
        out_shape=(jax.ShapeDtypeStruct((b, rows, LANES), F32),
                   jax.ShapeDtypeStruct((b, GATE_ROWS, rows), F32)),
        grid=(b,),
        in_specs=[pl.BlockSpec((None, rows, LANES), lambda i: (i, 0, 0)),
                  pl.BlockSpec((1, LANES), lambda i: (0, 0)),
                  pl.BlockSpec((1, LANES), lambda i: (0, 0))],
        out_specs=(pl.BlockSpec((None, rows, LANES), lambda i: (i, 0, 0)),
                   pl.BlockSpec((None, GATE_ROWS, rows), lambda i: (i, 0, 0))),
        compiler_params=pltpu.CompilerParams(
            dimension_semantics=("parallel",), vmem_limit_bytes=VMEM_LIMIT),
        name="gate_prep_meta" if is_meta else "gate_prep",
    )(gate3d, add_vec, alog_vec)


def _conv_silu(load_rows, w):
    y = load_rows(0) * w[0:1, :]
    for j in range(1, CONV_WIDTH):
        y = y + load_rows(j) * w[j:j + 1, :]
    return _silu(y)


def _l2norm(x):
    return x * lax.rsqrt(jnp.sum(x * x, axis=-1, keepdims=True) + EPS)


def _gdn_pointwise(q, k, v, beta_b, g_b):
    r = q.shape[0]
    q = _l2norm(q) * (HEAD_DIM ** -0.5)
    k = _l2norm(k)
    g3 = g_b.reshape(r // CHUNK, CHUNK, LANES)
    g_last = jnp.broadcast_to(g3[:, CHUNK - 1:CHUNK, :], g3.shape).reshape(r, LANES)
    e_g = jnp.exp(g_b)
    q_dec = q * e_g
    k_dec = k * jnp.exp(g_last - g_b)
    y = jnp.concatenate([v * beta_b, k * (beta_b * e_g)], axis=1)
    return q, k, q_dec, k_dec, y, jnp.exp(g_last)


def _gdn_groups(probs, fillers=()):
    r = probs[0][0].shape[0]
    ri = lax.broadcasted_iota(jnp.int32, (r, r), 0)
    ci = lax.broadcasted_iota(jnp.int32, (r, r), 1)
    same = _chunk_of(ri) == _chunk_of(ci)
    causal = same & (ci <= ri)
    strict = same & (ci < ri)

    def widen(t):
        return jnp.concatenate([t] * (r // LANES), axis=1) if r >= LANES else t[:, :r]

    fillers = list(fillers)
    n_ticks = CHUNK.bit_length()
    per_tick = -(-len(fillers) // n_ticks)

    def tick():
        for thunk in fillers[:per_tick]:
            thunk()
        del fillers[:per_tick]

    kks = [_dot_nt(k, k) for _, k, _, _, _, _ in probs]
    qks = [_dot_nt(q, k) for q, k, _, _, _, _ in probs]
    tick()
    dmats = [jnp.where(causal, jnp.exp(jnp.where(causal, widen(g_b) - g_row, 0.0)), 0.0)
             for _, _, _, _, g_b, g_row in probs]
    xs = [jnp.where(strict, widen(p[3]) * kk * d, 0.0).astype(BF16)
          for p, kk, d in zip(probs, kks, dmats)]
    a_qks = [qk * d for qk, d in zip(qks, dmats)]
    zs = [p[2].astype(F32) - _dot(x, p[2]) for p, x in zip(probs, xs)]
    span = 2
    while span < CHUNK:
        xs = [_dot(x, x).astype(BF16) for x in xs]
        tick()
        zs = [z + _dot(x, z.astype(BF16)) for x, z in zip(xs, zs)]
        span *= 2
    tick()
    assert not fillers
    return list(zip(zs, a_qks))


def _gdn_state0_kernel(km_ref, vm_ref, wk_ref, wv_ref, colm_ref, growm_ref, s_ref, pad_ref):
    heads = range(GDN_HEADS)
    probs, kd_t = [], []
    for h in heads:
        ls = slice(h * HEAD_DIM, (h + 1) * HEAD_DIM)
        beta_m = _lane_bcast(colm_ref[...], BETA_LANE + h)
        g_m = _lane_bcast(colm_ref[...], DECAY_LANE + h)
        conv = []
        for t, (src, w_ref) in enumerate(((km_ref, wk_ref), (vm_ref, wv_ref))):
            win = pad_ref.at[2 * h + t]
            win[0:SUBLANES, :] = jnp.zeros((SUBLANES, LANES), F32)
            win[SUBLANES:SUBLANES + META_ROWS, :] = src[h].astype(F32)
            conv.append(_conv_silu(
                lambda j, win=win: win[SUBLANES - (CONV_WIDTH - 1) + j:
                                       SUBLANES - (CONV_WIDTH - 1) + j + META_ROWS, :],
                w_ref[:, ls]))
        _, k_m, _, kd_m, y_m, _ = _gdn_pointwise(conv[0], conv[0], conv[1], beta_m, g_m)
        k_bf = k_m.astype(BF16)
        probs.append((k_bf, k_bf, y_m.astype(BF16), beta_m, g_m, growm_ref[h]))
        kd_t.append(kd_m.T.astype(BF16))
    for h, (uw_m, _) in zip(heads, _gdn_groups(probs)):
        s_ref[h] = _dot(kd_t[h], uw_m[:, :HEAD_DIM].astype(BF16))


def _gdn_state0(proj_m, conv_wt, col_m, row_m4):
    heads_block = lambda blk: pl.BlockSpec((GDN_HEADS, META_ROWS, HEAD_DIM),
                                           lambda j, blk=blk: (blk, 0, 0))
    taps_block = lambda blk: pl.BlockSpec((CONV_WIDTH, GDN_WIDTH), lambda j, blk=blk: (0, blk))
    return pl.pallas_call(
        _gdn_state0_kernel,
        out_shape=jax.ShapeDtypeStruct((GDN_HEADS, HEAD_DIM, HEAD_DIM), F32),
        grid=(1,),
        in_specs=[heads_block(1), heads_block(2), taps_block(1), taps_block(2),
                  pl.BlockSpec((None, META_ROWS, LANES), lambda j: (0, 0, 0)),
                  pl.BlockSpec((None, GDN_HEADS, 1, META_ROWS),
                               lambda j: (0, DECAY_LANE // GDN_HEADS, 0, 0))],
        out_specs=pl.BlockSpec((GDN_HEADS, HEAD_DIM, HEAD_DIM), lambda j: (0, 0, 0)),
        scratch_shapes=[pltpu.VMEM((2 * GDN_HEADS, META_ROWS + SUBLANES, HEAD_DIM), F32)],
        compiler_params=pltpu.CompilerParams(
            dimension_semantics=("arbitrary",), vmem_limit_bytes=VMEM_LIMIT),
        name="gdn_state0",
    )(proj_m, proj_m, conv_wt, conv_wt, col_m, row_m4)


def _gdn_kernel(*refs, seq):
    n_qkv = 3 * HEAD_PAIR
    src_refs = [refs[t * HEAD_PAIR:(t + 1) * HEAD_PAIR] for t in range(3)]
    z_ref = refs[n_qkv]
    meta_refs = [refs[n_qkv + 1 + t * HEAD_PAIR:n_qkv + 1 + (t + 1) * HEAD_PAIR] for t in range(3)]
    tap_refs = [refs[2 * n_qkv + 1 + t * HEAD_PAIR:2 * n_qkv + 1 + (t + 1) * HEAD_PAIR]
                for t in range(3)]
    (brow_ref, grow_ref, s0_ref, nw_ref, o_ref, pad_ref, qs_ref, ks_ref, y_ref, bb_ref, gb_ref,
     qd_ref, kdt_ref, dec_ref, u_ref, w_ref, aqk_ref, st_ref,
     qd_stage, kdt_stage, dec_stage) = refs[3 * n_qkv + 1:]
    pair = pl.program_id(1)
    slot = pair % REC_PAIRS
    n_blocks = seq // ROW_BLOCK
    n_groups = seq // GROUP
    cpg = GROUP // CHUNK
    hist = CONV_WIDTH - 1

    def pointwise_head(i, bank, hh):
        r0 = pl.multiple_of(i * ROW_BLOCK, ROW_BLOCK)
        rows = pl.ds(r0, ROW_BLOCK)
        convs = []
        for t in range(3):
            win = pad_ref.at[(bank * 3 + t) * HEAD_PAIR + hh]
            if isinstance(i, int) and i == 0:
                past = meta_refs[t][hh][META_ROWS - HIST_ROWS:META_ROWS, :]
            else:
                past = src_refs[t][hh][pl.ds(pl.multiple_of(r0 - HIST_ROWS, HIST_ROWS), HIST_ROWS), :]
            win[0:HIST_ROWS, :] = past.astype(F32)
            win[HIST_ROWS:HIST_ROWS + ROW_BLOCK, :] = src_refs[t][hh][rows, :].astype(F32)
            convs.append(_conv_silu(
                lambda j, win=win: win[HIST_ROWS - hist + j:HIST_ROWS - hist + j + ROW_BLOCK, :],
                tap_refs[t][hh][...]))
        beta_b = jnp.broadcast_to(brow_ref[hh, :, rows], (LANES, ROW_BLOCK)).T
        g_b = jnp.broadcast_to(grow_ref[hh, :, rows], (LANES, ROW_BLOCK)).T
        q, k, q_dec, k_dec, y, dec = _gdn_pointwise(convs[0], convs[1], convs[2], beta_b, g_b)
        qs_ref[hh, rows, :] = q.astype(BF16)
        ks_ref[hh, rows, :] = k.astype(BF16)
        y_ref[hh, rows, :] = y.astype(BF16)
        bb_ref[hh, rows, :] = beta_b
        gb_ref[hh, rows, :] = g_b
        qd_stage[hh, rows, :] = q_dec.astype(BF16)
        kdt_stage[hh, :, rows] = k_dec.T.astype(BF16)
        n_chunks = ROW_BLOCK // CHUNK
        dec_rows = [dec[c * CHUNK:c * CHUNK + 1, :] for c in range(n_chunks)]
        dec_rows.append(jnp.zeros((DEC_ROWS - n_chunks, LANES), F32))
        dec_stage[hh, pl.ds(pl.multiple_of(i * DEC_ROWS, DEC_ROWS), DEC_ROWS), :] = (
            jnp.concatenate(dec_rows, axis=0))

    def pointwise_thunks(i, bank):
        return [functools.partial(pointwise_head, i, bank, hh) for hh in range(HEAD_PAIR)]

    for i in range(PREP_UNROLL):
        for thunk in pointwise_thunks(i, i):
            thunk()

    def prep_groups(gi, fillers):
        groups = [gi * PREP_UNROLL + u for u in range(PREP_UNROLL) for _ in range(HEAD_PAIR)]
        keys = [(hh, pl.ds(pl.multiple_of((gi * PREP_UNROLL + u) * GROUP, GROUP), GROUP))
                for u in range(PREP_UNROLL) for hh in range(HEAD_PAIR)]
        probs = [(qs_ref[hh, rows, :], ks_ref[hh, rows, :], y_ref[hh, rows, :],
                  bb_ref[hh, rows, :], gb_ref[hh, rows, :], grow_ref[hh, :, rows])
                 for hh, rows in keys]
        for g, (hh, rows), (uw, a_qk) in zip(groups, keys, _gdn_groups(probs, fillers)):
            hs = slot * HEAD_PAIR + hh
            u_ref[hs, rows, :] = uw[:, :HEAD_DIM]
            w_ref[hs, rows, :] = uw[:, HEAD_DIM:].astype(BF16)
            aqk_ref[hs, rows, :] = a_qk.astype(BF16)
            drows = pl.ds(pl.multiple_of(g * DEC_ROWS, DEC_ROWS), DEC_ROWS)
            qd_ref[hs, rows, :] = qd_stage[hh, rows, :]
            kdt_ref[hs, :, rows] = kdt_stage[hh, :, rows]
            dec_ref[hs, drows, :] = dec_stage[hh, drows, :]

    def next_pointwise(gi):
        return [th for u in range(PREP_UNROLL)
                for th in pointwise_thunks((gi + 1) * PREP_UNROLL + u, u)]

    def rec_thunks(gi):
        r0 = pl.multiple_of(gi * GROUP, GROUP)
        rows = pl.ds(r0, GROUP)
        heads = range(REC_HEADS)
        outs = [[] for _ in heads]

        held = {}

        def read_state(c):
            crow = pl.ds(r0 + c * CHUNK, CHUNK)
            states = [st_ref[h] for h in heads]
            s_bf = [s.astype(BF16) for s in states]
            ws = [_dot(jnp.concatenate([w_ref[h, crow, :], qd_ref[h, crow, :]], axis=0), s_bf[h])
                  for h in heads]
            held[c] = (states, ws)

        def write_state(c):
            crow = pl.ds(r0 + c * CHUNK, CHUNK)
            states, ws = held.pop(c)
            v_new = [u_ref[h, crow, :] - ws[h][:CHUNK, :] for h in heads]
            zero = lambda n: jnp.zeros((n * CHUNK, HEAD_DIM), BF16)
            v_pad = [jnp.concatenate(([zero(c)] if c else []) + [v.astype(BF16)]
                                     + ([zero(cpg - 1 - c)] if c < cpg - 1 else []), axis=0)
                     for v in v_new]
            upd = [_dot(jnp.concatenate([aqk_ref[h, crow, :], kdt_ref[h, :, rows]], axis=0),
                        v_pad[h]) for h in heads]
            for h in heads:
                dec = dec_ref[h, pl.ds(gi * DEC_ROWS + c, 1), :]
                st_ref[h] = states[h] * dec + upd[h][CHUNK:, :]
                outs[h].append(ws[h][CHUNK:, :] + upd[h][:CHUNK, :])
            if c == cpg - 1:
                for h in heads:
                    o = _rms(jnp.concatenate(outs[h], axis=0), nw_ref[...])
                    o_ref[h, rows, :] = (o * _silu(z_ref[h, rows, :].astype(F32))).astype(BF16)

        return [functools.partial(half, c) for c in range(cpg) for half in (read_state, write_state)]

    def interleave(a, b):
        if not a or not b:
            return list(a) + list(b)
        out, j = [], 0
        for i, th in enumerate(a):
            out.append(th)
            while j < len(b) and (j + 1) * len(a) <= (i + 1) * len(b):
                out.append(b[j])
                j += 1
        return out + list(b[j:])

    n_trips = n_groups // PREP_UNROLL

    @pl.when(slot != REC_PAIRS - 1)
    def _():
        def prep_body(gi, _):
            prep_groups(gi, next_pointwise(gi))
            return 0
        lax.fori_loop(0, n_trips - 1, prep_body, 0)
        prep_groups(n_trips - 1, [])

    @pl.when(slot == REC_PAIRS - 1)
    def _():
        st_ref[...] = s0_ref[...]

        def rec_of_trip(t):
            return [th for u in range(PREP_UNROLL) for th in rec_thunks(t * PREP_UNROLL + u)]

        prep_groups(0, next_pointwise(0))

        def prep_rec_body(gi, _):
            prep_groups(gi, interleave(rec_of_trip(gi - 1), next_pointwise(gi)))
            return 0
        lax.fori_loop(1, n_trips - 1, prep_rec_body, 0)
        prep_groups(n_trips - 1, rec_of_trip(n_trips - 2))
        for thunk in rec_of_trip(n_trips - 1):
            thunk()


def _gdn(proj, proj_m, conv_wt, row4, state0, norm_w):
    _, b, seq, _ = proj.shape
    n_pairs = GDN_HEADS // HEAD_PAIR

    def head_specs(shape, imap):
        return [pl.BlockSpec(shape, functools.partial(imap, t * GDN_HEADS + hh))
                for t in range(3) for hh in range(HEAD_PAIR)]

    src_specs = head_specs((None, None, seq, HEAD_DIM),
                           lambda off, i, j: (off + HEAD_PAIR * j, i, 0, 0))
    meta_specs = head_specs((None, META_ROWS, HEAD_DIM), lambda off, i, j: (off + HEAD_PAIR * j, 0, 0))
    tap_specs = head_specs((CONV_WIDTH, HEAD_DIM), lambda off, i, j: (0, off + HEAD_PAIR * j))
    n_qkv = 3 * HEAD_PAIR
    z_block0 = 3 * GDN_HEADS // REC_HEADS

    per_pair = lambda width, dt: pltpu.VMEM((HEAD_PAIR, seq, width), dt)
    per_rec = lambda width, dt: pltpu.VMEM((REC_HEADS, seq, width), dt)
    return pl.pallas_call(
        functools.partial(_gdn_kernel, seq=seq),
        out_shape=jax.ShapeDtypeStruct((GDN_HEADS, b, seq, HEAD_DIM), BF16),
        grid=(b, n_pairs),
        in_specs=src_specs
        + [pl.BlockSpec((REC_HEADS, None, seq, HEAD_DIM),
                        lambda i, j: (z_block0 + j // REC_PAIRS, i, 0, 0))]
        + meta_specs + tap_specs
        + [pl.BlockSpec((None, HEAD_PAIR, 1, seq),
                        lambda i, j: (i, BETA_LANE // HEAD_PAIR + j, 0, 0)),
           pl.BlockSpec((None, HEAD_PAIR, 1, seq),
                        lambda i, j: (i, DECAY_LANE // HEAD_PAIR + j, 0, 0)),
           pl.BlockSpec((REC_HEADS, HEAD_DIM, HEAD_DIM), lambda i, j: (j // REC_PAIRS, 0, 0)),
           pl.BlockSpec((1, HEAD_DIM), lambda i, j: (0, 0))],
        out_specs=pl.BlockSpec((REC_HEADS, None, seq, HEAD_DIM),
                               lambda i, j: (j // REC_PAIRS, i, 0, 0)),
        scratch_shapes=[pltpu.VMEM((PREP_UNROLL * n_qkv, HIST_ROWS + ROW_BLOCK, HEAD_DIM), F32),
                        per_pair(HEAD_DIM, BF16), per_pair(HEAD_DIM, BF16),
                        per_pair(2 * HEAD_DIM, BF16),
                        per_pair(HEAD_DIM, F32), per_pair(HEAD_DIM, F32),
                        per_rec(HEAD_DIM, BF16),
                        pltpu.VMEM((REC_HEADS, HEAD_DIM, seq), BF16),
                        pltpu.VMEM((REC_HEADS, seq // GROUP * DEC_ROWS, HEAD_DIM), F32),
                        per_rec(HEAD_DIM, F32), per_rec(HEAD_DIM, BF16), per_rec(GROUP, BF16),
                        pltpu.VMEM((REC_HEADS, HEAD_DIM, HEAD_DIM), F32),
                        per_pair(HEAD_DIM, BF16),
                        pltpu.VMEM((HEAD_PAIR, HEAD_DIM, seq), BF16),
                        pltpu.VMEM((HEAD_PAIR, seq // GROUP * DEC_ROWS, HEAD_DIM), F32)],
        compiler_params=pltpu.CompilerParams(
            dimension_semantics=("parallel", "arbitrary"), vmem_limit_bytes=VMEM_LIMIT),
        name="gdn",
    )(*([proj] * n_qkv), proj, *([proj_m] * n_qkv), *([conv_wt] * n_qkv),
      row4, row4, state0, norm_w)


def _fox_kernel(q_ref, k_ref, v_ref, g_ref, km_ref, vm_ref, colm_ref, crow_ref,
                qw_ref, kw_ref, o_ref, qt_ref, ka_ref, kam_ref, vt_ref, vtm_ref,
                m_ref, l_ref, acc_ref, *, seq):
    n_blocks = seq // ROW_BLOCK
    aug_r = lax.broadcasted_iota(jnp.int32, (LANES, ROW_BLOCK), 0)

    def key_aug(ck, valid=None):
        hi, mid, lo = _split3(ck)
        lane = lax.broadcasted_iota(jnp.int32, ck.shape, 1)
        neg_hi = -hi.astype(F32)
        if valid is not None:
            neg_hi = jnp.where(valid, neg_hi, NEG_BIG)
        blk = jnp.where(lane < 3, 1.0,
                        jnp.where(lane == 3, neg_hi,
                                  jnp.where(lane == 4, -mid.astype(F32),
                                            jnp.where(lane == 5, -lo.astype(F32), 0.0))))
        return blk.astype(BF16)

    def pro_body(i, _):
        r0 = pl.multiple_of(i * ROW_BLOCK, ROW_BLOCK)
        rows = pl.ds(r0, ROW_BLOCK)
        for hd in range(FOX_GROUP):
            qn = _rms(q_ref[hd, rows, :].astype(F32), qw_ref[...]) * (HEAD_DIM ** -0.5 * LOG2E)
            qt_ref[hd, 0:HEAD_DIM, rows] = qn.T.astype(BF16)
            hi, mid, lo = _split3(crow_ref[hd, :, rows] * LOG2E)
            aug = jnp.where(aug_r == 0, hi.astype(F32),
                            jnp.where(aug_r == 1, mid.astype(F32),
                                      jnp.where(aug_r == 2, lo.astype(F32),
                                                jnp.where(aug_r < 6, 1.0, 0.0))))
            qt_ref[hd, HEAD_DIM:2 * HEAD_DIM, rows] = aug.astype(BF16)
            ka_ref[hd, rows, 0:HEAD_DIM] = _rms(k_ref[hd, rows, :].astype(F32),
                                                kw_ref[...]).astype(BF16)
            aug_k = jnp.where(aug_r < 3, 1.0,
                              jnp.where(aug_r == 3, -hi.astype(F32),
                                        jnp.where(aug_r == 4, -mid.astype(F32),
                                                  jnp.where(aug_r == 5, -lo.astype(F32), 0.0))))
            ka_ref[hd, rows, HEAD_DIM:2 * HEAD_DIM] = aug_k.T.astype(BF16)
            vt_ref[hd, :, rows] = v_ref[hd, rows, :].astype(F32).T.astype(BF16)
        return 0
    lax.fori_loop(0, n_blocks, pro_body, 0, unroll=BLOCK_UNROLL // FOX_GROUP)

    mrow = lax.broadcasted_iota(jnp.int32, (META_ROWS, LANES), 0)
    for hd in range(FOX_GROUP):
        head = pl.program_id(1) * FOX_GROUP + hd
        kam_ref[hd, :, 0:HEAD_DIM] = _rms(km_ref[hd].astype(F32), kw_ref[...]).astype(BF16)
        ck_m = _lane_bcast(colm_ref[...], FORGET_LANE + head) * LOG2E
        kam_ref[hd, :, HEAD_DIM:2 * HEAD_DIM] = key_aug(ck_m, mrow >= META_PAD)
        vtm_ref[hd] = vm_ref[hd].astype(F32).T.astype(BF16)

    kidx = lax.broadcasted_iota(jnp.int32, (FOX_TK, FOX_TK), 0)
    qidx = lax.broadcasted_iota(jnp.int32, (FOX_TK, FOX_TK), 1)
    diag_ok = kidx <= qidx

    items = []
    for kj in range(-1, seq // FOX_TK):
        lane0 = max(kj, 0) * FOX_TK
        for p0 in range(lane0, seq, FOX_PIECE):
            for hd in range(FOX_GROUP):
                items.append((hd, kj, slice(p0, min(p0 + FOX_PIECE, seq)), p0 == lane0))

    def scores(item):
        hd, kj, ln, leads = item
        k_aug = kam_ref[hd] if kj < 0 else ka_ref[hd, kj * FOX_TK:(kj + 1) * FOX_TK, :]
        s = _dot(k_aug, qt_ref[hd, :, ln])
        if kj >= 0 and leads:
            masked = jnp.where(diag_ok, s[:, :FOX_TK], NEG_BIG)
            s = jnp.concatenate([masked, s[:, FOX_TK:]], axis=1) if s.shape[1] > FOX_TK else masked
        return s

    def softmax_stats(item, s):
        hd, kj, ln, _ = item
        top = jnp.max(s, axis=0, keepdims=True)
        if kj < 0:
            m_new, alpha = top, None
        else:
            m_old = m_ref[hd, :, ln]
            m_new = jnp.maximum(m_old, top)
            alpha = jnp.exp2(m_old - m_new)
        p = jnp.exp2(s - m_new)
        psum = jnp.sum(p, axis=0, keepdims=True)
        m_ref[hd, :, ln] = m_new
        l_ref[hd, :, ln] = psum if kj < 0 else alpha * l_ref[hd, :, ln] + psum
        return p.astype(BF16), alpha

    def values(item, p, alpha):
        hd, kj, ln, _ = item
        v_t = vtm_ref[hd] if kj < 0 else vt_ref[hd, :, kj * FOX_TK:(kj + 1) * FOX_TK]
        pv = _dot(v_t, p)
        acc_ref[hd, :, ln] = pv if kj < 0 else alpha * acc_ref[hd, :, ln] + pv

    s_cur = scores(items[0])
    p_cur = None
    for t in range(len(items) + 1):
        s_next = scores(items[t + 1]) if t + 1 < len(items) else None
        p_next = softmax_stats(items[t], s_cur) if t < len(items) else None
        if p_cur is not None:
            values(items[t - 1], *p_cur)
        s_cur, p_cur = s_next, p_next

    def out_body(i, _):
        r0 = pl.multiple_of(i * ROW_BLOCK, ROW_BLOCK)
        rows = pl.ds(r0, ROW_BLOCK)
        for hd in range(FOX_GROUP):
            out_t = acc_ref[hd, :, rows] * (1.0 / l_ref[hd, :, rows])
            o_ref[hd, rows, :] = (out_t.T * _silu(g_ref[hd, rows, :].astype(F32))).astype(BF16)
        return 0
    lax.fori_loop(0, n_blocks, out_body, 0, unroll=BLOCK_UNROLL // FOX_GROUP)


def _fox(proj, proj_m, col_m, row4, q_w, k_w):
    _, b, seq, _ = proj.shape
    base = 4 * GDN_HEADS // FOX_GROUP
    hb = FOX_HEADS // FOX_GROUP

    def head_block(off):
        return pl.BlockSpec((FOX_GROUP, None, seq, HEAD_DIM),
                            lambda i, j, off=off: (off + j, i, 0, 0))

    def meta_block(off):
        return pl.BlockSpec((FOX_GROUP, META_ROWS, HEAD_DIM), lambda i, j, off=off: (off + j, 0, 0))

    per_head = lambda shape, dt: pltpu.VMEM((FOX_GROUP,) + shape, dt)
    return pl.pallas_call(
        functools.partial(_fox_kernel, seq=seq),
        out_shape=jax.ShapeDtypeStruct((FOX_HEADS, b, seq, HEAD_DIM), BF16),
        grid=(b, FOX_HEADS // FOX_GROUP),
        in_specs=[head_block(base), head_block(base + hb), head_block(base + 2 * hb),
                  head_block(base + 3 * hb),
                  meta_block(base + hb), meta_block(base + 2 * hb),
                  pl.BlockSpec((None, META_ROWS, LANES), lambda i, j: (0, 0, 0)),
                  pl.BlockSpec((None, FOX_GROUP, 1, seq),
                               lambda i, j: (i, FORGET_LANE // FOX_GROUP + j, 0, 0)),
                  pl.BlockSpec((1, HEAD_DIM), lambda i, j: (0, 0)),
                  pl.BlockSpec((1, HEAD_DIM), lambda i, j: (0, 0))],
        out_specs=pl.BlockSpec((FOX_GROUP, None, seq, HEAD_DIM), lambda i, j: (j, i, 0, 0)),
        scratch_shapes=[per_head((2 * HEAD_DIM, seq), BF16),
                        per_head((seq, 2 * HEAD_DIM), BF16),
                        per_head((META_ROWS, 2 * HEAD_DIM), BF16),
                        per_head((HEAD_DIM, seq), BF16),
                        per_head((HEAD_DIM, META_ROWS), BF16),
                        per_head((1, seq), F32), per_head((1, seq), F32),
                        per_head((HEAD_DIM, seq), F32)],
        compiler_params=pltpu.CompilerParams(
            dimension_semantics=("parallel", "arbitrary"), vmem_limit_bytes=VMEM_LIMIT),
        name="fox",
    )(proj, proj, proj, proj, proj_m, proj_m, col_m, row4, q_w, k_w)


def _out_proj_kernel(mg_ref, mf_ref, wg_ref, wf_ref, pw_ref, x_ref, o_ref):
    def rows_of(m_ref):
        return jnp.concatenate([m_ref[h] for h in range(m_ref.shape[0])], axis=1)
    out = _dot(rows_of(mg_ref), wg_ref[...]) + _dot(rows_of(mf_ref), wf_ref[...])
    o_ref[...] = x_ref[...] + _rms(out, pw_ref[...])


def _out_proj(mg, mf, w_g, w_f, post_w, x2d, tm):
    m, d = x2d.shape
    return pl.pallas_call(
        _out_proj_kernel,
        out_shape=jax.ShapeDtypeStruct((m, d), F32),
        grid=(m // tm,),
        in_specs=[pl.BlockSpec((GDN_HEADS, tm, HEAD_DIM), lambda i: (0, i, 0)),
                  pl.BlockSpec((FOX_HEADS, tm, HEAD_DIM), lambda i: (0, i, 0)),
                  pl.BlockSpec((GDN_WIDTH, d), lambda i: (0, 0)),
                  pl.BlockSpec((FOX_WIDTH, d), lambda i: (0, 0)),
                  pl.BlockSpec((1, d), lambda i: (0, 0)),
                  pl.BlockSpec((tm, d), lambda i: (i, 0))],
        out_specs=pl.BlockSpec((tm, d), lambda i: (i, 0)),
        compiler_params=pltpu.CompilerParams(
            dimension_semantics=("parallel",), vmem_limit_bytes=VMEM_LIMIT),
        name="out_proj",
    )(mg, mf, w_g, w_f, post_w, x2d)


def _tile(total, want):
    t = min(total, want)
    while total % t:
        t //= 2
    return t


def _layer(x, meta_pad, pre_w, w_in, conv_w, a_log, dt_bias, gdn_norm_w, fox_q_w, fox_k_w,
           fox_f_bias, w_out, post_w):
    b, seq, d = x.shape
    assert seq % (GROUP * PREP_UNROLL) == 0 and seq % FOX_TK == 0 and seq % ROW_BLOCK == 0
    gw, fw = GDN_WIDTH, FOX_WIDTH
    o_gb = 4 * gw
    o_f = o_gb + 2 * GDN_HEADS
    o_ff = o_f + 4 * fw
    w_t = w_in.T
    w_cat = jnp.concatenate(
        [w_t[:o_gb], w_t[o_f:o_ff], w_t[o_gb:o_f], w_t[o_ff:],
         jnp.zeros((GATE_WIDTH - 2 * GDN_HEADS - FOX_HEADS, d), w_in.dtype)], axis=0).astype(BF16)
    zpad = jnp.zeros((GATE_WIDTH - FORGET_LANE - FOX_HEADS,), F32)
    add_vec = jnp.concatenate([jnp.zeros((GDN_HEADS,), F32), dt_bias, fox_f_bias, zpad])[None]
    alog_vec = jnp.concatenate([jnp.zeros((GDN_HEADS,), F32), a_log,
                                jnp.zeros((FOX_HEADS,), F32), zpad])[None]

    x2d = x.reshape(b * seq, d)
    pre_w2 = pre_w[None]
    proj, gate, proj_m, gate_m = _in_proj(x2d, meta_pad, pre_w2, w_cat,
                                          _tile(b * seq, IN_PROJ_TM), IN_PROJ_TN)
    proj = proj.reshape(MAIN_WIDTH // HEAD_DIM, b, seq, HEAD_DIM)

    col, row = _gate_prep(gate.reshape(b, seq, GATE_WIDTH), add_vec, alog_vec, False)
    col_m, row_m = _gate_prep(gate_m[None], add_vec, alog_vec, True)
    row4 = row.reshape(b, GATE_ROWS, 1, seq)
    row_m4 = row_m.reshape(1, GATE_ROWS, 1, META_ROWS)

    conv_wt = conv_w.T
    state0 = _gdn_state0(proj_m, conv_wt, col_m, row_m4)
    o_gdn = _gdn(proj, proj_m, conv_wt, row4, state0, gdn_norm_w[None])
    o_fox = _fox(proj, proj_m, col_m, row4, fox_q_w[None], fox_k_w[None])

    w_out_b = w_out.astype(BF16)
    out = _out_proj(o_gdn.reshape(GDN_HEADS, b * seq, HEAD_DIM),
                    o_fox.reshape(FOX_HEADS, b * seq, HEAD_DIM),
                    w_out_b[:gw], w_out_b[gw:], post_w[None], x2d, _tile(b * seq, OUT_PROJ_TM))
    return out.reshape(b, seq, d)


def kernel(x, meta_tokens, pre_norm_w, w_in, conv_w, a_log, dt_bias, gdn_norm_w, fox_q_norm_w,
           fox_k_norm_w, fox_f_bias, w_out, post_norm_w):
    assert pre_norm_w.shape[0] == 1, "single-layer stack"
    meta_pad = jnp.concatenate(
        [jnp.zeros((META_PAD, x.shape[-1]), x.dtype), meta_tokens.astype(x.dtype)], axis=0)
    return _layer(x, meta_pad, pre_norm_w[0], w_in[0], conv_w[0], a_log[0], dt_bias[0],
                  gdn_norm_w[0], fox_q_norm_w[0], fox_k_norm_w[0], fox_f_bias[0], w_out[0],
                  post_norm_w[0])
```

```python
import functools
import math

import jax
import jax.numpy as jnp
from jax import lax
from jax.experimental import pallas as pl
from jax.experimental.pallas import tpu as pltpu

N_META = 16
HEAD_DIM = 128
GDN_HEADS = 8
FOX_HEADS = 8
GDN_WIDTH = GDN_HEADS * HEAD_DIM
FOX_WIDTH = FOX_HEADS * HEAD_DIM
CONV_WIDTH = 4
CHUNK = 64
EPS = 1e-6

LANES = 128
SUBLANES = 8
MXU_DIM = 256
MAIN_WIDTH = 4 * GDN_WIDTH + 4 * FOX_WIDTH
GATE_WIDTH = LANES
BETA_LANE, DECAY_LANE, FORGET_LANE = 0, GDN_HEADS, 2 * GDN_HEADS
GATE_ROWS = 32
META_ROWS = CHUNK
META_PAD = META_ROWS - N_META
GROUP = MXU_DIM
ROW_BLOCK = 256
HIST_ROWS = 2 * SUBLANES
IN_PROJ_TM, IN_PROJ_TN = 1024, 2048
OUT_PROJ_TM = 1024
OUT_PROJ_CHUNK = 512
HEAD_PAIR = 2
REC_PAIRS = 2
REC_HEADS = REC_PAIRS * HEAD_PAIR
PREP_UNROLL = 2
DEC_ROWS = SUBLANES
BLOCK_UNROLL = 4
FOX_GROUP = 4
FOX_TK = MXU_DIM
FOX_PIECE = 2 * MXU_DIM
VMEM_LIMIT = 56 * 1024 * 1024

F32 = jnp.float32
BF16 = jnp.bfloat16
NEG_BIG = -1e30
LOG2E = math.log2(math.e)


def _dot(a, b):
    return jnp.dot(a, b, preferred_element_type=F32)


def _dot_nt(a, b):
    return lax.dot_general(a, b, (((1,), (1,)), ((), ())), preferred_element_type=F32)


def _split3(x):
    hi = x.astype(BF16)
    r1 = x - hi.astype(F32)
    mid = r1.astype(BF16)
    lo = (r1 - mid.astype(F32)).astype(BF16)
    return hi, mid, lo


def _dot_exact_rhs01(parts, m):
    return _dot(parts[0], m) + _dot(parts[1], m) + _dot(parts[2], m)


def _dot_exact_lhs01(m, parts):
    return _dot(m, parts[0]) + _dot(m, parts[1]) + _dot(m, parts[2])


def _lane_bcast(col_tile, lane):
    sel = (lax.broadcasted_iota(jnp.int32, (LANES, LANES), 0) == lane).astype(BF16)
    return _dot_exact_rhs01(_split3(col_tile), sel)


def _chunk_of(idx):
    return jnp.right_shift(idx, CHUNK.bit_length() - 1)


def _rms(x, w):
    return x * lax.rsqrt(jnp.mean(x * x, axis=-1, keepdims=True) + EPS) * w


def _silu(x):
    return x * (1.0 / (1.0 + jnp.exp(-x)))


def _softplus(x):
    return jnp.maximum(x, 0.0) + jnp.log1p(jnp.exp(-jnp.abs(x)))


def _in_proj_kernel(x_ref, nw_ref, w_ref, wg_ref, o_ref, og_ref, xn_ref):
    @pl.when(pl.program_id(1) == 0)
    def _():
        xn = _rms(x_ref[...], nw_ref[...]).astype(BF16)
        xn_ref[...] = xn
        og_ref[...] = _dot_nt(xn, wg_ref[...])

    res = _dot_nt(xn_ref[...], w_ref[...])
    for hd in range(o_ref.shape[0]):
        o_ref[hd] = res[:, hd * HEAD_DIM:(hd + 1) * HEAD_DIM].astype(o_ref.dtype)


def _in_proj(x2d, norm_w, w_cat, tm, tn):
    m, d = x2d.shape
    n = MAIN_WIDTH
    return pl.pallas_call(
        _in_proj_kernel,
        out_shape=(jax.ShapeDtypeStruct((n // HEAD_DIM, m, HEAD_DIM), BF16),
                   jax.ShapeDtypeStruct((m, GATE_WIDTH), F32)),
        grid=(m // tm, n // tn),
        in_specs=[pl.BlockSpec((tm, d), lambda i, j: (i, 0)),
                  pl.BlockSpec((1, d), lambda i, j: (0, 0)),
                  pl.BlockSpec((tn, d), lambda i, j: (j, 0)),
                  pl.BlockSpec((GATE_WIDTH, d), lambda i, j: (MAIN_WIDTH // GATE_WIDTH, 0))],
        out_specs=(pl.BlockSpec((tn // HEAD_DIM, tm, HEAD_DIM), lambda i, j: (j, i, 0)),
                   pl.BlockSpec((tm, GATE_WIDTH), lambda i, j: (i, 0))),
        scratch_shapes=[pltpu.VMEM((tm, d), BF16)],
        compiler_params=pltpu.CompilerParams(
            dimension_semantics=("parallel", "arbitrary"), vmem_limit_bytes=VMEM_LIMIT),
        name="in_proj",
    )(x2d, norm_w, w_cat, w_cat)


def _gate_kernel(t_ref, add_ref, alog_ref, col_ref, row_ref, *, rows, is_meta):
    blk = min(rows, ROW_BLOCK)
    lane = lax.broadcasted_iota(jnp.int32, (blk, LANES), 1)
    ri = lax.broadcasted_iota(jnp.int32, (blk, blk), 0)
    ci = lax.broadcasted_iota(jnp.int32, (blk, blk), 1)
    tri = (ci <= ri).astype(BF16)
    tri_chunk = ((ci <= ri) & (_chunk_of(ri) == _chunk_of(ci))).astype(BF16)
    is_beta = lane < DECAY_LANE
    is_decay = (lane >= DECAY_LANE) & (lane < FORGET_LANE)
    is_forget = (lane >= FORGET_LANE) & (lane < FORGET_LANE + FOX_HEADS)

    carry = jnp.zeros((1, LANES), F32)
    for r in range(rows // blk):
        t = t_ref[r * blk:(r + 1) * blk, :]
        ta = t + add_ref[...]
        beta = 1.0 / (1.0 + jnp.exp(-t))
        g = -jnp.exp(alog_ref[...]) * _softplus(ta)
        logf = -_softplus(-ta)
        val = jnp.where(is_decay, g, jnp.where(is_forget, logf, 0.0))
        if is_meta:
            row = lax.broadcasted_iota(jnp.int32, (blk, LANES), 0)
            val = jnp.where(row >= META_PAD, val, 0.0)
            beta = jnp.where(row >= META_PAD, beta, 0.0)
        parts = _split3(val)
        cum_chunk = _dot_exact_lhs01(tri_chunk, parts)
        cum_all = _dot_exact_lhs01(tri, parts) + carry
        carry = cum_all[blk - 1:blk, :]
        if is_meta:
            cum_all = cum_all - carry
        res = jnp.where(is_beta, beta, jnp.where(is_decay, cum_chunk, cum_all))
        col_ref[r * blk:(r + 1) * blk, :] = res
        row_ref[:, r * blk:(r + 1) * blk] = res.T[:GATE_ROWS, :]


def _gate_rows_kernel(t_ref, add_ref, alog_ref, row_ref, *, rows):
    blk = ROW_BLOCK
    row = lax.broadcasted_iota(jnp.int32, (GATE_ROWS, blk), 0)
    si = lax.broadcasted_iota(jnp.int32, (blk, blk), 0)
    ti = lax.broadcasted_iota(jnp.int32, (blk, blk), 1)
    tri = (si <= ti).astype(BF16)
    tri_chunk = ((si <= ti) & (_chunk_of(si) == _chunk_of(ti))).astype(BF16)
    is_beta = row < DECAY_LANE
    is_decay = (row >= DECAY_LANE) & (row < FORGET_LANE)
    is_forget = (row >= FORGET_LANE) & (row < FORGET_LANE + FOX_HEADS)
    add = jnp.concatenate([add_ref[...]] * (blk // LANES), axis=1)
    neg_rate = -jnp.exp(jnp.concatenate([alog_ref[...]] * (blk // LANES), axis=1))

    carry = jnp.zeros((GATE_ROWS, 1), F32)
    for r in range(rows // blk):
        t = t_ref[r * blk:(r + 1) * blk, :].T[:GATE_ROWS, :]
        ta = t + add
        beta = 1.0 / (1.0 + jnp.exp(-t))
        val = jnp.where(is_decay, neg_rate * _softplus(ta), jnp.where(is_forget, -_softplus(-ta), 0.0))
        parts = _split3(val)
        cum_chunk = _dot_exact_rhs01(parts, tri_chunk)
        cum_all = _dot_exact_rhs01(parts, tri) + carry
        carry = cum_all[:, blk - 1:blk]
        row_ref[:, r * blk:(r + 1) * blk] = jnp.where(is_beta, beta,
                                                      jnp.where(is_decay, cum_chunk, cum_all))


def _gate_rows(gate3d, add_vec, alog_vec):
    b, rows, _ = gate3d.shape
    per_row = lambda v: jnp.broadcast_to(v[0, :GATE_ROWS, None], (GATE_ROWS, LANES))
    return pl.pallas_call(
        functools.partial(_gate_rows_kernel, rows=rows),
        out_shape=jax.ShapeDtypeStruct((b, GATE_ROWS, rows), F32),
        grid=(b,),
        in_specs=[pl.BlockSpec((None, rows, LANES), lambda i: (i, 0, 0)),
                  pl.BlockSpec((GATE_ROWS, LANES), lambda i: (0, 0)),
                  pl.BlockSpec((GATE_ROWS, LANES), lambda i: (0, 0))],
        out_specs=pl.BlockSpec((None, GATE_ROWS, rows), lambda i: (i, 0, 0)),
        compiler_params=pltpu.CompilerParams(
            dimension_semantics=("parallel",), vmem_limit_bytes=VMEM_LIMIT),
        name="gate_rows",
    )(gate3d, per_row(add_vec), per_row(alog_vec))


def _gate_prep(gate3d, add_vec, alog_vec, is_meta):
    b, rows, _ = gate3d.shape
    kern = functools.partial(_gate_kernel, rows=rows, is_meta=is_meta)
    return pl.pallas_call(
        kern,
        out_shape=(jax.ShapeDtypeStruct((b, rows, LANES), F32),
                   jax.ShapeDtypeStruct((b, GATE_ROWS, rows), F32)),
        grid=(b,),
        in_specs=[pl.BlockSpec((None, rows, LANES), lambda i: (i, 0, 0)),
                  pl.BlockSpec((1, LANES), lambda i: (0, 0)),
                  pl.BlockSpec((1, LANES), lambda i: (0, 0))],
        out_specs=(pl.BlockSpec((None, rows, LANES), lambda i: (i, 0, 0)),
                   pl.BlockSpec((None, GATE_ROWS, rows), lambda i: (i, 0, 0))),
        compiler_params=pltpu.CompilerParams(
            dimension_semantics=("parallel",), vmem_limit_bytes=VMEM_LIMIT),
        name="gate_prep_meta" if is_meta else "gate_prep",
    )(gate3d, add_vec, alog_vec)


def _conv_silu(load_rows, w):
    y = load_rows(0) * w[0:1, :]
    for j in range(1, CONV_WIDTH):
        y = y + load_rows(j) * w[j:j + 1, :]
    return _silu(y)


def _l2norm(x):
    return x * lax.rsqrt(jnp.sum(x * x, axis=-1, keepdims=True) + EPS)


def _gdn_pointwise(q, k, v, beta_b, g_b):
    r = q.shape[0]
    q = _l2norm(q) * (HEAD_DIM ** -0.5)
    k = _l2norm(k)
    g3 = g_b.reshape(r // CHUNK, CHUNK, LANES)
    g_last = jnp.broadcast_to(g3[:, CHUNK - 1:CHUNK, :], g3.shape).reshape(r, LANES)
    e_g = jnp.exp(g_b)
    q_dec = q * e_g
    k_dec = k * jnp.exp(g_last - g_b)
    y = jnp.concatenate([v * beta_b, k * (beta_b * e_g)], axis=1)
    return q, k, q_dec, k_dec, y, jnp.exp(g_last)


def _gdn_groups(probs, fillers=()):
    r = probs[0][0].shape[0]
    ri = lax.broadcasted_iota(jnp.int32, (r, r), 0)
    ci = lax.broadcasted_iota(jnp.int32, (r, r), 1)
    same = _chunk_of(ri) == _chunk_of(ci)
    causal = same & (ci <= ri)
    strict = same & (ci < ri)

    def widen(t):
        return jnp.concatenate([t] * (r // LANES), axis=1) if r >= LANES else t[:, :r]

    fillers = list(fillers)
    n_ticks = CHUNK.bit_length()
    per_tick = -(-len(fillers) // n_ticks)

    def tick():
        for thunk in fillers[:per_tick]:
            thunk()
        del fillers[:per_tick]

    kks = [_dot_nt(k, k) for _, k, _, _, _, _ in probs]
    qks = [_dot_nt(q, k) for q, k, _, _, _, _ in probs]
    tick()
    dmats = [jnp.where(causal, jnp.exp(jnp.where(causal, widen(g_b) - g_row, 0.0)), 0.0)
             for _, _, _, _, g_b, g_row in probs]
    xs = [jnp.where(strict, widen(p[3]) * kk * d, 0.0).astype(BF16)
          for p, kk, d in zip(probs, kks, dmats)]
    a_qks = [qk * d for qk, d in zip(qks, dmats)]
    zs = [p[2].astype(F32) - _dot(x, p[2]) for p, x in zip(probs, xs)]
    span = 2
    while span < CHUNK:
        xs = [_dot(x, x).astype(BF16) for x in xs]
        tick()
        zs = [z + _dot(x, z.astype(BF16)) for x, z in zip(xs, zs)]
        span *= 2
    tick()
    assert not fillers
    return list(zip(zs, a_qks))


def _gdn_state0_kernel(km_ref, vm_ref, wk_ref, wv_ref, colm_ref, growm_ref, s_ref, pad_ref):
    heads = range(GDN_HEADS)
    probs, kd_t = [], []
    for h in heads:
        ls = slice(h * HEAD_DIM, (h + 1) * HEAD_DIM)
        beta_m = _lane_bcast(colm_ref[...], BETA_LANE + h)
        g_m = _lane_bcast(colm_ref[...], DECAY_LANE + h)
        conv = []
        for t, (src, w_ref) in enumerate(((km_ref, wk_ref), (vm_ref, wv_ref))):
            win = pad_ref.at[2 * h + t]
            win[0:SUBLANES, :] = jnp.zeros((SUBLANES, LANES), F32)
            win[SUBLANES:SUBLANES + META_ROWS, :] = src[h].astype(F32)
            conv.append(_conv_silu(
                lambda j, win=win: win[SUBLANES - (CONV_WIDTH - 1) + j:
                                       SUBLANES - (CONV_WIDTH - 1) + j + META_ROWS, :],
                w_ref[:, ls]))
        _, k_m, _, kd_m, y_m, _ = _gdn_pointwise(conv[0], conv[0], conv[1], beta_m, g_m)
        k_bf = k_m.astype(BF16)
        probs.append((k_bf, k_bf, y_m.astype(BF16), beta_m, g_m, growm_ref[h]))
        kd_t.append(kd_m.T.astype(BF16))
    for h, (uw_m, _) in zip(heads, _gdn_groups(probs)):
        s_ref[h] = _dot(kd_t[h], uw_m[:, :HEAD_DIM].astype(BF16))


def _gdn_state0(proj_m, conv_wt, col_m, row_m4):
    heads_block = lambda blk: pl.BlockSpec((GDN_HEADS, META_ROWS, HEAD_DIM),
                                           lambda j, blk=blk: (blk, 0, 0))
    taps_block = lambda blk: pl.BlockSpec((CONV_WIDTH, GDN_WIDTH), lambda j, blk=blk: (0, blk))
    return pl.pallas_call(
        _gdn_state0_kernel,
        out_shape=jax.ShapeDtypeStruct((GDN_HEADS, HEAD_DIM, HEAD_DIM), F32),
        grid=(1,),
        in_specs=[heads_block(1), heads_block(2), taps_block(1), taps_block(2),
                  pl.BlockSpec((None, META_ROWS, LANES), lambda j: (0, 0, 0)),
                  pl.BlockSpec((None, GDN_HEADS, 1, META_ROWS),
                               lambda j: (0, DECAY_LANE // GDN_HEADS, 0, 0))],
        out_specs=pl.BlockSpec((GDN_HEADS, HEAD_DIM, HEAD_DIM), lambda j: (0, 0, 0)),
        scratch_shapes=[pltpu.VMEM((2 * GDN_HEADS, META_ROWS + SUBLANES, HEAD_DIM), F32)],
        compiler_params=pltpu.CompilerParams(
            dimension_semantics=("arbitrary",), vmem_limit_bytes=VMEM_LIMIT),
        name="gdn_state0",
    )(proj_m, proj_m, conv_wt, conv_wt, col_m, row_m4)


def _gdn_kernel(*refs, seq):
    n_qkv = 3 * HEAD_PAIR
    src_refs = [refs[t * HEAD_PAIR:(t + 1) * HEAD_PAIR] for t in range(3)]
    z_ref = refs[n_qkv]
    meta_refs = [refs[n_qkv + 1 + t * HEAD_PAIR:n_qkv + 1 + (t + 1) * HEAD_PAIR] for t in range(3)]
    tap_refs = [refs[2 * n_qkv + 1 + t * HEAD_PAIR:2 * n_qkv + 1 + (t + 1) * HEAD_PAIR]
                for t in range(3)]
    (brow_ref, grow_ref, s0_ref, nw_ref, o_ref, pad_ref, qs_ref, ks_ref, y_ref, bb_ref, gb_ref,
     qd_ref, kdt_ref, dec_ref, u_ref, w_ref, aqk_ref, st_ref,
     qd_stage, kdt_stage, dec_stage) = refs[3 * n_qkv + 1:]
    pair = pl.program_id(1)
    slot = pair % REC_PAIRS
    n_blocks = seq // ROW_BLOCK
    n_groups = seq // GROUP
    cpg = GROUP // CHUNK
    hist = CONV_WIDTH - 1

    def pointwise_head(i, bank, hh):
        r0 = pl.multiple_of(i * ROW_BLOCK, ROW_BLOCK)
        rows = pl.ds(r0, ROW_BLOCK)
        convs = []
        for t in range(3):
            win = pad_ref.at[(bank * 3 + t) * HEAD_PAIR + hh]
            if isinstance(i, int) and i == 0:
                past = meta_refs[t][hh][META_ROWS - HIST_ROWS:META_ROWS, :]
            else:
                past = src_refs[t][hh][pl.ds(pl.multiple_of(r0 - HIST_ROWS, HIST_ROWS), HIST_ROWS), :]
            win[0:HIST_ROWS, :] = past.astype(F32)
            win[HIST_ROWS:HIST_ROWS + ROW_BLOCK, :] = src_refs[t][hh][rows, :].astype(F32)
            convs.append(_conv_silu(
                lambda j, win=win: win[HIST_ROWS - hist + j:HIST_ROWS - hist + j + ROW_BLOCK, :],
                tap_refs[t][hh][...]))
        beta_b = jnp.broadcast_to(brow_ref[hh, :, rows], (LANES, ROW_BLOCK)).T
        g_b = jnp.broadcast_to(grow_ref[hh, :, rows], (LANES, ROW_BLOCK)).T
        q, k, q_dec, k_dec, y, dec = _gdn_pointwise(convs[0], convs[1], convs[2], beta_b, g_b)
        qs_ref[hh, rows, :] = q.astype(BF16)
        ks_ref[hh, rows, :] = k.astype(BF16)
        y_ref[hh, rows, :] = y.astype(BF16)
        bb_ref[hh, rows, :] = beta_b
        gb_ref[hh, rows, :] = g_b
        qd_stage[hh, rows, :] = q_dec.astype(BF16)
        kdt_stage[hh, :, rows] = k_dec.T.astype(BF16)
        n_chunks = ROW_BLOCK // CHUNK
        dec_rows = [dec[c * CHUNK:c * CHUNK + 1, :] for c in range(n_chunks)]
        dec_rows.append(jnp.zeros((DEC_ROWS - n_chunks, LANES), F32))
        dec_stage[hh, pl.ds(pl.multiple_of(i * DEC_ROWS, DEC_ROWS), DEC_ROWS), :] = (
            jnp.concatenate(dec_rows, axis=0))

    def pointwise_thunks(i, bank):
        return [functools.partial(pointwise_head, i, bank, hh) for hh in range(HEAD_PAIR)]

    for i in range(PREP_UNROLL):
        for thunk in pointwise_thunks(i, i):
            thunk()

    def prep_groups(gi, fillers):
        groups = [gi * PREP_UNROLL + u for u in range(PREP_UNROLL) for _ in range(HEAD_PAIR)]
        keys = [(hh, pl.ds(pl.multiple_of((gi * PREP_UNROLL + u) * GROUP, GROUP), GROUP))
                for u in range(PREP_UNROLL) for hh in range(HEAD_PAIR)]
        probs = [(qs_ref[hh, rows, :], ks_ref[hh, rows, :], y_ref[hh, rows, :],
                  bb_ref[hh, rows, :], gb_ref[hh, rows, :], grow_ref[hh, :, rows])
                 for hh, rows in keys]
        for g, (hh, rows), (uw, a_qk) in zip(groups, keys, _gdn_groups(probs, fillers)):
            hs = slot * HEAD_PAIR + hh
            u_ref[hs, rows, :] = uw[:, :HEAD_DIM]
            w_ref[hs, rows, :] = uw[:, HEAD_DIM:].astype(BF16)
            aqk_ref[hs, rows, :] = a_qk.astype(BF16)
            drows = pl.ds(pl.multiple_of(g * DEC_ROWS, DEC_ROWS), DEC_ROWS)
            qd_ref[hs, rows, :] = qd_stage[hh, rows, :]
            kdt_ref[hs, :, rows] = kdt_stage[hh, :, rows]
            dec_ref[hs, drows, :] = dec_stage[hh, drows, :]

    def next_pointwise(gi):
        return [th for u in range(PREP_UNROLL)
                for th in pointwise_thunks((gi + 1) * PREP_UNROLL + u, u)]

    def rec_thunks(gi):
        r0 = pl.multiple_of(gi * GROUP, GROUP)
        rows = pl.ds(r0, GROUP)
        heads = range(REC_HEADS)
        outs = [[] for _ in heads]

        held = {}

        def read_state(c):
            crow = pl.ds(r0 + c * CHUNK, CHUNK)
            states = [st_ref[h] for h in heads]
            s_bf = [s.astype(BF16) for s in states]
            ws = [_dot(jnp.concatenate([w_ref[h, crow, :], qd_ref[h, crow, :]], axis=0), s_bf[h])
                  for h in heads]
            held[c] = (states, ws)

        def write_state(c):
            crow = pl.ds(r0 + c * CHUNK, CHUNK)
            states, ws = held.pop(c)
            v_new = [u_ref[h, crow, :] - ws[h][:CHUNK, :] for h in heads]
            zero = lambda n: jnp.zeros((n * CHUNK, HEAD_DIM), BF16)
            v_pad = [jnp.concatenate(([zero(c)] if c else []) + [v.astype(BF16)]
                                     + ([zero(cpg - 1 - c)] if c < cpg - 1 else []), axis=0)
                     for v in v_new]
            upd = [_dot(jnp.concatenate([aqk_ref[h, crow, :], kdt_ref[h, :, rows]], axis=0),
                        v_pad[h]) for h in heads]
            for h in heads:
                dec = dec_ref[h, pl.ds(gi * DEC_ROWS + c, 1), :]
                st_ref[h] = states[h] * dec + upd[h][CHUNK:, :]
                outs[h].append(ws[h][CHUNK:, :] + upd[h][:CHUNK, :])
            if c == cpg - 1:
                for h in heads:
                    o = _rms(jnp.concatenate(outs[h], axis=0), nw_ref[...])
                    o_ref[h, rows, :] = (o * _silu(z_ref[h, rows, :].astype(F32))).astype(BF16)

        return [functools.partial(half, c) for c in range(cpg) for half in (read_state, write_state)]

    def interleave(a, b):
        if not a or not b:
            return list(a) + list(b)
        out, j = [], 0
        for i, th in enumerate(a):
            out.append(th)
            while j < len(b) and (j + 1) * len(a) <= (i + 1) * len(b):
                out.append(b[j])
                j += 1
        return out + list(b[j:])

    n_trips = n_groups // PREP_UNROLL

    @pl.when(slot != REC_PAIRS - 1)
    def _():
        def prep_body(gi, _):
            prep_groups(gi, next_pointwise(gi))
            return 0
        lax.fori_loop(0, n_trips - 1, prep_body, 0)
        prep_groups(n_trips - 1, [])

    @pl.when(slot == REC_PAIRS - 1)
    def _():
        st_ref[...] = s0_ref[...]

        def rec_of_trip(t):
            return [th for u in range(PREP_UNROLL) for th in rec_thunks(t * PREP_UNROLL + u)]

        prep_groups(0, next_pointwise(0))

        def prep_rec_body(gi, _):
            prep_groups(gi, interleave(rec_of_trip(gi - 1), next_pointwise(gi)))
            return 0
        lax.fori_loop(1, n_trips - 1, prep_rec_body, 0)
        prep_groups(n_trips - 1, rec_of_trip(n_trips - 2))
        for thunk in rec_of_trip(n_trips - 1):
            thunk()


def _gdn(proj, proj_m, conv_wt, row4, state0, norm_w):
    _, b, seq, _ = proj.shape
    n_pairs = GDN_HEADS // HEAD_PAIR

    def head_specs(shape, imap):
        return [pl.BlockSpec(shape, functools.partial(imap, t * GDN_HEADS + hh))
                for t in range(3) for hh in range(HEAD_PAIR)]

    src_specs = head_specs((None, None, seq, HEAD_DIM),
                           lambda off, i, j: (off + HEAD_PAIR * j, i, 0, 0))
    meta_specs = head_specs((None, META_ROWS, HEAD_DIM), lambda off, i, j: (off + HEAD_PAIR * j, 0, 0))
    tap_specs = head_specs((CONV_WIDTH, HEAD_DIM), lambda off, i, j: (0, off + HEAD_PAIR * j))
    n_qkv = 3 * HEAD_PAIR
    z_block0 = 3 * GDN_HEADS // REC_HEADS

    per_pair = lambda width, dt: pltpu.VMEM((HEAD_PAIR, seq, width), dt)
    per_rec = lambda width, dt: pltpu.VMEM((REC_HEADS, seq, width), dt)
    return pl.pallas_call(
        functools.partial(_gdn_kernel, seq=seq),
        out_shape=jax.ShapeDtypeStruct((GDN_HEADS, b, seq, HEAD_DIM), BF16),
        grid=(b, n_pairs),
        in_specs=src_specs
        + [pl.BlockSpec((REC_HEADS, None, seq, HEAD_DIM),
                        lambda i, j: (z_block0 + j // REC_PAIRS, i, 0, 0))]
        + meta_specs + tap_specs
        + [pl.BlockSpec((None, HEAD_PAIR, 1, seq),
                        lambda i, j: (i, BETA_LANE // HEAD_PAIR + j, 0, 0)),
           pl.BlockSpec((None, HEAD_PAIR, 1, seq),
                        lambda i, j: (i, DECAY_LANE // HEAD_PAIR + j, 0, 0)),
           pl.BlockSpec((REC_HEADS, HEAD_DIM, HEAD_DIM), lambda i, j: (j // REC_PAIRS, 0, 0)),
           pl.BlockSpec((1, HEAD_DIM), lambda i, j: (0, 0))],
        out_specs=pl.BlockSpec((REC_HEADS, None, seq, HEAD_DIM),
                               lambda i, j: (j // REC_PAIRS, i, 0, 0)),
        scratch_shapes=[pltpu.VMEM((PREP_UNROLL * n_qkv, HIST_ROWS + ROW_BLOCK, HEAD_DIM), F32),
                        per_pair(HEAD_DIM, BF16), per_pair(HEAD_DIM, BF16),
                        per_pair(2 * HEAD_DIM, BF16),
                        per_pair(HEAD_DIM, F32), per_pair(HEAD_DIM, F32),
                        per_rec(HEAD_DIM, BF16),
                        pltpu.VMEM((REC_HEADS, HEAD_DIM, seq), BF16),
                        pltpu.VMEM((REC_HEADS, seq // GROUP * DEC_ROWS, HEAD_DIM), F32),
                        per_rec(HEAD_DIM, F32), per_rec(HEAD_DIM, BF16), per_rec(GROUP, BF16),
                        pltpu.VMEM((REC_HEADS, HEAD_DIM, HEAD_DIM), F32),
                        per_pair(HEAD_DIM, BF16),
                        pltpu.VMEM((HEAD_PAIR, HEAD_DIM, seq), BF16),
                        pltpu.VMEM((HEAD_PAIR, seq // GROUP * DEC_ROWS, HEAD_DIM), F32)],
        compiler_params=pltpu.CompilerParams(
            dimension_semantics=("parallel", "arbitrary"), vmem_limit_bytes=VMEM_LIMIT),
        name="gdn",
    )(*([proj] * n_qkv), proj, *([proj_m] * n_qkv), *([conv_wt] * n_qkv),
      row4, row4, state0, norm_w)


def _fox_kernel(q_ref, k_ref, v_ref, g_ref, km_ref, vm_ref, colm_ref, crow_ref,
                qw_ref, kw_ref, o_ref, qt_ref, ka_ref, kam_ref, vt_ref, vtm_ref,
                m_ref, l_ref, acc_ref, *, seq):
    n_blocks = seq // ROW_BLOCK
    aug_r = lax.broadcasted_iota(jnp.int32, (LANES, ROW_BLOCK), 0)

    def key_aug(ck, valid=None):
        hi, mid, lo = _split3(ck)
        lane = lax.broadcasted_iota(jnp.int32, ck.shape, 1)
        neg_hi = -hi.astype(F32)
        if valid is not None:
            neg_hi = jnp.where(valid, neg_hi, NEG_BIG)
        blk = jnp.where(lane < 3, 1.0,
                        jnp.where(lane == 3, neg_hi,
                                  jnp.where(lane == 4, -mid.astype(F32),
                                            jnp.where(lane == 5, -lo.astype(F32), 0.0))))
        return blk.astype(BF16)

    def pro_body(i, _):
        r0 = pl.multiple_of(i * ROW_BLOCK, ROW_BLOCK)
        rows = pl.ds(r0, ROW_BLOCK)
        for hd in range(FOX_GROUP):
            qn = _rms(q_ref[hd, rows, :].astype(F32), qw_ref[...]) * (HEAD_DIM ** -0.5 * LOG2E)
            qt_ref[hd, 0:HEAD_DIM, rows] = qn.T.astype(BF16)
            hi, mid, lo = _split3(crow_ref[hd, :, rows] * LOG2E)
            aug = jnp.where(aug_r == 0, hi.astype(F32),
                            jnp.where(aug_r == 1, mid.astype(F32),
                                      jnp.where(aug_r == 2, lo.astype(F32),
                                                jnp.where(aug_r < 6, 1.0, 0.0))))
            qt_ref[hd, HEAD_DIM:2 * HEAD_DIM, rows] = aug.astype(BF16)
            ka_ref[hd, rows, 0:HEAD_DIM] = _rms(k_ref[hd, rows, :].astype(F32),
                                                kw_ref[...]).astype(BF16)
            aug_k = jnp.where(aug_r < 3, 1.0,
                              jnp.where(aug_r == 3, -hi.astype(F32),
                                        jnp.where(aug_r == 4, -mid.astype(F32),
                                                  jnp.where(aug_r == 5, -lo.astype(F32), 0.0))))
            ka_ref[hd, rows, HEAD_DIM:2 * HEAD_DIM] = aug_k.T.astype(BF16)
            vt_ref[hd, :, rows] = v_ref[hd, rows, :].astype(F32).T.astype(BF16)
        return 0
    lax.fori_loop(0, n_blocks, pro_body, 0, unroll=BLOCK_UNROLL // FOX_GROUP)

    mrow = lax.broadcasted_iota(jnp.int32, (META_ROWS, LANES), 0)
    for hd in range(FOX_GROUP):
        head = pl.program_id(1) * FOX_GROUP + hd
        kam_ref[hd, :, 0:HEAD_DIM] = _rms(km_ref[hd].astype(F32), kw_ref[...]).astype(BF16)
        ck_m = _lane_bcast(colm_ref[...], FORGET_LANE + head) * LOG2E
        kam_ref[hd, :, HEAD_DIM:2 * HEAD_DIM] = key_aug(ck_m, mrow >= META_PAD)
        vtm_ref[hd] = vm_ref[hd].astype(F32).T.astype(BF16)

    kidx = lax.broadcasted_iota(jnp.int32, (FOX_TK, FOX_TK), 0)
    qidx = lax.broadcasted_iota(jnp.int32, (FOX_TK, FOX_TK), 1)
    diag_ok = kidx <= qidx

    items = []
    for kj in range(-1, seq // FOX_TK):
        lane0 = max(kj, 0) * FOX_TK
        for p0 in range(lane0, seq, FOX_PIECE):
            for hd in range(FOX_GROUP):
                items.append((hd, kj, slice(p0, min(p0 + FOX_PIECE, seq)), p0 == lane0))

    def scores(item):
        hd, kj, ln, leads = item
        k_aug = kam_ref[hd] if kj < 0 else ka_ref[hd, kj * FOX_TK:(kj + 1) * FOX_TK, :]
        s = _dot(k_aug, qt_ref[hd, :, ln])
        if kj >= 0 and leads:
            masked = jnp.where(diag_ok, s[:, :FOX_TK], NEG_BIG)
            s = jnp.concatenate([masked, s[:, FOX_TK:]], axis=1) if s.shape[1] > FOX_TK else masked
        return s

    def softmax_stats(item, s):
        hd, kj, ln, _ = item
        top = jnp.max(s, axis=0, keepdims=True)
        if kj < 0:
            m_new, alpha = top, None
        else:
            m_old = m_ref[hd, :, ln]
            m_new = jnp.maximum(m_old, top)
            alpha = jnp.exp2(m_old - m_new)
        p = jnp.exp2(s - m_new)
        psum = jnp.sum(p, axis=0, keepdims=True)
        m_ref[hd, :, ln] = m_new
        l_ref[hd, :, ln] = psum if kj < 0 else alpha * l_ref[hd, :, ln] + psum
        return p.astype(BF16), alpha

    def values(item, p, alpha):
        hd, kj, ln, _ = item
        v_t = vtm_ref[hd] if kj < 0 else vt_ref[hd, :, kj * FOX_TK:(kj + 1) * FOX_TK]
        pv = _dot(v_t, p)
        acc_ref[hd, :, ln] = pv if kj < 0 else alpha * acc_ref[hd, :, ln] + pv

    s_cur = scores(items[0])
    p_cur = None
    for t in range(len(items) + 1):
        s_next = scores(items[t + 1]) if t + 1 < len(items) else None
        p_next = softmax_stats(items[t], s_cur) if t < len(items) else None
        if p_cur is not None:
            values(items[t - 1], *p_cur)
        s_cur, p_cur = s_next, p_next

    def out_body(i, _):
        r0 = pl.multiple_of(i * ROW_BLOCK, ROW_BLOCK)
        rows = pl.ds(r0, ROW_BLOCK)
        for hd in range(FOX_GROUP):
            out_t = acc_ref[hd, :, rows] * (1.0 / l_ref[hd, :, rows])
            o_ref[hd, rows, :] = (out_t.T * _silu(g_ref[hd, rows, :].astype(F32))).astype(BF16)
        return 0
    lax.fori_loop(0, n_blocks, out_body, 0, unroll=BLOCK_UNROLL // FOX_GROUP)


def _fox(proj, proj_m, col_m, row4, q_w, k_w):
    _, b, seq, _ = proj.shape
    base = 4 * GDN_HEADS // FOX_GROUP
    hb = FOX_HEADS // FOX_GROUP

    def head_block(off):
        return pl.BlockSpec((FOX_GROUP, None, seq, HEAD_DIM),
                            lambda i, j, off=off: (off + j, i, 0, 0))

    def meta_block(off):
        return pl.BlockSpec((FOX_GROUP, META_ROWS, HEAD_DIM), lambda i, j, off=off: (off + j, 0, 0))

    per_head = lambda shape, dt: pltpu.VMEM((FOX_GROUP,) + shape, dt)
    return pl.pallas_call(
        functools.partial(_fox_kernel, seq=seq),
        out_shape=jax.ShapeDtypeStruct((FOX_HEADS, b, seq, HEAD_DIM), BF16),
        grid=(b, FOX_HEADS // FOX_GROUP),
        in_specs=[head_block(base), head_block(base + hb), head_block(base + 2 * hb),
                  head_block(base + 3 * hb),
                  meta_block(base + hb), meta_block(base + 2 * hb),
                  pl.BlockSpec((None, META_ROWS, LANES), lambda i, j: (0, 0, 0)),
                  pl.BlockSpec((None, FOX_GROUP, 1, seq),
                               lambda i, j: (i, FORGET_LANE // FOX_GROUP + j, 0, 0)),
                  pl.BlockSpec((1, HEAD_DIM), lambda i, j: (0, 0)),
                  pl.BlockSpec((1, HEAD_DIM), lambda i, j: (0, 0))],
        out_specs=pl.BlockSpec((FOX_GROUP, None, seq, HEAD_DIM), lambda i, j: (j, i, 0, 0)),
        scratch_shapes=[per_head((2 * HEAD_DIM, seq), BF16),
                        per_head((seq, 2 * HEAD_DIM), BF16),
                        per_head((META_ROWS, 2 * HEAD_DIM), BF16),
                        per_head((HEAD_DIM, seq), BF16),
                        per_head((HEAD_DIM, META_ROWS), BF16),
                        per_head((1, seq), F32), per_head((1, seq), F32),
                        per_head((HEAD_DIM, seq), F32)],
        compiler_params=pltpu.CompilerParams(
            dimension_semantics=("parallel", "arbitrary"), vmem_limit_bytes=VMEM_LIMIT),
        name="fox",
    )(proj, proj, proj, proj, proj_m, proj_m, col_m, row4, q_w, k_w)


def _out_proj_kernel(mg_ref, mf_ref, wg_ref, wf_ref, pw_ref, x_ref, o_ref):
    for r in range(o_ref.shape[0] // OUT_PROJ_CHUNK):
        rows = slice(r * OUT_PROJ_CHUNK, (r + 1) * OUT_PROJ_CHUNK)

        def rows_of(m_ref):
            return jnp.concatenate([m_ref[h, rows, :] for h in range(m_ref.shape[0])], axis=1)
        out = _dot(rows_of(mg_ref), wg_ref[...]) + _dot(rows_of(mf_ref), wf_ref[...])
        o_ref[rows, :] = x_ref[rows, :] + _rms(out, pw_ref[...])


def _out_proj(mg, mf, w_g, w_f, post_w, x2d, tm):
    m, d = x2d.shape
    return pl.pallas_call(
        _out_proj_kernel,
        out_shape=jax.ShapeDtypeStruct((m, d), F32),
        grid=(m // tm,),
        in_specs=[pl.BlockSpec((GDN_HEADS, tm, HEAD_DIM), lambda i: (0, i, 0)),
                  pl.BlockSpec((FOX_HEADS, tm, HEAD_DIM), lambda i: (0, i, 0)),
                  pl.BlockSpec((GDN_WIDTH, d), lambda i: (0, 0), pipeline_mode=pl.Buffered(1)),
                  pl.BlockSpec((FOX_WIDTH, d), lambda i: (0, 0), pipeline_mode=pl.Buffered(1)),
                  pl.BlockSpec((1, d), lambda i: (0, 0)),
                  pl.BlockSpec((tm, d), lambda i: (i, 0))],
        out_specs=pl.BlockSpec((tm, d), lambda i: (i, 0)),
        compiler_params=pltpu.CompilerParams(
            dimension_semantics=("parallel",), vmem_limit_bytes=VMEM_LIMIT),
        name="out_proj",
    )(mg, mf, w_g, w_f, post_w, x2d)


def _tile(total, want):
    t = min(total, want)
    while total % t:
        t //= 2
    return t


def _layer(x, meta_pad, pre_w, w_in, conv_w, a_log, dt_bias, gdn_norm_w, fox_q_w, fox_k_w,
           fox_f_bias, w_out, post_w):
    b, seq, d = x.shape
    assert seq % (GROUP * PREP_UNROLL) == 0 and seq % FOX_TK == 0 and seq % ROW_BLOCK == 0
    gw, fw = GDN_WIDTH, FOX_WIDTH
    o_gb = 4 * gw
    o_f = o_gb + 2 * GDN_HEADS
    o_ff = o_f + 4 * fw
    w_t = w_in.T
    w_cat = jnp.concatenate(
        [w_t[:o_gb], w_t[o_f:o_ff], w_t[o_gb:o_f], w_t[o_ff:],
         jnp.zeros((GATE_WIDTH - 2 * GDN_HEADS - FOX_HEADS, d), w_in.dtype)], axis=0).astype(BF16)
    zpad = jnp.zeros((GATE_WIDTH - FORGET_LANE - FOX_HEADS,), F32)
    add_vec = jnp.concatenate([jnp.zeros((GDN_HEADS,), F32), dt_bias, fox_f_bias, zpad])[None]
    alog_vec = jnp.concatenate([jnp.zeros((GDN_HEADS,), F32), a_log,
                                jnp.zeros((FOX_HEADS,), F32), zpad])[None]

    x2d = x.reshape(b * seq, d)
    pre_w2 = pre_w[None]
    proj, gate = _in_proj(x2d, pre_w2, w_cat, _tile(b * seq, IN_PROJ_TM), IN_PROJ_TN)
    proj_m, gate_m = _in_proj(meta_pad, pre_w2, w_cat, META_ROWS, IN_PROJ_TN)
    proj = proj.reshape(MAIN_WIDTH // HEAD_DIM, b, seq, HEAD_DIM)

    row = _gate_rows(gate.reshape(b, seq, GATE_WIDTH), add_vec, alog_vec)
    col_m, row_m = _gate_prep(gate_m[None], add_vec, alog_vec, True)
    row4 = row.reshape(b, GATE_ROWS, 1, seq)
    row_m4 = row_m.reshape(1, GATE_ROWS, 1, META_ROWS)

    conv_wt = conv_w.T
    state0 = _gdn_state0(proj_m, conv_wt, col_m, row_m4)
    o_gdn = _gdn(proj, proj_m, conv_wt, row4, state0, gdn_norm_w[None])
    o_fox = _fox(proj, proj_m, col_m, row4, fox_q_w[None], fox_k_w[None])

    w_out_b = w_out.astype(BF16)
    out = _out_proj(o_gdn.reshape(GDN_HEADS, b * seq, HEAD_DIM),
                    o_fox.reshape(FOX_HEADS, b * seq, HEAD_DIM),
                    w_out_b[:gw], w_out_b[gw:], post_w[None], x2d, _tile(b * seq, OUT_PROJ_TM))
    return out.reshape(b, seq, d)


def kernel(x, meta_tokens, pre_norm_w, w_in, conv_w, a_log, dt_bias, gdn_norm_w, fox_q_norm_w,
           fox_k_norm_w, fox_f_bias, w_out, post_norm_w):
    assert pre_norm_w.shape[0] == 1, "single-layer stack"
    meta_pad = jnp.concatenate(
        [jnp.zeros((META_PAD, x.shape[-1]), x.dtype), meta_tokens.astype(x.dtype)], axis=0)
    return _layer(x, meta_pad, pre_norm_w[0], w_in[0], conv_w[0], a_log[0], dt_bias[0],
                  gdn_norm_w[0], fox_q_norm_w[0], fox_k_norm_w[0], fox_f_bias[0], w_out[0],
                  post_norm_w[0])
```

```python
import functools
import math

import jax
import jax.numpy as jnp
from jax import lax
from jax.experimental import pallas as pl
from jax.experimental.pallas import tpu as pltpu

N_META = 16
HEAD_DIM = 128
GDN_HEADS = 8
FOX_HEADS = 8
GDN_WIDTH = GDN_HEADS * HEAD_DIM
FOX_WIDTH = FOX_HEADS * HEAD_DIM
CONV_WIDTH = 4
CHUNK = 64
EPS = 1e-6

LANES = 128
SUBLANES = 8
MXU_DIM = 256
MAIN_WIDTH = 4 * GDN_WIDTH + 4 * FOX_WIDTH
GATE_WIDTH = LANES
BETA_LANE, DECAY_LANE, FORGET_LANE = 0, GDN_HEADS, 2 * GDN_HEADS
GATE_ROWS = 32
META_ROWS = CHUNK
META_PAD = META_ROWS - N_META
GROUP = MXU_DIM
ROW_BLOCK = 256
HIST_ROWS = 2 * SUBLANES
IN_PROJ_TM, IN_PROJ_TN = 1024, 2048
W_PREP_ROWS = 512
OUT_PROJ_TM = 1024
OUT_PROJ_CHUNK = 512
HEAD_PAIR = 2
REC_PAIRS = 2
REC_HEADS = REC_PAIRS * HEAD_PAIR
PREP_UNROLL = 2
DEC_ROWS = SUBLANES
BLOCK_UNROLL = 4
FOX_GROUP = 4
FOX_TK = MXU_DIM
FOX_PIECE = 2 * MXU_DIM
VMEM_LIMIT = 56 * 1024 * 1024

F32 = jnp.float32
BF16 = jnp.bfloat16
NEG_BIG = -1e30
LOG2E = math.log2(math.e)


def _dot(a, b):
    return jnp.dot(a, b, preferred_element_type=F32)


def _dot_nt(a, b):
    return lax.dot_general(a, b, (((1,), (1,)), ((), ())), preferred_element_type=F32)


def _split3(x):
    hi = x.astype(BF16)
    r1 = x - hi.astype(F32)
    mid = r1.astype(BF16)
    lo = (r1 - mid.astype(F32)).astype(BF16)
    return hi, mid, lo


def _dot_exact_rhs01(parts, m):
    return _dot(parts[0], m) + _dot(parts[1], m) + _dot(parts[2], m)


def _dot_exact_lhs01(m, parts):
    return _dot(m, parts[0]) + _dot(m, parts[1]) + _dot(m, parts[2])


def _lane_bcast(col_tile, lane):
    sel = (lax.broadcasted_iota(jnp.int32, (LANES, LANES), 0) == lane).astype(BF16)
    return _dot_exact_rhs01(_split3(col_tile), sel)


def _chunk_of(idx):
    return jnp.right_shift(idx, CHUNK.bit_length() - 1)


def _rms(x, w):
    return x * lax.rsqrt(jnp.mean(x * x, axis=-1, keepdims=True) + EPS) * w


def _silu(x):
    return x * (1.0 / (1.0 + jnp.exp(-x)))


def _softplus(x):
    return jnp.maximum(x, 0.0) + jnp.log1p(jnp.exp(-jnp.abs(x)))


def _w_prep_kernel(w_hbm, o_ref, buf, sem, *, rows, shift_from, shift):
    r = pl.program_id(0)

    def fetch(step, slot):
        start = step * rows
        src = pl.multiple_of(start + jnp.where(start >= shift_from, shift, 0), SUBLANES)
        return pltpu.make_async_copy(w_hbm.at[pl.ds(src, rows), :], buf.at[slot], sem.at[slot])

    @pl.when(r == 0)
    def _():
        fetch(0, 0).start()

    @pl.when(r + 1 < pl.num_programs(0))
    def _():
        fetch(r + 1, (r + 1) % 2).start()

    fetch(r, r % 2).wait()
    o_ref[...] = buf[r % 2].astype(BF16)


def _w_prep(w_t, n_rows, shift_from, shift):
    d = w_t.shape[1]
    rows = W_PREP_ROWS
    return pl.pallas_call(
        functools.partial(_w_prep_kernel, rows=rows, shift_from=shift_from, shift=shift),
        out_shape=jax.ShapeDtypeStruct((n_rows, d), BF16),
        grid=(n_rows // rows,),
        in_specs=[pl.BlockSpec(memory_space=pl.ANY)],
        out_specs=pl.BlockSpec((rows, d), lambda r: (r, 0)),
        scratch_shapes=[pltpu.VMEM((2, rows, d), F32), pltpu.SemaphoreType.DMA((2,))],
        compiler_params=pltpu.CompilerParams(
            dimension_semantics=("arbitrary",), vmem_limit_bytes=VMEM_LIMIT),
        name="w_prep",
    )(w_t)


def _in_proj_kernel(x_ref, nw_ref, w_ref, wg_ref, o_ref, og_ref, xn_ref):
    @pl.when(pl.program_id(1) == 0)
    def _():
        xn = _rms(x_ref[...], nw_ref[...]).astype(BF16)
        xn_ref[...] = xn
        og_ref[...] = _dot_nt(xn, wg_ref[...].astype(BF16))

    res = _dot_nt(xn_ref[...], w_ref[...])
    for hd in range(o_ref.shape[0]):
        o_ref[hd] = res[:, hd * HEAD_DIM:(hd + 1) * HEAD_DIM].astype(o_ref.dtype)


def _in_proj(x2d, norm_w, w_main, w_gate, tm, tn):
    m, d = x2d.shape
    n = w_main.shape[0]
    return pl.pallas_call(
        _in_proj_kernel,
        out_shape=(jax.ShapeDtypeStruct((n // HEAD_DIM, m, HEAD_DIM), BF16),
                   jax.ShapeDtypeStruct((m, GATE_WIDTH), F32)),
        grid=(m // tm, n // tn),
        in_specs=[pl.BlockSpec((tm, d), lambda i, j: (i, 0)),
                  pl.BlockSpec((1, d), lambda i, j: (0, 0)),
                  pl.BlockSpec((tn, d), lambda i, j: (j, 0)),
                  pl.BlockSpec((GATE_WIDTH, d), lambda i, j: (0, 0))],
        out_specs=(pl.BlockSpec((tn // HEAD_DIM, tm, HEAD_DIM), lambda i, j: (j, i, 0)),
                   pl.BlockSpec((tm, GATE_WIDTH), lambda i, j: (i, 0))),
        scratch_shapes=[pltpu.VMEM((tm, d), BF16)],
        compiler_params=pltpu.CompilerParams(
            dimension_semantics=("parallel", "arbitrary"), vmem_limit_bytes=VMEM_LIMIT),
        name="in_proj",
    )(x2d, norm_w, w_main, w_gate)


def _gate_kernel(t_ref, add_ref, alog_ref, col_ref, row_ref, *, rows, is_meta):
    blk = min(rows, ROW_BLOCK)
    lane = lax.broadcasted_iota(jnp.int32, (blk, LANES), 1)
    ri = lax.broadcasted_iota(jnp.int32, (blk, blk), 0)
    ci = lax.broadcasted_iota(jnp.int32, (blk, blk), 1)
    tri = (ci <= ri).astype(BF16)
    tri_chunk = ((ci <= ri) & (_chunk_of(ri) == _chunk_of(ci))).astype(BF16)
    is_beta = lane < DECAY_LANE
    is_decay = (lane >= DECAY_LANE) & (lane < FORGET_LANE)
    is_forget = (lane >= FORGET_LANE) & (lane < FORGET_LANE + FOX_HEADS)

    carry = jnp.zeros((1, LANES), F32)
    for r in range(rows // blk):
        t = t_ref[r * blk:(r + 1) * blk, :]
        ta = t + add_ref[...]
        beta = 1.0 / (1.0 + jnp.exp(-t))
        g = -jnp.exp(alog_ref[...]) * _softplus(ta)
        logf = -_softplus(-ta)
        val = jnp.where(is_decay, g, jnp.where(is_forget, logf, 0.0))
        if is_meta:
            row = lax.broadcasted_iota(jnp.int32, (blk, LANES), 0)
            val = jnp.where(row >= META_PAD, val, 0.0)
            beta = jnp.where(row >= META_PAD, beta, 0.0)
        parts = _split3(val)
        cum_chunk = _dot_exact_lhs01(tri_chunk, parts)
        cum_all = _dot_exact_lhs01(tri, parts) + carry
        carry = cum_all[blk - 1:blk, :]
        if is_meta:
            cum_all = cum_all - carry
        res = jnp.where(is_beta, beta, jnp.where(is_decay, cum_chunk, cum_all))
        col_ref[r * blk:(r + 1) * blk, :] = res
        row_ref[:, r * blk:(r + 1) * blk] = res.T[:GATE_ROWS, :]


def _gate_rows_kernel(t_ref, add_ref, alog_ref, row_ref, *, rows):
    blk = ROW_BLOCK
    row = lax.broadcasted_iota(jnp.int32, (GATE_ROWS, blk), 0)
    si = lax.broadcasted_iota(jnp.int32, (blk, blk), 0)
    ti = lax.broadcasted_iota(jnp.int32, (blk, blk), 1)
    tri = (si <= ti).astype(BF16)
    tri_chunk = ((si <= ti) & (_chunk_of(si) == _chunk_of(ti))).astype(BF16)
    is_beta = row < DECAY_LANE
    is_decay = (row >= DECAY_LANE) & (row < FORGET_LANE)
    is_forget = (row >= FORGET_LANE) & (row < FORGET_LANE + FOX_HEADS)
    add = jnp.concatenate([add_ref[...]] * (blk // LANES), axis=1)
    neg_rate = -jnp.exp(jnp.concatenate([alog_ref[...]] * (blk // LANES), axis=1))

    carry = jnp.zeros((GATE_ROWS, 1), F32)
    for r in range(rows // blk):
        t = t_ref[r * blk:(r + 1) * blk, :].T[:GATE_ROWS, :]
        ta = t + add
        beta = 1.0 / (1.0 + jnp.exp(-t))
        val = jnp.where(is_decay, neg_rate * _softplus(ta), jnp.where(is_forget, -_softplus(-ta), 0.0))
        parts = _split3(val)
        cum_chunk = _dot_exact_rhs01(parts, tri_chunk)
        cum_all = _dot_exact_rhs01(parts, tri) + carry
        carry = cum_all[:, blk - 1:blk]
        row_ref[:, r * blk:(r + 1) * blk] = jnp.where(is_beta, beta,
                                                      jnp.where(is_decay, cum_chunk, cum_all))


def _gate_rows(gate3d, add_vec, alog_vec):
    b, rows, _ = gate3d.shape
    per_row = lambda v: jnp.broadcast_to(v[0, :GATE_ROWS, None], (GATE_ROWS, LANES))
    return pl.pallas_call(
        functools.partial(_gate_rows_kernel, rows=rows),
        out_shape=jax.ShapeDtypeStruct((b, GATE_ROWS, rows), F32),
        grid=(b,),
        in_specs=[pl.BlockSpec((None, rows, LANES), lambda i: (i, 0, 0)),
                  pl.BlockSpec((GATE_ROWS, LANES), lambda i: (0, 0)),
                  pl.BlockSpec((GATE_ROWS, LANES), lambda i: (0, 0))],
        out_specs=pl.BlockSpec((None, GATE_ROWS, rows), lambda i: (i, 0, 0)),
        compiler_params=pltpu.CompilerParams(
            dimension_semantics=("parallel",), vmem_limit_bytes=VMEM_LIMIT),
        name="gate_rows",
    )(gate3d, per_row(add_vec), per_row(alog_vec))


def _gate_prep(gate3d, add_vec, alog_vec, is_meta):
    b, rows, _ = gate3d.shape
    kern = functools.partial(_gate_kernel, rows=rows, is_meta=is_meta)
    return pl.pallas_call(
        kern,
        out_shape=(jax.ShapeDtypeStruct((b, rows, LANES), F32),
                   jax.ShapeDtypeStruct((b, GATE_ROWS, rows), F32)),
        grid=(b,),
        in_specs=[pl.BlockSpec((None, rows, LANES), lambda i: (i, 0, 0)),
                  pl.BlockSpec((1, LANES), lambda i: (0, 0)),
                  pl.BlockSpec((1, LANES), lambda i: (0, 0))],
        out_specs=(pl.BlockSpec((None, rows, LANES), lambda i: (i, 0, 0)),
                   pl.BlockSpec((None, GATE_ROWS, rows), lambda i: (i, 0, 0))),
        compiler_params=pltpu.CompilerParams(
            dimension_semantics=("parallel",), vmem_limit_bytes=VMEM_LIMIT),
        name="gate_prep_meta" if is_meta else "gate_prep",
    )(gate3d, add_vec, alog_vec)


def _conv_silu(load_rows, w):
    y = load_rows(0) * w[0:1, :]
    for j in range(1, CONV_WIDTH):
        y = y + load_rows(j) * w[j:j + 1, :]
    return _silu(y)


def _l2norm(x):
    return x * lax.rsqrt(jnp.sum(x * x, axis=-1, keepdims=True) + EPS)


def _gdn_pointwise(q, k, v, beta_b, g_b):
    r = q.shape[0]
    q = _l2norm(q) * (HEAD_DIM ** -0.5)
    k = _l2norm(k)
    g3 = g_b.reshape(r // CHUNK, CHUNK, LANES)
    g_last = jnp.broadcast_to(g3[:, CHUNK - 1:CHUNK, :], g3.shape).reshape(r, LANES)
    e_g = jnp.exp(g_b)
    q_dec = q * e_g
    k_dec = k * jnp.exp(g_last - g_b)
    y = jnp.concatenate([v * beta_b, k * (beta_b * e_g)], axis=1)
    return q, k, q_dec, k_dec, y, jnp.exp(g_last)


def _gdn_groups(probs, fillers=()):
    r = probs[0][0].shape[0]
    ri = lax.broadcasted_iota(jnp.int32, (r, r), 0)
    ci = lax.broadcasted_iota(jnp.int32, (r, r), 1)
    same = _chunk_of(ri) == _chunk_of(ci)
    causal = same & (ci <= ri)
    strict = same & (ci < ri)

    def widen(t):
        return jnp.concatenate([t] * (r // LANES), axis=1) if r >= LANES else t[:, :r]

    fillers = list(fillers)
    n_ticks = CHUNK.bit_length()
    per_tick = -(-len(fillers) // n_ticks)

    def tick():
        for thunk in fillers[:per_tick]:
            thunk()
        del fillers[:per_tick]

    kks = [_dot_nt(k, k) for _, k, _, _, _, _ in probs]
    qks = [_dot_nt(q, k) for q, k, _, _, _, _ in probs]
    tick()
    dmats = [jnp.where(causal, jnp.exp(jnp.where(causal, widen(g_b) - g_row, 0.0)), 0.0)
             for _, _, _, _, g_b, g_row in probs]
    xs = [jnp.where(strict, widen(p[3]) * kk * d, 0.0).astype(BF16)
          for p, kk, d in zip(probs, kks, dmats)]
    a_qks = [qk * d for qk, d in zip(qks, dmats)]
    zs = [p[2].astype(F32) - _dot(x, p[2]) for p, x in zip(probs, xs)]
    span = 2
    while span < CHUNK:
        xs = [_dot(x, x).astype(BF16) for x in xs]
        tick()
        zs = [z + _dot(x, z.astype(BF16)) for x, z in zip(xs, zs)]
        span *= 2
    tick()
    assert not fillers
    return list(zip(zs, a_qks))


def _gdn_state0_kernel(km_ref, vm_ref, wk_ref, wv_ref, colm_ref, growm_ref, s_ref, pad_ref):
    heads = range(GDN_HEADS)
    probs, kd_t = [], []
    for h in heads:
        ls = slice(h * HEAD_DIM, (h + 1) * HEAD_DIM)
        beta_m = _lane_bcast(colm_ref[...], BETA_LANE + h)
        g_m = _lane_bcast(colm_ref[...], DECAY_LANE + h)
        conv = []
        for t, (src, w_ref) in enumerate(((km_ref, wk_ref), (vm_ref, wv_ref))):
            win = pad_ref.at[2 * h + t]
            win[0:SUBLANES, :] = jnp.zeros((SUBLANES, LANES), F32)
            win[SUBLANES:SUBLANES + META_ROWS, :] = src[h].astype(F32)
            conv.append(_conv_silu(
                lambda j, win=win: win[SUBLANES - (CONV_WIDTH - 1) + j:
                                       SUBLANES - (CONV_WIDTH - 1) + j + META_ROWS, :],
                w_ref[:, ls]))
        _, k_m, _, kd_m, y_m, _ = _gdn_pointwise(conv[0], conv[0], conv[1], beta_m, g_m)
        k_bf = k_m.astype(BF16)
        probs.append((k_bf, k_bf, y_m.astype(BF16), beta_m, g_m, growm_ref[h]))
        kd_t.append(kd_m.T.astype(BF16))
    for h, (uw_m, _) in zip(heads, _gdn_groups(probs)):
        s_ref[h] = _dot(kd_t[h], uw_m[:, :HEAD_DIM].astype(BF16))


def _gdn_state0(proj_m, conv_wt, col_m, row_m4):
    heads_block = lambda blk: pl.BlockSpec((GDN_HEADS, META_ROWS, HEAD_DIM),
                                           lambda j, blk=blk: (blk, 0, 0))
    taps_block = lambda blk: pl.BlockSpec((CONV_WIDTH, GDN_WIDTH), lambda j, blk=blk: (0, blk))
    return pl.pallas_call(
        _gdn_state0_kernel,
        out_shape=jax.ShapeDtypeStruct((GDN_HEADS, HEAD_DIM, HEAD_DIM), F32),
        grid=(1,),
        in_specs=[heads_block(1), heads_block(2), taps_block(1), taps_block(2),
                  pl.BlockSpec((None, META_ROWS, LANES), lambda j: (0, 0, 0)),
                  pl.BlockSpec((None, GDN_HEADS, 1, META_ROWS),
                               lambda j: (0, DECAY_LANE // GDN_HEADS, 0, 0))],
        out_specs=pl.BlockSpec((GDN_HEADS, HEAD_DIM, HEAD_DIM), lambda j: (0, 0, 0)),
        scratch_shapes=[pltpu.VMEM((2 * GDN_HEADS, META_ROWS + SUBLANES, HEAD_DIM), F32)],
        compiler_params=pltpu.CompilerParams(
            dimension_semantics=("arbitrary",), vmem_limit_bytes=VMEM_LIMIT),
        name="gdn_state0",
    )(proj_m, proj_m, conv_wt, conv_wt, col_m, row_m4)


def _gdn_kernel(*refs, seq):
    n_qkv = 3 * HEAD_PAIR
    src_refs = [refs[t * HEAD_PAIR:(t + 1) * HEAD_PAIR] for t in range(3)]
    z_ref = refs[n_qkv]
    meta_refs = [refs[n_qkv + 1 + t * HEAD_PAIR:n_qkv + 1 + (t + 1) * HEAD_PAIR] for t in range(3)]
    tap_refs = [refs[2 * n_qkv + 1 + t * HEAD_PAIR:2 * n_qkv + 1 + (t + 1) * HEAD_PAIR]
                for t in range(3)]
    (brow_ref, grow_ref, s0_ref, nw_ref, o_ref, pad_ref, qs_ref, ks_ref, y_ref, bb_ref, gb_ref,
     qd_ref, kdt_ref, dec_ref, u_ref, w_ref, aqk_ref, st_ref,
     qd_stage, kdt_stage, dec_stage) = refs[3 * n_qkv + 1:]
    pair = pl.program_id(1)
    slot = pair % REC_PAIRS
    n_blocks = seq // ROW_BLOCK
    n_groups = seq // GROUP
    cpg = GROUP // CHUNK
    hist = CONV_WIDTH - 1

    def pointwise_head(i, bank, hh):
        r0 = pl.multiple_of(i * ROW_BLOCK, ROW_BLOCK)
        rows = pl.ds(r0, ROW_BLOCK)
        convs = []
        for t in range(3):
            win = pad_ref.at[(bank * 3 + t) * HEAD_PAIR + hh]
            if isinstance(i, int) and i == 0:
                past = meta_refs[t][hh][META_ROWS - HIST_ROWS:META_ROWS, :]
            else:
                past = src_refs[t][hh][pl.ds(pl.multiple_of(r0 - HIST_ROWS, HIST_ROWS), HIST_ROWS), :]
            win[0:HIST_ROWS, :] = past.astype(F32)
            win[HIST_ROWS:HIST_ROWS + ROW_BLOCK, :] = src_refs[t][hh][rows, :].astype(F32)
            convs.append(_conv_silu(
                lambda j, win=win: win[HIST_ROWS - hist + j:HIST_ROWS - hist + j + ROW_BLOCK, :],
                tap_refs[t][hh][...]))
        beta_b = jnp.broadcast_to(brow_ref[hh, :, rows], (LANES, ROW_BLOCK)).T
        g_b = jnp.broadcast_to(grow_ref[hh, :, rows], (LANES, ROW_BLOCK)).T
        q, k, q_dec, k_dec, y, dec = _gdn_pointwise(convs[0], convs[1], convs[2], beta_b, g_b)
        qs_ref[hh, rows, :] = q.astype(BF16)
        ks_ref[hh, rows, :] = k.astype(BF16)
        y_ref[hh, rows, :] = y.astype(BF16)
        bb_ref[hh, rows, :] = beta_b
        gb_ref[hh, rows, :] = g_b
        qd_stage[hh, rows, :] = q_dec.astype(BF16)
        kdt_stage[hh, :, rows] = k_dec.T.astype(BF16)
        n_chunks = ROW_BLOCK // CHUNK
        dec_rows = [dec[c * CHUNK:c * CHUNK + 1, :] for c in range(n_chunks)]
        dec_rows.append(jnp.zeros((DEC_ROWS - n_chunks, LANES), F32))
        dec_stage[hh, pl.ds(pl.multiple_of(i * DEC_ROWS, DEC_ROWS), DEC_ROWS), :] = (
            jnp.concatenate(dec_rows, axis=0))

    def pointwise_thunks(i, bank):
        return [functools.partial(pointwise_head, i, bank, hh) for hh in range(HEAD_PAIR)]

    for i in range(PREP_UNROLL):
        for thunk in pointwise_thunks(i, i):
            thunk()

    def prep_groups(gi, fillers):
        groups = [gi * PREP_UNROLL + u for u in range(PREP_UNROLL) for _ in range(HEAD_PAIR)]
        keys = [(hh, pl.ds(pl.multiple_of((gi * PREP_UNROLL + u) * GROUP, GROUP), GROUP))
                for u in range(PREP_UNROLL) for hh in range(HEAD_PAIR)]
        probs = [(qs_ref[hh, rows, :], ks_ref[hh, rows, :], y_ref[hh, rows, :],
                  bb_ref[hh, rows, :], gb_ref[hh, rows, :], grow_ref[hh, :, rows])
                 for hh, rows in keys]
        for g, (hh, rows), (uw, a_qk) in zip(groups, keys, _gdn_groups(probs, fillers)):
            hs = slot * HEAD_PAIR + hh
            u_ref[hs, rows, :] = uw[:, :HEAD_DIM]
            w_ref[hs, rows, :] = uw[:, HEAD_DIM:].astype(BF16)
            aqk_ref[hs, rows, :] = a_qk.astype(BF16)
            drows = pl.ds(pl.multiple_of(g * DEC_ROWS, DEC_ROWS), DEC_ROWS)
            qd_ref[hs, rows, :] = qd_stage[hh, rows, :]
            kdt_ref[hs, :, rows] = kdt_stage[hh, :, rows]
            dec_ref[hs, drows, :] = dec_stage[hh, drows, :]

    def next_pointwise(gi):
        return [th for u in range(PREP_UNROLL)
                for th in pointwise_thunks((gi + 1) * PREP_UNROLL + u, u)]

    def rec_thunks(gi):
        r0 = pl.multiple_of(gi * GROUP, GROUP)
        rows = pl.ds(r0, GROUP)
        heads = range(REC_HEADS)
        outs = [[] for _ in heads]

        held = {}

        def read_state(c):
            crow = pl.ds(r0 + c * CHUNK, CHUNK)
            states = [st_ref[h] for h in heads]
            s_bf = [s.astype(BF16) for s in states]
            ws = [_dot(jnp.concatenate([w_ref[h, crow, :], qd_ref[h, crow, :]], axis=0), s_bf[h])
                  for h in heads]
            held[c] = (states, ws)

        def write_state(c):
            crow = pl.ds(r0 + c * CHUNK, CHUNK)
            states, ws = held.pop(c)
            v_new = [u_ref[h, crow, :] - ws[h][:CHUNK, :] for h in heads]
            zero = lambda n: jnp.zeros((n * CHUNK, HEAD_DIM), BF16)
            v_pad = [jnp.concatenate(([zero(c)] if c else []) + [v.astype(BF16)]
                                     + ([zero(cpg - 1 - c)] if c < cpg - 1 else []), axis=0)
                     for v in v_new]
            upd = [_dot(jnp.concatenate([aqk_ref[h, crow, :], kdt_ref[h, :, rows]], axis=0),
                        v_pad[h]) for h in heads]
            for h in heads:
                dec = dec_ref[h, pl.ds(gi * DEC_ROWS + c, 1), :]
                st_ref[h] = states[h] * dec + upd[h][CHUNK:, :]
                outs[h].append(ws[h][CHUNK:, :] + upd[h][:CHUNK, :])
            if c == cpg - 1:
                for h in heads:
                    o = _rms(jnp.concatenate(outs[h], axis=0), nw_ref[...])
                    o_ref[h, rows, :] = (o * _silu(z_ref[h, rows, :].astype(F32))).astype(BF16)

        return [functools.partial(half, c) for c in range(cpg) for half in (read_state, write_state)]

    def interleave(a, b):
        if not a or not b:
            return list(a) + list(b)
        out, j = [], 0
        for i, th in enumerate(a):
            out.append(th)
            while j < len(b) and (j + 1) * len(a) <= (i + 1) * len(b):
                out.append(b[j])
                j += 1
        return out + list(b[j:])

    n_trips = n_groups // PREP_UNROLL

    @pl.when(slot != REC_PAIRS - 1)
    def _():
        def prep_body(gi, _):
            prep_groups(gi, next_pointwise(gi))
            return 0
        lax.fori_loop(0, n_trips - 1, prep_body, 0)
        prep_groups(n_trips - 1, [])

    @pl.when(slot == REC_PAIRS - 1)
    def _():
        st_ref[...] = s0_ref[...]

        def rec_of_trip(t):
            return [th for u in range(PREP_UNROLL) for th in rec_thunks(t * PREP_UNROLL + u)]

        prep_groups(0, next_pointwise(0))

        def prep_rec_body(gi, _):
            prep_groups(gi, interleave(rec_of_trip(gi - 1), next_pointwise(gi)))
            return 0
        lax.fori_loop(1, n_trips - 1, prep_rec_body, 0)
        prep_groups(n_trips - 1, rec_of_trip(n_trips - 2))
        for thunk in rec_of_trip(n_trips - 1):
            thunk()


def _gdn(proj, proj_m, conv_wt, row4, state0, norm_w):
    _, b, seq, _ = proj.shape
    n_pairs = GDN_HEADS // HEAD_PAIR

    def head_specs(shape, imap):
        return [pl.BlockSpec(shape, functools.partial(imap, t * GDN_HEADS + hh))
                for t in range(3) for hh in range(HEAD_PAIR)]

    src_specs = head_specs((None, None, seq, HEAD_DIM),
                           lambda off, i, j: (off + HEAD_PAIR * j, i, 0, 0))
    meta_specs = head_specs((None, META_ROWS, HEAD_DIM), lambda off, i, j: (off + HEAD_PAIR * j, 0, 0))
    tap_specs = head_specs((CONV_WIDTH, HEAD_DIM), lambda off, i, j: (0, off + HEAD_PAIR * j))
    n_qkv = 3 * HEAD_PAIR
    z_block0 = 3 * GDN_HEADS // REC_HEADS

    per_pair = lambda width, dt: pltpu.VMEM((HEAD_PAIR, seq, width), dt)
    per_rec = lambda width, dt: pltpu.VMEM((REC_HEADS, seq, width), dt)
    return pl.pallas_call(
        functools.partial(_gdn_kernel, seq=seq),
        out_shape=jax.ShapeDtypeStruct((GDN_HEADS, b, seq, HEAD_DIM), BF16),
        grid=(b, n_pairs),
        in_specs=src_specs
        + [pl.BlockSpec((REC_HEADS, None, seq, HEAD_DIM),
                        lambda i, j: (z_block0 + j // REC_PAIRS, i, 0, 0))]
        + meta_specs + tap_specs
        + [pl.BlockSpec((None, HEAD_PAIR, 1, seq),
                        lambda i, j: (i, BETA_LANE // HEAD_PAIR + j, 0, 0)),
           pl.BlockSpec((None, HEAD_PAIR, 1, seq),
                        lambda i, j: (i, DECAY_LANE // HEAD_PAIR + j, 0, 0)),
           pl.BlockSpec((REC_HEADS, HEAD_DIM, HEAD_DIM), lambda i, j: (j // REC_PAIRS, 0, 0)),
           pl.BlockSpec((1, HEAD_DIM), lambda i, j: (0, 0))],
        out_specs=pl.BlockSpec((REC_HEADS, None, seq, HEAD_DIM),
                               lambda i, j: (j // REC_PAIRS, i, 0, 0)),
        scratch_shapes=[pltpu.VMEM((PREP_UNROLL * n_qkv, HIST_ROWS + ROW_BLOCK, HEAD_DIM), F32),
                        per_pair(HEAD_DIM, BF16), per_pair(HEAD_DIM, BF16),
                        per_pair(2 * HEAD_DIM, BF16),
                        per_pair(HEAD_DIM, F32), per_pair(HEAD_DIM, F32),
                        per_rec(HEAD_DIM, BF16),
                        pltpu.VMEM((REC_HEADS, HEAD_DIM, seq), BF16),
                        pltpu.VMEM((REC_HEADS, seq // GROUP * DEC_ROWS, HEAD_DIM), F32),
                        per_rec(HEAD_DIM, F32), per_rec(HEAD_DIM, BF16), per_rec(GROUP, BF16),
                        pltpu.VMEM((REC_HEADS, HEAD_DIM, HEAD_DIM), F32),
                        per_pair(HEAD_DIM, BF16),
                        pltpu.VMEM((HEAD_PAIR, HEAD_DIM, seq), BF16),
                        pltpu.VMEM((HEAD_PAIR, seq // GROUP * DEC_ROWS, HEAD_DIM), F32)],
        compiler_params=pltpu.CompilerParams(
            dimension_semantics=("parallel", "arbitrary"), vmem_limit_bytes=VMEM_LIMIT),
        name="gdn",
    )(*([proj] * n_qkv), proj, *([proj_m] * n_qkv), *([conv_wt] * n_qkv),
      row4, row4, state0, norm_w)


def _fox_kernel(q_ref, k_ref, v_ref, g_ref, km_ref, vm_ref, colm_ref, crow_ref,
                qw_ref, kw_ref, o_ref, qt_ref, ka_ref, kam_ref, vt_ref, vtm_ref,
                m_ref, l_ref, acc_ref, *, seq):
    n_blocks = seq // ROW_BLOCK
    aug_r = lax.broadcasted_iota(jnp.int32, (LANES, ROW_BLOCK), 0)

    def key_aug(ck, valid=None):
        hi, mid, lo = _split3(ck)
        lane = lax.broadcasted_iota(jnp.int32, ck.shape, 1)
        neg_hi = -hi.astype(F32)
        if valid is not None:
            neg_hi = jnp.where(valid, neg_hi, NEG_BIG)
        blk = jnp.where(lane < 3, 1.0,
                        jnp.where(lane == 3, neg_hi,
                                  jnp.where(lane == 4, -mid.astype(F32),
                                            jnp.where(lane == 5, -lo.astype(F32), 0.0))))
        return blk.astype(BF16)

    def pro_body(i, _):
        r0 = pl.multiple_of(i * ROW_BLOCK, ROW_BLOCK)
        rows = pl.ds(r0, ROW_BLOCK)
        for hd in range(FOX_GROUP):
            qn = _rms(q_ref[hd, rows, :].astype(F32), qw_ref[...]) * (HEAD_DIM ** -0.5 * LOG2E)
            qt_ref[hd, 0:HEAD_DIM, rows] = qn.T.astype(BF16)
            hi, mid, lo = _split3(crow_ref[hd, :, rows] * LOG2E)
            aug = jnp.where(aug_r == 0, hi.astype(F32),
                            jnp.where(aug_r == 1, mid.astype(F32),
                                      jnp.where(aug_r == 2, lo.astype(F32),
                                                jnp.where(aug_r < 6, 1.0, 0.0))))
            qt_ref[hd, HEAD_DIM:2 * HEAD_DIM, rows] = aug.astype(BF16)
            ka_ref[hd, rows, 0:HEAD_DIM] = _rms(k_ref[hd, rows, :].astype(F32),
                                                kw_ref[...]).astype(BF16)
            aug_k = jnp.where(aug_r < 3, 1.0,
                              jnp.where(aug_r == 3, -hi.astype(F32),
                                        jnp.where(aug_r == 4, -mid.astype(F32),
                                                  jnp.where(aug_r == 5, -lo.astype(F32), 0.0))))
            ka_ref[hd, rows, HEAD_DIM:2 * HEAD_DIM] = aug_k.T.astype(BF16)
            vt_ref[hd, :, rows] = v_ref[hd, rows, :].astype(F32).T.astype(BF16)
        return 0
    lax.fori_loop(0, n_blocks, pro_body, 0, unroll=BLOCK_UNROLL // FOX_GROUP)

    mrow = lax.broadcasted_iota(jnp.int32, (META_ROWS, LANES), 0)
    for hd in range(FOX_GROUP):
        head = pl.program_id(1) * FOX_GROUP + hd
        kam_ref[hd, :, 0:HEAD_DIM] = _rms(km_ref[hd].astype(F32), kw_ref[...]).astype(BF16)
        ck_m = _lane_bcast(colm_ref[...], FORGET_LANE + head) * LOG2E
        kam_ref[hd, :, HEAD_DIM:2 * HEAD_DIM] = key_aug(ck_m, mrow >= META_PAD)
        vtm_ref[hd] = vm_ref[hd].astype(F32).T.astype(BF16)

    kidx = lax.broadcasted_iota(jnp.int32, (FOX_TK, FOX_TK), 0)
    qidx = lax.broadcasted_iota(jnp.int32, (FOX_TK, FOX_TK), 1)
    diag_ok = kidx <= qidx

    items = []
    for kj in range(-1, seq // FOX_TK):
        lane0 = max(kj, 0) * FOX_TK
        for p0 in range(lane0, seq, FOX_PIECE):
            for hd in range(FOX_GROUP):
                items.append((hd, kj, slice(p0, min(p0 + FOX_PIECE, seq)), p0 == lane0))

    def scores(item):
        hd, kj, ln, leads = item
        k_aug = kam_ref[hd] if kj < 0 else ka_ref[hd, kj * FOX_TK:(kj + 1) * FOX_TK, :]
        s = _dot(k_aug, qt_ref[hd, :, ln])
        if kj >= 0 and leads:
            masked = jnp.where(diag_ok, s[:, :FOX_TK], NEG_BIG)
            s = jnp.concatenate([masked, s[:, FOX_TK:]], axis=1) if s.shape[1] > FOX_TK else masked
        return s

    def softmax_stats(item, s):
        hd, kj, ln, _ = item
        top = jnp.max(s, axis=0, keepdims=True)
        if kj < 0:
            m_new, alpha = top, None
        else:
            m_old = m_ref[hd, :, ln]
            m_new = jnp.maximum(m_old, top)
            alpha = jnp.exp2(m_old - m_new)
        p = jnp.exp2(s - m_new)
        psum = jnp.sum(p, axis=0, keepdims=True)
        m_ref[hd, :, ln] = m_new
        l_ref[hd, :, ln] = psum if kj < 0 else alpha * l_ref[hd, :, ln] + psum
        return p.astype(BF16), alpha

    def values(item, p, alpha):
        hd, kj, ln, _ = item
        v_t = vtm_ref[hd] if kj < 0 else vt_ref[hd, :, kj * FOX_TK:(kj + 1) * FOX_TK]
        pv = _dot(v_t, p)
        acc_ref[hd, :, ln] = pv if kj < 0 else alpha * acc_ref[hd, :, ln] + pv

    s_cur = scores(items[0])
    p_cur = None
    for t in range(len(items) + 1):
        s_next = scores(items[t + 1]) if t + 1 < len(items) else None
        p_next = softmax_stats(items[t], s_cur) if t < len(items) else None
        if p_cur is not None:
            values(items[t - 1], *p_cur)
        s_cur, p_cur = s_next, p_next

    def out_body(i, _):
        r0 = pl.multiple_of(i * ROW_BLOCK, ROW_BLOCK)
        rows = pl.ds(r0, ROW_BLOCK)
        for hd in range(FOX_GROUP):
            out_t = acc_ref[hd, :, rows] * (1.0 / l_ref[hd, :, rows])
            o_ref[hd, rows, :] = (out_t.T * _silu(g_ref[hd, rows, :].astype(F32))).astype(BF16)
        return 0
    lax.fori_loop(0, n_blocks, out_body, 0, unroll=BLOCK_UNROLL // FOX_GROUP)


def _fox(proj, proj_m, col_m, row4, q_w, k_w):
    _, b, seq, _ = proj.shape
    base = 4 * GDN_HEADS // FOX_GROUP
    hb = FOX_HEADS // FOX_GROUP

    def head_block(off):
        return pl.BlockSpec((FOX_GROUP, None, seq, HEAD_DIM),
                            lambda i, j, off=off: (off + j, i, 0, 0))

    def meta_block(off):
        return pl.BlockSpec((FOX_GROUP, META_ROWS, HEAD_DIM), lambda i, j, off=off: (off + j, 0, 0))

    per_head = lambda shape, dt: pltpu.VMEM((FOX_GROUP,) + shape, dt)
    return pl.pallas_call(
        functools.partial(_fox_kernel, seq=seq),
        out_shape=jax.ShapeDtypeStruct((FOX_HEADS, b, seq, HEAD_DIM), BF16),
        grid=(b, FOX_HEADS // FOX_GROUP),
        in_specs=[head_block(base), head_block(base + hb), head_block(base + 2 * hb),
                  head_block(base + 3 * hb),
                  meta_block(base + hb), meta_block(base + 2 * hb),
                  pl.BlockSpec((None, META_ROWS, LANES), lambda i, j: (0, 0, 0)),
                  pl.BlockSpec((None, FOX_GROUP, 1, seq),
                               lambda i, j: (i, FORGET_LANE // FOX_GROUP + j, 0, 0)),
                  pl.BlockSpec((1, HEAD_DIM), lambda i, j: (0, 0)),
                  pl.BlockSpec((1, HEAD_DIM), lambda i, j: (0, 0))],
        out_specs=pl.BlockSpec((FOX_GROUP, None, seq, HEAD_DIM), lambda i, j: (j, i, 0, 0)),
        scratch_shapes=[per_head((2 * HEAD_DIM, seq), BF16),
                        per_head((seq, 2 * HEAD_DIM), BF16),
                        per_head((META_ROWS, 2 * HEAD_DIM), BF16),
                        per_head((HEAD_DIM, seq), BF16),
                        per_head((HEAD_DIM, META_ROWS), BF16),
                        per_head((1, seq), F32), per_head((1, seq), F32),
                        per_head((HEAD_DIM, seq), F32)],
        compiler_params=pltpu.CompilerParams(
            dimension_semantics=("parallel", "arbitrary"), vmem_limit_bytes=VMEM_LIMIT),
        name="fox",
    )(proj, proj, proj, proj, proj_m, proj_m, col_m, row4, q_w, k_w)


def _out_proj_kernel(mg_ref, mf_ref, wg_ref, wf_ref, pw_ref, x_ref, o_ref):
    for r in range(o_ref.shape[0] // OUT_PROJ_CHUNK):
        rows = slice(r * OUT_PROJ_CHUNK, (r + 1) * OUT_PROJ_CHUNK)

        def rows_of(m_ref):
            return jnp.concatenate([m_ref[h, rows, :] for h in range(m_ref.shape[0])], axis=1)
        out = _dot(rows_of(mg_ref), wg_ref[...]) + _dot(rows_of(mf_ref), wf_ref[...])
        o_ref[rows, :] = x_ref[rows, :] + _rms(out, pw_ref[...])


def _out_proj(mg, mf, w_g, w_f, post_w, x2d, tm):
    m, d = x2d.shape
    return pl.pallas_call(
        _out_proj_kernel,
        out_shape=jax.ShapeDtypeStruct((m, d), F32),
        grid=(m // tm,),
        in_specs=[pl.BlockSpec((GDN_HEADS, tm, HEAD_DIM), lambda i: (0, i, 0)),
                  pl.BlockSpec((FOX_HEADS, tm, HEAD_DIM), lambda i: (0, i, 0)),
                  pl.BlockSpec((GDN_WIDTH, d), lambda i: (0, 0), pipeline_mode=pl.Buffered(1)),
                  pl.BlockSpec((FOX_WIDTH, d), lambda i: (0, 0), pipeline_mode=pl.Buffered(1)),
                  pl.BlockSpec((1, d), lambda i: (0, 0)),
                  pl.BlockSpec((tm, d), lambda i: (i, 0))],
        out_specs=pl.BlockSpec((tm, d), lambda i: (i, 0)),
        compiler_params=pltpu.CompilerParams(
            dimension_semantics=("parallel",), vmem_limit_bytes=VMEM_LIMIT),
        name="out_proj",
    )(mg, mf, w_g, w_f, post_w, x2d)


def _tile(total, want):
    t = min(total, want)
    while total % t:
        t //= 2
    return t


def _layer(x, meta_pad, pre_w, w_in, conv_w, a_log, dt_bias, gdn_norm_w, fox_q_w, fox_k_w,
           fox_f_bias, w_out, post_w):
    b, seq, d = x.shape
    assert seq % (GROUP * PREP_UNROLL) == 0 and seq % FOX_TK == 0 and seq % ROW_BLOCK == 0
    gw, fw = GDN_WIDTH, FOX_WIDTH
    o_gb = 4 * gw
    o_f = o_gb + 2 * GDN_HEADS
    o_ff = o_f + 4 * fw
    w_t = w_in.T
    w_main = _w_prep(w_t, MAIN_WIDTH, o_gb, o_f - o_gb)
    w_gate = jnp.concatenate(
        [w_t[o_gb:o_f], w_t[o_ff:],
         jnp.zeros((GATE_WIDTH - 2 * GDN_HEADS - FOX_HEADS, d), w_in.dtype)], axis=0)
    zpad = jnp.zeros((GATE_WIDTH - FORGET_LANE - FOX_HEADS,), F32)
    add_vec = jnp.concatenate([jnp.zeros((GDN_HEADS,), F32), dt_bias, fox_f_bias, zpad])[None]
    alog_vec = jnp.concatenate([jnp.zeros((GDN_HEADS,), F32), a_log,
                                jnp.zeros((FOX_HEADS,), F32), zpad])[None]

    x2d = x.reshape(b * seq, d)
    pre_w2 = pre_w[None]
    proj, gate = _in_proj(x2d, pre_w2, w_main, w_gate, _tile(b * seq, IN_PROJ_TM), IN_PROJ_TN)
    proj_m, gate_m = _in_proj(meta_pad, pre_w2, w_main, w_gate, META_ROWS, IN_PROJ_TN)
    proj = proj.reshape(MAIN_WIDTH // HEAD_DIM, b, seq, HEAD_DIM)

    row = _gate_rows(gate.reshape(b, seq, GATE_WIDTH), add_vec, alog_vec)
    col_m, row_m = _gate_prep(gate_m[None], add_vec, alog_vec, True)
    row4 = row.reshape(b, GATE_ROWS, 1, seq)
    row_m4 = row_m.reshape(1, GATE_ROWS, 1, META_ROWS)

    conv_wt = conv_w.T
    state0 = _gdn_state0(proj_m, conv_wt, col_m, row_m4)
    o_gdn = _gdn(proj, proj_m, conv_wt, row4, state0, gdn_norm_w[None])
    o_fox = _fox(proj, proj_m, col_m, row4, fox_q_w[None], fox_k_w[None])

    w_out_b = w_out.astype(BF16)
    out = _out_proj(o_gdn.reshape(GDN_HEADS, b * seq, HEAD_DIM),
                    o_fox.reshape(FOX_HEADS, b * seq, HEAD_DIM),
                    w_out_b[:gw], w_out_b[gw:], post_w[None], x2d, _tile(b * seq, OUT_PROJ_TM))
    return out.reshape(b, seq, d)


def kernel(x, meta_tokens, pre_norm_w, w_in, conv_w, a_log, dt_bias, gdn_norm_w, fox_q_norm_w,
           fox_k_norm_w, fox_f_bias, w_out, post_norm_w):
    assert pre_norm_w.shape[0] == 1, "single-layer stack"
    meta_pad = jnp.concatenate(
        [jnp.zeros((META_PAD, x.shape[-1]), x.dtype), meta_tokens.astype(x.dtype)], axis=0)
    return _layer(x, meta_pad, pre_norm_w[0], w_in[0], conv_w[0], a_log[0], dt_bias[0],
                  gdn_norm_w[0], fox_q_norm_w[0], fox_k_norm_w[0], fox_f_bias[0], w_out[0],
                  post_norm_w[0])
```

```python
import functools
import math

import jax
import jax.numpy as jnp
from jax import lax
from jax.experimental import pallas as pl
from jax.experimental.pallas import tpu as pltpu

N_META = 16
HEAD_DIM = 128
GDN_HEADS = 8
FOX_HEADS = 8
GDN_WIDTH = GDN_HEADS * HEAD_DIM
FOX_WIDTH = FOX_HEADS * HEAD_DIM
CONV_WIDTH = 4
CHUNK = 64
EPS = 1e-6

LANES = 128
SUBLANES = 8
MXU_DIM = 256
MAIN_WIDTH = 4 * GDN_WIDTH + 4 * FOX_WIDTH
GATE_WIDTH = LANES
BETA_LANE, DECAY_LANE, FORGET_LANE = 0, GDN_HEADS, 2 * GDN_HEADS
GATE_ROWS = 32
META_ROWS = CHUNK
META_PAD = META_ROWS - N_META
GROUP = MXU_DIM
ROW_BLOCK = 256
HIST_ROWS = 2 * SUBLANES
IN_PROJ_TM, IN_PROJ_TN = 1024, 2048
W_PREP_ROWS = 512
OUT_PROJ_TM = 1024
OUT_PROJ_CHUNK = 512
HEAD_PAIR = 2
REC_PAIRS = 2
REC_HEADS = REC_PAIRS * HEAD_PAIR
PREP_UNROLL = 2
DEC_ROWS = SUBLANES
BLOCK_UNROLL = 4
FOX_GROUP = 4
FOX_TK = MXU_DIM
FOX_PIECE = 2 * MXU_DIM
VMEM_LIMIT = 56 * 1024 * 1024

F32 = jnp.float32
BF16 = jnp.bfloat16
NEG_BIG = -1e30
LOG2E = math.log2(math.e)


def _dot(a, b):
    return jnp.dot(a, b, preferred_element_type=F32)


def _dot_nt(a, b):
    return lax.dot_general(a, b, (((1,), (1,)), ((), ())), preferred_element_type=F32)


def _split3(x):
    hi = x.astype(BF16)
    r1 = x - hi.astype(F32)
    mid = r1.astype(BF16)
    lo = (r1 - mid.astype(F32)).astype(BF16)
    return hi, mid, lo


def _dot_exact_rhs01(parts, m):
    return _dot(parts[0], m) + _dot(parts[1], m) + _dot(parts[2], m)


def _dot_exact_lhs01(m, parts):
    return _dot(m, parts[0]) + _dot(m, parts[1]) + _dot(m, parts[2])


def _lane_bcast(col_tile, lane):
    sel = (lax.broadcasted_iota(jnp.int32, (LANES, LANES), 0) == lane).astype(BF16)
    return _dot_exact_rhs01(_split3(col_tile), sel)


def _chunk_of(idx):
    return jnp.right_shift(idx, CHUNK.bit_length() - 1)


def _rms(x, w):
    return x * lax.rsqrt(jnp.mean(x * x, axis=-1, keepdims=True) + EPS) * w


def _silu(x):
    half = 0.5 * x
    return half + half * jnp.tanh(half)


def _softplus(x):
    return jnp.maximum(x, 0.0) + jnp.log1p(jnp.exp(-jnp.abs(x)))


def _w_prep_kernel(w_hbm, o_ref, buf, sem, *, rows, shift_from, shift):
    r = pl.program_id(0)

    def fetch(step, slot):
        start = step * rows
        src = pl.multiple_of(start + jnp.where(start >= shift_from, shift, 0), SUBLANES)
        return pltpu.make_async_copy(w_hbm.at[pl.ds(src, rows), :], buf.at[slot], sem.at[slot])

    @pl.when(r == 0)
    def _():
        fetch(0, 0).start()

    @pl.when(r + 1 < pl.num_programs(0))
    def _():
        fetch(r + 1, (r + 1) % 2).start()

    fetch(r, r % 2).wait()
    o_ref[...] = buf[r % 2].astype(BF16)


def _w_prep(w_t, n_rows, shift_from, shift):
    d = w_t.shape[1]
    rows = W_PREP_ROWS
    return pl.pallas_call(
        functools.partial(_w_prep_kernel, rows=rows, shift_from=shift_from, shift=shift),
        out_shape=jax.ShapeDtypeStruct((n_rows, d), BF16),
        grid=(n_rows // rows,),
        in_specs=[pl.BlockSpec(memory_space=pl.ANY)],
        out_specs=pl.BlockSpec((rows, d), lambda r: (r, 0)),
        scratch_shapes=[pltpu.VMEM((2, rows, d), F32), pltpu.SemaphoreType.DMA((2,))],
        compiler_params=pltpu.CompilerParams(
            dimension_semantics=("arbitrary",), vmem_limit_bytes=VMEM_LIMIT),
        name="w_prep",
    )(w_t)


def _in_proj_kernel(x_ref, nw_ref, w_ref, wg_ref, o_ref, og_ref, xn_ref):
    @pl.when(pl.program_id(1) == 0)
    def _():
        xn = _rms(x_ref[...], nw_ref[...]).astype(BF16)
        xn_ref[...] = xn
        og_ref[...] = _dot_nt(xn, wg_ref[...].astype(BF16))

    res = _dot_nt(xn_ref[...], w_ref[...])
    for hd in range(o_ref.shape[0]):
        o_ref[hd] = res[:, hd * HEAD_DIM:(hd + 1) * HEAD_DIM].astype(o_ref.dtype)


def _in_proj(x2d, norm_w, w_main, w_gate, tm, tn):
    m, d = x2d.shape
    n = w_main.shape[0]
    return pl.pallas_call(
        _in_proj_kernel,
        out_shape=(jax.ShapeDtypeStruct((n // HEAD_DIM, m, HEAD_DIM), BF16),
                   jax.ShapeDtypeStruct((m, GATE_WIDTH), F32)),
        grid=(m // tm, n // tn),
        in_specs=[pl.BlockSpec((tm, d), lambda i, j: (i, 0)),
                  pl.BlockSpec((1, d), lambda i, j: (0, 0)),
                  pl.BlockSpec((tn, d), lambda i, j: (j, 0)),
                  pl.BlockSpec((GATE_WIDTH, d), lambda i, j: (0, 0))],
        out_specs=(pl.BlockSpec((tn // HEAD_DIM, tm, HEAD_DIM), lambda i, j: (j, i, 0)),
                   pl.BlockSpec((tm, GATE_WIDTH), lambda i, j: (i, 0))),
        scratch_shapes=[pltpu.VMEM((tm, d), BF16)],
        compiler_params=pltpu.CompilerParams(
            dimension_semantics=("parallel", "arbitrary"), vmem_limit_bytes=VMEM_LIMIT),
        name="in_proj",
    )(x2d, norm_w, w_main, w_gate)


def _gate_meta_kernel(t_ref, add_ref, alog_ref, col_ref, row_ref):
    lane = lax.broadcasted_iota(jnp.int32, (META_ROWS, LANES), 1)
    row = lax.broadcasted_iota(jnp.int32, (META_ROWS, LANES), 0)
    ri = lax.broadcasted_iota(jnp.int32, (META_ROWS, META_ROWS), 0)
    ci = lax.broadcasted_iota(jnp.int32, (META_ROWS, META_ROWS), 1)
    tri = (ci <= ri).astype(BF16)
    is_beta = lane < DECAY_LANE
    is_decay = (lane >= DECAY_LANE) & (lane < FORGET_LANE)
    is_forget = (lane >= FORGET_LANE) & (lane < FORGET_LANE + FOX_HEADS)
    real = row >= META_PAD

    t = t_ref[...]
    ta = t + add_ref[...]
    beta = jnp.where(real, 1.0 / (1.0 + jnp.exp(-t)), 0.0)
    g = -jnp.exp(alog_ref[...]) * _softplus(ta)
    logf = -_softplus(-ta)
    val = jnp.where(real, jnp.where(is_decay, g, jnp.where(is_forget, logf, 0.0)), 0.0)
    cum = _dot_exact_lhs01(tri, _split3(val))
    to_come = cum - cum[META_ROWS - 1:META_ROWS, :]
    res = jnp.where(is_beta, beta, jnp.where(is_decay, cum, to_come))
    col_ref[...] = res
    row_ref[...] = res.T[:GATE_ROWS, :]


def _gate_rows_kernel(t_ref, add_ref, alog_ref, row_ref, *, rows):
    blk = ROW_BLOCK
    row = lax.broadcasted_iota(jnp.int32, (GATE_ROWS, blk), 0)
    si = lax.broadcasted_iota(jnp.int32, (blk, blk), 0)
    ti = lax.broadcasted_iota(jnp.int32, (blk, blk), 1)
    tri = (si <= ti).astype(BF16)
    tri_chunk = ((si <= ti) & (_chunk_of(si) == _chunk_of(ti))).astype(BF16)
    is_beta = row < DECAY_LANE
    is_decay = (row >= DECAY_LANE) & (row < FORGET_LANE)
    is_forget = (row >= FORGET_LANE) & (row < FORGET_LANE + FOX_HEADS)
    add = jnp.concatenate([add_ref[...]] * (blk // LANES), axis=1)
    neg_rate = -jnp.exp(jnp.concatenate([alog_ref[...]] * (blk // LANES), axis=1))

    carry = jnp.zeros((GATE_ROWS, 1), F32)
    for r in range(rows // blk):
        t = t_ref[r * blk:(r + 1) * blk, :].T[:GATE_ROWS, :]
        ta = t + add
        beta = 1.0 / (1.0 + jnp.exp(-t))
        val = jnp.where(is_decay, neg_rate * _softplus(ta), jnp.where(is_forget, -_softplus(-ta), 0.0))
        parts = _split3(val)
        cum_chunk = _dot_exact_rhs01(parts, tri_chunk)
        cum_all = _dot_exact_rhs01(parts, tri) + carry
        carry = cum_all[:, blk - 1:blk]
        row_ref[:, 0, r * blk:(r + 1) * blk] = jnp.where(is_beta, beta,
                                                         jnp.where(is_decay, cum_chunk, cum_all))


def _gate_rows(gate3d, add_vec, alog_vec):
    b, rows, _ = gate3d.shape
    per_row = lambda v: jnp.broadcast_to(v[0, :GATE_ROWS, None], (GATE_ROWS, LANES))
    return pl.pallas_call(
        functools.partial(_gate_rows_kernel, rows=rows),
        out_shape=jax.ShapeDtypeStruct((b, GATE_ROWS, 1, rows), F32),
        grid=(b,),
        in_specs=[pl.BlockSpec((None, rows, LANES), lambda i: (i, 0, 0)),
                  pl.BlockSpec((GATE_ROWS, LANES), lambda i: (0, 0)),
                  pl.BlockSpec((GATE_ROWS, LANES), lambda i: (0, 0))],
        out_specs=pl.BlockSpec((None, GATE_ROWS, 1, rows), lambda i: (i, 0, 0, 0)),
        compiler_params=pltpu.CompilerParams(
            dimension_semantics=("parallel",), vmem_limit_bytes=VMEM_LIMIT),
        name="gate_rows",
    )(gate3d, per_row(add_vec), per_row(alog_vec))


def _gate_meta(gate_m, add_vec, alog_vec):
    return pl.pallas_call(
        _gate_meta_kernel,
        out_shape=(jax.ShapeDtypeStruct((META_ROWS, LANES), F32),
                   jax.ShapeDtypeStruct((GATE_ROWS, META_ROWS), F32)),
        grid=(1,),
        in_specs=[pl.BlockSpec((META_ROWS, LANES), lambda i: (0, 0)),
                  pl.BlockSpec((1, LANES), lambda i: (0, 0)),
                  pl.BlockSpec((1, LANES), lambda i: (0, 0))],
        out_specs=(pl.BlockSpec((META_ROWS, LANES), lambda i: (0, 0)),
                   pl.BlockSpec((GATE_ROWS, META_ROWS), lambda i: (0, 0))),
        compiler_params=pltpu.CompilerParams(
            dimension_semantics=("arbitrary",), vmem_limit_bytes=VMEM_LIMIT),
        name="gate_meta",
    )(gate_m, add_vec, alog_vec)


def _conv_silu(load_rows, w):
    y = load_rows(0) * w[0:1, :]
    for j in range(1, CONV_WIDTH):
        y = y + load_rows(j) * w[j:j + 1, :]
    return _silu(y)


def _l2norm(x):
    return x * lax.rsqrt(jnp.sum(x * x, axis=-1, keepdims=True) + EPS)


def _gdn_pointwise(q, k, v, beta_b, g_b):
    r = q.shape[0]
    q = q * (lax.rsqrt(jnp.sum(q * q, axis=-1, keepdims=True) + EPS) * (HEAD_DIM ** -0.5))
    k = _l2norm(k)
    g3 = g_b.reshape(r // CHUNK, CHUNK, LANES)
    g_last = jnp.broadcast_to(g3[:, CHUNK - 1:CHUNK, :], g3.shape).reshape(r, LANES)
    e_g = jnp.exp(g_b)
    q_dec = q * e_g
    k_dec = k * jnp.exp(g_last - g_b)
    y = jnp.concatenate([v * beta_b, k * (beta_b * e_g)], axis=1)
    return q, k, q_dec, k_dec, y, jnp.exp(g_last)


def _gdn_groups(probs, fillers=()):
    r = probs[0][0].shape[0]
    ri = lax.broadcasted_iota(jnp.int32, (r, r), 0)
    ci = lax.broadcasted_iota(jnp.int32, (r, r), 1)
    same = _chunk_of(ri) == _chunk_of(ci)
    causal = same & (ci <= ri)
    strict = same & (ci < ri)

    def widen(t):
        return jnp.concatenate([t] * (r // LANES), axis=1) if r >= LANES else t[:, :r]

    fillers = list(fillers)
    n_ticks = CHUNK.bit_length()
    per_tick = -(-len(fillers) // n_ticks)

    def tick():
        for thunk in fillers[:per_tick]:
            thunk()
        del fillers[:per_tick]

    kks = [_dot_nt(k, k) for _, k, _, _, _, _ in probs]
    qks = [_dot_nt(q, k) for q, k, _, _, _, _ in probs]
    tick()
    dmats = [jnp.where(causal, jnp.exp(jnp.where(causal, widen(g_b) - g_row, 0.0)), 0.0)
             for _, _, _, _, g_b, g_row in probs]
    xs = [jnp.where(strict, widen(p[3]) * kk * d, 0.0).astype(BF16)
          for p, kk, d in zip(probs, kks, dmats)]
    a_qks = [qk * d for qk, d in zip(qks, dmats)]
    eye = (ri == ci).astype(BF16)
    zs = [_dot(eye - x, p[2]) for p, x in zip(probs, xs)]
    span = 2
    while span < CHUNK:
        xs = [_dot(x, x).astype(BF16) for x in xs]
        tick()
        zs = [z + _dot(x, z.astype(BF16)) for x, z in zip(xs, zs)]
        span *= 2
    tick()
    assert not fillers
    return list(zip(zs, a_qks))


def _gdn_state0_kernel(km_ref, vm_ref, wk_ref, wv_ref, colm_ref, growm_ref, s_ref, pad_ref):
    heads = range(GDN_HEADS)
    probs, kd_t = [], []
    for h in heads:
        ls = slice(h * HEAD_DIM, (h + 1) * HEAD_DIM)
        beta_m = _lane_bcast(colm_ref[...], BETA_LANE + h)
        g_m = _lane_bcast(colm_ref[...], DECAY_LANE + h)
        conv = []
        for t, (src, w_ref) in enumerate(((km_ref, wk_ref), (vm_ref, wv_ref))):
            win = pad_ref.at[2 * h + t]
            win[0:SUBLANES, :] = jnp.zeros((SUBLANES, LANES), F32)
            win[SUBLANES:SUBLANES + META_ROWS, :] = src[h].astype(F32)
            conv.append(_conv_silu(
                lambda j, win=win: win[SUBLANES - (CONV_WIDTH - 1) + j:
                                       SUBLANES - (CONV_WIDTH - 1) + j + META_ROWS, :],
                w_ref[:, ls]))
        _, k_m, _, kd_m, y_m, _ = _gdn_pointwise(conv[0], conv[0], conv[1], beta_m, g_m)
        k_bf = k_m.astype(BF16)
        probs.append((k_bf, k_bf, y_m.astype(BF16), beta_m, g_m, growm_ref[h]))
        kd_t.append(kd_m.T.astype(BF16))
    for h, (uw_m, _) in zip(heads, _gdn_groups(probs)):
        s_ref[h] = _dot(kd_t[h], uw_m[:, :HEAD_DIM].astype(BF16))


def _gdn_state0(proj_m, conv_wt, col_m, row_m4):
    heads_block = lambda blk: pl.BlockSpec((GDN_HEADS, META_ROWS, HEAD_DIM),
                                           lambda j, blk=blk: (blk, 0, 0))
    taps_block = lambda blk: pl.BlockSpec((CONV_WIDTH, GDN_WIDTH), lambda j, blk=blk: (0, blk))
    return pl.pallas_call(
        _gdn_state0_kernel,
        out_shape=jax.ShapeDtypeStruct((GDN_HEADS, HEAD_DIM, HEAD_DIM), F32),
        grid=(1,),
        in_specs=[heads_block(1), heads_block(2), taps_block(1), taps_block(2),
                  pl.BlockSpec((META_ROWS, LANES), lambda j: (0, 0)),
                  pl.BlockSpec((None, GDN_HEADS, 1, META_ROWS),
                               lambda j: (0, DECAY_LANE // GDN_HEADS, 0, 0))],
        out_specs=pl.BlockSpec((GDN_HEADS, HEAD_DIM, HEAD_DIM), lambda j: (0, 0, 0)),
        scratch_shapes=[pltpu.VMEM((2 * GDN_HEADS, META_ROWS + SUBLANES, HEAD_DIM), F32)],
        compiler_params=pltpu.CompilerParams(
            dimension_semantics=("arbitrary",), vmem_limit_bytes=VMEM_LIMIT),
        name="gdn_state0",
    )(proj_m, proj_m, conv_wt, conv_wt, col_m, row_m4)


def _gdn_kernel(*refs, seq):
    n_qkv = 3 * HEAD_PAIR
    src_refs = [refs[t * HEAD_PAIR:(t + 1) * HEAD_PAIR] for t in range(3)]
    z_ref = refs[n_qkv]
    meta_refs = [refs[n_qkv + 1 + t * HEAD_PAIR:n_qkv + 1 + (t + 1) * HEAD_PAIR] for t in range(3)]
    tap_refs = [refs[2 * n_qkv + 1 + t * HEAD_PAIR:2 * n_qkv + 1 + (t + 1) * HEAD_PAIR]
                for t in range(3)]
    (brow_ref, grow_ref, s0_ref, nw_ref, o_ref, pad_ref, qs_ref, ks_ref, y_ref, bb_ref, gb_ref,
     qd_ref, kdt_ref, dec_ref, u_ref, w_ref, aqk_ref, st_ref,
     qd_stage, kdt_stage, dec_stage) = refs[3 * n_qkv + 1:]
    pair = pl.program_id(1)
    slot = pair % REC_PAIRS
    n_blocks = seq // ROW_BLOCK
    n_groups = seq // GROUP
    cpg = GROUP // CHUNK
    hist = CONV_WIDTH - 1

    def pointwise_head(i, bank, hh):
        r0 = pl.multiple_of(i * ROW_BLOCK, ROW_BLOCK)
        rows = pl.ds(r0, ROW_BLOCK)
        convs = []
        for t in range(3):
            win = pad_ref.at[(bank * 3 + t) * HEAD_PAIR + hh]
            if isinstance(i, int) and i == 0:
                past = meta_refs[t][hh][META_ROWS - HIST_ROWS:META_ROWS, :]
            else:
                past = src_refs[t][hh][pl.ds(pl.multiple_of(r0 - HIST_ROWS, HIST_ROWS), HIST_ROWS), :]
            win[0:HIST_ROWS, :] = past.astype(F32)
            win[HIST_ROWS:HIST_ROWS + ROW_BLOCK, :] = src_refs[t][hh][rows, :].astype(F32)
            convs.append(_conv_silu(
                lambda j, win=win: win[HIST_ROWS - hist + j:HIST_ROWS - hist + j + ROW_BLOCK, :],
                tap_refs[t][hh][...]))
        beta_b = jnp.broadcast_to(brow_ref[hh, :, rows], (LANES, ROW_BLOCK)).T
        g_b = jnp.broadcast_to(grow_ref[hh, :, rows], (LANES, ROW_BLOCK)).T
        q, k, q_dec, k_dec, y, dec = _gdn_pointwise(convs[0], convs[1], convs[2], beta_b, g_b)
        qs_ref[hh, rows, :] = q.astype(BF16)
        ks_ref[hh, rows, :] = k.astype(BF16)
        y_ref[hh, rows, :] = y.astype(BF16)
        bb_ref[hh, rows, :] = beta_b
        gb_ref[hh, rows, :] = g_b
        qd_stage[hh, rows, :] = q_dec.astype(BF16)
        kdt_stage[hh, :, rows] = k_dec.T.astype(BF16)
        n_chunks = ROW_BLOCK // CHUNK
        dec_rows = [dec[c * CHUNK:c * CHUNK + 1, :] for c in range(n_chunks)]
        dec_rows.append(jnp.zeros((DEC_ROWS - n_chunks, LANES), F32))
        dec_stage[hh, pl.ds(pl.multiple_of(i * DEC_ROWS, DEC_ROWS), DEC_ROWS), :] = (
            jnp.concatenate(dec_rows, axis=0))

    def pointwise_thunks(i, bank):
        return [functools.partial(pointwise_head, i, bank, hh) for hh in range(HEAD_PAIR)]

    for i in range(PREP_UNROLL):
        for thunk in pointwise_thunks(i, i):
            thunk()

    def prep_groups(gi, fillers):
        groups = [gi * PREP_UNROLL + u for u in range(PREP_UNROLL) for _ in range(HEAD_PAIR)]
        keys = [(hh, pl.ds(pl.multiple_of((gi * PREP_UNROLL + u) * GROUP, GROUP), GROUP))
                for u in range(PREP_UNROLL) for hh in range(HEAD_PAIR)]
        probs = [(qs_ref[hh, rows, :], ks_ref[hh, rows, :], y_ref[hh, rows, :],
                  bb_ref[hh, rows, :], gb_ref[hh, rows, :], grow_ref[hh, :, rows])
                 for hh, rows in keys]
        for g, (hh, rows), (uw, a_qk) in zip(groups, keys, _gdn_groups(probs, fillers)):
            hs = slot * HEAD_PAIR + hh
            u_ref[hs, rows, :] = uw[:, :HEAD_DIM]
            w_ref[hs, rows, :] = uw[:, HEAD_DIM:].astype(BF16)
            aqk_ref[hs, rows, :] = a_qk.astype(BF16)
            drows = pl.ds(pl.multiple_of(g * DEC_ROWS, DEC_ROWS), DEC_ROWS)
            qd_ref[hs, rows, :] = qd_stage[hh, rows, :]
            kdt_ref[hs, :, rows] = kdt_stage[hh, :, rows]
            dec_ref[hs, drows, :] = dec_stage[hh, drows, :]

    def next_pointwise(gi):
        return [th for u in range(PREP_UNROLL)
                for th in pointwise_thunks((gi + 1) * PREP_UNROLL + u, u)]

    def rec_thunks(gi):
        r0 = pl.multiple_of(gi * GROUP, GROUP)
        rows = pl.ds(r0, GROUP)
        heads = range(REC_HEADS)
        outs = [[] for _ in heads]

        held = {}

        def read_state(c):
            crow = pl.ds(r0 + c * CHUNK, CHUNK)
            states = [st_ref[h] for h in heads]
            s_bf = [s.astype(BF16) for s in states]
            ws = [_dot(jnp.concatenate([w_ref[h, crow, :], qd_ref[h, crow, :]], axis=0), s_bf[h])
                  for h in heads]
            held[c] = (states, ws)

        def write_state(c):
            crow = pl.ds(r0 + c * CHUNK, CHUNK)
            states, ws = held.pop(c)
            v_new = [u_ref[h, crow, :] - ws[h][:CHUNK, :] for h in heads]
            zero = lambda n: jnp.zeros((n * CHUNK, HEAD_DIM), BF16)
            v_pad = [jnp.concatenate(([zero(c)] if c else []) + [v.astype(BF16)]
                                     + ([zero(cpg - 1 - c)] if c < cpg - 1 else []), axis=0)
                     for v in v_new]
            upd = [_dot(jnp.concatenate([aqk_ref[h, crow, :], kdt_ref[h, :, rows]], axis=0),
                        v_pad[h]) for h in heads]
            for h in heads:
                dec = dec_ref[h, pl.ds(gi * DEC_ROWS + c, 1), :]
                st_ref[h] = states[h] * dec + upd[h][CHUNK:, :]
                outs[h].append(ws[h][CHUNK:, :] + upd[h][:CHUNK, :])
            if c == cpg - 1:
                for h in heads:
                    o = _rms(jnp.concatenate(outs[h], axis=0), nw_ref[...])
                    o_ref[h, rows, :] = (o * _silu(z_ref[h, rows, :].astype(F32))).astype(BF16)

        return [functools.partial(half, c) for c in range(cpg) for half in (read_state, write_state)]

    def interleave(a, b):
        if not a or not b:
            return list(a) + list(b)
        out, j = [], 0
        for i, th in enumerate(a):
            out.append(th)
            while j < len(b) and (j + 1) * len(a) <= (i + 1) * len(b):
                out.append(b[j])
                j += 1
        return out + list(b[j:])

    n_trips = n_groups // PREP_UNROLL

    @pl.when(slot != REC_PAIRS - 1)
    def _():
        def prep_body(gi, _):
            prep_groups(gi, next_pointwise(gi))
            return 0
        lax.fori_loop(0, n_trips - 1, prep_body, 0)
        prep_groups(n_trips - 1, [])

    @pl.when(slot == REC_PAIRS - 1)
    def _():
        st_ref[...] = s0_ref[...]

        def rec_of_trip(t):
            return [th for u in range(PREP_UNROLL) for th in rec_thunks(t * PREP_UNROLL + u)]

        prep_groups(0, next_pointwise(0))

        def prep_rec_body(gi, _):
            prep_groups(gi, interleave(rec_of_trip(gi - 1), next_pointwise(gi)))
            return 0
        lax.fori_loop(1, n_trips - 1, prep_rec_body, 0)
        prep_groups(n_trips - 1, rec_of_trip(n_trips - 2))
        for thunk in rec_of_trip(n_trips - 1):
            thunk()


def _gdn(proj, proj_m, conv_wt, row4, state0, norm_w):
    _, b, seq, _ = proj.shape
    n_pairs = GDN_HEADS // HEAD_PAIR

    def head_specs(shape, imap):
        return [pl.BlockSpec(shape, functools.partial(imap, t * GDN_HEADS + hh))
                for t in range(3) for hh in range(HEAD_PAIR)]

    src_specs = head_specs((None, None, seq, HEAD_DIM),
                           lambda off, i, j: (off + HEAD_PAIR * j, i, 0, 0))
    meta_specs = head_specs((None, META_ROWS, HEAD_DIM), lambda off, i, j: (off + HEAD_PAIR * j, 0, 0))
    tap_specs = head_specs((CONV_WIDTH, HEAD_DIM), lambda off, i, j: (0, off + HEAD_PAIR * j))
    n_qkv = 3 * HEAD_PAIR
    z_block0 = 3 * GDN_HEADS // REC_HEADS

    per_pair = lambda width, dt: pltpu.VMEM((HEAD_PAIR, seq, width), dt)
    per_rec = lambda width, dt: pltpu.VMEM((REC_HEADS, seq, width), dt)
    return pl.pallas_call(
        functools.partial(_gdn_kernel, seq=seq),
        out_shape=jax.ShapeDtypeStruct((GDN_HEADS, b, seq, HEAD_DIM), BF16),
        grid=(b, n_pairs),
        in_specs=src_specs
        + [pl.BlockSpec((REC_HEADS, None, seq, HEAD_DIM),
                        lambda i, j: (z_block0 + j // REC_PAIRS, i, 0, 0))]
        + meta_specs + tap_specs
        + [pl.BlockSpec((None, HEAD_PAIR, 1, seq),
                        lambda i, j: (i, BETA_LANE // HEAD_PAIR + j, 0, 0)),
           pl.BlockSpec((None, HEAD_PAIR, 1, seq),
                        lambda i, j: (i, DECAY_LANE // HEAD_PAIR + j, 0, 0)),
           pl.BlockSpec((REC_HEADS, HEAD_DIM, HEAD_DIM), lambda i, j: (j // REC_PAIRS, 0, 0)),
           pl.BlockSpec((1, HEAD_DIM), lambda i, j: (0, 0))],
        out_specs=pl.BlockSpec((REC_HEADS, None, seq, HEAD_DIM),
                               lambda i, j: (j // REC_PAIRS, i, 0, 0)),
        scratch_shapes=[pltpu.VMEM((PREP_UNROLL * n_qkv, HIST_ROWS + ROW_BLOCK, HEAD_DIM), F32),
                        per_pair(HEAD_DIM, BF16), per_pair(HEAD_DIM, BF16),
                        per_pair(2 * HEAD_DIM, BF16),
                        per_pair(HEAD_DIM, F32), per_pair(HEAD_DIM, F32),
                        per_rec(HEAD_DIM, BF16),
                        pltpu.VMEM((REC_HEADS, HEAD_DIM, seq), BF16),
                        pltpu.VMEM((REC_HEADS, seq // GROUP * DEC_ROWS, HEAD_DIM), F32),
                        per_rec(HEAD_DIM, F32), per_rec(HEAD_DIM, BF16), per_rec(GROUP, BF16),
                        pltpu.VMEM((REC_HEADS, HEAD_DIM, HEAD_DIM), F32),
                        per_pair(HEAD_DIM, BF16),
                        pltpu.VMEM((HEAD_PAIR, HEAD_DIM, seq), BF16),
                        pltpu.VMEM((HEAD_PAIR, seq // GROUP * DEC_ROWS, HEAD_DIM), F32)],
        compiler_params=pltpu.CompilerParams(
            dimension_semantics=("parallel", "arbitrary"), vmem_limit_bytes=VMEM_LIMIT),
        name="gdn",
    )(*([proj] * n_qkv), proj, *([proj_m] * n_qkv), *([conv_wt] * n_qkv),
      row4, row4, state0, norm_w)


def _fox_kernel(q_ref, k_ref, v_ref, g_ref, km_ref, vm_ref, colm_ref, crow_ref,
                qw_ref, kw_ref, o_ref, qt_ref, ka_ref, kam_ref, vt_ref, vtm_ref,
                m_ref, l_ref, acc_ref, *, seq):
    n_blocks = seq // ROW_BLOCK
    aug_r = lax.broadcasted_iota(jnp.int32, (LANES, ROW_BLOCK), 0)

    def key_aug(ck, valid=None):
        hi, mid, lo = _split3(ck)
        lane = lax.broadcasted_iota(jnp.int32, ck.shape, 1)
        neg_hi = -hi.astype(F32)
        if valid is not None:
            neg_hi = jnp.where(valid, neg_hi, NEG_BIG)
        blk = jnp.where(lane < 3, 1.0,
                        jnp.where(lane == 3, neg_hi,
                                  jnp.where(lane == 4, -mid.astype(F32),
                                            jnp.where(lane == 5, -lo.astype(F32), 0.0))))
        return blk.astype(BF16)

    def pro_body(i, _):
        r0 = pl.multiple_of(i * ROW_BLOCK, ROW_BLOCK)
        rows = pl.ds(r0, ROW_BLOCK)
        for hd in range(FOX_GROUP):
            qn = _rms(q_ref[hd, rows, :].astype(F32), qw_ref[...]) * (HEAD_DIM ** -0.5 * LOG2E)
            qt_ref[hd, 0:HEAD_DIM, rows] = qn.T.astype(BF16)
            hi, mid, lo = _split3(crow_ref[hd, :, rows] * LOG2E)
            aug = jnp.where(aug_r == 0, hi.astype(F32),
                            jnp.where(aug_r == 1, mid.astype(F32),
                                      jnp.where(aug_r == 2, lo.astype(F32),
                                                jnp.where(aug_r < 6, 1.0, 0.0))))
            qt_ref[hd, HEAD_DIM:2 * HEAD_DIM, rows] = aug.astype(BF16)
            ka_ref[hd, rows, 0:HEAD_DIM] = _rms(k_ref[hd, rows, :].astype(F32),
                                                kw_ref[...]).astype(BF16)
            aug_k = jnp.where(aug_r < 3, 1.0,
                              jnp.where(aug_r == 3, -hi.astype(F32),
                                        jnp.where(aug_r == 4, -mid.astype(F32),
                                                  jnp.where(aug_r == 5, -lo.astype(F32), 0.0))))
            ka_ref[hd, rows, HEAD_DIM:2 * HEAD_DIM] = aug_k.T.astype(BF16)
            vt_ref[hd, :, rows] = v_ref[hd, rows, :].astype(F32).T.astype(BF16)
        return 0
    lax.fori_loop(0, n_blocks, pro_body, 0, unroll=BLOCK_UNROLL // FOX_GROUP)

    mrow = lax.broadcasted_iota(jnp.int32, (META_ROWS, LANES), 0)
    for hd in range(FOX_GROUP):
        head = pl.program_id(1) * FOX_GROUP + hd
        kam_ref[hd, :, 0:HEAD_DIM] = _rms(km_ref[hd].astype(F32), kw_ref[...]).astype(BF16)
        ck_m = _lane_bcast(colm_ref[...], FORGET_LANE + head) * LOG2E
        kam_ref[hd, :, HEAD_DIM:2 * HEAD_DIM] = key_aug(ck_m, mrow >= META_PAD)
        vtm_ref[hd] = vm_ref[hd].astype(F32).T.astype(BF16)

    kidx = lax.broadcasted_iota(jnp.int32, (FOX_TK, FOX_TK), 0)
    qidx = lax.broadcasted_iota(jnp.int32, (FOX_TK, FOX_TK), 1)
    diag_ok = kidx <= qidx

    items = []
    for kj in range(-1, seq // FOX_TK):
        lane0 = max(kj, 0) * FOX_TK
        for p0 in range(lane0, seq, FOX_PIECE):
            for hd in range(FOX_GROUP):
                items.append((hd, kj, slice(p0, min(p0 + FOX_PIECE, seq)), p0 == lane0))

    def scores(item):
        hd, kj, ln, leads = item
        k_aug = kam_ref[hd] if kj < 0 else ka_ref[hd, kj * FOX_TK:(kj + 1) * FOX_TK, :]
        s = _dot(k_aug, qt_ref[hd, :, ln])
        if kj >= 0 and leads:
            masked = jnp.where(diag_ok, s[:, :FOX_TK], NEG_BIG)
            s = jnp.concatenate([masked, s[:, FOX_TK:]], axis=1) if s.shape[1] > FOX_TK else masked
        return s

    def softmax_stats(item, s):
        hd, kj, ln, _ = item
        top = jnp.max(s, axis=0, keepdims=True)
        if kj < 0:
            m_new, alpha = top, None
        else:
            m_old = m_ref[hd, :, ln]
            m_new = jnp.maximum(m_old, top)
            alpha = jnp.exp2(m_old - m_new)
        p = jnp.exp2(s - m_new)
        psum = jnp.sum(p, axis=0, keepdims=True)
        m_ref[hd, :, ln] = m_new
        l_ref[hd, :, ln] = psum if kj < 0 else alpha * l_ref[hd, :, ln] + psum
        return p.astype(BF16), alpha

    def values(item, p, alpha):
        hd, kj, ln, _ = item
        v_t = vtm_ref[hd] if kj < 0 else vt_ref[hd, :, kj * FOX_TK:(kj + 1) * FOX_TK]
        pv = _dot(v_t, p)
        acc_ref[hd, :, ln] = pv if kj < 0 else alpha * acc_ref[hd, :, ln] + pv

    s_cur = scores(items[0])
    p_cur = None
    for t in range(len(items) + 1):
        s_next = scores(items[t + 1]) if t + 1 < len(items) else None
        p_next = softmax_stats(items[t], s_cur) if t < len(items) else None
        if p_cur is not None:
            values(items[t - 1], *p_cur)
        s_cur, p_cur = s_next, p_next

    def out_body(i, _):
        r0 = pl.multiple_of(i * ROW_BLOCK, ROW_BLOCK)
        rows = pl.ds(r0, ROW_BLOCK)
        for hd in range(FOX_GROUP):
            out_t = acc_ref[hd, :, rows] * (1.0 / l_ref[hd, :, rows])
            o_ref[hd, rows, :] = (out_t.T * _silu(g_ref[hd, rows, :].astype(F32))).astype(BF16)
        return 0
    lax.fori_loop(0, n_blocks, out_body, 0, unroll=BLOCK_UNROLL // FOX_GROUP)


def _fox(proj, proj_m, col_m, row4, q_w, k_w):
    _, b, seq, _ = proj.shape
    base = 4 * GDN_HEADS // FOX_GROUP
    hb = FOX_HEADS // FOX_GROUP

    def head_block(off):
        return pl.BlockSpec((FOX_GROUP, None, seq, HEAD_DIM),
                            lambda i, j, off=off: (off + j, i, 0, 0))

    def meta_block(off):
        return pl.BlockSpec((FOX_GROUP, META_ROWS, HEAD_DIM), lambda i, j, off=off: (off + j, 0, 0))

    per_head = lambda shape, dt: pltpu.VMEM((FOX_GROUP,) + shape, dt)
    return pl.pallas_call(
        functools.partial(_fox_kernel, seq=seq),
        out_shape=jax.ShapeDtypeStruct((FOX_HEADS, b, seq, HEAD_DIM), BF16),
        grid=(b, FOX_HEADS // FOX_GROUP),
        in_specs=[head_block(base), head_block(base + hb), head_block(base + 2 * hb),
                  head_block(base + 3 * hb),
                  meta_block(base + hb), meta_block(base + 2 * hb),
                  pl.BlockSpec((META_ROWS, LANES), lambda i, j: (0, 0)),
                  pl.BlockSpec((None, FOX_GROUP, 1, seq),
                               lambda i, j: (i, FORGET_LANE // FOX_GROUP + j, 0, 0)),
                  pl.BlockSpec((1, HEAD_DIM), lambda i, j: (0, 0)),
                  pl.BlockSpec((1, HEAD_DIM), lambda i, j: (0, 0))],
        out_specs=pl.BlockSpec((FOX_GROUP, None, seq, HEAD_DIM), lambda i, j: (j, i, 0, 0)),
        scratch_shapes=[per_head((2 * HEAD_DIM, seq), BF16),
                        per_head((seq, 2 * HEAD_DIM), BF16),
                        per_head((META_ROWS, 2 * HEAD_DIM), BF16),
                        per_head((HEAD_DIM, seq), BF16),
                        per_head((HEAD_DIM, META_ROWS), BF16),
                        per_head((1, seq), F32), per_head((1, seq), F32),
                        per_head((HEAD_DIM, seq), F32)],
        compiler_params=pltpu.CompilerParams(
            dimension_semantics=("parallel", "arbitrary"), vmem_limit_bytes=VMEM_LIMIT),
        name="fox",
    )(proj, proj, proj, proj, proj_m, proj_m, col_m, row4, q_w, k_w)


def _out_proj_kernel(mg_ref, mf_ref, wg_ref, wf_ref, pw_ref, x_ref, o_ref):
    for r in range(o_ref.shape[0] // OUT_PROJ_CHUNK):
        rows = slice(r * OUT_PROJ_CHUNK, (r + 1) * OUT_PROJ_CHUNK)

        def rows_of(m_ref):
            return jnp.concatenate([m_ref[h, rows, :] for h in range(m_ref.shape[0])], axis=1)
        out = _dot(rows_of(mg_ref), wg_ref[...]) + _dot(rows_of(mf_ref), wf_ref[...])
        o_ref[rows, :] = x_ref[rows, :] + _rms(out, pw_ref[...])


def _out_proj(mg, mf, w_g, w_f, post_w, x2d, tm):
    m, d = x2d.shape
    return pl.pallas_call(
        _out_proj_kernel,
        out_shape=jax.ShapeDtypeStruct((m, d), F32),
        grid=(m // tm,),
        in_specs=[pl.BlockSpec((GDN_HEADS, tm, HEAD_DIM), lambda i: (0, i, 0)),
                  pl.BlockSpec((FOX_HEADS, tm, HEAD_DIM), lambda i: (0, i, 0)),
                  pl.BlockSpec((GDN_WIDTH, d), lambda i: (0, 0), pipeline_mode=pl.Buffered(1)),
                  pl.BlockSpec((FOX_WIDTH, d), lambda i: (0, 0), pipeline_mode=pl.Buffered(1)),
                  pl.BlockSpec((1, d), lambda i: (0, 0)),
                  pl.BlockSpec((tm, d), lambda i: (i, 0))],
        out_specs=pl.BlockSpec((tm, d), lambda i: (i, 0)),
        compiler_params=pltpu.CompilerParams(
            dimension_semantics=("parallel",), vmem_limit_bytes=VMEM_LIMIT),
        name="out_proj",
    )(mg, mf, w_g, w_f, post_w, x2d)


def _tile(total, want):
    t = min(total, want)
    while total % t:
        t //= 2
    return t


def _layer(x, meta_pad, pre_w, w_in, conv_w, a_log, dt_bias, gdn_norm_w, fox_q_w, fox_k_w,
           fox_f_bias, w_out, post_w):
    b, seq, d = x.shape
    assert seq % (GROUP * PREP_UNROLL) == 0 and seq % FOX_TK == 0 and seq % ROW_BLOCK == 0
    gw, fw = GDN_WIDTH, FOX_WIDTH
    o_gb = 4 * gw
    o_f = o_gb + 2 * GDN_HEADS
    o_ff = o_f + 4 * fw
    w_t = w_in.T
    w_main = _w_prep(w_t, MAIN_WIDTH, o_gb, o_f - o_gb)
    w_gate = jnp.concatenate(
        [w_t[o_gb:o_f], w_t[o_ff:],
         jnp.zeros((GATE_WIDTH - 2 * GDN_HEADS - FOX_HEADS, d), w_in.dtype)], axis=0)
    zpad = jnp.zeros((GATE_WIDTH - FORGET_LANE - FOX_HEADS,), F32)
    add_vec = jnp.concatenate([jnp.zeros((GDN_HEADS,), F32), dt_bias, fox_f_bias, zpad])[None]
    alog_vec = jnp.concatenate([jnp.zeros((GDN_HEADS,), F32), a_log,
                                jnp.zeros((FOX_HEADS,), F32), zpad])[None]

    x2d = x.reshape(b * seq, d)
    pre_w2 = pre_w[None]
    proj, gate = _in_proj(x2d, pre_w2, w_main, w_gate, _tile(b * seq, IN_PROJ_TM), IN_PROJ_TN)
    proj_m, gate_m = _in_proj(meta_pad, pre_w2, w_main, w_gate, META_ROWS, IN_PROJ_TN)
    proj = proj.reshape(MAIN_WIDTH // HEAD_DIM, b, seq, HEAD_DIM)

    row4 = _gate_rows(gate.reshape(b, seq, GATE_WIDTH), add_vec, alog_vec)
    col_m, row_m = _gate_meta(gate_m, add_vec, alog_vec)
    row_m4 = row_m.reshape(1, GATE_ROWS, 1, META_ROWS)

    conv_wt = conv_w.T
    state0 = _gdn_state0(proj_m, conv_wt, col_m, row_m4)
    o_gdn = _gdn(proj, proj_m, conv_wt, row4, state0, gdn_norm_w[None])
    o_fox = _fox(proj, proj_m, col_m, row4, fox_q_w[None], fox_k_w[None])

    w_out_b = w_out.astype(BF16)
    out = _out_proj(o_gdn.reshape(GDN_HEADS, b * seq, HEAD_DIM),
                    o_fox.reshape(FOX_HEADS, b * seq, HEAD_DIM),
                    w_out_b[:gw], w_out_b[gw:], post_w[None], x2d, _tile(b * seq, OUT_PROJ_TM))
    return out.reshape(b, seq, d)


def kernel(x, meta_tokens, pre_norm_w, w_in, conv_w, a_log, dt_bias, gdn_norm_w, fox_q_norm_w,
           fox_k_norm_w, fox_f_bias, w_out, post_norm_w):
    assert pre_norm_w.shape[0] == 1, "single-layer stack"
    meta_pad = jnp.concatenate(
        [jnp.zeros((META_PAD, x.shape[-1]), x.dtype), meta_tokens.astype(x.dtype)], axis=0)
    return _layer(x, meta_pad, pre_norm_w[0], w_in[0], conv_w[0], a_log[0], dt_bias[0],
                  gdn_norm_w[0], fox_q_norm_w[0], fox_k_norm_w[0], fox_f_bias[0], w_out[0],
                  post_norm_w[0])
```

```python
import functools
import math

import jax
import jax.numpy as jnp
from jax import lax
from jax.experimental import pallas as pl
from jax.experimental.pallas import tpu as pltpu

N_META = 16
HEAD_DIM = 128
GDN_HEADS = 8
FOX_HEADS = 8
GDN_WIDTH = GDN_HEADS * HEAD_DIM
FOX_WIDTH = FOX_HEADS * HEAD_DIM
CONV_WIDTH = 4
CHUNK = 64
EPS = 1e-6

LANES = 128
SUBLANES = 8
MXU_DIM = 256
MAIN_WIDTH = 4 * GDN_WIDTH + 4 * FOX_WIDTH
GATE_WIDTH = LANES
BETA_LANE, DECAY_LANE, FORGET_LANE = 0, GDN_HEADS, 2 * GDN_HEADS
GATE_ROWS = 32
META_ROWS = CHUNK
META_PAD = META_ROWS - N_META
GROUP = MXU_DIM
ROW_BLOCK = 256
HIST_ROWS = 2 * SUBLANES
IN_PROJ_TM, IN_PROJ_TN = 1024, 2048
W_PREP_ROWS = 512
OUT_PROJ_TM = 1024
OUT_PROJ_CHUNK = 512
HEAD_PAIR = 2
REC_PAIRS = 2
REC_HEADS = REC_PAIRS * HEAD_PAIR
PREP_UNROLL = 2
DEC_ROWS = SUBLANES
BLOCK_UNROLL = 4
FOX_GROUP = 4
FOX_TK = MXU_DIM
FOX_PIECE = 2 * MXU_DIM
VMEM_LIMIT = 56 * 1024 * 1024

F32 = jnp.float32
BF16 = jnp.bfloat16
NEG_BIG = -1e30
LOG2E = math.log2(math.e)


def _dot(a, b):
    return jnp.dot(a, b, preferred_element_type=F32)


def _dot_nt(a, b):
    return lax.dot_general(a, b, (((1,), (1,)), ((), ())), preferred_element_type=F32)


def _split3(x):
    hi = x.astype(BF16)
    r1 = x - hi.astype(F32)
    mid = r1.astype(BF16)
    lo = (r1 - mid.astype(F32)).astype(BF16)
    return hi, mid, lo


def _dot_exact_rhs01(parts, m):
    return _dot(parts[0], m) + _dot(parts[1], m) + _dot(parts[2], m)


def _dot_exact_lhs01(m, parts):
    return _dot(m, parts[0]) + _dot(m, parts[1]) + _dot(m, parts[2])


def _lane_bcast(col_tile, lane):
    sel = (lax.broadcasted_iota(jnp.int32, (LANES, LANES), 0) == lane).astype(BF16)
    return _dot_exact_rhs01(_split3(col_tile), sel)


def _chunk_of(idx):
    return jnp.right_shift(idx, CHUNK.bit_length() - 1)


def _rms(x, w):
    return x * lax.rsqrt(jnp.mean(x * x, axis=-1, keepdims=True) + EPS) * w


def _silu(x):
    half = 0.5 * x
    return half + half * jnp.tanh(half)


def _softplus(x):
    return jnp.maximum(x, 0.0) + jnp.log1p(jnp.exp(-jnp.abs(x)))


def _w_prep_kernel(w_hbm, meta_ref, nw_ref, wg_ref, o_ref, om_ref, ogm_ref, buf, sem, xm_ref,
                   *, rows, shift_from, shift):
    r = pl.program_id(0)

    def fetch(step, slot):
        start = step * rows
        src = pl.multiple_of(start + jnp.where(start >= shift_from, shift, 0), SUBLANES)
        return pltpu.make_async_copy(w_hbm.at[pl.ds(src, rows), :], buf.at[slot], sem.at[slot])

    @pl.when(r == 0)
    def _():
        fetch(0, 0).start()
        xm = _rms(meta_ref[...], nw_ref[...]).astype(BF16)
        xm_ref[...] = xm
        ogm_ref[...] = _dot_nt(xm, wg_ref[...].astype(BF16))

    @pl.when(r + 1 < pl.num_programs(0))
    def _():
        fetch(r + 1, (r + 1) % 2).start()

    fetch(r, r % 2).wait()
    w_bf = buf[r % 2].astype(BF16)
    o_ref[...] = w_bf
    res = _dot_nt(xm_ref[...], w_bf)
    for hd in range(om_ref.shape[0]):
        om_ref[hd] = res[:, hd * HEAD_DIM:(hd + 1) * HEAD_DIM].astype(om_ref.dtype)


def _w_prep(w_t, meta_pad, norm_w, w_gate, n_rows, shift_from, shift):
    d = w_t.shape[1]
    rows = W_PREP_ROWS
    return pl.pallas_call(
        functools.partial(_w_prep_kernel, rows=rows, shift_from=shift_from, shift=shift),
        out_shape=(jax.ShapeDtypeStruct((n_rows, d), BF16),
                   jax.ShapeDtypeStruct((n_rows // HEAD_DIM, META_ROWS, HEAD_DIM), BF16),
                   jax.ShapeDtypeStruct((META_ROWS, GATE_WIDTH), F32)),
        grid=(n_rows // rows,),
        in_specs=[pl.BlockSpec(memory_space=pl.ANY),
                  pl.BlockSpec((META_ROWS, d), lambda r: (0, 0)),
                  pl.BlockSpec((1, d), lambda r: (0, 0)),
                  pl.BlockSpec((GATE_WIDTH, d), lambda r: (0, 0))],
        out_specs=(pl.BlockSpec((rows, d), lambda r: (r, 0)),
                   pl.BlockSpec((rows // HEAD_DIM, META_ROWS, HEAD_DIM), lambda r: (r, 0, 0)),
                   pl.BlockSpec((META_ROWS, GATE_WIDTH), lambda r: (0, 0))),
        scratch_shapes=[pltpu.VMEM((2, rows, d), F32), pltpu.SemaphoreType.DMA((2,)),
                        pltpu.VMEM((META_ROWS, d), BF16)],
        compiler_params=pltpu.CompilerParams(
            dimension_semantics=("arbitrary",), vmem_limit_bytes=VMEM_LIMIT),
        name="w_prep",
    )(w_t, meta_pad, norm_w, w_gate)


def _in_proj_kernel(x_ref, nw_ref, w_ref, wg_ref, o_ref, og_ref, xn_ref):
    @pl.when(pl.program_id(1) == 0)
    def _():
        xn = _rms(x_ref[...], nw_ref[...]).astype(BF16)
        xn_ref[...] = xn
        og_ref[...] = _dot_nt(xn, wg_ref[...].astype(BF16))

    res = _dot_nt(xn_ref[...], w_ref[...])
    for hd in range(o_ref.shape[0]):
        o_ref[hd] = res[:, hd * HEAD_DIM:(hd + 1) * HEAD_DIM].astype(o_ref.dtype)


def _in_proj(x2d, norm_w, w_main, w_gate, tm, tn):
    m, d = x2d.shape
    n = w_main.shape[0]
    return pl.pallas_call(
        _in_proj_kernel,
        out_shape=(jax.ShapeDtypeStruct((n // HEAD_DIM, m, HEAD_DIM), BF16),
                   jax.ShapeDtypeStruct((m, GATE_WIDTH), F32)),
        grid=(m // tm, n // tn),
        in_specs=[pl.BlockSpec((tm, d), lambda i, j: (i, 0)),
                  pl.BlockSpec((1, d), lambda i, j: (0, 0)),
                  pl.BlockSpec((tn, d), lambda i, j: (j, 0)),
                  pl.BlockSpec((GATE_WIDTH, d), lambda i, j: (0, 0))],
        out_specs=(pl.BlockSpec((tn // HEAD_DIM, tm, HEAD_DIM), lambda i, j: (j, i, 0)),
                   pl.BlockSpec((tm, GATE_WIDTH), lambda i, j: (i, 0))),
        scratch_shapes=[pltpu.VMEM((tm, d), BF16)],
        compiler_params=pltpu.CompilerParams(
            dimension_semantics=("parallel", "arbitrary"), vmem_limit_bytes=VMEM_LIMIT),
        name="in_proj",
    )(x2d, norm_w, w_main, w_gate)


def _gate_meta_kernel(t_ref, add_ref, alog_ref, col_ref, row_ref):
    lane = lax.broadcasted_iota(jnp.int32, (META_ROWS, LANES), 1)
    row = lax.broadcasted_iota(jnp.int32, (META_ROWS, LANES), 0)
    ri = lax.broadcasted_iota(jnp.int32, (META_ROWS, META_ROWS), 0)
    ci = lax.broadcasted_iota(jnp.int32, (META_ROWS, META_ROWS), 1)
    tri = (ci <= ri).astype(BF16)
    is_beta = lane < DECAY_LANE
    is_decay = (lane >= DECAY_LANE) & (lane < FORGET_LANE)
    is_forget = (lane >= FORGET_LANE) & (lane < FORGET_LANE + FOX_HEADS)
    real = row >= META_PAD

    t = t_ref[...]
    ta = t + add_ref[...]
    beta = jnp.where(real, 1.0 / (1.0 + jnp.exp(-t)), 0.0)
    g = -jnp.exp(alog_ref[...]) * _softplus(ta)
    logf = -_softplus(-ta)
    val = jnp.where(real, jnp.where(is_decay, g, jnp.where(is_forget, logf, 0.0)), 0.0)
    cum = _dot_exact_lhs01(tri, _split3(val))
    to_come = cum - cum[META_ROWS - 1:META_ROWS, :]
    res = jnp.where(is_beta, beta, jnp.where(is_decay, cum, to_come))
    col_ref[...] = res
    row_ref[...] = res.T[:GATE_ROWS, :]


def _gate_rows_kernel(t_ref, add_ref, alog_ref, row_ref, *, rows):
    blk = ROW_BLOCK
    row = lax.broadcasted_iota(jnp.int32, (GATE_ROWS, blk), 0)
    si = lax.broadcasted_iota(jnp.int32, (blk, blk), 0)
    ti = lax.broadcasted_iota(jnp.int32, (blk, blk), 1)
    tri = (si <= ti).astype(BF16)
    tri_chunk = ((si <= ti) & (_chunk_of(si) == _chunk_of(ti))).astype(BF16)
    is_beta = row < DECAY_LANE
    is_decay = (row >= DECAY_LANE) & (row < FORGET_LANE)
    is_forget = (row >= FORGET_LANE) & (row < FORGET_LANE + FOX_HEADS)
    add = jnp.concatenate([add_ref[...]] * (blk // LANES), axis=1)
    neg_rate = -jnp.exp(jnp.concatenate([alog_ref[...]] * (blk // LANES), axis=1))

    carry = jnp.zeros((GATE_ROWS, 1), F32)
    for r in range(rows // blk):
        t = t_ref[r * blk:(r + 1) * blk, :].T[:GATE_ROWS, :]
        ta = t + add
        beta = 1.0 / (1.0 + jnp.exp(-t))
        val = jnp.where(is_decay, neg_rate * _softplus(ta), jnp.where(is_forget, -_softplus(-ta), 0.0))
        parts = _split3(val)
        cum_chunk = _dot_exact_rhs01(parts, tri_chunk)
        cum_all = _dot_exact_rhs01(parts, tri) + carry
        carry = cum_all[:, blk - 1:blk]
        row_ref[:, 0, r * blk:(r + 1) * blk] = jnp.where(is_beta, beta,
                                                         jnp.where(is_decay, cum_chunk, cum_all))


def _gate_rows(gate3d, add_vec, alog_vec):
    b, rows, _ = gate3d.shape
    per_row = lambda v: jnp.broadcast_to(v[0, :GATE_ROWS, None], (GATE_ROWS, LANES))
    return pl.pallas_call(
        functools.partial(_gate_rows_kernel, rows=rows),
        out_shape=jax.ShapeDtypeStruct((b, GATE_ROWS, 1, rows), F32),
        grid=(b,),
        in_specs=[pl.BlockSpec((None, rows, LANES), lambda i: (i, 0, 0)),
                  pl.BlockSpec((GATE_ROWS, LANES), lambda i: (0, 0)),
                  pl.BlockSpec((GATE_ROWS, LANES), lambda i: (0, 0))],
        out_specs=pl.BlockSpec((None, GATE_ROWS, 1, rows), lambda i: (i, 0, 0, 0)),
        compiler_params=pltpu.CompilerParams(
            dimension_semantics=("parallel",), vmem_limit_bytes=VMEM_LIMIT),
        name="gate_rows",
    )(gate3d, per_row(add_vec), per_row(alog_vec))


def _gate_meta(gate_m, add_vec, alog_vec):
    return pl.pallas_call(
        _gate_meta_kernel,
        out_shape=(jax.ShapeDtypeStruct((META_ROWS, LANES), F32),
                   jax.ShapeDtypeStruct((GATE_ROWS, META_ROWS), F32)),
        grid=(1,),
        in_specs=[pl.BlockSpec((META_ROWS, LANES), lambda i: (0, 0)),
                  pl.BlockSpec((1, LANES), lambda i: (0, 0)),
                  pl.BlockSpec((1, LANES), lambda i: (0, 0))],
        out_specs=(pl.BlockSpec((META_ROWS, LANES), lambda i: (0, 0)),
                   pl.BlockSpec((GATE_ROWS, META_ROWS), lambda i: (0, 0))),
        compiler_params=pltpu.CompilerParams(
            dimension_semantics=("arbitrary",), vmem_limit_bytes=VMEM_LIMIT),
        name="gate_meta",
    )(gate_m, add_vec, alog_vec)


def _conv_silu(load_rows, w):
    y = load_rows(0) * w[0:1, :]
    for j in range(1, CONV_WIDTH):
        y = y + load_rows(j) * w[j:j + 1, :]
    return _silu(y)


def _l2norm(x):
    return x * lax.rsqrt(jnp.sum(x * x, axis=-1, keepdims=True) + EPS)


def _gdn_pointwise(q, k, v, beta_b, g_b):
    r = q.shape[0]
    q = q * (lax.rsqrt(jnp.sum(q * q, axis=-1, keepdims=True) + EPS) * (HEAD_DIM ** -0.5))
    k = _l2norm(k)
    g3 = g_b.reshape(r // CHUNK, CHUNK, LANES)
    g_last = jnp.broadcast_to(g3[:, CHUNK - 1:CHUNK, :], g3.shape).reshape(r, LANES)
    e_g = jnp.exp(g_b)
    q_dec = q * e_g
    k_dec = k * jnp.exp(g_last - g_b)
    y = jnp.concatenate([v * beta_b, k * (beta_b * e_g)], axis=1)
    return q, k, q_dec, k_dec, y, jnp.exp(g_last)


def _gdn_groups(probs, fillers=()):
    r = probs[0][0].shape[0]
    ri = lax.broadcasted_iota(jnp.int32, (r, r), 0)
    ci = lax.broadcasted_iota(jnp.int32, (r, r), 1)
    same = _chunk_of(ri) == _chunk_of(ci)
    causal = same & (ci <= ri)
    strict = same & (ci < ri)

    def widen(t):
        return jnp.concatenate([t] * (r // LANES), axis=1) if r >= LANES else t[:, :r]

    fillers = list(fillers)
    n_ticks = CHUNK.bit_length()
    per_tick = -(-len(fillers) // n_ticks)

    def tick():
        for thunk in fillers[:per_tick]:
            thunk()
        del fillers[:per_tick]

    kks = [_dot_nt(k, k) for _, k, _, _, _, _ in probs]
    qks = [_dot_nt(q, k) for q, k, _, _, _, _ in probs]
    tick()
    dmats = [jnp.where(causal, jnp.exp(jnp.where(causal, widen(g_b) - g_row, 0.0)), 0.0)
             for _, _, _, _, g_b, g_row in probs]
    xs = [jnp.where(strict, widen(p[3]) * kk * d, 0.0).astype(BF16)
          for p, kk, d in zip(probs, kks, dmats)]
    a_qks = [qk * d for qk, d in zip(qks, dmats)]
    eye = (ri == ci).astype(BF16)
    zs = [_dot(eye - x, p[2]) for p, x in zip(probs, xs)]
    span = 2
    while span < CHUNK:
        xs = [_dot(x, x).astype(BF16) for x in xs]
        tick()
        zs = [z + _dot(x, z.astype(BF16)) for x, z in zip(xs, zs)]
        span *= 2
    tick()
    assert not fillers
    return list(zip(zs, a_qks))


def _gdn_state0_kernel(km_ref, vm_ref, wk_ref, wv_ref, colm_ref, growm_ref, s_ref, pad_ref):
    heads = range(GDN_HEADS)
    probs, kd_t = [], []
    for h in heads:
        ls = slice(h * HEAD_DIM, (h + 1) * HEAD_DIM)
        beta_m = _lane_bcast(colm_ref[...], BETA_LANE + h)
        g_m = _lane_bcast(colm_ref[...], DECAY_LANE + h)
        conv = []
        for t, (src, w_ref) in enumerate(((km_ref, wk_ref), (vm_ref, wv_ref))):
            win = pad_ref.at[2 * h + t]
            win[0:SUBLANES, :] = jnp.zeros((SUBLANES, LANES), F32)
            win[SUBLANES:SUBLANES + META_ROWS, :] = src[h].astype(F32)
            conv.append(_conv_silu(
                lambda j, win=win: win[SUBLANES - (CONV_WIDTH - 1) + j:
                                       SUBLANES - (CONV_WIDTH - 1) + j + META_ROWS, :],
                w_ref[:, ls]))
        _, k_m, _, kd_m, y_m, _ = _gdn_pointwise(conv[0], conv[0], conv[1], beta_m, g_m)
        k_bf = k_m.astype(BF16)
        probs.append((k_bf, k_bf, y_m.astype(BF16), beta_m, g_m, growm_ref[h]))
        kd_t.append(kd_m.T.astype(BF16))
    for h, (uw_m, _) in zip(heads, _gdn_groups(probs)):
        s_ref[h] = _dot(kd_t[h], uw_m[:, :HEAD_DIM].astype(BF16))


def _gdn_state0(proj_m, conv_wt, col_m, row_m4):
    heads_block = lambda blk: pl.BlockSpec((GDN_HEADS, META_ROWS, HEAD_DIM),
                                           lambda j, blk=blk: (blk, 0, 0))
    taps_block = lambda blk: pl.BlockSpec((CONV_WIDTH, GDN_WIDTH), lambda j, blk=blk: (0, blk))
    return pl.pallas_call(
        _gdn_state0_kernel,
        out_shape=jax.ShapeDtypeStruct((GDN_HEADS, HEAD_DIM, HEAD_DIM), F32),
        grid=(1,),
        in_specs=[heads_block(1), heads_block(2), taps_block(1), taps_block(2),
                  pl.BlockSpec((META_ROWS, LANES), lambda j: (0, 0)),
                  pl.BlockSpec((None, GDN_HEADS, 1, META_ROWS),
                               lambda j: (0, DECAY_LANE // GDN_HEADS, 0, 0))],
        out_specs=pl.BlockSpec((GDN_HEADS, HEAD_DIM, HEAD_DIM), lambda j: (0, 0, 0)),
        scratch_shapes=[pltpu.VMEM((2 * GDN_HEADS, META_ROWS + SUBLANES, HEAD_DIM), F32)],
        compiler_params=pltpu.CompilerParams(
            dimension_semantics=("arbitrary",), vmem_limit_bytes=VMEM_LIMIT),
        name="gdn_state0",
    )(proj_m, proj_m, conv_wt, conv_wt, col_m, row_m4)


def _gdn_kernel(*refs, seq):
    n_qkv = 3 * HEAD_PAIR
    src_refs = [refs[t * HEAD_PAIR:(t + 1) * HEAD_PAIR] for t in range(3)]
    z_ref = refs[n_qkv]
    meta_refs = [refs[n_qkv + 1 + t * HEAD_PAIR:n_qkv + 1 + (t + 1) * HEAD_PAIR] for t in range(3)]
    tap_refs = [refs[2 * n_qkv + 1 + t * HEAD_PAIR:2 * n_qkv + 1 + (t + 1) * HEAD_PAIR]
                for t in range(3)]
    (brow_ref, grow_ref, s0_ref, nw_ref, o_ref, pad_ref, qs_ref, ks_ref, y_ref, bb_ref, gb_ref,
     qd_ref, kdt_ref, dec_ref, u_ref, w_ref, aqk_ref, st_ref,
     qd_stage, kdt_stage, dec_stage) = refs[3 * n_qkv + 1:]
    pair = pl.program_id(1)
    slot = pair % REC_PAIRS
    n_blocks = seq // ROW_BLOCK
    n_groups = seq // GROUP
    cpg = GROUP // CHUNK
    hist = CONV_WIDTH - 1

    def pointwise_head(i, bank, hh):
        r0 = pl.multiple_of(i * ROW_BLOCK, ROW_BLOCK)
        rows = pl.ds(r0, ROW_BLOCK)
        convs = []
        for t in range(3):
            win = pad_ref.at[(bank * 3 + t) * HEAD_PAIR + hh]
            if isinstance(i, int) and i == 0:
                past = meta_refs[t][hh][META_ROWS - HIST_ROWS:META_ROWS, :]
            else:
                past = src_refs[t][hh][pl.ds(pl.multiple_of(r0 - HIST_ROWS, HIST_ROWS), HIST_ROWS), :]
            win[0:HIST_ROWS, :] = past.astype(F32)
            win[HIST_ROWS:HIST_ROWS + ROW_BLOCK, :] = src_refs[t][hh][rows, :].astype(F32)
            convs.append(_conv_silu(
                lambda j, win=win: win[HIST_ROWS - hist + j:HIST_ROWS - hist + j + ROW_BLOCK, :],
                tap_refs[t][hh][...]))
        beta_b = jnp.broadcast_to(brow_ref[hh, :, rows], (LANES, ROW_BLOCK)).T
        g_b = jnp.broadcast_to(grow_ref[hh, :, rows], (LANES, ROW_BLOCK)).T
        q, k, q_dec, k_dec, y, dec = _gdn_pointwise(convs[0], convs[1], convs[2], beta_b, g_b)
        qs_ref[hh, rows, :] = q.astype(BF16)
        ks_ref[hh, rows, :] = k.astype(BF16)
        y_ref[hh, rows, :] = y.astype(BF16)
        bb_ref[hh, rows, :] = beta_b
        gb_ref[hh, rows, :] = g_b
        qd_stage[hh, rows, :] = q_dec.astype(BF16)
        kdt_stage[hh, :, rows] = k_dec.T.astype(BF16)
        n_chunks = ROW_BLOCK // CHUNK
        dec_rows = [dec[c * CHUNK:c * CHUNK + 1, :] for c in range(n_chunks)]
        dec_rows.append(jnp.zeros((DEC_ROWS - n_chunks, LANES), F32))
        dec_stage[hh, pl.ds(pl.multiple_of(i * DEC_ROWS, DEC_ROWS), DEC_ROWS), :] = (
            jnp.concatenate(dec_rows, axis=0))

    def pointwise_thunks(i, bank):
        return [functools.partial(pointwise_head, i, bank, hh) for hh in range(HEAD_PAIR)]

    for i in range(PREP_UNROLL):
        for thunk in pointwise_thunks(i, i):
            thunk()

    def prep_groups(gi, fillers):
        groups = [gi * PREP_UNROLL + u for u in range(PREP_UNROLL) for _ in range(HEAD_PAIR)]
        keys = [(hh, pl.ds(pl.multiple_of((gi * PREP_UNROLL + u) * GROUP, GROUP), GROUP))
                for u in range(PREP_UNROLL) for hh in range(HEAD_PAIR)]
        probs = [(qs_ref[hh, rows, :], ks_ref[hh, rows, :], y_ref[hh, rows, :],
                  bb_ref[hh, rows, :], gb_ref[hh, rows, :], grow_ref[hh, :, rows])
                 for hh, rows in keys]
        for g, (hh, rows), (uw, a_qk) in zip(groups, keys, _gdn_groups(probs, fillers)):
            hs = slot * HEAD_PAIR + hh
            u_ref[hs, rows, :] = uw[:, :HEAD_DIM]
            w_ref[hs, rows, :] = uw[:, HEAD_DIM:].astype(BF16)
            aqk_ref[hs, rows, :] = a_qk.astype(BF16)
            drows = pl.ds(pl.multiple_of(g * DEC_ROWS, DEC_ROWS), DEC_ROWS)
            qd_ref[hs, rows, :] = qd_stage[hh, rows, :]
            kdt_ref[hs, :, rows] = kdt_stage[hh, :, rows]
            dec_ref[hs, drows, :] = dec_stage[hh, drows, :]

    def next_pointwise(gi):
        return [th for u in range(PREP_UNROLL)
                for th in pointwise_thunks((gi + 1) * PREP_UNROLL + u, u)]

    def rec_thunks(gi):
        r0 = pl.multiple_of(gi * GROUP, GROUP)
        rows = pl.ds(r0, GROUP)
        heads = range(REC_HEADS)
        outs = [[] for _ in heads]

        held = {}

        def read_state(c):
            crow = pl.ds(r0 + c * CHUNK, CHUNK)
            states = [st_ref[h] for h in heads]
            s_bf = [s.astype(BF16) for s in states]
            ws = [_dot(jnp.concatenate([w_ref[h, crow, :], qd_ref[h, crow, :]], axis=0), s_bf[h])
                  for h in heads]
            held[c] = (states, ws)

        def write_state(c):
            crow = pl.ds(r0 + c * CHUNK, CHUNK)
            states, ws = held.pop(c)
            v_new = [u_ref[h, crow, :] - ws[h][:CHUNK, :] for h in heads]
            zero = lambda n: jnp.zeros((n * CHUNK, HEAD_DIM), BF16)
            v_pad = [jnp.concatenate(([zero(c)] if c else []) + [v.astype(BF16)]
                                     + ([zero(cpg - 1 - c)] if c < cpg - 1 else []), axis=0)
                     for v in v_new]
            upd = [_dot(jnp.concatenate([aqk_ref[h, crow, :], kdt_ref[h, :, rows]], axis=0),
                        v_pad[h]) for h in heads]
            for h in heads:
                dec = dec_ref[h, pl.ds(gi * DEC_ROWS + c, 1), :]
                st_ref[h] = states[h] * dec + upd[h][CHUNK:, :]
                outs[h].append(ws[h][CHUNK:, :] + upd[h][:CHUNK, :])
            if c == cpg - 1:
                for h in heads:
                    o = _rms(jnp.concatenate(outs[h], axis=0), nw_ref[...])
                    o_ref[h, rows, :] = (o * _silu(z_ref[h, rows, :].astype(F32))).astype(BF16)

        return [functools.partial(half, c) for c in range(cpg) for half in (read_state, write_state)]

    def interleave(a, b):
        if not a or not b:
            return list(a) + list(b)
        out, j = [], 0
        for i, th in enumerate(a):
            out.append(th)
            while j < len(b) and (j + 1) * len(a) <= (i + 1) * len(b):
                out.append(b[j])
                j += 1
        return out + list(b[j:])

    n_trips = n_groups // PREP_UNROLL

    @pl.when(slot != REC_PAIRS - 1)
    def _():
        def prep_body(gi, _):
            prep_groups(gi, next_pointwise(gi))
            return 0
        lax.fori_loop(0, n_trips - 1, prep_body, 0)
        prep_groups(n_trips - 1, [])

    @pl.when(slot == REC_PAIRS - 1)
    def _():
        st_ref[...] = s0_ref[...]

        def rec_of_trip(t):
            return [th for u in range(PREP_UNROLL) for th in rec_thunks(t * PREP_UNROLL + u)]

        prep_groups(0, next_pointwise(0))

        def prep_rec_body(gi, _):
            prep_groups(gi, interleave(rec_of_trip(gi - 1), next_pointwise(gi)))
            return 0
        lax.fori_loop(1, n_trips - 1, prep_rec_body, 0)
        prep_groups(n_trips - 1, rec_of_trip(n_trips - 2))
        for thunk in rec_of_trip(n_trips - 1):
            thunk()


def _gdn(proj, proj_m, conv_wt, row4, state0, norm_w):
    _, b, seq, _ = proj.shape
    n_pairs = GDN_HEADS // HEAD_PAIR

    def head_specs(shape, imap):
        return [pl.BlockSpec(shape, functools.partial(imap, t * GDN_HEADS + hh))
                for t in range(3) for hh in range(HEAD_PAIR)]

    src_specs = head_specs((None, None, seq, HEAD_DIM),
                           lambda off, i, j: (off + HEAD_PAIR * j, i, 0, 0))
    meta_specs = head_specs((None, META_ROWS, HEAD_DIM), lambda off, i, j: (off + HEAD_PAIR * j, 0, 0))
    tap_specs = head_specs((CONV_WIDTH, HEAD_DIM), lambda off, i, j: (0, off + HEAD_PAIR * j))
    n_qkv = 3 * HEAD_PAIR
    z_block0 = 3 * GDN_HEADS // REC_HEADS

    per_pair = lambda width, dt: pltpu.VMEM((HEAD_PAIR, seq, width), dt)
    per_rec = lambda width, dt: pltpu.VMEM((REC_HEADS, seq, width), dt)
    return pl.pallas_call(
        functools.partial(_gdn_kernel, seq=seq),
        out_shape=jax.ShapeDtypeStruct((GDN_HEADS, b, seq, HEAD_DIM), BF16),
        grid=(b, n_pairs),
        in_specs=src_specs
        + [pl.BlockSpec((REC_HEADS, None, seq, HEAD_DIM),
                        lambda i, j: (z_block0 + j // REC_PAIRS, i, 0, 0))]
        + meta_specs + tap_specs
        + [pl.BlockSpec((None, HEAD_PAIR, 1, seq),
                        lambda i, j: (i, BETA_LANE // HEAD_PAIR + j, 0, 0)),
           pl.BlockSpec((None, HEAD_PAIR, 1, seq),
                        lambda i, j: (i, DECAY_LANE // HEAD_PAIR + j, 0, 0)),
           pl.BlockSpec((REC_HEADS, HEAD_DIM, HEAD_DIM), lambda i, j: (j // REC_PAIRS, 0, 0)),
           pl.BlockSpec((1, HEAD_DIM), lambda i, j: (0, 0))],
        out_specs=pl.BlockSpec((REC_HEADS, None, seq, HEAD_DIM),
                               lambda i, j: (j // REC_PAIRS, i, 0, 0)),
        scratch_shapes=[pltpu.VMEM((PREP_UNROLL * n_qkv, HIST_ROWS + ROW_BLOCK, HEAD_DIM), F32),
                        per_pair(HEAD_DIM, BF16), per_pair(HEAD_DIM, BF16),
                        per_pair(2 * HEAD_DIM, BF16),
                        per_pair(HEAD_DIM, F32), per_pair(HEAD_DIM, F32),
                        per_rec(HEAD_DIM, BF16),
                        pltpu.VMEM((REC_HEADS, HEAD_DIM, seq), BF16),
                        pltpu.VMEM((REC_HEADS, seq // GROUP * DEC_ROWS, HEAD_DIM), F32),
                        per_rec(HEAD_DIM, F32), per_rec(HEAD_DIM, BF16), per_rec(GROUP, BF16),
                        pltpu.VMEM((REC_HEADS, HEAD_DIM, HEAD_DIM), F32),
                        per_pair(HEAD_DIM, BF16),
                        pltpu.VMEM((HEAD_PAIR, HEAD_DIM, seq), BF16),
                        pltpu.VMEM((HEAD_PAIR, seq // GROUP * DEC_ROWS, HEAD_DIM), F32)],
        compiler_params=pltpu.CompilerParams(
            dimension_semantics=("parallel", "arbitrary"), vmem_limit_bytes=VMEM_LIMIT),
        name="gdn",
    )(*([proj] * n_qkv), proj, *([proj_m] * n_qkv), *([conv_wt] * n_qkv),
      row4, row4, state0, norm_w)


def _fox_kernel(q_ref, k_ref, v_ref, g_ref, km_ref, vm_ref, colm_ref, crow_ref,
                qw_ref, kw_ref, o_ref, qt_ref, ka_ref, kam_ref, vt_ref, vtm_ref,
                m_ref, l_ref, acc_ref, *, seq):
    n_blocks = seq // ROW_BLOCK
    aug_r = lax.broadcasted_iota(jnp.int32, (LANES, ROW_BLOCK), 0)

    def key_aug(ck, valid=None):
        hi, mid, lo = _split3(ck)
        lane = lax.broadcasted_iota(jnp.int32, ck.shape, 1)
        neg_hi = -hi.astype(F32)
        if valid is not None:
            neg_hi = jnp.where(valid, neg_hi, NEG_BIG)
        blk = jnp.where(lane < 3, 1.0,
                        jnp.where(lane == 3, neg_hi,
                                  jnp.where(lane == 4, -mid.astype(F32),
                                            jnp.where(lane == 5, -lo.astype(F32), 0.0))))
        return blk.astype(BF16)

    def pro_body(i, _):
        r0 = pl.multiple_of(i * ROW_BLOCK, ROW_BLOCK)
        rows = pl.ds(r0, ROW_BLOCK)
        for hd in range(FOX_GROUP):
            qn = _rms(q_ref[hd, rows, :].astype(F32), qw_ref[...]) * (HEAD_DIM ** -0.5 * LOG2E)
            qt_ref[hd, 0:HEAD_DIM, rows] = qn.T.astype(BF16)
            hi, mid, lo = _split3(crow_ref[hd, :, rows] * LOG2E)
            aug = jnp.where(aug_r == 0, hi.astype(F32),
                            jnp.where(aug_r == 1, mid.astype(F32),
                                      jnp.where(aug_r == 2, lo.astype(F32),
                                                jnp.where(aug_r < 6, 1.0, 0.0))))
            qt_ref[hd, HEAD_DIM:2 * HEAD_DIM, rows] = aug.astype(BF16)
            ka_ref[hd, rows, 0:HEAD_DIM] = _rms(k_ref[hd, rows, :].astype(F32),
                                                kw_ref[...]).astype(BF16)
            aug_k = jnp.where(aug_r < 3, 1.0,
                              jnp.where(aug_r == 3, -hi.astype(F32),
                                        jnp.where(aug_r == 4, -mid.astype(F32),
                                                  jnp.where(aug_r == 5, -lo.astype(F32), 0.0))))
            ka_ref[hd, rows, HEAD_DIM:2 * HEAD_DIM] = aug_k.T.astype(BF16)
            vt_ref[hd, :, rows] = v_ref[hd, rows, :].astype(F32).T.astype(BF16)
        return 0
    lax.fori_loop(0, n_blocks, pro_body, 0, unroll=BLOCK_UNROLL // FOX_GROUP)

    mrow = lax.broadcasted_iota(jnp.int32, (META_ROWS, LANES), 0)
    for hd in range(FOX_GROUP):
        head = pl.program_id(1) * FOX_GROUP + hd
        kam_ref[hd, :, 0:HEAD_DIM] = _rms(km_ref[hd].astype(F32), kw_ref[...]).astype(BF16)
        ck_m = _lane_bcast(colm_ref[...], FORGET_LANE + head) * LOG2E
        kam_ref[hd, :, HEAD_DIM:2 * HEAD_DIM] = key_aug(ck_m, mrow >= META_PAD)
        vtm_ref[hd] = vm_ref[hd].astype(F32).T.astype(BF16)

    kidx = lax.broadcasted_iota(jnp.int32, (FOX_TK, FOX_TK), 0)
    qidx = lax.broadcasted_iota(jnp.int32, (FOX_TK, FOX_TK), 1)
    diag_ok = kidx <= qidx

    items = []
    for kj in range(-1, seq // FOX_TK):
        lane0 = max(kj, 0) * FOX_TK
        for p0 in range(lane0, seq, FOX_PIECE):
            for hd in range(FOX_GROUP):
                items.append((hd, kj, slice(p0, min(p0 + FOX_PIECE, seq)), p0 == lane0))

    def scores(item):
        hd, kj, ln, leads = item
        k_aug = kam_ref[hd] if kj < 0 else ka_ref[hd, kj * FOX_TK:(kj + 1) * FOX_TK, :]
        s = _dot(k_aug, qt_ref[hd, :, ln])
        if kj >= 0 and leads:
            masked = jnp.where(diag_ok, s[:, :FOX_TK], NEG_BIG)
            s = jnp.concatenate([masked, s[:, FOX_TK:]], axis=1) if s.shape[1] > FOX_TK else masked
        return s

    def softmax_stats(item, s):
        hd, kj, ln, _ = item
        top = jnp.max(s, axis=0, keepdims=True)
        if kj < 0:
            m_new, alpha = top, None
        else:
            m_old = m_ref[hd, :, ln]
            m_new = jnp.maximum(m_old, top)
            alpha = jnp.exp2(m_old - m_new)
        p = jnp.exp2(s - m_new)
        psum = jnp.sum(p, axis=0, keepdims=True)
        m_ref[hd, :, ln] = m_new
        l_ref[hd, :, ln] = psum if kj < 0 else alpha * l_ref[hd, :, ln] + psum
        return p.astype(BF16), alpha

    def values(item, p, alpha):
        hd, kj, ln, _ = item
        v_t = vtm_ref[hd] if kj < 0 else vt_ref[hd, :, kj * FOX_TK:(kj + 1) * FOX_TK]
        pv = _dot(v_t, p)
        acc_ref[hd, :, ln] = pv if kj < 0 else alpha * acc_ref[hd, :, ln] + pv

    s_cur = scores(items[0])
    p_cur = None
    for t in range(len(items) + 1):
        s_next = scores(items[t + 1]) if t + 1 < len(items) else None
        p_next = softmax_stats(items[t], s_cur) if t < len(items) else None
        if p_cur is not None:
            values(items[t - 1], *p_cur)
        s_cur, p_cur = s_next, p_next

    def out_body(i, _):
        r0 = pl.multiple_of(i * ROW_BLOCK, ROW_BLOCK)
        rows = pl.ds(r0, ROW_BLOCK)
        for hd in range(FOX_GROUP):
            out_t = acc_ref[hd, :, rows] * (1.0 / l_ref[hd, :, rows])
            o_ref[hd, rows, :] = (out_t.T * _silu(g_ref[hd, rows, :].astype(F32))).astype(BF16)
        return 0
    lax.fori_loop(0, n_blocks, out_body, 0, unroll=BLOCK_UNROLL // FOX_GROUP)


def _fox(proj, proj_m, col_m, row4, q_w, k_w):
    _, b, seq, _ = proj.shape
    base = 4 * GDN_HEADS // FOX_GROUP
    hb = FOX_HEADS // FOX_GROUP

    def head_block(off):
        return pl.BlockSpec((FOX_GROUP, None, seq, HEAD_DIM),
                            lambda i, j, off=off: (off + j, i, 0, 0))

    def meta_block(off):
        return pl.BlockSpec((FOX_GROUP, META_ROWS, HEAD_DIM), lambda i, j, off=off: (off + j, 0, 0))

    per_head = lambda shape, dt: pltpu.VMEM((FOX_GROUP,) + shape, dt)
    return pl.pallas_call(
        functools.partial(_fox_kernel, seq=seq),
        out_shape=jax.ShapeDtypeStruct((FOX_HEADS, b, seq, HEAD_DIM), BF16),
        grid=(b, FOX_HEADS // FOX_GROUP),
        in_specs=[head_block(base), head_block(base + hb), head_block(base + 2 * hb),
                  head_block(base + 3 * hb),
                  meta_block(base + hb), meta_block(base + 2 * hb),
                  pl.BlockSpec((META_ROWS, LANES), lambda i, j: (0, 0)),
                  pl.BlockSpec((None, FOX_GROUP, 1, seq),
                               lambda i, j: (i, FORGET_LANE // FOX_GROUP + j, 0, 0)),
                  pl.BlockSpec((1, HEAD_DIM), lambda i, j: (0, 0)),
                  pl.BlockSpec((1, HEAD_DIM), lambda i, j: (0, 0))],
        out_specs=pl.BlockSpec((FOX_GROUP, None, seq, HEAD_DIM), lambda i, j: (j, i, 0, 0)),
        scratch_shapes=[per_head((2 * HEAD_DIM, seq), BF16),
                        per_head((seq, 2 * HEAD_DIM), BF16),
                        per_head((META_ROWS, 2 * HEAD_DIM), BF16),
                        per_head((HEAD_DIM, seq), BF16),
                        per_head((HEAD_DIM, META_ROWS), BF16),
                        per_head((1, seq), F32), per_head((1, seq), F32),
                        per_head((HEAD_DIM, seq), F32)],
        compiler_params=pltpu.CompilerParams(
            dimension_semantics=("parallel", "arbitrary"), vmem_limit_bytes=VMEM_LIMIT),
        name="fox",
    )(proj, proj, proj, proj, proj_m, proj_m, col_m, row4, q_w, k_w)


def _out_proj_kernel(mg_ref, mf_ref, wg_ref, wf_ref, pw_ref, x_ref, o_ref):
    for r in range(o_ref.shape[0] // OUT_PROJ_CHUNK):
        rows = slice(r * OUT_PROJ_CHUNK, (r + 1) * OUT_PROJ_CHUNK)

        def rows_of(m_ref):
            return jnp.concatenate([m_ref[h, rows, :] for h in range(m_ref.shape[0])], axis=1)
        out = _dot(rows_of(mg_ref), wg_ref[...]) + _dot(rows_of(mf_ref), wf_ref[...])
        o_ref[rows, :] = x_ref[rows, :] + _rms(out, pw_ref[...])


def _out_proj(mg, mf, w_g, w_f, post_w, x2d, tm):
    m, d = x2d.shape
    return pl.pallas_call(
        _out_proj_kernel,
        out_shape=jax.ShapeDtypeStruct((m, d), F32),
        grid=(m // tm,),
        in_specs=[pl.BlockSpec((GDN_HEADS, tm, HEAD_DIM), lambda i: (0, i, 0)),
                  pl.BlockSpec((FOX_HEADS, tm, HEAD_DIM), lambda i: (0, i, 0)),
                  pl.BlockSpec((GDN_WIDTH, d), lambda i: (0, 0), pipeline_mode=pl.Buffered(1)),
                  pl.BlockSpec((FOX_WIDTH, d), lambda i: (0, 0), pipeline_mode=pl.Buffered(1)),
                  pl.BlockSpec((1, d), lambda i: (0, 0)),
                  pl.BlockSpec((tm, d), lambda i: (i, 0))],
        out_specs=pl.BlockSpec((tm, d), lambda i: (i, 0)),
        compiler_params=pltpu.CompilerParams(
            dimension_semantics=("parallel",), vmem_limit_bytes=VMEM_LIMIT),
        name="out_proj",
    )(mg, mf, w_g, w_f, post_w, x2d)


def _tile(total, want):
    t = min(total, want)
    while total % t:
        t //= 2
    return t


def _layer(x, meta_pad, pre_w, w_in, conv_w, a_log, dt_bias, gdn_norm_w, fox_q_w, fox_k_w,
           fox_f_bias, w_out, post_w):
    b, seq, d = x.shape
    assert seq % (GROUP * PREP_UNROLL) == 0 and seq % FOX_TK == 0 and seq % ROW_BLOCK == 0
    gw, fw = GDN_WIDTH, FOX_WIDTH
    o_gb = 4 * gw
    o_f = o_gb + 2 * GDN_HEADS
    o_ff = o_f + 4 * fw
    w_t = w_in.T
    w_gate = jnp.concatenate(
        [w_t[o_gb:o_f], w_t[o_ff:],
         jnp.zeros((GATE_WIDTH - 2 * GDN_HEADS - FOX_HEADS, d), w_in.dtype)], axis=0)
    pre_w2 = pre_w[None]
    w_main, proj_m, gate_m = _w_prep(w_t, meta_pad, pre_w2, w_gate, MAIN_WIDTH, o_gb, o_f - o_gb)
    zpad = jnp.zeros((GATE_WIDTH - FORGET_LANE - FOX_HEADS,), F32)
    add_vec = jnp.concatenate([jnp.zeros((GDN_HEADS,), F32), dt_bias, fox_f_bias, zpad])[None]
    alog_vec = jnp.concatenate([jnp.zeros((GDN_HEADS,), F32), a_log,
                                jnp.zeros((FOX_HEADS,), F32), zpad])[None]

    x2d = x.reshape(b * seq, d)
    proj, gate = _in_proj(x2d, pre_w2, w_main, w_gate, _tile(b * seq, IN_PROJ_TM), IN_PROJ_TN)
    proj = proj.reshape(MAIN_WIDTH // HEAD_DIM, b, seq, HEAD_DIM)

    row4 = _gate_rows(gate.reshape(b, seq, GATE_WIDTH), add_vec, alog_vec)
    col_m, row_m = _gate_meta(gate_m, add_vec, alog_vec)
    row_m4 = row_m.reshape(1, GATE_ROWS, 1, META_ROWS)

    conv_wt = conv_w.T
    state0 = _gdn_state0(proj_m, conv_wt, col_m, row_m4)
    o_gdn = _gdn(proj, proj_m, conv_wt, row4, state0, gdn_norm_w[None])
    o_fox = _fox(proj, proj_m, col_m, row4, fox_q_w[None], fox_k_w[None])

    w_out_b = w_out.astype(BF16)
    out = _out_proj(o_gdn.reshape(GDN_HEADS, b * seq, HEAD_DIM),
                    o_fox.reshape(FOX_HEADS, b * seq, HEAD_DIM),
                    w_out_b[:gw], w_out_b[gw:], post_w[None], x2d, _tile(b * seq, OUT_PROJ_TM))
    return out.reshape(b, seq, d)


def kernel(x, meta_tokens, pre_norm_w, w_in, conv_w, a_log, dt_bias, gdn_norm_w, fox_q_norm_w,
           fox_k_norm_w, fox_f_bias, w_out, post_norm_w):
    assert pre_norm_w.shape[0] == 1, "single-layer stack"
    meta_pad = jnp.concatenate(
        [jnp.zeros((META_PAD, x.shape[-1]), x.dtype), meta_tokens.astype(x.dtype)], axis=0)
    return _layer(x, meta_pad, pre_norm_w[0], w_in[0], conv_w[0], a_log[0], dt_bias[0],
                  gdn_norm_w[0], fox_q_norm_w[0], fox_k_norm_w[0], fox_f_bias[0], w_out[0],
                  post_norm_w[0])
```

```python
import functools
import math

import jax
import jax.numpy as jnp
from jax import lax
from jax.experimental import pallas as pl
from jax.experimental.pallas import tpu as pltpu

N_META = 16
HEAD_DIM = 128
GDN_HEADS = 8
FOX_HEADS = 8
GDN_WIDTH = GDN_HEADS * HEAD_DIM
FOX_WIDTH = FOX_HEADS * HEAD_DIM
CONV_WIDTH = 4
CHUNK = 64
EPS = 1e-6

LANES = 128
SUBLANES = 8
MXU_DIM = 256
MAIN_WIDTH = 4 * GDN_WIDTH + 4 * FOX_WIDTH
GATE_WIDTH = LANES
BETA_LANE, DECAY_LANE, FORGET_LANE = 0, GDN_HEADS, 2 * GDN_HEADS
GATE_ROWS = 32
META_ROWS = CHUNK
META_PAD = META_ROWS - N_META
GROUP = MXU_DIM
ROW_BLOCK = 256
HIST_ROWS = 2 * SUBLANES
IN_PROJ_TM, IN_PROJ_TN = 1024, 2048
W_PREP_ROWS = 512
OUT_PROJ_TM = 1024
OUT_PROJ_CHUNK = 512
HEAD_PAIR = 2
REC_PAIRS = 2
REC_HEADS = REC_PAIRS * HEAD_PAIR
PREP_UNROLL = 2
REC_PREP_UNROLL = 1
DEC_ROWS = SUBLANES
BLOCK_UNROLL = 4
FOX_GROUP = 4
FOX_TK = MXU_DIM
FOX_PIECE = 2 * MXU_DIM
VMEM_LIMIT = 56 * 1024 * 1024

F32 = jnp.float32
BF16 = jnp.bfloat16
NEG_BIG = -1e30
LOG2E = math.log2(math.e)


def _dot(a, b):
    return jnp.dot(a, b, preferred_element_type=F32)


def _dot_nt(a, b):
    return lax.dot_general(a, b, (((1,), (1,)), ((), ())), preferred_element_type=F32)


def _split3(x):
    hi = x.astype(BF16)
    r1 = x - hi.astype(F32)
    mid = r1.astype(BF16)
    lo = (r1 - mid.astype(F32)).astype(BF16)
    return hi, mid, lo


def _dot_exact_rhs01(parts, m):
    return _dot(parts[0], m) + _dot(parts[1], m) + _dot(parts[2], m)


def _dot_exact_lhs01(m, parts):
    return _dot(m, parts[0]) + _dot(m, parts[1]) + _dot(m, parts[2])


def _lane_bcast(col_tile, lane):
    sel = (lax.broadcasted_iota(jnp.int32, (LANES, LANES), 0) == lane).astype(BF16)
    return _dot_exact_rhs01(_split3(col_tile), sel)


def _chunk_of(idx):
    return jnp.right_shift(idx, CHUNK.bit_length() - 1)


def _rms(x, w):
    return x * lax.rsqrt(jnp.mean(x * x, axis=-1, keepdims=True) + EPS) * w


def _silu(x):
    half = 0.5 * x
    return half + half * jnp.tanh(half)


def _softplus(x):
    return jnp.maximum(x, 0.0) + jnp.log1p(jnp.exp(-jnp.abs(x)))


def _w_prep_kernel(w_hbm, meta_ref, nw_ref, wg_ref, o_ref, om_ref, ogm_ref, buf, sem, xm_ref,
                   *, rows, shift_from, shift):
    r = pl.program_id(0)

    def fetch(step, slot):
        start = step * rows
        src = pl.multiple_of(start + jnp.where(start >= shift_from, shift, 0), SUBLANES)
        return pltpu.make_async_copy(w_hbm.at[pl.ds(src, rows), :], buf.at[slot], sem.at[slot])

    @pl.when(r == 0)
    def _():
        fetch(0, 0).start()
        xm = _rms(meta_ref[...], nw_ref[...]).astype(BF16)
        xm_ref[...] = xm
        ogm_ref[...] = _dot_nt(xm, wg_ref[...].astype(BF16))

    @pl.when(r + 1 < pl.num_programs(0))
    def _():
        fetch(r + 1, (r + 1) % 2).start()

    fetch(r, r % 2).wait()
    w_bf = buf[r % 2].astype(BF16)
    o_ref[...] = w_bf
    res = _dot_nt(xm_ref[...], w_bf)
    for hd in range(om_ref.shape[0]):
        om_ref[hd] = res[:, hd * HEAD_DIM:(hd + 1) * HEAD_DIM].astype(om_ref.dtype)


def _w_prep(w_t, meta_pad, norm_w, w_gate, n_rows, shift_from, shift):
    d = w_t.shape[1]
    rows = W_PREP_ROWS
    return pl.pallas_call(
        functools.partial(_w_prep_kernel, rows=rows, shift_from=shift_from, shift=shift),
        out_shape=(jax.ShapeDtypeStruct((n_rows, d), BF16),
                   jax.ShapeDtypeStruct((n_rows // HEAD_DIM, META_ROWS, HEAD_DIM), BF16),
                   jax.ShapeDtypeStruct((META_ROWS, GATE_WIDTH), F32)),
        grid=(n_rows // rows,),
        in_specs=[pl.BlockSpec(memory_space=pl.ANY),
                  pl.BlockSpec((META_ROWS, d), lambda r: (0, 0)),
                  pl.BlockSpec((1, d), lambda r: (0, 0)),
                  pl.BlockSpec((GATE_WIDTH, d), lambda r: (0, 0))],
        out_specs=(pl.BlockSpec((rows, d), lambda r: (r, 0)),
                   pl.BlockSpec((rows // HEAD_DIM, META_ROWS, HEAD_DIM), lambda r: (r, 0, 0)),
                   pl.BlockSpec((META_ROWS, GATE_WIDTH), lambda r: (0, 0))),
        scratch_shapes=[pltpu.VMEM((2, rows, d), F32), pltpu.SemaphoreType.DMA((2,)),
                        pltpu.VMEM((META_ROWS, d), BF16)],
        compiler_params=pltpu.CompilerParams(
            dimension_semantics=("arbitrary",), vmem_limit_bytes=VMEM_LIMIT),
        name="w_prep",
    )(w_t, meta_pad, norm_w, w_gate)


def _in_proj_kernel(x_ref, nw_ref, w_ref, wg_ref, o_ref, og_ref, xn_ref):
    @pl.when(pl.program_id(1) == 0)
    def _():
        xn = _rms(x_ref[...], nw_ref[...]).astype(BF16)
        xn_ref[...] = xn
        og_ref[...] = _dot_nt(xn, wg_ref[...].astype(BF16))

    res = _dot_nt(xn_ref[...], w_ref[...])
    for hd in range(o_ref.shape[0]):
        o_ref[hd] = res[:, hd * HEAD_DIM:(hd + 1) * HEAD_DIM].astype(o_ref.dtype)


def _in_proj(x2d, norm_w, w_main, w_gate, tm, tn):
    m, d = x2d.shape
    n = w_main.shape[0]
    return pl.pallas_call(
        _in_proj_kernel,
        out_shape=(jax.ShapeDtypeStruct((n // HEAD_DIM, m, HEAD_DIM), BF16),
                   jax.ShapeDtypeStruct((m, GATE_WIDTH), F32)),
        grid=(m // tm, n // tn),
        in_specs=[pl.BlockSpec((tm, d), lambda i, j: (i, 0)),
                  pl.BlockSpec((1, d), lambda i, j: (0, 0)),
                  pl.BlockSpec((tn, d), lambda i, j: (j, 0)),
                  pl.BlockSpec((GATE_WIDTH, d), lambda i, j: (0, 0))],
        out_specs=(pl.BlockSpec((tn // HEAD_DIM, tm, HEAD_DIM), lambda i, j: (j, i, 0)),
                   pl.BlockSpec((tm, GATE_WIDTH), lambda i, j: (i, 0))),
        scratch_shapes=[pltpu.VMEM((tm, d), BF16)],
        compiler_params=pltpu.CompilerParams(
            dimension_semantics=("parallel", "arbitrary"), vmem_limit_bytes=VMEM_LIMIT),
        name="in_proj",
    )(x2d, norm_w, w_main, w_gate)


def _gate_meta_kernel(t_ref, add_ref, alog_ref, col_ref, row_ref):
    lane = lax.broadcasted_iota(jnp.int32, (META_ROWS, LANES), 1)
    row = lax.broadcasted_iota(jnp.int32, (META_ROWS, LANES), 0)
    ri = lax.broadcasted_iota(jnp.int32, (META_ROWS, META_ROWS), 0)
    ci = lax.broadcasted_iota(jnp.int32, (META_ROWS, META_ROWS), 1)
    tri = (ci <= ri).astype(BF16)
    is_beta = lane < DECAY_LANE
    is_decay = (lane >= DECAY_LANE) & (lane < FORGET_LANE)
    is_forget = (lane >= FORGET_LANE) & (lane < FORGET_LANE + FOX_HEADS)
    real = row >= META_PAD

    t = t_ref[...]
    ta = t + add_ref[...]
    beta = jnp.where(real, 1.0 / (1.0 + jnp.exp(-t)), 0.0)
    g = -jnp.exp(alog_ref[...]) * _softplus(ta)
    logf = -_softplus(-ta)
    val = jnp.where(real, jnp.where(is_decay, g, jnp.where(is_forget, logf, 0.0)), 0.0)
    cum = _dot_exact_lhs01(tri, _split3(val))
    to_come = cum - cum[META_ROWS - 1:META_ROWS, :]
    res = jnp.where(is_beta, beta, jnp.where(is_decay, cum, to_come))
    col_ref[...] = res
    row_ref[...] = res.T[:GATE_ROWS, :]


def _gate_rows_kernel(t_ref, add_ref, alog_ref, row_ref, *, rows):
    blk = ROW_BLOCK
    row = lax.broadcasted_iota(jnp.int32, (GATE_ROWS, blk), 0)
    si = lax.broadcasted_iota(jnp.int32, (blk, blk), 0)
    ti = lax.broadcasted_iota(jnp.int32, (blk, blk), 1)
    tri = (si <= ti).astype(BF16)
    tri_chunk = ((si <= ti) & (_chunk_of(si) == _chunk_of(ti))).astype(BF16)
    is_beta = row < DECAY_LANE
    is_decay = (row >= DECAY_LANE) & (row < FORGET_LANE)
    is_forget = (row >= FORGET_LANE) & (row < FORGET_LANE + FOX_HEADS)
    add = jnp.concatenate([add_ref[...]] * (blk // LANES), axis=1)
    neg_rate = -jnp.exp(jnp.concatenate([alog_ref[...]] * (blk // LANES), axis=1))

    carry = jnp.zeros((GATE_ROWS, 1), F32)
    for r in range(rows // blk):
        t = t_ref[r * blk:(r + 1) * blk, :].T[:GATE_ROWS, :]
        ta = t + add
        beta = 1.0 / (1.0 + jnp.exp(-t))
        val = jnp.where(is_decay, neg_rate * _softplus(ta), jnp.where(is_forget, -_softplus(-ta), 0.0))
        parts = _split3(val)
        cum_chunk = _dot_exact_rhs01(parts, tri_chunk)
        cum_all = _dot_exact_rhs01(parts, tri) + carry
        carry = cum_all[:, blk - 1:blk]
        row_ref[:, 0, r * blk:(r + 1) * blk] = jnp.where(is_beta, beta,
                                                         jnp.where(is_decay, cum_chunk, cum_all))


def _gate_rows(gate3d, add_vec, alog_vec):
    b, rows, _ = gate3d.shape
    per_row = lambda v: jnp.broadcast_to(v[0, :GATE_ROWS, None], (GATE_ROWS, LANES))
    return pl.pallas_call(
        functools.partial(_gate_rows_kernel, rows=rows),
        out_shape=jax.ShapeDtypeStruct((b, GATE_ROWS, 1, rows), F32),
        grid=(b,),
        in_specs=[pl.BlockSpec((None, rows, LANES), lambda i: (i, 0, 0)),
                  pl.BlockSpec((GATE_ROWS, LANES), lambda i: (0, 0)),
                  pl.BlockSpec((GATE_ROWS, LANES), lambda i: (0, 0))],
        out_specs=pl.BlockSpec((None, GATE_ROWS, 1, rows), lambda i: (i, 0, 0, 0)),
        compiler_params=pltpu.CompilerParams(
            dimension_semantics=("parallel",), vmem_limit_bytes=VMEM_LIMIT),
        name="gate_rows",
    )(gate3d, per_row(add_vec), per_row(alog_vec))


def _gate_meta(gate_m, add_vec, alog_vec):
    return pl.pallas_call(
        _gate_meta_kernel,
        out_shape=(jax.ShapeDtypeStruct((META_ROWS, LANES), F32),
                   jax.ShapeDtypeStruct((GATE_ROWS, META_ROWS), F32)),
        grid=(1,),
        in_specs=[pl.BlockSpec((META_ROWS, LANES), lambda i: (0, 0)),
                  pl.BlockSpec((1, LANES), lambda i: (0, 0)),
                  pl.BlockSpec((1, LANES), lambda i: (0, 0))],
        out_specs=(pl.BlockSpec((META_ROWS, LANES), lambda i: (0, 0)),
                   pl.BlockSpec((GATE_ROWS, META_ROWS), lambda i: (0, 0))),
        compiler_params=pltpu.CompilerParams(
            dimension_semantics=("arbitrary",), vmem_limit_bytes=VMEM_LIMIT),
        name="gate_meta",
    )(gate_m, add_vec, alog_vec)


def _conv_silu(load_rows, w):
    y = load_rows(0) * w[0:1, :]
    for j in range(1, CONV_WIDTH):
        y = y + load_rows(j) * w[j:j + 1, :]
    return _silu(y)


def _l2norm(x):
    return x * lax.rsqrt(jnp.sum(x * x, axis=-1, keepdims=True) + EPS)


def _gdn_pointwise(q, k, v, beta_b, g_b):
    r = q.shape[0]
    q = q * (lax.rsqrt(jnp.sum(q * q, axis=-1, keepdims=True) + EPS) * (HEAD_DIM ** -0.5))
    k = _l2norm(k)
    g3 = g_b.reshape(r // CHUNK, CHUNK, LANES)
    g_last = jnp.broadcast_to(g3[:, CHUNK - 1:CHUNK, :], g3.shape).reshape(r, LANES)
    e_g = jnp.exp(g_b)
    q_dec = q * e_g
    k_dec = k * jnp.exp(g_last - g_b)
    y = jnp.concatenate([v * beta_b, k * (beta_b * e_g)], axis=1)
    return q, k, q_dec, k_dec, y, jnp.exp(g_last)


def _gdn_groups(probs, fillers=()):
    r = probs[0][0].shape[0]
    ri = lax.broadcasted_iota(jnp.int32, (r, r), 0)
    ci = lax.broadcasted_iota(jnp.int32, (r, r), 1)
    same = _chunk_of(ri) == _chunk_of(ci)
    causal = same & (ci <= ri)
    strict = same & (ci < ri)

    def widen(t):
        return jnp.concatenate([t] * (r // LANES), axis=1) if r >= LANES else t[:, :r]

    fillers = list(fillers)
    n_ticks = CHUNK.bit_length()
    per_tick = -(-len(fillers) // n_ticks)

    def tick():
        for thunk in fillers[:per_tick]:
            thunk()
        del fillers[:per_tick]

    kks = [_dot_nt(k, k) for _, k, _, _, _, _ in probs]
    qks = [_dot_nt(q, k) for q, k, _, _, _, _ in probs]
    tick()
    dmats = [jnp.where(causal, jnp.exp(jnp.where(causal, widen(g_b) - g_row, 0.0)), 0.0)
             for _, _, _, _, g_b, g_row in probs]
    xs = [jnp.where(strict, widen(p[3]) * kk * d, 0.0).astype(BF16)
          for p, kk, d in zip(probs, kks, dmats)]
    a_qks = [qk * d for qk, d in zip(qks, dmats)]
    eye = (ri == ci).astype(BF16)
    zs = [_dot(eye - x, p[2]) for p, x in zip(probs, xs)]
    span = 2
    while span < CHUNK:
        xs = [_dot(x, x).astype(BF16) for x in xs]
        tick()
        zs = [z + _dot(x, z.astype(BF16)) for x, z in zip(xs, zs)]
        span *= 2
    tick()
    assert not fillers
    return list(zip(zs, a_qks))


def _gdn_state0_kernel(km_ref, vm_ref, wk_ref, wv_ref, colm_ref, growm_ref, s_ref, pad_ref):
    heads = range(GDN_HEADS)
    probs, kd_t = [], []
    for h in heads:
        ls = slice(h * HEAD_DIM, (h + 1) * HEAD_DIM)
        beta_m = _lane_bcast(colm_ref[...], BETA_LANE + h)
        g_m = _lane_bcast(colm_ref[...], DECAY_LANE + h)
        conv = []
        for t, (src, w_ref) in enumerate(((km_ref, wk_ref), (vm_ref, wv_ref))):
            win = pad_ref.at[2 * h + t]
            win[0:SUBLANES, :] = jnp.zeros((SUBLANES, LANES), F32)
            win[SUBLANES:SUBLANES + META_ROWS, :] = src[h].astype(F32)
            conv.append(_conv_silu(
                lambda j, win=win: win[SUBLANES - (CONV_WIDTH - 1) + j:
                                       SUBLANES - (CONV_WIDTH - 1) + j + META_ROWS, :],
                w_ref[:, ls]))
        _, k_m, _, kd_m, y_m, _ = _gdn_pointwise(conv[0], conv[0], conv[1], beta_m, g_m)
        k_bf = k_m.astype(BF16)
        probs.append((k_bf, k_bf, y_m.astype(BF16), beta_m, g_m, growm_ref[h]))
        kd_t.append(kd_m.T.astype(BF16))
    for h, (uw_m, _) in zip(heads, _gdn_groups(probs)):
        s_ref[h] = _dot(kd_t[h], uw_m[:, :HEAD_DIM].astype(BF16))


def _gdn_state0(proj_m, conv_wt, col_m, row_m4):
    heads_block = lambda blk: pl.BlockSpec((GDN_HEADS, META_ROWS, HEAD_DIM),
                                           lambda j, blk=blk: (blk, 0, 0))
    taps_block = lambda blk: pl.BlockSpec((CONV_WIDTH, GDN_WIDTH), lambda j, blk=blk: (0, blk))
    return pl.pallas_call(
        _gdn_state0_kernel,
        out_shape=jax.ShapeDtypeStruct((GDN_HEADS, HEAD_DIM, HEAD_DIM), F32),
        grid=(1,),
        in_specs=[heads_block(1), heads_block(2), taps_block(1), taps_block(2),
                  pl.BlockSpec((META_ROWS, LANES), lambda j: (0, 0)),
                  pl.BlockSpec((None, GDN_HEADS, 1, META_ROWS),
                               lambda j: (0, DECAY_LANE // GDN_HEADS, 0, 0))],
        out_specs=pl.BlockSpec((GDN_HEADS, HEAD_DIM, HEAD_DIM), lambda j: (0, 0, 0)),
        scratch_shapes=[pltpu.VMEM((2 * GDN_HEADS, META_ROWS + SUBLANES, HEAD_DIM), F32)],
        compiler_params=pltpu.CompilerParams(
            dimension_semantics=("arbitrary",), vmem_limit_bytes=VMEM_LIMIT),
        name="gdn_state0",
    )(proj_m, proj_m, conv_wt, conv_wt, col_m, row_m4)


def _gdn_kernel(*refs, seq):
    n_qkv = 3 * HEAD_PAIR
    src_refs = [refs[t * HEAD_PAIR:(t + 1) * HEAD_PAIR] for t in range(3)]
    z_ref = refs[n_qkv]
    meta_refs = [refs[n_qkv + 1 + t * HEAD_PAIR:n_qkv + 1 + (t + 1) * HEAD_PAIR] for t in range(3)]
    tap_refs = [refs[2 * n_qkv + 1 + t * HEAD_PAIR:2 * n_qkv + 1 + (t + 1) * HEAD_PAIR]
                for t in range(3)]
    (brow_ref, grow_ref, s0_ref, nw_ref, o_ref, pad_ref, qs_ref, ks_ref, y_ref, bb_ref, gb_ref,
     qd_ref, kdt_ref, dec_ref, u_ref, w_ref, aqk_ref, st_ref,
     qd_stage, kdt_stage, dec_stage) = refs[3 * n_qkv + 1:]
    pair = pl.program_id(1)
    slot = pair % REC_PAIRS
    n_blocks = seq // ROW_BLOCK
    n_groups = seq // GROUP
    cpg = GROUP // CHUNK
    hist = CONV_WIDTH - 1

    def pointwise_head(i, bank, hh):
        r0 = pl.multiple_of(i * ROW_BLOCK, ROW_BLOCK)
        rows = pl.ds(r0, ROW_BLOCK)
        convs = []
        for t in range(3):
            win = pad_ref.at[(bank * 3 + t) * HEAD_PAIR + hh]
            if isinstance(i, int) and i == 0:
                past = meta_refs[t][hh][META_ROWS - HIST_ROWS:META_ROWS, :]
            else:
                past = src_refs[t][hh][pl.ds(pl.multiple_of(r0 - HIST_ROWS, HIST_ROWS), HIST_ROWS), :]
            win[0:HIST_ROWS, :] = past.astype(F32)
            win[HIST_ROWS:HIST_ROWS + ROW_BLOCK, :] = src_refs[t][hh][rows, :].astype(F32)
            convs.append(_conv_silu(
                lambda j, win=win: win[HIST_ROWS - hist + j:HIST_ROWS - hist + j + ROW_BLOCK, :],
                tap_refs[t][hh][...]))
        beta_b = jnp.broadcast_to(brow_ref[hh, :, rows], (LANES, ROW_BLOCK)).T
        g_b = jnp.broadcast_to(grow_ref[hh, :, rows], (LANES, ROW_BLOCK)).T
        q, k, q_dec, k_dec, y, dec = _gdn_pointwise(convs[0], convs[1], convs[2], beta_b, g_b)
        qs_ref[hh, rows, :] = q.astype(BF16)
        ks_ref[hh, rows, :] = k.astype(BF16)
        y_ref[hh, rows, :] = y.astype(BF16)
        bb_ref[hh, rows, :] = beta_b
        gb_ref[hh, rows, :] = g_b
        qd_stage[hh, rows, :] = q_dec.astype(BF16)
        kdt_stage[hh, :, rows] = k_dec.T.astype(BF16)
        n_chunks = ROW_BLOCK // CHUNK
        dec_rows = [dec[c * CHUNK:c * CHUNK + 1, :] for c in range(n_chunks)]
        dec_rows.append(jnp.zeros((DEC_ROWS - n_chunks, LANES), F32))
        dec_stage[hh, pl.ds(pl.multiple_of(i * DEC_ROWS, DEC_ROWS), DEC_ROWS), :] = (
            jnp.concatenate(dec_rows, axis=0))

    def pointwise_thunks(i, bank):
        return [functools.partial(pointwise_head, i, bank, hh) for hh in range(HEAD_PAIR)]

    def first_pointwise(unroll):
        for i in range(unroll):
            for thunk in pointwise_thunks(i, i):
                thunk()

    def prep_groups(gi, fillers, unroll):
        groups = [gi * unroll + u for u in range(unroll) for _ in range(HEAD_PAIR)]
        keys = [(hh, pl.ds(pl.multiple_of((gi * unroll + u) * GROUP, GROUP), GROUP))
                for u in range(unroll) for hh in range(HEAD_PAIR)]
        probs = [(qs_ref[hh, rows, :], ks_ref[hh, rows, :], y_ref[hh, rows, :],
                  bb_ref[hh, rows, :], gb_ref[hh, rows, :], grow_ref[hh, :, rows])
                 for hh, rows in keys]
        for g, (hh, rows), (uw, a_qk) in zip(groups, keys, _gdn_groups(probs, fillers)):
            hs = slot * HEAD_PAIR + hh
            u_ref[hs, rows, :] = uw[:, :HEAD_DIM]
            w_ref[hs, rows, :] = uw[:, HEAD_DIM:].astype(BF16)
            aqk_ref[hs, rows, :] = a_qk.astype(BF16)
            drows = pl.ds(pl.multiple_of(g * DEC_ROWS, DEC_ROWS), DEC_ROWS)
            qd_ref[hs, rows, :] = qd_stage[hh, rows, :]
            kdt_ref[hs, :, rows] = kdt_stage[hh, :, rows]
            dec_ref[hs, drows, :] = dec_stage[hh, drows, :]

    def next_pointwise(gi, unroll):
        return [th for u in range(unroll) for th in pointwise_thunks((gi + 1) * unroll + u, u)]

    def rec_thunks(gi):
        r0 = pl.multiple_of(gi * GROUP, GROUP)
        rows = pl.ds(r0, GROUP)
        heads = range(REC_HEADS)
        outs = [[] for _ in heads]

        held = {}

        def read_state(c):
            crow = pl.ds(r0 + c * CHUNK, CHUNK)
            states = [st_ref[h] for h in heads]
            s_bf = [s.astype(BF16) for s in states]
            ws = [_dot(jnp.concatenate([w_ref[h, crow, :], qd_ref[h, crow, :]], axis=0), s_bf[h])
                  for h in heads]
            held[c] = (states, ws)

        def write_state(c):
            crow = pl.ds(r0 + c * CHUNK, CHUNK)
            states, ws = held.pop(c)
            v_new = [u_ref[h, crow, :] - ws[h][:CHUNK, :] for h in heads]
            zero = lambda n: jnp.zeros((n * CHUNK, HEAD_DIM), BF16)
            v_pad = [jnp.concatenate(([zero(c)] if c else []) + [v.astype(BF16)]
                                     + ([zero(cpg - 1 - c)] if c < cpg - 1 else []), axis=0)
                     for v in v_new]
            upd = [_dot(jnp.concatenate([aqk_ref[h, crow, :], kdt_ref[h, :, rows]], axis=0),
                        v_pad[h]) for h in heads]
            for h in heads:
                dec = dec_ref[h, pl.ds(gi * DEC_ROWS + c, 1), :]
                st_ref[h] = states[h] * dec + upd[h][CHUNK:, :]
                outs[h].append(ws[h][CHUNK:, :] + upd[h][:CHUNK, :])
            if c == cpg - 1:
                for h in heads:
                    o = _rms(jnp.concatenate(outs[h], axis=0), nw_ref[...])
                    o_ref[h, rows, :] = (o * _silu(z_ref[h, rows, :].astype(F32))).astype(BF16)

        return [functools.partial(half, c) for c in range(cpg) for half in (read_state, write_state)]

    def interleave(a, b):
        if not a or not b:
            return list(a) + list(b)
        out, j = [], 0
        for i, th in enumerate(a):
            out.append(th)
            while j < len(b) and (j + 1) * len(a) <= (i + 1) * len(b):
                out.append(b[j])
                j += 1
        return out + list(b[j:])

    @pl.when(slot != REC_PAIRS - 1)
    def _():
        unroll = PREP_UNROLL
        n_trips = n_groups // unroll
        first_pointwise(unroll)

        def prep_body(gi, _):
            prep_groups(gi, next_pointwise(gi, unroll), unroll)
            return 0
        lax.fori_loop(0, n_trips - 1, prep_body, 0)
        prep_groups(n_trips - 1, [], unroll)

    @pl.when(slot == REC_PAIRS - 1)
    def _():
        unroll = REC_PREP_UNROLL
        n_trips = n_groups // unroll
        st_ref[...] = s0_ref[...]
        first_pointwise(unroll)

        def rec_of_trip(t):
            return [th for u in range(unroll) for th in rec_thunks(t * unroll + u)]

        prep_groups(0, next_pointwise(0, unroll), unroll)

        def prep_rec_body(gi, _):
            prep_groups(gi, interleave(rec_of_trip(gi - 1), next_pointwise(gi, unroll)), unroll)
            return 0
        lax.fori_loop(1, n_trips - 1, prep_rec_body, 0)
        prep_groups(n_trips - 1, rec_of_trip(n_trips - 2), unroll)
        for thunk in rec_of_trip(n_trips - 1):
            thunk()


def _gdn(proj, proj_m, conv_wt, row4, state0, norm_w):
    _, b, seq, _ = proj.shape
    n_pairs = GDN_HEADS // HEAD_PAIR

    def head_specs(shape, imap):
        return [pl.BlockSpec(shape, functools.partial(imap, t * GDN_HEADS + hh))
                for t in range(3) for hh in range(HEAD_PAIR)]

    src_specs = head_specs((None, None, seq, HEAD_DIM),
                           lambda off, i, j: (off + HEAD_PAIR * j, i, 0, 0))
    meta_specs = head_specs((None, META_ROWS, HEAD_DIM), lambda off, i, j: (off + HEAD_PAIR * j, 0, 0))
    tap_specs = head_specs((CONV_WIDTH, HEAD_DIM), lambda off, i, j: (0, off + HEAD_PAIR * j))
    n_qkv = 3 * HEAD_PAIR
    z_block0 = 3 * GDN_HEADS // REC_HEADS

    per_pair = lambda width, dt: pltpu.VMEM((HEAD_PAIR, seq, width), dt)
    per_rec = lambda width, dt: pltpu.VMEM((REC_HEADS, seq, width), dt)
    return pl.pallas_call(
        functools.partial(_gdn_kernel, seq=seq),
        out_shape=jax.ShapeDtypeStruct((GDN_HEADS, b, seq, HEAD_DIM), BF16),
        grid=(b, n_pairs),
        in_specs=src_specs
        + [pl.BlockSpec((REC_HEADS, None, seq, HEAD_DIM),
                        lambda i, j: (z_block0 + j // REC_PAIRS, i, 0, 0))]
        + meta_specs + tap_specs
        + [pl.BlockSpec((None, HEAD_PAIR, 1, seq),
                        lambda i, j: (i, BETA_LANE // HEAD_PAIR + j, 0, 0)),
           pl.BlockSpec((None, HEAD_PAIR, 1, seq),
                        lambda i, j: (i, DECAY_LANE // HEAD_PAIR + j, 0, 0)),
           pl.BlockSpec((REC_HEADS, HEAD_DIM, HEAD_DIM), lambda i, j: (j // REC_PAIRS, 0, 0)),
           pl.BlockSpec((1, HEAD_DIM), lambda i, j: (0, 0))],
        out_specs=pl.BlockSpec((REC_HEADS, None, seq, HEAD_DIM),
                               lambda i, j: (j // REC_PAIRS, i, 0, 0)),
        scratch_shapes=[pltpu.VMEM((max(PREP_UNROLL, REC_PREP_UNROLL) * n_qkv, HIST_ROWS + ROW_BLOCK,
                                    HEAD_DIM), F32),
                        per_pair(HEAD_DIM, BF16), per_pair(HEAD_DIM, BF16),
                        per_pair(2 * HEAD_DIM, BF16),
                        per_pair(HEAD_DIM, F32), per_pair(HEAD_DIM, F32),
                        per_rec(HEAD_DIM, BF16),
                        pltpu.VMEM((REC_HEADS, HEAD_DIM, seq), BF16),
                        pltpu.VMEM((REC_HEADS, seq // GROUP * DEC_ROWS, HEAD_DIM), F32),
                        per_rec(HEAD_DIM, F32), per_rec(HEAD_DIM, BF16), per_rec(GROUP, BF16),
                        pltpu.VMEM((REC_HEADS, HEAD_DIM, HEAD_DIM), F32),
                        per_pair(HEAD_DIM, BF16),
                        pltpu.VMEM((HEAD_PAIR, HEAD_DIM, seq), BF16),
                        pltpu.VMEM((HEAD_PAIR, seq // GROUP * DEC_ROWS, HEAD_DIM), F32)],
        compiler_params=pltpu.CompilerParams(
            dimension_semantics=("parallel", "arbitrary"), vmem_limit_bytes=VMEM_LIMIT),
        name="gdn",
    )(*([proj] * n_qkv), proj, *([proj_m] * n_qkv), *([conv_wt] * n_qkv),
      row4, row4, state0, norm_w)


def _fox_kernel(q_ref, k_ref, v_ref, g_ref, km_ref, vm_ref, colm_ref, crow_ref,
                qw_ref, kw_ref, o_ref, qt_ref, ka_ref, kam_ref, vt_ref, vtm_ref,
                m_ref, l_ref, acc_ref, *, seq):
    n_blocks = seq // ROW_BLOCK
    aug_r = lax.broadcasted_iota(jnp.int32, (LANES, ROW_BLOCK), 0)

    def key_aug(ck, valid=None):
        hi, mid, lo = _split3(ck)
        lane = lax.broadcasted_iota(jnp.int32, ck.shape, 1)
        neg_hi = -hi.astype(F32)
        if valid is not None:
            neg_hi = jnp.where(valid, neg_hi, NEG_BIG)
        blk = jnp.where(lane < 3, 1.0,
                        jnp.where(lane == 3, neg_hi,
                                  jnp.where(lane == 4, -mid.astype(F32),
                                            jnp.where(lane == 5, -lo.astype(F32), 0.0))))
        return blk.astype(BF16)

    def pro_body(i, _):
        r0 = pl.multiple_of(i * ROW_BLOCK, ROW_BLOCK)
        rows = pl.ds(r0, ROW_BLOCK)
        for hd in range(FOX_GROUP):
            qn = _rms(q_ref[hd, rows, :].astype(F32), qw_ref[...]) * (HEAD_DIM ** -0.5 * LOG2E)
            qt_ref[hd, 0:HEAD_DIM, rows] = qn.T.astype(BF16)
            hi, mid, lo = _split3(crow_ref[hd, :, rows] * LOG2E)
            aug = jnp.where(aug_r == 0, hi.astype(F32),
                            jnp.where(aug_r == 1, mid.astype(F32),
                                      jnp.where(aug_r == 2, lo.astype(F32),
                                                jnp.where(aug_r < 6, 1.0, 0.0))))
            qt_ref[hd, HEAD_DIM:2 * HEAD_DIM, rows] = aug.astype(BF16)
            ka_ref[hd, rows, 0:HEAD_DIM] = _rms(k_ref[hd, rows, :].astype(F32),
                                                kw_ref[...]).astype(BF16)
            aug_k = jnp.where(aug_r < 3, 1.0,
                              jnp.where(aug_r == 3, -hi.astype(F32),
                                        jnp.where(aug_r == 4, -mid.astype(F32),
                                                  jnp.where(aug_r == 5, -lo.astype(F32), 0.0))))
            ka_ref[hd, rows, HEAD_DIM:2 * HEAD_DIM] = aug_k.T.astype(BF16)
            vt_ref[hd, :, rows] = v_ref[hd, rows, :].astype(F32).T.astype(BF16)
        return 0
    lax.fori_loop(0, n_blocks, pro_body, 0, unroll=BLOCK_UNROLL // FOX_GROUP)

    mrow = lax.broadcasted_iota(jnp.int32, (META_ROWS, LANES), 0)
    for hd in range(FOX_GROUP):
        head = pl.program_id(1) * FOX_GROUP + hd
        kam_ref[hd, :, 0:HEAD_DIM] = _rms(km_ref[hd].astype(F32), kw_ref[...]).astype(BF16)
        ck_m = _lane_bcast(colm_ref[...], FORGET_LANE + head) * LOG2E
        kam_ref[hd, :, HEAD_DIM:2 * HEAD_DIM] = key_aug(ck_m, mrow >= META_PAD)
        vtm_ref[hd] = vm_ref[hd].astype(F32).T.astype(BF16)

    kidx = lax.broadcasted_iota(jnp.int32, (FOX_TK, FOX_TK), 0)
    qidx = lax.broadcasted_iota(jnp.int32, (FOX_TK, FOX_TK), 1)
    diag_ok = kidx <= qidx

    items = []
    for kj in range(-1, seq // FOX_TK):
        lane0 = max(kj, 0) * FOX_TK
        for p0 in range(lane0, seq, FOX_PIECE):
            for hd in range(FOX_GROUP):
                items.append((hd, kj, slice(p0, min(p0 + FOX_PIECE, seq)), p0 == lane0))

    def scores(item):
        hd, kj, ln, leads = item
        k_aug = kam_ref[hd] if kj < 0 else ka_ref[hd, kj * FOX_TK:(kj + 1) * FOX_TK, :]
        s = _dot(k_aug, qt_ref[hd, :, ln])
        if kj >= 0 and leads:
            masked = jnp.where(diag_ok, s[:, :FOX_TK], NEG_BIG)
            s = jnp.concatenate([masked, s[:, FOX_TK:]], axis=1) if s.shape[1] > FOX_TK else masked
        return s

    def softmax_stats(item, s):
        hd, kj, ln, _ = item
        top = jnp.max(s, axis=0, keepdims=True)
        if kj < 0:
            m_new, alpha = top, None
        else:
            m_old = m_ref[hd, :, ln]
            m_new = jnp.maximum(m_old, top)
            alpha = jnp.exp2(m_old - m_new)
        p = jnp.exp2(s - m_new)
        psum = jnp.sum(p, axis=0, keepdims=True)
        m_ref[hd, :, ln] = m_new
        l_ref[hd, :, ln] = psum if kj < 0 else alpha * l_ref[hd, :, ln] + psum
        return p.astype(BF16), alpha

    def values(item, p, alpha):
        hd, kj, ln, _ = item
        v_t = vtm_ref[hd] if kj < 0 else vt_ref[hd, :, kj * FOX_TK:(kj + 1) * FOX_TK]
        pv = _dot(v_t, p)
        acc_ref[hd, :, ln] = pv if kj < 0 else alpha * acc_ref[hd, :, ln] + pv

    s_cur = scores(items[0])
    p_cur = None
    for t in range(len(items) + 1):
        s_next = scores(items[t + 1]) if t + 1 < len(items) else None
        p_next = softmax_stats(items[t], s_cur) if t < len(items) else None
        if p_cur is not None:
            values(items[t - 1], *p_cur)
        s_cur, p_cur = s_next, p_next

    def out_body(i, _):
        r0 = pl.multiple_of(i * ROW_BLOCK, ROW_BLOCK)
        rows = pl.ds(r0, ROW_BLOCK)
        for hd in range(FOX_GROUP):
            out_t = acc_ref[hd, :, rows] * (1.0 / l_ref[hd, :, rows])
            o_ref[hd, rows, :] = (out_t.T * _silu(g_ref[hd, rows, :].astype(F32))).astype(BF16)
        return 0
    lax.fori_loop(0, n_blocks, out_body, 0, unroll=BLOCK_UNROLL // FOX_GROUP)


def _fox(proj, proj_m, col_m, row4, q_w, k_w):
    _, b, seq, _ = proj.shape
    base = 4 * GDN_HEADS // FOX_GROUP
    hb = FOX_HEADS // FOX_GROUP

    def head_block(off):
        return pl.BlockSpec((FOX_GROUP, None, seq, HEAD_DIM),
                            lambda i, j, off=off: (off + j, i, 0, 0))

    def meta_block(off):
        return pl.BlockSpec((FOX_GROUP, META_ROWS, HEAD_DIM), lambda i, j, off=off: (off + j, 0, 0))

    per_head = lambda shape, dt: pltpu.VMEM((FOX_GROUP,) + shape, dt)
    return pl.pallas_call(
        functools.partial(_fox_kernel, seq=seq),
        out_shape=jax.ShapeDtypeStruct((FOX_HEADS, b, seq, HEAD_DIM), BF16),
        grid=(b, FOX_HEADS // FOX_GROUP),
        in_specs=[head_block(base), head_block(base + hb), head_block(base + 2 * hb),
                  head_block(base + 3 * hb),
                  meta_block(base + hb), meta_block(base + 2 * hb),
                  pl.BlockSpec((META_ROWS, LANES), lambda i, j: (0, 0)),
                  pl.BlockSpec((None, FOX_GROUP, 1, seq),
                               lambda i, j: (i, FORGET_LANE // FOX_GROUP + j, 0, 0)),
                  pl.BlockSpec((1, HEAD_DIM), lambda i, j: (0, 0)),
                  pl.BlockSpec((1, HEAD_DIM), lambda i, j: (0, 0))],
        out_specs=pl.BlockSpec((FOX_GROUP, None, seq, HEAD_DIM), lambda i, j: (j, i, 0, 0)),
        scratch_shapes=[per_head((2 * HEAD_DIM, seq), BF16),
                        per_head((seq, 2 * HEAD_DIM), BF16),
                        per_head((META_ROWS, 2 * HEAD_DIM), BF16),
                        per_head((HEAD_DIM, seq), BF16),
                        per_head((HEAD_DIM, META_ROWS), BF16),
                        per_head((1, seq), F32), per_head((1, seq), F32),
                        per_head((HEAD_DIM, seq), F32)],
        compiler_params=pltpu.CompilerParams(
            dimension_semantics=("parallel", "arbitrary"), vmem_limit_bytes=VMEM_LIMIT),
        name="fox",
    )(proj, proj, proj, proj, proj_m, proj_m, col_m, row4, q_w, k_w)


def _out_proj_kernel(mg_ref, mf_ref, wg_ref, wf_ref, pw_ref, x_ref, o_ref):
    for r in range(o_ref.shape[0] // OUT_PROJ_CHUNK):
        rows = slice(r * OUT_PROJ_CHUNK, (r + 1) * OUT_PROJ_CHUNK)

        def rows_of(m_ref):
            return jnp.concatenate([m_ref[h, rows, :] for h in range(m_ref.shape[0])], axis=1)
        out = _dot(rows_of(mg_ref), wg_ref[...]) + _dot(rows_of(mf_ref), wf_ref[...])
        o_ref[rows, :] = x_ref[rows, :] + _rms(out, pw_ref[...])


def _out_proj(mg, mf, w_g, w_f, post_w, x2d, tm):
    m, d = x2d.shape
    return pl.pallas_call(
        _out_proj_kernel,
        out_shape=jax.ShapeDtypeStruct((m, d), F32),
        grid=(m // tm,),
        in_specs=[pl.BlockSpec((GDN_HEADS, tm, HEAD_DIM), lambda i: (0, i, 0)),
                  pl.BlockSpec((FOX_HEADS, tm, HEAD_DIM), lambda i: (0, i, 0)),
                  pl.BlockSpec((GDN_WIDTH, d), lambda i: (0, 0), pipeline_mode=pl.Buffered(1)),
                  pl.BlockSpec((FOX_WIDTH, d), lambda i: (0, 0), pipeline_mode=pl.Buffered(1)),
                  pl.BlockSpec((1, d), lambda i: (0, 0)),
                  pl.BlockSpec((tm, d), lambda i: (i, 0))],
        out_specs=pl.BlockSpec((tm, d), lambda i: (i, 0)),
        compiler_params=pltpu.CompilerParams(
            dimension_semantics=("parallel",), vmem_limit_bytes=VMEM_LIMIT),
        name="out_proj",
    )(mg, mf, w_g, w_f, post_w, x2d)


def _tile(total, want):
    t = min(total, want)
    while total % t:
        t //= 2
    return t


def _layer(x, meta_pad, pre_w, w_in, conv_w, a_log, dt_bias, gdn_norm_w, fox_q_w, fox_k_w,
           fox_f_bias, w_out, post_w):
    b, seq, d = x.shape
    assert seq % (GROUP * PREP_UNROLL) == 0 and seq % FOX_TK == 0 and seq % ROW_BLOCK == 0
    gw, fw = GDN_WIDTH, FOX_WIDTH
    o_gb = 4 * gw
    o_f = o_gb + 2 * GDN_HEADS
    o_ff = o_f + 4 * fw
    w_t = w_in.T
    w_gate = jnp.concatenate(
        [w_t[o_gb:o_f], w_t[o_ff:],
         jnp.zeros((GATE_WIDTH - 2 * GDN_HEADS - FOX_HEADS, d), w_in.dtype)], axis=0)
    pre_w2 = pre_w[None]
    w_main, proj_m, gate_m = _w_prep(w_t, meta_pad, pre_w2, w_gate, MAIN_WIDTH, o_gb, o_f - o_gb)
    zpad = jnp.zeros((GATE_WIDTH - FORGET_LANE - FOX_HEADS,), F32)
    add_vec = jnp.concatenate([jnp.zeros((GDN_HEADS,), F32), dt_bias, fox_f_bias, zpad])[None]
    alog_vec = jnp.concatenate([jnp.zeros((GDN_HEADS,), F32), a_log,
                                jnp.zeros((FOX_HEADS,), F32), zpad])[None]

    x2d = x.reshape(b * seq, d)
    proj, gate = _in_proj(x2d, pre_w2, w_main, w_gate, _tile(b * seq, IN_PROJ_TM), IN_PROJ_TN)
    proj = proj.reshape(MAIN_WIDTH // HEAD_DIM, b, seq, HEAD_DIM)

    row4 = _gate_rows(gate.reshape(b, seq, GATE_WIDTH), add_vec, alog_vec)
    col_m, row_m = _gate_meta(gate_m, add_vec, alog_vec)
    row_m4 = row_m.reshape(1, GATE_ROWS, 1, META_ROWS)

    conv_wt = conv_w.T
    state0 = _gdn_state0(proj_m, conv_wt, col_m, row_m4)
    o_gdn = _gdn(proj, proj_m, conv_wt, row4, state0, gdn_norm_w[None])
    o_fox = _fox(proj, proj_m, col_m, row4, fox_q_w[None], fox_k_w[None])

    w_out_b = w_out.astype(BF16)
    out = _out_proj(o_gdn.reshape(GDN_HEADS, b * seq, HEAD_DIM),
                    o_fox.reshape(FOX_HEADS, b * seq, HEAD_DIM),
                    w_out_b[:gw], w_out_b[gw:], post_w[None], x2d, _tile(b * seq, OUT_PROJ_TM))
    return out.reshape(b, seq, d)


def kernel(x, meta_tokens, pre_norm_w, w_in, conv_w, a_log, dt_bias, gdn_norm_w, fox_q_norm_w,
           fox_k_norm_w, fox_f_bias, w_out, post_norm_w):
    assert pre_norm_w.shape[0] == 1, "single-layer stack"
    meta_pad = jnp.concatenate(
        [jnp.zeros((META_PAD, x.shape[-1]), x.dtype), meta_tokens.astype(x.dtype)], axis=0)
    return _layer(x, meta_pad, pre_norm_w[0], w_in[0], conv_w[0], a_log[0], dt_bias[0],
                  gdn_norm_w[0], fox_q_norm_w[0], fox_k_norm_w[0], fox_f_bias[0], w_out[0],
                  post_norm_w[0])
```

```python
import functools
import math

import jax
import jax.numpy as jnp
from jax import lax
from jax.experimental import pallas as pl
from jax.experimental.pallas import tpu as pltpu

N_META = 16
HEAD_DIM = 128
GDN_HEADS = 8
FOX_HEADS = 8
GDN_WIDTH = GDN_HEADS * HEAD_DIM
FOX_WIDTH = FOX_HEADS * HEAD_DIM
CONV_WIDTH = 4
CHUNK = 64
EPS = 1e-6

LANES = 128
SUBLANES = 8
MXU_DIM = 256
MAIN_WIDTH = 4 * GDN_WIDTH + 4 * FOX_WIDTH
GATE_WIDTH = LANES
BETA_LANE, DECAY_LANE, FORGET_LANE = 0, GDN_HEADS, 2 * GDN_HEADS
GATE_ROWS = 32
META_ROWS = CHUNK
META_PAD = META_ROWS - N_META
GROUP = MXU_DIM
ROW_BLOCK = 256
HIST_ROWS = 2 * SUBLANES
IN_PROJ_TM, IN_PROJ_TN = 1024, 2048
W_PREP_ROWS = 512
OUT_PROJ_TM = 1024
OUT_PROJ_CHUNK = 512
HEAD_PAIR = 2
REC_PAIRS = 2
REC_HEADS = REC_PAIRS * HEAD_PAIR
PREP_UNROLL = 2
REC_PREP_UNROLL = 1
DEC_ROWS = SUBLANES
BLOCK_UNROLL = 4
FOX_GROUP = 4
FOX_TK = MXU_DIM
FOX_PIECE = 2 * MXU_DIM
VMEM_LIMIT = 56 * 1024 * 1024

F32 = jnp.float32
BF16 = jnp.bfloat16
NEG_BIG = -1e30
LOG2E = math.log2(math.e)


def _dot(a, b):
    return jnp.dot(a, b, preferred_element_type=F32)


def _dot_nt(a, b):
    return lax.dot_general(a, b, (((1,), (1,)), ((), ())), preferred_element_type=F32)


def _split3(x):
    hi = x.astype(BF16)
    r1 = x - hi.astype(F32)
    mid = r1.astype(BF16)
    lo = (r1 - mid.astype(F32)).astype(BF16)
    return hi, mid, lo


def _dot_exact_rhs01(parts, m):
    return _dot(parts[0], m) + _dot(parts[1], m) + _dot(parts[2], m)


def _dot_exact_lhs01(m, parts):
    return _dot(m, parts[0]) + _dot(m, parts[1]) + _dot(m, parts[2])


def _lane_bcast(col_tile, lane):
    sel = (lax.broadcasted_iota(jnp.int32, (LANES, LANES), 0) == lane).astype(BF16)
    return _dot_exact_rhs01(_split3(col_tile), sel)


def _chunk_of(idx):
    return jnp.right_shift(idx, CHUNK.bit_length() - 1)


def _rms(x, w):
    return x * lax.rsqrt(jnp.mean(x * x, axis=-1, keepdims=True) + EPS) * w


def _silu(x):
    half = 0.5 * x
    return half + half * jnp.tanh(half)


def _softplus(x):
    return jnp.maximum(x, 0.0) + jnp.log1p(jnp.exp(-jnp.abs(x)))


def _w_prep_kernel(w_hbm, meta_ref, nw_ref, wg_ref, o_ref, om_ref, ogm_ref, buf, sem, xm_ref,
                   *, rows, shift_from, shift):
    r = pl.program_id(0)

    def fetch(step, slot):
        start = step * rows
        src = pl.multiple_of(start + jnp.where(start >= shift_from, shift, 0), SUBLANES)
        return pltpu.make_async_copy(w_hbm.at[pl.ds(src, rows), :], buf.at[slot], sem.at[slot])

    @pl.when(r == 0)
    def _():
        fetch(0, 0).start()
        xm = _rms(meta_ref[...], nw_ref[...]).astype(BF16)
        xm_ref[...] = xm
        ogm_ref[...] = _dot_nt(xm, wg_ref[...].astype(BF16))

    @pl.when(r + 1 < pl.num_programs(0))
    def _():
        fetch(r + 1, (r + 1) % 2).start()

    fetch(r, r % 2).wait()
    w_bf = buf[r % 2].astype(BF16)
    o_ref[...] = w_bf
    res = _dot_nt(xm_ref[...], w_bf)
    for hd in range(om_ref.shape[0]):
        om_ref[hd] = res[:, hd * HEAD_DIM:(hd + 1) * HEAD_DIM].astype(om_ref.dtype)


def _w_prep(w_t, meta_pad, norm_w, w_gate, n_rows, shift_from, shift):
    d = w_t.shape[1]
    rows = W_PREP_ROWS
    return pl.pallas_call(
        functools.partial(_w_prep_kernel, rows=rows, shift_from=shift_from, shift=shift),
        out_shape=(jax.ShapeDtypeStruct((n_rows, d), BF16),
                   jax.ShapeDtypeStruct((n_rows // HEAD_DIM, META_ROWS, HEAD_DIM), BF16),
                   jax.ShapeDtypeStruct((META_ROWS, GATE_WIDTH), F32)),
        grid=(n_rows // rows,),
        in_specs=[pl.BlockSpec(memory_space=pl.ANY),
                  pl.BlockSpec((META_ROWS, d), lambda r: (0, 0)),
                  pl.BlockSpec((1, d), lambda r: (0, 0)),
                  pl.BlockSpec((GATE_WIDTH, d), lambda r: (0, 0))],
        out_specs=(pl.BlockSpec((rows, d), lambda r: (r, 0)),
                   pl.BlockSpec((rows // HEAD_DIM, META_ROWS, HEAD_DIM), lambda r: (r, 0, 0)),
                   pl.BlockSpec((META_ROWS, GATE_WIDTH), lambda r: (0, 0))),
        scratch_shapes=[pltpu.VMEM((2, rows, d), F32), pltpu.SemaphoreType.DMA((2,)),
                        pltpu.VMEM((META_ROWS, d), BF16)],
        compiler_params=pltpu.CompilerParams(
            dimension_semantics=("arbitrary",), vmem_limit_bytes=VMEM_LIMIT),
        name="w_prep",
    )(w_t, meta_pad, norm_w, w_gate)


def _in_proj_kernel(x_ref, nw_ref, w_ref, wg_ref, o_ref, og_ref, xn_ref):
    @pl.when(pl.program_id(1) == 0)
    def _():
        xn = _rms(x_ref[...], nw_ref[...]).astype(BF16)
        xn_ref[...] = xn
        og_ref[...] = _dot_nt(xn, wg_ref[...].astype(BF16))

    res = _dot_nt(xn_ref[...], w_ref[...])
    for hd in range(o_ref.shape[0]):
        o_ref[hd] = res[:, hd * HEAD_DIM:(hd + 1) * HEAD_DIM].astype(o_ref.dtype)


def _in_proj(x2d, norm_w, w_main, w_gate, tm, tn):
    m, d = x2d.shape
    n = w_main.shape[0]
    return pl.pallas_call(
        _in_proj_kernel,
        out_shape=(jax.ShapeDtypeStruct((n // HEAD_DIM, m, HEAD_DIM), BF16),
                   jax.ShapeDtypeStruct((m, GATE_WIDTH), F32)),
        grid=(m // tm, n // tn),
        in_specs=[pl.BlockSpec((tm, d), lambda i, j: (i, 0)),
                  pl.BlockSpec((1, d), lambda i, j: (0, 0)),
                  pl.BlockSpec((tn, d), lambda i, j: (j, 0)),
                  pl.BlockSpec((GATE_WIDTH, d), lambda i, j: (0, 0))],
        out_specs=(pl.BlockSpec((tn // HEAD_DIM, tm, HEAD_DIM), lambda i, j: (j, i, 0)),
                   pl.BlockSpec((tm, GATE_WIDTH), lambda i, j: (i, 0))),
        scratch_shapes=[pltpu.VMEM((tm, d), BF16)],
        compiler_params=pltpu.CompilerParams(
            dimension_semantics=("parallel", "arbitrary"), vmem_limit_bytes=VMEM_LIMIT),
        name="in_proj",
    )(x2d, norm_w, w_main, w_gate)


def _gate_meta_kernel(t_ref, add_ref, alog_ref, col_ref, row_ref):
    lane = lax.broadcasted_iota(jnp.int32, (META_ROWS, LANES), 1)
    row = lax.broadcasted_iota(jnp.int32, (META_ROWS, LANES), 0)
    ri = lax.broadcasted_iota(jnp.int32, (META_ROWS, META_ROWS), 0)
    ci = lax.broadcasted_iota(jnp.int32, (META_ROWS, META_ROWS), 1)
    tri = (ci <= ri).astype(BF16)
    is_beta = lane < DECAY_LANE
    is_decay = (lane >= DECAY_LANE) & (lane < FORGET_LANE)
    is_forget = (lane >= FORGET_LANE) & (lane < FORGET_LANE + FOX_HEADS)
    real = row >= META_PAD

    t = t_ref[...]
    ta = t + add_ref[...]
    beta = jnp.where(real, 1.0 / (1.0 + jnp.exp(-t)), 0.0)
    g = -jnp.exp(alog_ref[...]) * _softplus(ta)
    logf = -_softplus(-ta)
    val = jnp.where(real, jnp.where(is_decay, g, jnp.where(is_forget, logf, 0.0)), 0.0)
    cum = _dot_exact_lhs01(tri, _split3(val))
    to_come = cum - cum[META_ROWS - 1:META_ROWS, :]
    res = jnp.where(is_beta, beta, jnp.where(is_decay, cum, to_come))
    col_ref[...] = res
    row_ref[...] = res.T[:GATE_ROWS, :]


def _gate_rows_kernel(t_ref, add_ref, alog_ref, row_ref, *, rows):
    blk = ROW_BLOCK
    row = lax.broadcasted_iota(jnp.int32, (GATE_ROWS, blk), 0)
    si = lax.broadcasted_iota(jnp.int32, (blk, blk), 0)
    ti = lax.broadcasted_iota(jnp.int32, (blk, blk), 1)
    tri = (si <= ti).astype(BF16)
    tri_chunk = ((si <= ti) & (_chunk_of(si) == _chunk_of(ti))).astype(BF16)
    is_beta = row < DECAY_LANE
    is_decay = (row >= DECAY_LANE) & (row < FORGET_LANE)
    is_forget = (row >= FORGET_LANE) & (row < FORGET_LANE + FOX_HEADS)
    add = jnp.concatenate([add_ref[...]] * (blk // LANES), axis=1)
    neg_rate = -jnp.exp(jnp.concatenate([alog_ref[...]] * (blk // LANES), axis=1))

    carry = jnp.zeros((GATE_ROWS, 1), F32)
    for r in range(rows // blk):
        t = t_ref[r * blk:(r + 1) * blk, :].T[:GATE_ROWS, :]
        ta = t + add
        beta = 1.0 / (1.0 + jnp.exp(-t))
        val = jnp.where(is_decay, neg_rate * _softplus(ta), jnp.where(is_forget, -_softplus(-ta), 0.0))
        parts = _split3(val)
        cum_chunk = _dot_exact_rhs01(parts, tri_chunk)
        cum_all = _dot_exact_rhs01(parts, tri) + carry
        carry = cum_all[:, blk - 1:blk]
        row_ref[:, 0, r * blk:(r + 1) * blk] = jnp.where(is_beta, beta,
                                                         jnp.where(is_decay, cum_chunk, cum_all))


def _gate_rows(gate3d, add_vec, alog_vec):
    b, rows, _ = gate3d.shape
    per_row = lambda v: jnp.broadcast_to(v[0, :GATE_ROWS, None], (GATE_ROWS, LANES))
    return pl.pallas_call(
        functools.partial(_gate_rows_kernel, rows=rows),
        out_shape=jax.ShapeDtypeStruct((b, GATE_ROWS, 1, rows), F32),
        grid=(b,),
        in_specs=[pl.BlockSpec((None, rows, LANES), lambda i: (i, 0, 0)),
                  pl.BlockSpec((GATE_ROWS, LANES), lambda i: (0, 0)),
                  pl.BlockSpec((GATE_ROWS, LANES), lambda i: (0, 0))],
        out_specs=pl.BlockSpec((None, GATE_ROWS, 1, rows), lambda i: (i, 0, 0, 0)),
        compiler_params=pltpu.CompilerParams(
            dimension_semantics=("parallel",), vmem_limit_bytes=VMEM_LIMIT),
        name="gate_rows",
    )(gate3d, per_row(add_vec), per_row(alog_vec))


def _gate_meta(gate_m, add_vec, alog_vec):
    return pl.pallas_call(
        _gate_meta_kernel,
        out_shape=(jax.ShapeDtypeStruct((META_ROWS, LANES), F32),
                   jax.ShapeDtypeStruct((GATE_ROWS, META_ROWS), F32)),
        grid=(1,),
        in_specs=[pl.BlockSpec((META_ROWS, LANES), lambda i: (0, 0)),
                  pl.BlockSpec((1, LANES), lambda i: (0, 0)),
                  pl.BlockSpec((1, LANES), lambda i: (0, 0))],
        out_specs=(pl.BlockSpec((META_ROWS, LANES), lambda i: (0, 0)),
                   pl.BlockSpec((GATE_ROWS, META_ROWS), lambda i: (0, 0))),
        compiler_params=pltpu.CompilerParams(
            dimension_semantics=("arbitrary",), vmem_limit_bytes=VMEM_LIMIT),
        name="gate_meta",
    )(gate_m, add_vec, alog_vec)


def _conv_silu(load_rows, w):
    y = load_rows(0) * w[0:1, :]
    for j in range(1, CONV_WIDTH):
        y = y + load_rows(j) * w[j:j + 1, :]
    return _silu(y)


def _l2norm(x):
    return x * lax.rsqrt(jnp.sum(x * x, axis=-1, keepdims=True) + EPS)


def _gdn_pointwise(q, k, v, beta_b, g_b):
    r = q.shape[0]
    q = q * (lax.rsqrt(jnp.sum(q * q, axis=-1, keepdims=True) + EPS) * (HEAD_DIM ** -0.5))
    k = _l2norm(k)
    g3 = g_b.reshape(r // CHUNK, CHUNK, LANES)
    g_last = jnp.broadcast_to(g3[:, CHUNK - 1:CHUNK, :], g3.shape).reshape(r, LANES)
    e_g = jnp.exp(g_b)
    q_dec = q * e_g
    k_dec = k * jnp.exp(g_last - g_b)
    y = jnp.concatenate([v * beta_b, k * (beta_b * e_g)], axis=1)
    return q, k, q_dec, k_dec, y, jnp.exp(g_last)


def _gdn_groups(probs, fillers=()):
    r = probs[0][0].shape[0]
    ri = lax.broadcasted_iota(jnp.int32, (r, r), 0)
    ci = lax.broadcasted_iota(jnp.int32, (r, r), 1)
    same = _chunk_of(ri) == _chunk_of(ci)
    causal = same & (ci <= ri)
    strict = same & (ci < ri)

    def widen(t):
        return jnp.concatenate([t] * (r // LANES), axis=1) if r >= LANES else t[:, :r]

    fillers = list(fillers)
    n_ticks = 2 * CHUNK.bit_length() - 3
    per_tick = -(-len(fillers) // n_ticks)

    def tick():
        for thunk in fillers[:per_tick]:
            thunk()
        del fillers[:per_tick]

    kks = [_dot_nt(k, k) for _, k, _, _, _, _ in probs]
    qks = [_dot_nt(q, k) for q, k, _, _, _, _ in probs]
    tick()
    dmats = [jnp.where(causal, jnp.exp(jnp.where(causal, widen(g_b) - g_row, 0.0)), 0.0)
             for _, _, _, _, g_b, g_row in probs]
    xs = [jnp.where(strict, widen(p[3]) * kk * d, 0.0).astype(BF16)
          for p, kk, d in zip(probs, kks, dmats)]
    a_qks = [qk * d for qk, d in zip(qks, dmats)]
    eye = (ri == ci).astype(BF16)
    zs = [_dot(eye - x, p[2]) for p, x in zip(probs, xs)]
    span = 2
    while span < CHUNK:
        xs = [_dot(x, x).astype(BF16) for x in xs]
        tick()
        zs = [z + _dot(x, z.astype(BF16)) for x, z in zip(xs, zs)]
        tick()
        span *= 2
    assert not fillers
    return list(zip(zs, a_qks))


def _gdn_state0_kernel(km_ref, vm_ref, wk_ref, wv_ref, colm_ref, growm_ref, s_ref, pad_ref):
    heads = range(GDN_HEADS)
    probs, kd_t = [], []
    for h in heads:
        ls = slice(h * HEAD_DIM, (h + 1) * HEAD_DIM)
        beta_m = _lane_bcast(colm_ref[...], BETA_LANE + h)
        g_m = _lane_bcast(colm_ref[...], DECAY_LANE + h)
        conv = []
        for t, (src, w_ref) in enumerate(((km_ref, wk_ref), (vm_ref, wv_ref))):
            win = pad_ref.at[2 * h + t]
            win[0:SUBLANES, :] = jnp.zeros((SUBLANES, LANES), F32)
            win[SUBLANES:SUBLANES + META_ROWS, :] = src[h].astype(F32)
            conv.append(_conv_silu(
                lambda j, win=win: win[SUBLANES - (CONV_WIDTH - 1) + j:
                                       SUBLANES - (CONV_WIDTH - 1) + j + META_ROWS, :],
                w_ref[:, ls]))
        _, k_m, _, kd_m, y_m, _ = _gdn_pointwise(conv[0], conv[0], conv[1], beta_m, g_m)
        k_bf = k_m.astype(BF16)
        probs.append((k_bf, k_bf, y_m.astype(BF16), beta_m, g_m, growm_ref[h]))
        kd_t.append(kd_m.T.astype(BF16))
    for h, (uw_m, _) in zip(heads, _gdn_groups(probs)):
        s_ref[h] = _dot(kd_t[h], uw_m[:, :HEAD_DIM].astype(BF16))


def _gdn_state0(proj_m, conv_wt, col_m, row_m4):
    heads_block = lambda blk: pl.BlockSpec((GDN_HEADS, META_ROWS, HEAD_DIM),
                                           lambda j, blk=blk: (blk, 0, 0))
    taps_block = lambda blk: pl.BlockSpec((CONV_WIDTH, GDN_WIDTH), lambda j, blk=blk: (0, blk))
    return pl.pallas_call(
        _gdn_state0_kernel,
        out_shape=jax.ShapeDtypeStruct((GDN_HEADS, HEAD_DIM, HEAD_DIM), F32),
        grid=(1,),
        in_specs=[heads_block(1), heads_block(2), taps_block(1), taps_block(2),
                  pl.BlockSpec((META_ROWS, LANES), lambda j: (0, 0)),
                  pl.BlockSpec((None, GDN_HEADS, 1, META_ROWS),
                               lambda j: (0, DECAY_LANE // GDN_HEADS, 0, 0))],
        out_specs=pl.BlockSpec((GDN_HEADS, HEAD_DIM, HEAD_DIM), lambda j: (0, 0, 0)),
        scratch_shapes=[pltpu.VMEM((2 * GDN_HEADS, META_ROWS + SUBLANES, HEAD_DIM), F32)],
        compiler_params=pltpu.CompilerParams(
            dimension_semantics=("arbitrary",), vmem_limit_bytes=VMEM_LIMIT),
        name="gdn_state0",
    )(proj_m, proj_m, conv_wt, conv_wt, col_m, row_m4)


def _gdn_kernel(*refs, seq):
    n_qkv = 3 * HEAD_PAIR
    src_refs = [refs[t * HEAD_PAIR:(t + 1) * HEAD_PAIR] for t in range(3)]
    z_ref = refs[n_qkv]
    meta_refs = [refs[n_qkv + 1 + t * HEAD_PAIR:n_qkv + 1 + (t + 1) * HEAD_PAIR] for t in range(3)]
    tap_refs = [refs[2 * n_qkv + 1 + t * HEAD_PAIR:2 * n_qkv + 1 + (t + 1) * HEAD_PAIR]
                for t in range(3)]
    (brow_ref, grow_ref, s0_ref, nw_ref, o_ref, pad_ref, qs_ref, ks_ref, y_ref, bb_ref, gb_ref,
     qd_ref, kdt_ref, dec_ref, u_ref, w_ref, aqk_ref, st_ref,
     qd_stage, kdt_stage, dec_stage) = refs[3 * n_qkv + 1:]
    pair = pl.program_id(1)
    slot = pair % REC_PAIRS
    n_blocks = seq // ROW_BLOCK
    n_groups = seq // GROUP
    cpg = GROUP // CHUNK
    hist = CONV_WIDTH - 1

    def pointwise_head(i, bank, hh):
        r0 = pl.multiple_of(i * ROW_BLOCK, ROW_BLOCK)
        rows = pl.ds(r0, ROW_BLOCK)
        convs = []
        for t in range(3):
            win = pad_ref.at[(bank * 3 + t) * HEAD_PAIR + hh]
            if isinstance(i, int) and i == 0:
                past = meta_refs[t][hh][META_ROWS - HIST_ROWS:META_ROWS, :]
            else:
                past = src_refs[t][hh][pl.ds(pl.multiple_of(r0 - HIST_ROWS, HIST_ROWS), HIST_ROWS), :]
            win[0:HIST_ROWS, :] = past.astype(F32)
            win[HIST_ROWS:HIST_ROWS + ROW_BLOCK, :] = src_refs[t][hh][rows, :].astype(F32)
            convs.append(_conv_silu(
                lambda j, win=win: win[HIST_ROWS - hist + j:HIST_ROWS - hist + j + ROW_BLOCK, :],
                tap_refs[t][hh][...]))
        beta_b = jnp.broadcast_to(brow_ref[hh, :, rows], (LANES, ROW_BLOCK)).T
        g_b = jnp.broadcast_to(grow_ref[hh, :, rows], (LANES, ROW_BLOCK)).T
        q, k, q_dec, k_dec, y, dec = _gdn_pointwise(convs[0], convs[1], convs[2], beta_b, g_b)
        qs_ref[hh, rows, :] = q.astype(BF16)
        ks_ref[hh, rows, :] = k.astype(BF16)
        y_ref[hh, rows, :] = y.astype(BF16)
        bb_ref[hh, rows, :] = beta_b
        gb_ref[hh, rows, :] = g_b
        qd_stage[hh, rows, :] = q_dec.astype(BF16)
        kdt_stage[hh, :, rows] = k_dec.T.astype(BF16)
        n_chunks = ROW_BLOCK // CHUNK
        dec_rows = [dec[c * CHUNK:c * CHUNK + 1, :] for c in range(n_chunks)]
        dec_rows.append(jnp.zeros((DEC_ROWS - n_chunks, LANES), F32))
        dec_stage[hh, pl.ds(pl.multiple_of(i * DEC_ROWS, DEC_ROWS), DEC_ROWS), :] = (
            jnp.concatenate(dec_rows, axis=0))

    def pointwise_thunks(i, bank):
        return [functools.partial(pointwise_head, i, bank, hh) for hh in range(HEAD_PAIR)]

    def first_pointwise(unroll):
        for i in range(unroll):
            for thunk in pointwise_thunks(i, i):
                thunk()

    def prep_groups(gi, fillers, unroll):
        groups = [gi * unroll + u for u in range(unroll) for _ in range(HEAD_PAIR)]
        keys = [(hh, pl.ds(pl.multiple_of((gi * unroll + u) * GROUP, GROUP), GROUP))
                for u in range(unroll) for hh in range(HEAD_PAIR)]
        probs = [(qs_ref[hh, rows, :], ks_ref[hh, rows, :], y_ref[hh, rows, :],
                  bb_ref[hh, rows, :], gb_ref[hh, rows, :], grow_ref[hh, :, rows])
                 for hh, rows in keys]
        for g, (hh, rows), (uw, a_qk) in zip(groups, keys, _gdn_groups(probs, fillers)):
            hs = slot * HEAD_PAIR + hh
            u_ref[hs, rows, :] = uw[:, :HEAD_DIM]
            w_ref[hs, rows, :] = uw[:, HEAD_DIM:].astype(BF16)
            aqk_ref[hs, rows, :] = a_qk.astype(BF16)
            drows = pl.ds(pl.multiple_of(g * DEC_ROWS, DEC_ROWS), DEC_ROWS)
            qd_ref[hs, rows, :] = qd_stage[hh, rows, :]
            kdt_ref[hs, :, rows] = kdt_stage[hh, :, rows]
            dec_ref[hs, drows, :] = dec_stage[hh, drows, :]

    def next_pointwise(gi, unroll):
        return [th for u in range(unroll) for th in pointwise_thunks((gi + 1) * unroll + u, u)]

    def rec_thunks(gi):
        r0 = pl.multiple_of(gi * GROUP, GROUP)
        rows = pl.ds(r0, GROUP)
        heads = range(REC_HEADS)
        outs = [[] for _ in heads]

        held = {}

        def read_state(c):
            crow = pl.ds(r0 + c * CHUNK, CHUNK)
            states = [st_ref[h] for h in heads]
            s_bf = [s.astype(BF16) for s in states]
            ws = [_dot(jnp.concatenate([w_ref[h, crow, :], qd_ref[h, crow, :]], axis=0), s_bf[h])
                  for h in heads]
            held[c] = (states, ws)

        def write_state(c):
            crow = pl.ds(r0 + c * CHUNK, CHUNK)
            states, ws = held.pop(c)
            v_new = [u_ref[h, crow, :] - ws[h][:CHUNK, :] for h in heads]
            zero = lambda n: jnp.zeros((n * CHUNK, HEAD_DIM), BF16)
            v_pad = [jnp.concatenate(([zero(c)] if c else []) + [v.astype(BF16)]
                                     + ([zero(cpg - 1 - c)] if c < cpg - 1 else []), axis=0)
                     for v in v_new]
            upd = [_dot(jnp.concatenate([aqk_ref[h, crow, :], kdt_ref[h, :, rows]], axis=0),
                        v_pad[h]) for h in heads]
            for h in heads:
                dec = dec_ref[h, pl.ds(gi * DEC_ROWS + c, 1), :]
                st_ref[h] = states[h] * dec + upd[h][CHUNK:, :]
                outs[h].append(ws[h][CHUNK:, :] + upd[h][:CHUNK, :])
            if c == cpg - 1:
                for h in heads:
                    o = _rms(jnp.concatenate(outs[h], axis=0), nw_ref[...])
                    o_ref[h, rows, :] = (o * _silu(z_ref[h, rows, :].astype(F32))).astype(BF16)

        return [functools.partial(half, c) for c in range(cpg) for half in (read_state, write_state)]

    def interleave(a, b):
        if not a or not b:
            return list(a) + list(b)
        out, j = [], 0
        for i, th in enumerate(a):
            out.append(th)
            while j < len(b) and (j + 1) * len(a) <= (i + 1) * len(b):
                out.append(b[j])
                j += 1
        return out + list(b[j:])

    @pl.when(slot != REC_PAIRS - 1)
    def _():
        unroll = PREP_UNROLL
        n_trips = n_groups // unroll
        first_pointwise(unroll)

        def prep_body(gi, _):
            prep_groups(gi, next_pointwise(gi, unroll), unroll)
            return 0
        lax.fori_loop(0, n_trips - 1, prep_body, 0)
        prep_groups(n_trips - 1, [], unroll)

    @pl.when(slot == REC_PAIRS - 1)
    def _():
        unroll = REC_PREP_UNROLL
        n_trips = n_groups // unroll
        st_ref[...] = s0_ref[...]
        first_pointwise(unroll)

        def rec_of_trip(t):
            return [th for u in range(unroll) for th in rec_thunks(t * unroll + u)]

        prep_groups(0, next_pointwise(0, unroll), unroll)

        def prep_rec_body(gi, _):
            prep_groups(gi, interleave(rec_of_trip(gi - 1), next_pointwise(gi, unroll)), unroll)
            return 0
        lax.fori_loop(1, n_trips - 1, prep_rec_body, 0)
        prep_groups(n_trips - 1, rec_of_trip(n_trips - 2), unroll)
        for thunk in rec_of_trip(n_trips - 1):
            thunk()


def _gdn(proj, proj_m, conv_wt, row4, state0, norm_w):
    _, b, seq, _ = proj.shape
    n_pairs = GDN_HEADS // HEAD_PAIR

    def head_specs(shape, imap):
        return [pl.BlockSpec(shape, functools.partial(imap, t * GDN_HEADS + hh))
                for t in range(3) for hh in range(HEAD_PAIR)]

    src_specs = head_specs((None, None, seq, HEAD_DIM),
                           lambda off, i, j: (off + HEAD_PAIR * j, i, 0, 0))
    meta_specs = head_specs((None, META_ROWS, HEAD_DIM), lambda off, i, j: (off + HEAD_PAIR * j, 0, 0))
    tap_specs = head_specs((CONV_WIDTH, HEAD_DIM), lambda off, i, j: (0, off + HEAD_PAIR * j))
    n_qkv = 3 * HEAD_PAIR
    z_block0 = 3 * GDN_HEADS // REC_HEADS

    per_pair = lambda width, dt: pltpu.VMEM((HEAD_PAIR, seq, width), dt)
    per_rec = lambda width, dt: pltpu.VMEM((REC_HEADS, seq, width), dt)
    return pl.pallas_call(
        functools.partial(_gdn_kernel, seq=seq),
        out_shape=jax.ShapeDtypeStruct((GDN_HEADS, b, seq, HEAD_DIM), BF16),
        grid=(b, n_pairs),
        in_specs=src_specs
        + [pl.BlockSpec((REC_HEADS, None, seq, HEAD_DIM),
                        lambda i, j: (z_block0 + j // REC_PAIRS, i, 0, 0))]
        + meta_specs + tap_specs
        + [pl.BlockSpec((None, HEAD_PAIR, 1, seq),
                        lambda i, j: (i, BETA_LANE // HEAD_PAIR + j, 0, 0)),
           pl.BlockSpec((None, HEAD_PAIR, 1, seq),
                        lambda i, j: (i, DECAY_LANE // HEAD_PAIR + j, 0, 0)),
           pl.BlockSpec((REC_HEADS, HEAD_DIM, HEAD_DIM), lambda i, j: (j // REC_PAIRS, 0, 0)),
           pl.BlockSpec((1, HEAD_DIM), lambda i, j: (0, 0))],
        out_specs=pl.BlockSpec((REC_HEADS, None, seq, HEAD_DIM),
                               lambda i, j: (j // REC_PAIRS, i, 0, 0)),
        scratch_shapes=[pltpu.VMEM((max(PREP_UNROLL, REC_PREP_UNROLL) * n_qkv, HIST_ROWS + ROW_BLOCK,
                                    HEAD_DIM), F32),
                        per_pair(HEAD_DIM, BF16), per_pair(HEAD_DIM, BF16),
                        per_pair(2 * HEAD_DIM, BF16),
                        per_pair(HEAD_DIM, F32), per_pair(HEAD_DIM, F32),
                        per_rec(HEAD_DIM, BF16),
                        pltpu.VMEM((REC_HEADS, HEAD_DIM, seq), BF16),
                        pltpu.VMEM((REC_HEADS, seq // GROUP * DEC_ROWS, HEAD_DIM), F32),
                        per_rec(HEAD_DIM, F32), per_rec(HEAD_DIM, BF16), per_rec(GROUP, BF16),
                        pltpu.VMEM((REC_HEADS, HEAD_DIM, HEAD_DIM), F32),
                        per_pair(HEAD_DIM, BF16),
                        pltpu.VMEM((HEAD_PAIR, HEAD_DIM, seq), BF16),
                        pltpu.VMEM((HEAD_PAIR, seq // GROUP * DEC_ROWS, HEAD_DIM), F32)],
        compiler_params=pltpu.CompilerParams(
            dimension_semantics=("parallel", "arbitrary"), vmem_limit_bytes=VMEM_LIMIT),
        name="gdn",
    )(*([proj] * n_qkv), proj, *([proj_m] * n_qkv), *([conv_wt] * n_qkv),
      row4, row4, state0, norm_w)


def _fox_kernel(q_ref, k_ref, v_ref, g_ref, km_ref, vm_ref, colm_ref, crow_ref,
                qw_ref, kw_ref, o_ref, qt_ref, ka_ref, kam_ref, vt_ref, vtm_ref,
                m_ref, l_ref, acc_ref, *, seq):
    n_blocks = seq // ROW_BLOCK
    aug_r = lax.broadcasted_iota(jnp.int32, (LANES, ROW_BLOCK), 0)

    def key_aug(ck, valid=None):
        hi, mid, lo = _split3(ck)
        lane = lax.broadcasted_iota(jnp.int32, ck.shape, 1)
        neg_hi = -hi.astype(F32)
        if valid is not None:
            neg_hi = jnp.where(valid, neg_hi, NEG_BIG)
        blk = jnp.where(lane < 3, 1.0,
                        jnp.where(lane == 3, neg_hi,
                                  jnp.where(lane == 4, -mid.astype(F32),
                                            jnp.where(lane == 5, -lo.astype(F32), 0.0))))
        return blk.astype(BF16)

    def pro_body(i, _):
        r0 = pl.multiple_of(i * ROW_BLOCK, ROW_BLOCK)
        rows = pl.ds(r0, ROW_BLOCK)
        for hd in range(FOX_GROUP):
            qn = _rms(q_ref[hd, rows, :].astype(F32), qw_ref[...]) * (HEAD_DIM ** -0.5 * LOG2E)
            qt_ref[hd, 0:HEAD_DIM, rows] = qn.T.astype(BF16)
            hi, mid, lo = _split3(crow_ref[hd, :, rows] * LOG2E)
            aug = jnp.where(aug_r == 0, hi.astype(F32),
                            jnp.where(aug_r == 1, mid.astype(F32),
                                      jnp.where(aug_r == 2, lo.astype(F32),
                                                jnp.where(aug_r < 6, 1.0, 0.0))))
            qt_ref[hd, HEAD_DIM:2 * HEAD_DIM, rows] = aug.astype(BF16)
            ka_ref[hd, rows, 0:HEAD_DIM] = _rms(k_ref[hd, rows, :].astype(F32),
                                                kw_ref[...]).astype(BF16)
            aug_k = jnp.where(aug_r < 3, 1.0,
                              jnp.where(aug_r == 3, -hi.astype(F32),
                                        jnp.where(aug_r == 4, -mid.astype(F32),
                                                  jnp.where(aug_r == 5, -lo.astype(F32), 0.0))))
            ka_ref[hd, rows, HEAD_DIM:2 * HEAD_DIM] = aug_k.T.astype(BF16)
            vt_ref[hd, :, rows] = v_ref[hd, rows, :].astype(F32).T.astype(BF16)
        return 0
    lax.fori_loop(0, n_blocks, pro_body, 0, unroll=BLOCK_UNROLL // FOX_GROUP)

    mrow = lax.broadcasted_iota(jnp.int32, (META_ROWS, LANES), 0)
    for hd in range(FOX_GROUP):
        head = pl.program_id(1) * FOX_GROUP + hd
        kam_ref[hd, :, 0:HEAD_DIM] = _rms(km_ref[hd].astype(F32), kw_ref[...]).astype(BF16)
        ck_m = _lane_bcast(colm_ref[...], FORGET_LANE + head) * LOG2E
        kam_ref[hd, :, HEAD_DIM:2 * HEAD_DIM] = key_aug(ck_m, mrow >= META_PAD)
        vtm_ref[hd] = vm_ref[hd].astype(F32).T.astype(BF16)

    kidx = lax.broadcasted_iota(jnp.int32, (FOX_TK, FOX_TK), 0)
    qidx = lax.broadcasted_iota(jnp.int32, (FOX_TK, FOX_TK), 1)
    diag_ok = kidx <= qidx

    items = []
    for kj in range(-1, seq // FOX_TK):
        lane0 = max(kj, 0) * FOX_TK
        for p0 in range(lane0, seq, FOX_PIECE):
            for hd in range(FOX_GROUP):
                items.append((hd, kj, slice(p0, min(p0 + FOX_PIECE, seq)), p0 == lane0))

    def scores(item):
        hd, kj, ln, leads = item
        k_aug = kam_ref[hd] if kj < 0 else ka_ref[hd, kj * FOX_TK:(kj + 1) * FOX_TK, :]
        s = _dot(k_aug, qt_ref[hd, :, ln])
        if kj >= 0 and leads:
            masked = jnp.where(diag_ok, s[:, :FOX_TK], NEG_BIG)
            s = jnp.concatenate([masked, s[:, FOX_TK:]], axis=1) if s.shape[1] > FOX_TK else masked
        return s

    def softmax_stats(item, s):
        hd, kj, ln, _ = item
        top = jnp.max(s, axis=0, keepdims=True)
        if kj < 0:
            m_new, alpha = top, None
        else:
            m_old = m_ref[hd, :, ln]
            m_new = jnp.maximum(m_old, top)
            alpha = jnp.exp2(m_old - m_new)
        p = jnp.exp2(s - m_new)
        psum = jnp.sum(p, axis=0, keepdims=True)
        m_ref[hd, :, ln] = m_new
        l_ref[hd, :, ln] = psum if kj < 0 else alpha * l_ref[hd, :, ln] + psum
        return p.astype(BF16), alpha

    def values(item, p, alpha):
        hd, kj, ln, _ = item
        v_t = vtm_ref[hd] if kj < 0 else vt_ref[hd, :, kj * FOX_TK:(kj + 1) * FOX_TK]
        pv = _dot(v_t, p)
        acc_ref[hd, :, ln] = pv if kj < 0 else alpha * acc_ref[hd, :, ln] + pv

    s_cur = scores(items[0])
    p_cur = None
    for t in range(len(items) + 1):
        s_next = scores(items[t + 1]) if t + 1 < len(items) else None
        p_next = softmax_stats(items[t], s_cur) if t < len(items) else None
        if p_cur is not None:
            values(items[t - 1], *p_cur)
        s_cur, p_cur = s_next, p_next

    def out_body(i, _):
        r0 = pl.multiple_of(i * ROW_BLOCK, ROW_BLOCK)
        rows = pl.ds(r0, ROW_BLOCK)
        for hd in range(FOX_GROUP):
            out_t = acc_ref[hd, :, rows] * (1.0 / l_ref[hd, :, rows])
            o_ref[hd, rows, :] = (out_t.T * _silu(g_ref[hd, rows, :].astype(F32))).astype(BF16)
        return 0
    lax.fori_loop(0, n_blocks, out_body, 0, unroll=BLOCK_UNROLL // FOX_GROUP)


def _fox(proj, proj_m, col_m, row4, q_w, k_w):
    _, b, seq, _ = proj.shape
    base = 4 * GDN_HEADS // FOX_GROUP
    hb = FOX_HEADS // FOX_GROUP

    def head_block(off):
        return pl.BlockSpec((FOX_GROUP, None, seq, HEAD_DIM),
                            lambda i, j, off=off: (off + j, i, 0, 0))

    def meta_block(off):
        return pl.BlockSpec((FOX_GROUP, META_ROWS, HEAD_DIM), lambda i, j, off=off: (off + j, 0, 0))

    per_head = lambda shape, dt: pltpu.VMEM((FOX_GROUP,) + shape, dt)
    return pl.pallas_call(
        functools.partial(_fox_kernel, seq=seq),
        out_shape=jax.ShapeDtypeStruct((FOX_HEADS, b, seq, HEAD_DIM), BF16),
        grid=(b, FOX_HEADS // FOX_GROUP),
        in_specs=[head_block(base), head_block(base + hb), head_block(base + 2 * hb),
                  head_block(base + 3 * hb),
                  meta_block(base + hb), meta_block(base + 2 * hb),
                  pl.BlockSpec((META_ROWS, LANES), lambda i, j: (0, 0)),
                  pl.BlockSpec((None, FOX_GROUP, 1, seq),
                               lambda i, j: (i, FORGET_LANE // FOX_GROUP + j, 0, 0)),
                  pl.BlockSpec((1, HEAD_DIM), lambda i, j: (0, 0)),
                  pl.BlockSpec((1, HEAD_DIM), lambda i, j: (0, 0))],
        out_specs=pl.BlockSpec((FOX_GROUP, None, seq, HEAD_DIM), lambda i, j: (j, i, 0, 0)),
        scratch_shapes=[per_head((2 * HEAD_DIM, seq), BF16),
                        per_head((seq, 2 * HEAD_DIM), BF16),
                        per_head((META_ROWS, 2 * HEAD_DIM), BF16),
                        per_head((HEAD_DIM, seq), BF16),
                        per_head((HEAD_DIM, META_ROWS), BF16),
                        per_head((1, seq), F32), per_head((1, seq), F32),
                        per_head((HEAD_DIM, seq), F32)],
        compiler_params=pltpu.CompilerParams(
            dimension_semantics=("parallel", "arbitrary"), vmem_limit_bytes=VMEM_LIMIT),
        name="fox",
    )(proj, proj, proj, proj, proj_m, proj_m, col_m, row4, q_w, k_w)


def _out_proj_kernel(mg_ref, mf_ref, wg_ref, wf_ref, pw_ref, x_ref, o_ref):
    for r in range(o_ref.shape[0] // OUT_PROJ_CHUNK):
        rows = slice(r * OUT_PROJ_CHUNK, (r + 1) * OUT_PROJ_CHUNK)

        def rows_of(m_ref):
            return jnp.concatenate([m_ref[h, rows, :] for h in range(m_ref.shape[0])], axis=1)
        out = _dot(rows_of(mg_ref), wg_ref[...]) + _dot(rows_of(mf_ref), wf_ref[...])
        o_ref[rows, :] = x_ref[rows, :] + _rms(out, pw_ref[...])


def _out_proj(mg, mf, w_g, w_f, post_w, x2d, tm):
    m, d = x2d.shape
    return pl.pallas_call(
        _out_proj_kernel,
        out_shape=jax.ShapeDtypeStruct((m, d), F32),
        grid=(m // tm,),
        in_specs=[pl.BlockSpec((GDN_HEADS, tm, HEAD_DIM), lambda i: (0, i, 0)),
                  pl.BlockSpec((FOX_HEADS, tm, HEAD_DIM), lambda i: (0, i, 0)),
                  pl.BlockSpec((GDN_WIDTH, d), lambda i: (0, 0), pipeline_mode=pl.Buffered(1)),
                  pl.BlockSpec((FOX_WIDTH, d), lambda i: (0, 0), pipeline_mode=pl.Buffered(1)),
                  pl.BlockSpec((1, d), lambda i: (0, 0)),
                  pl.BlockSpec((tm, d), lambda i: (i, 0))],
        out_specs=pl.BlockSpec((tm, d), lambda i: (i, 0)),
        compiler_params=pltpu.CompilerParams(
            dimension_semantics=("parallel",), vmem_limit_bytes=VMEM_LIMIT),
        name="out_proj",
    )(mg, mf, w_g, w_f, post_w, x2d)


def _tile(total, want):
    t = min(total, want)
    while total % t:
        t //= 2
    return t


def _layer(x, meta_pad, pre_w, w_in, conv_w, a_log, dt_bias, gdn_norm_w, fox_q_w, fox_k_w,
           fox_f_bias, w_out, post_w):
    b, seq, d = x.shape
    assert seq % (GROUP * PREP_UNROLL) == 0 and seq % FOX_TK == 0 and seq % ROW_BLOCK == 0
    gw, fw = GDN_WIDTH, FOX_WIDTH
    o_gb = 4 * gw
    o_f = o_gb + 2 * GDN_HEADS
    o_ff = o_f + 4 * fw
    w_t = w_in.T
    w_gate = jnp.concatenate(
        [w_t[o_gb:o_f], w_t[o_ff:],
         jnp.zeros((GATE_WIDTH - 2 * GDN_HEADS - FOX_HEADS, d), w_in.dtype)], axis=0)
    pre_w2 = pre_w[None]
    w_main, proj_m, gate_m = _w_prep(w_t, meta_pad, pre_w2, w_gate, MAIN_WIDTH, o_gb, o_f - o_gb)
    zpad = jnp.zeros((GATE_WIDTH - FORGET_LANE - FOX_HEADS,), F32)
    add_vec = jnp.concatenate([jnp.zeros((GDN_HEADS,), F32), dt_bias, fox_f_bias, zpad])[None]
    alog_vec = jnp.concatenate([jnp.zeros((GDN_HEADS,), F32), a_log,
                                jnp.zeros((FOX_HEADS,), F32), zpad])[None]

    x2d = x.reshape(b * seq, d)
    proj, gate = _in_proj(x2d, pre_w2, w_main, w_gate, _tile(b * seq, IN_PROJ_TM), IN_PROJ_TN)
    proj = proj.reshape(MAIN_WIDTH // HEAD_DIM, b, seq, HEAD_DIM)

    row4 = _gate_rows(gate.reshape(b, seq, GATE_WIDTH), add_vec, alog_vec)
    col_m, row_m = _gate_meta(gate_m, add_vec, alog_vec)
    row_m4 = row_m.reshape(1, GATE_ROWS, 1, META_ROWS)

    conv_wt = conv_w.T
    state0 = _gdn_state0(proj_m, conv_wt, col_m, row_m4)
    o_gdn = _gdn(proj, proj_m, conv_wt, row4, state0, gdn_norm_w[None])
    o_fox = _fox(proj, proj_m, col_m, row4, fox_q_w[None], fox_k_w[None])

    w_out_b = w_out.astype(BF16)
    out = _out_proj(o_gdn.reshape(GDN_HEADS, b * seq, HEAD_DIM),
                    o_fox.reshape(FOX_HEADS, b * seq, HEAD_DIM),
                    w_out_b[:gw], w_out_b[gw:], post_w[None], x2d, _tile(b * seq, OUT_PROJ_TM))
    return out.reshape(b, seq, d)


def kernel(x, meta_tokens, pre_norm_w, w_in, conv_w, a_log, dt_bias, gdn_norm_w, fox_q_norm_w,
           fox_k_norm_w, fox_f_bias, w_out, post_norm_w):
    assert pre_norm_w.shape[0] == 1, "single-layer stack"
    meta_pad = jnp.concatenate(
        [jnp.zeros((META_PAD, x.shape[-1]), x.dtype), meta_tokens.astype(x.dtype)], axis=0)
    return _layer(x, meta_pad, pre_norm_w[0], w_in[0], conv_w[0], a_log[0], dt_bias[0],
                  gdn_norm_w[0], fox_q_norm_w[0], fox_k_norm_w[0], fox_f_bias[0], w_out[0],
                  post_norm_w[0])
```

```python
import functools
import math

import jax
import jax.numpy as jnp
from jax import lax
from jax.experimental import pallas as pl
from jax.experimental.pallas import tpu as pltpu

N_META = 16
HEAD_DIM = 128
GDN_HEADS = 8
FOX_HEADS = 8
GDN_WIDTH = GDN_HEADS * HEAD_DIM
FOX_WIDTH = FOX_HEADS * HEAD_DIM
CONV_WIDTH = 4
CHUNK = 64
EPS = 1e-6

LANES = 128
SUBLANES = 8
MXU_DIM = 256
MAIN_WIDTH = 4 * GDN_WIDTH + 4 * FOX_WIDTH
GATE_WIDTH = LANES
BETA_LANE, DECAY_LANE, FORGET_LANE = 0, GDN_HEADS, 2 * GDN_HEADS
GATE_ROWS = 32
META_ROWS = CHUNK
META_PAD = META_ROWS - N_META
GROUP = MXU_DIM
ROW_BLOCK = 256
HIST_ROWS = 2 * SUBLANES
IN_PROJ_TM, IN_PROJ_TN = 1024, 2048
W_PREP_ROWS = 512
OUT_PROJ_TM = 1024
OUT_PROJ_CHUNK = 512
HEAD_PAIR = 2
REC_PAIRS = 2
REC_HEADS = REC_PAIRS * HEAD_PAIR
PREP_UNROLL = 2
REC_PREP_UNROLL = 1
DEC_ROWS = SUBLANES
BLOCK_UNROLL = 4
FOX_GROUP = 4
FOX_TK = MXU_DIM
FOX_PIECE = 2 * MXU_DIM
VMEM_LIMIT = 56 * 1024 * 1024

F32 = jnp.float32
BF16 = jnp.bfloat16
NEG_BIG = -1e30
LOG2E = math.log2(math.e)


def _dot(a, b):
    return jnp.dot(a, b, preferred_element_type=F32)


def _dot_nt(a, b):
    return lax.dot_general(a, b, (((1,), (1,)), ((), ())), preferred_element_type=F32)


def _split3(x):
    hi = x.astype(BF16)
    r1 = x - hi.astype(F32)
    mid = r1.astype(BF16)
    lo = (r1 - mid.astype(F32)).astype(BF16)
    return hi, mid, lo


def _dot_exact_rhs01(parts, m):
    return _dot(parts[0], m) + _dot(parts[1], m) + _dot(parts[2], m)


def _dot_exact_lhs01(m, parts):
    return _dot(m, parts[0]) + _dot(m, parts[1]) + _dot(m, parts[2])


def _lane_bcast(col_tile, lane):
    sel = (lax.broadcasted_iota(jnp.int32, (LANES, LANES), 0) == lane).astype(BF16)
    return _dot_exact_rhs01(_split3(col_tile), sel)


def _chunk_of(idx):
    return jnp.right_shift(idx, CHUNK.bit_length() - 1)


def _rms(x, w):
    return x * lax.rsqrt(jnp.mean(x * x, axis=-1, keepdims=True) + EPS) * w


def _silu(x):
    half = 0.5 * x
    return half + half * jnp.tanh(half)


def _softplus(x):
    return jnp.maximum(x, 0.0) + jnp.log1p(jnp.exp(-jnp.abs(x)))


def _w_prep_kernel(w_hbm, meta_ref, nw_ref, wg_ref, o_ref, om_ref, ogm_ref, buf, sem, xm_ref,
                   *, rows, shift_from, shift):
    r = pl.program_id(0)

    def fetch(step, slot):
        start = step * rows
        src = pl.multiple_of(start + jnp.where(start >= shift_from, shift, 0), SUBLANES)
        return pltpu.make_async_copy(w_hbm.at[pl.ds(src, rows), :], buf.at[slot], sem.at[slot])

    @pl.when(r == 0)
    def _():
        fetch(0, 0).start()
        xm = _rms(meta_ref[...], nw_ref[...]).astype(BF16)
        xm_ref[...] = xm
        ogm_ref[...] = _dot_nt(xm, wg_ref[...].astype(BF16))

    @pl.when(r + 1 < pl.num_programs(0))
    def _():
        fetch(r + 1, (r + 1) % 2).start()

    fetch(r, r % 2).wait()
    w_bf = buf[r % 2].astype(BF16)
    o_ref[...] = w_bf
    res = _dot_nt(xm_ref[...], w_bf)
    for hd in range(om_ref.shape[0]):
        om_ref[hd] = res[:, hd * HEAD_DIM:(hd + 1) * HEAD_DIM].astype(om_ref.dtype)


def _w_prep(w_t, meta_pad, norm_w, w_gate, n_rows, shift_from, shift):
    d = w_t.shape[1]
    rows = W_PREP_ROWS
    return pl.pallas_call(
        functools.partial(_w_prep_kernel, rows=rows, shift_from=shift_from, shift=shift),
        out_shape=(jax.ShapeDtypeStruct((n_rows, d), BF16),
                   jax.ShapeDtypeStruct((n_rows // HEAD_DIM, META_ROWS, HEAD_DIM), BF16),
                   jax.ShapeDtypeStruct((META_ROWS, GATE_WIDTH), F32)),
        grid=(n_rows // rows,),
        in_specs=[pl.BlockSpec(memory_space=pl.ANY),
                  pl.BlockSpec((META_ROWS, d), lambda r: (0, 0)),
                  pl.BlockSpec((1, d), lambda r: (0, 0)),
                  pl.BlockSpec((GATE_WIDTH, d), lambda r: (0, 0))],
        out_specs=(pl.BlockSpec((rows, d), lambda r: (r, 0)),
                   pl.BlockSpec((rows // HEAD_DIM, META_ROWS, HEAD_DIM), lambda r: (r, 0, 0)),
                   pl.BlockSpec((META_ROWS, GATE_WIDTH), lambda r: (0, 0))),
        scratch_shapes=[pltpu.VMEM((2, rows, d), F32), pltpu.SemaphoreType.DMA((2,)),
                        pltpu.VMEM((META_ROWS, d), BF16)],
        compiler_params=pltpu.CompilerParams(
            dimension_semantics=("arbitrary",), vmem_limit_bytes=VMEM_LIMIT),
        name="w_prep",
    )(w_t, meta_pad, norm_w, w_gate)


def _in_proj_kernel(x_ref, nw_ref, w_ref, wg_ref, o_ref, og_ref, xn_ref):
    @pl.when(pl.program_id(1) == 0)
    def _():
        xn = _rms(x_ref[...], nw_ref[...]).astype(BF16)
        xn_ref[...] = xn
        og_ref[...] = _dot_nt(xn, wg_ref[...].astype(BF16))

    res = _dot_nt(xn_ref[...], w_ref[...])
    for hd in range(o_ref.shape[0]):
        o_ref[hd] = res[:, hd * HEAD_DIM:(hd + 1) * HEAD_DIM].astype(o_ref.dtype)


def _in_proj(x2d, norm_w, w_main, w_gate, tm, tn):
    m, d = x2d.shape
    n = w_main.shape[0]
    return pl.pallas_call(
        _in_proj_kernel,
        out_shape=(jax.ShapeDtypeStruct((n // HEAD_DIM, m, HEAD_DIM), BF16),
                   jax.ShapeDtypeStruct((m, GATE_WIDTH), F32)),
        grid=(m // tm, n // tn),
        in_specs=[pl.BlockSpec((tm, d), lambda i, j: (i, 0)),
                  pl.BlockSpec((1, d), lambda i, j: (0, 0)),
                  pl.BlockSpec((tn, d), lambda i, j: (j, 0)),
                  pl.BlockSpec((GATE_WIDTH, d), lambda i, j: (0, 0))],
        out_specs=(pl.BlockSpec((tn // HEAD_DIM, tm, HEAD_DIM), lambda i, j: (j, i, 0)),
                   pl.BlockSpec((tm, GATE_WIDTH), lambda i, j: (i, 0))),
        scratch_shapes=[pltpu.VMEM((tm, d), BF16)],
        compiler_params=pltpu.CompilerParams(
            dimension_semantics=("parallel", "arbitrary"), vmem_limit_bytes=VMEM_LIMIT),
        name="in_proj",
    )(x2d, norm_w, w_main, w_gate)


def _gate_meta_kernel(t_ref, add_ref, alog_ref, col_ref, row_ref):
    lane = lax.broadcasted_iota(jnp.int32, (META_ROWS, LANES), 1)
    row = lax.broadcasted_iota(jnp.int32, (META_ROWS, LANES), 0)
    ri = lax.broadcasted_iota(jnp.int32, (META_ROWS, META_ROWS), 0)
    ci = lax.broadcasted_iota(jnp.int32, (META_ROWS, META_ROWS), 1)
    tri = (ci <= ri).astype(BF16)
    is_beta = lane < DECAY_LANE
    is_decay = (lane >= DECAY_LANE) & (lane < FORGET_LANE)
    is_forget = (lane >= FORGET_LANE) & (lane < FORGET_LANE + FOX_HEADS)
    real = row >= META_PAD

    t = t_ref[...]
    ta = t + add_ref[...]
    beta = jnp.where(real, 1.0 / (1.0 + jnp.exp(-t)), 0.0)
    g = -jnp.exp(alog_ref[...]) * _softplus(ta)
    logf = -_softplus(-ta)
    val = jnp.where(real, jnp.where(is_decay, g, jnp.where(is_forget, logf, 0.0)), 0.0)
    cum = _dot_exact_lhs01(tri, _split3(val))
    to_come = cum - cum[META_ROWS - 1:META_ROWS, :]
    res = jnp.where(is_beta, beta, jnp.where(is_decay, cum, to_come))
    col_ref[...] = res
    row_ref[...] = res.T[:GATE_ROWS, :]


def _gate_rows_kernel(t_ref, add_ref, alog_ref, row_ref, *, rows):
    blk = ROW_BLOCK
    row = lax.broadcasted_iota(jnp.int32, (GATE_ROWS, blk), 0)
    si = lax.broadcasted_iota(jnp.int32, (blk, blk), 0)
    ti = lax.broadcasted_iota(jnp.int32, (blk, blk), 1)
    tri = (si <= ti).astype(BF16)
    tri_chunk = ((si <= ti) & (_chunk_of(si) == _chunk_of(ti))).astype(BF16)
    is_beta = row < DECAY_LANE
    is_decay = (row >= DECAY_LANE) & (row < FORGET_LANE)
    is_forget = (row >= FORGET_LANE) & (row < FORGET_LANE + FOX_HEADS)
    add = jnp.concatenate([add_ref[...]] * (blk // LANES), axis=1)
    neg_rate = -jnp.exp(jnp.concatenate([alog_ref[...]] * (blk // LANES), axis=1))

    carry = jnp.zeros((GATE_ROWS, 1), F32)
    for r in range(rows // blk):
        t = t_ref[r * blk:(r + 1) * blk, :].T[:GATE_ROWS, :]
        ta = t + add
        beta = 1.0 / (1.0 + jnp.exp(-t))
        val = jnp.where(is_decay, neg_rate * _softplus(ta), jnp.where(is_forget, -_softplus(-ta), 0.0))
        parts = _split3(val)
        cum_chunk = _dot_exact_rhs01(parts, tri_chunk)
        cum_all = _dot_exact_rhs01(parts, tri) + carry
        carry = cum_all[:, blk - 1:blk]
        row_ref[:, 0, r * blk:(r + 1) * blk] = jnp.where(is_beta, beta,
                                                         jnp.where(is_decay, cum_chunk, cum_all))


def _gate_rows(gate3d, add_vec, alog_vec):
    b, rows, _ = gate3d.shape
    per_row = lambda v: jnp.broadcast_to(v[0, :GATE_ROWS, None], (GATE_ROWS, LANES))
    return pl.pallas_call(
        functools.partial(_gate_rows_kernel, rows=rows),
        out_shape=jax.ShapeDtypeStruct((b, GATE_ROWS, 1, rows), F32),
        grid=(b,),
        in_specs=[pl.BlockSpec((None, rows, LANES), lambda i: (i, 0, 0)),
                  pl.BlockSpec((GATE_ROWS, LANES), lambda i: (0, 0)),
                  pl.BlockSpec((GATE_ROWS, LANES), lambda i: (0, 0))],
        out_specs=pl.BlockSpec((None, GATE_ROWS, 1, rows), lambda i: (i, 0, 0, 0)),
        compiler_params=pltpu.CompilerParams(
            dimension_semantics=("parallel",), vmem_limit_bytes=VMEM_LIMIT),
        name="gate_rows",
    )(gate3d, per_row(add_vec), per_row(alog_vec))


def _gate_meta(gate_m, add_vec, alog_vec):
    return pl.pallas_call(
        _gate_meta_kernel,
        out_shape=(jax.ShapeDtypeStruct((META_ROWS, LANES), F32),
                   jax.ShapeDtypeStruct((GATE_ROWS, META_ROWS), F32)),
        grid=(1,),
        in_specs=[pl.BlockSpec((META_ROWS, LANES), lambda i: (0, 0)),
                  pl.BlockSpec((1, LANES), lambda i: (0, 0)),
                  pl.BlockSpec((1, LANES), lambda i: (0, 0))],
        out_specs=(pl.BlockSpec((META_ROWS, LANES), lambda i: (0, 0)),
                   pl.BlockSpec((GATE_ROWS, META_ROWS), lambda i: (0, 0))),
        compiler_params=pltpu.CompilerParams(
            dimension_semantics=("arbitrary",), vmem_limit_bytes=VMEM_LIMIT),
        name="gate_meta",
    )(gate_m, add_vec, alog_vec)


def _conv_silu(load_rows, w):
    y = load_rows(0) * w[0:1, :]
    for j in range(1, CONV_WIDTH):
        y = y + load_rows(j) * w[j:j + 1, :]
    return _silu(y)


def _l2norm(x):
    return x * lax.rsqrt(jnp.sum(x * x, axis=-1, keepdims=True) + EPS)


def _gdn_pointwise(q, k, v, beta_b, g_b):
    r = q.shape[0]
    q = q * (lax.rsqrt(jnp.sum(q * q, axis=-1, keepdims=True) + EPS) * (HEAD_DIM ** -0.5))
    k = _l2norm(k)
    g3 = g_b.reshape(r // CHUNK, CHUNK, LANES)
    g_last = jnp.broadcast_to(g3[:, CHUNK - 1:CHUNK, :], g3.shape).reshape(r, LANES)
    e_g = jnp.exp(g_b)
    q_dec = q * e_g
    k_dec = k * jnp.exp(g_last - g_b)
    y = jnp.concatenate([v * beta_b, k * (beta_b * e_g)], axis=1)
    return q, k, q_dec, k_dec, y, jnp.exp(g_last)


def _gdn_groups(probs, fillers=()):
    r = probs[0][0].shape[0]
    ri = lax.broadcasted_iota(jnp.int32, (r, r), 0)
    ci = lax.broadcasted_iota(jnp.int32, (r, r), 1)
    same = _chunk_of(ri) == _chunk_of(ci)
    causal = same & (ci <= ri)
    strict = same & (ci < ri)

    def widen(t):
        return jnp.concatenate([t] * (r // LANES), axis=1) if r >= LANES else t[:, :r]

    fillers = list(fillers)
    n_ticks = (2 * CHUNK.bit_length() - 3) * len(probs)
    ticks_done = [0, 0]

    def tick():
        ticks_done[0] += 1
        due = len(fillers) * ticks_done[0] // n_ticks
        for thunk in fillers[ticks_done[1]:due]:
            thunk()
        ticks_done[1] = due

    def ticking(values):
        out = []
        for v in values:
            out.append(v)
            tick()
        return out

    kks = [_dot_nt(k, k) for _, k, _, _, _, _ in probs]
    qks = ticking(_dot_nt(q, k) for q, k, _, _, _, _ in probs)
    dmats = [jnp.where(causal, jnp.exp(jnp.where(causal, widen(g_b) - g_row, 0.0)), 0.0)
             for _, _, _, _, g_b, g_row in probs]
    xs = [jnp.where(strict, widen(p[3]) * kk * d, 0.0).astype(BF16)
          for p, kk, d in zip(probs, kks, dmats)]
    a_qks = [qk * d for qk, d in zip(qks, dmats)]
    eye = (ri == ci).astype(BF16)
    zs = [_dot(eye - x, p[2]) for p, x in zip(probs, xs)]
    span = 2
    while span < CHUNK:
        xs = ticking(_dot(x, x).astype(BF16) for x in xs)
        zs = ticking(z + _dot(x, z.astype(BF16)) for x, z in zip(xs, zs))
        span *= 2
    assert ticks_done == [n_ticks, len(fillers)]
    return list(zip(zs, a_qks))


def _gdn_state0_kernel(km_ref, vm_ref, wk_ref, wv_ref, colm_ref, growm_ref, s_ref, pad_ref):
    heads = range(GDN_HEADS)
    probs, kd_t = [], []
    for h in heads:
        ls = slice(h * HEAD_DIM, (h + 1) * HEAD_DIM)
        beta_m = _lane_bcast(colm_ref[...], BETA_LANE + h)
        g_m = _lane_bcast(colm_ref[...], DECAY_LANE + h)
        conv = []
        for t, (src, w_ref) in enumerate(((km_ref, wk_ref), (vm_ref, wv_ref))):
            win = pad_ref.at[2 * h + t]
            win[0:SUBLANES, :] = jnp.zeros((SUBLANES, LANES), F32)
            win[SUBLANES:SUBLANES + META_ROWS, :] = src[h].astype(F32)
            conv.append(_conv_silu(
                lambda j, win=win: win[SUBLANES - (CONV_WIDTH - 1) + j:
                                       SUBLANES - (CONV_WIDTH - 1) + j + META_ROWS, :],
                w_ref[:, ls]))
        _, k_m, _, kd_m, y_m, _ = _gdn_pointwise(conv[0], conv[0], conv[1], beta_m, g_m)
        k_bf = k_m.astype(BF16)
        probs.append((k_bf, k_bf, y_m.astype(BF16), beta_m, g_m, growm_ref[h]))
        kd_t.append(kd_m.T.astype(BF16))
    for h, (uw_m, _) in zip(heads, _gdn_groups(probs)):
        s_ref[h] = _dot(kd_t[h], uw_m[:, :HEAD_DIM].astype(BF16))


def _gdn_state0(proj_m, conv_wt, col_m, row_m4):
    heads_block = lambda blk: pl.BlockSpec((GDN_HEADS, META_ROWS, HEAD_DIM),
                                           lambda j, blk=blk: (blk, 0, 0))
    taps_block = lambda blk: pl.BlockSpec((CONV_WIDTH, GDN_WIDTH), lambda j, blk=blk: (0, blk))
    return pl.pallas_call(
        _gdn_state0_kernel,
        out_shape=jax.ShapeDtypeStruct((GDN_HEADS, HEAD_DIM, HEAD_DIM), F32),
        grid=(1,),
        in_specs=[heads_block(1), heads_block(2), taps_block(1), taps_block(2),
                  pl.BlockSpec((META_ROWS, LANES), lambda j: (0, 0)),
                  pl.BlockSpec((None, GDN_HEADS, 1, META_ROWS),
                               lambda j: (0, DECAY_LANE // GDN_HEADS, 0, 0))],
        out_specs=pl.BlockSpec((GDN_HEADS, HEAD_DIM, HEAD_DIM), lambda j: (0, 0, 0)),
        scratch_shapes=[pltpu.VMEM((2 * GDN_HEADS, META_ROWS + SUBLANES, HEAD_DIM), F32)],
        compiler_params=pltpu.CompilerParams(
            dimension_semantics=("arbitrary",), vmem_limit_bytes=VMEM_LIMIT),
        name="gdn_state0",
    )(proj_m, proj_m, conv_wt, conv_wt, col_m, row_m4)


def _gdn_kernel(*refs, seq):
    n_qkv = 3 * HEAD_PAIR
    src_refs = [refs[t * HEAD_PAIR:(t + 1) * HEAD_PAIR] for t in range(3)]
    z_ref = refs[n_qkv]
    meta_refs = [refs[n_qkv + 1 + t * HEAD_PAIR:n_qkv + 1 + (t + 1) * HEAD_PAIR] for t in range(3)]
    tap_refs = [refs[2 * n_qkv + 1 + t * HEAD_PAIR:2 * n_qkv + 1 + (t + 1) * HEAD_PAIR]
                for t in range(3)]
    (brow_ref, grow_ref, s0_ref, nw_ref, o_ref, pad_ref, qs_ref, ks_ref, y_ref, bb_ref, gb_ref,
     qd_ref, kdt_ref, dec_ref, u_ref, w_ref, aqk_ref, st_ref,
     qd_stage, kdt_stage, dec_stage) = refs[3 * n_qkv + 1:]
    pair = pl.program_id(1)
    slot = pair % REC_PAIRS
    n_blocks = seq // ROW_BLOCK
    n_groups = seq // GROUP
    cpg = GROUP // CHUNK
    hist = CONV_WIDTH - 1

    def pointwise_thunks_of(i, bank, hh):
        r0 = pl.multiple_of(i * ROW_BLOCK, ROW_BLOCK)
        rows = pl.ds(r0, ROW_BLOCK)
        convs = []

        def conv(t):
            win = pad_ref.at[(bank * 3 + t) * HEAD_PAIR + hh]
            if isinstance(i, int) and i == 0:
                past = meta_refs[t][hh][META_ROWS - HIST_ROWS:META_ROWS, :]
            else:
                past = src_refs[t][hh][pl.ds(pl.multiple_of(r0 - HIST_ROWS, HIST_ROWS), HIST_ROWS), :]
            win[0:HIST_ROWS, :] = past.astype(F32)
            win[HIST_ROWS:HIST_ROWS + ROW_BLOCK, :] = src_refs[t][hh][rows, :].astype(F32)
            convs.append(_conv_silu(
                lambda j, win=win: win[HIST_ROWS - hist + j:HIST_ROWS - hist + j + ROW_BLOCK, :],
                tap_refs[t][hh][...]))

        def finish():
            beta_b = jnp.broadcast_to(brow_ref[hh, :, rows], (LANES, ROW_BLOCK)).T
            g_b = jnp.broadcast_to(grow_ref[hh, :, rows], (LANES, ROW_BLOCK)).T
            q, k, q_dec, k_dec, y, dec = _gdn_pointwise(convs[0], convs[1], convs[2], beta_b, g_b)
            qs_ref[hh, rows, :] = q.astype(BF16)
            ks_ref[hh, rows, :] = k.astype(BF16)
            y_ref[hh, rows, :] = y.astype(BF16)
            bb_ref[hh, rows, :] = beta_b
            gb_ref[hh, rows, :] = g_b
            qd_stage[hh, rows, :] = q_dec.astype(BF16)
            kdt_stage[hh, :, rows] = k_dec.T.astype(BF16)
            n_chunks = ROW_BLOCK // CHUNK
            dec_rows = [dec[c * CHUNK:c * CHUNK + 1, :] for c in range(n_chunks)]
            dec_rows.append(jnp.zeros((DEC_ROWS - n_chunks, LANES), F32))
            dec_stage[hh, pl.ds(pl.multiple_of(i * DEC_ROWS, DEC_ROWS), DEC_ROWS), :] = (
                jnp.concatenate(dec_rows, axis=0))

        return [functools.partial(conv, t) for t in range(3)] + [finish]

    def pointwise_thunks(i, bank):
        return [th for hh in range(HEAD_PAIR) for th in pointwise_thunks_of(i, bank, hh)]

    def first_pointwise(unroll):
        for i in range(unroll):
            for thunk in pointwise_thunks(i, i):
                thunk()

    def prep_groups(gi, fillers, unroll):
        groups = [gi * unroll + u for u in range(unroll) for _ in range(HEAD_PAIR)]
        keys = [(hh, pl.ds(pl.multiple_of((gi * unroll + u) * GROUP, GROUP), GROUP))
                for u in range(unroll) for hh in range(HEAD_PAIR)]
        probs = [(qs_ref[hh, rows, :], ks_ref[hh, rows, :], y_ref[hh, rows, :],
                  bb_ref[hh, rows, :], gb_ref[hh, rows, :], grow_ref[hh, :, rows])
                 for hh, rows in keys]
        for g, (hh, rows), (uw, a_qk) in zip(groups, keys, _gdn_groups(probs, fillers)):
            hs = slot * HEAD_PAIR + hh
            u_ref[hs, rows, :] = uw[:, :HEAD_DIM]
            w_ref[hs, rows, :] = uw[:, HEAD_DIM:].astype(BF16)
            aqk_ref[hs, rows, :] = a_qk.astype(BF16)
            drows = pl.ds(pl.multiple_of(g * DEC_ROWS, DEC_ROWS), DEC_ROWS)
            qd_ref[hs, rows, :] = qd_stage[hh, rows, :]
            kdt_ref[hs, :, rows] = kdt_stage[hh, :, rows]
            dec_ref[hs, drows, :] = dec_stage[hh, drows, :]

    def next_pointwise(gi, unroll):
        return [th for u in range(unroll) for th in pointwise_thunks((gi + 1) * unroll + u, u)]

    def rec_thunks(gi):
        r0 = pl.multiple_of(gi * GROUP, GROUP)
        rows = pl.ds(r0, GROUP)
        heads = range(REC_HEADS)
        outs = [[] for _ in heads]

        held = {}

        def read_state(c):
            crow = pl.ds(r0 + c * CHUNK, CHUNK)
            states = [st_ref[h] for h in heads]
            s_bf = [s.astype(BF16) for s in states]
            ws = [_dot(jnp.concatenate([w_ref[h, crow, :], qd_ref[h, crow, :]], axis=0), s_bf[h])
                  for h in heads]
            held[c] = (states, ws)

        def write_state(c):
            crow = pl.ds(r0 + c * CHUNK, CHUNK)
            states, ws = held.pop(c)
            v_new = [u_ref[h, crow, :] - ws[h][:CHUNK, :] for h in heads]
            zero = lambda n: jnp.zeros((n * CHUNK, HEAD_DIM), BF16)
            v_pad = [jnp.concatenate(([zero(c)] if c else []) + [v.astype(BF16)]
                                     + ([zero(cpg - 1 - c)] if c < cpg - 1 else []), axis=0)
                     for v in v_new]
            upd = [_dot(jnp.concatenate([aqk_ref[h, crow, :], kdt_ref[h, :, rows]], axis=0),
                        v_pad[h]) for h in heads]
            for h in heads:
                dec = dec_ref[h, pl.ds(gi * DEC_ROWS + c, 1), :]
                st_ref[h] = states[h] * dec + upd[h][CHUNK:, :]
                outs[h].append(ws[h][CHUNK:, :] + upd[h][:CHUNK, :])
            if c == cpg - 1:
                for h in heads:
                    o = _rms(jnp.concatenate(outs[h], axis=0), nw_ref[...])
                    o_ref[h, rows, :] = (o * _silu(z_ref[h, rows, :].astype(F32))).astype(BF16)

        return [functools.partial(half, c) for c in range(cpg) for half in (read_state, write_state)]

    def interleave(a, b):
        if not a or not b:
            return list(a) + list(b)
        out, j = [], 0
        for i, th in enumerate(a):
            out.append(th)
            while j < len(b) and (j + 1) * len(a) <= (i + 1) * len(b):
                out.append(b[j])
                j += 1
        return out + list(b[j:])

    @pl.when(slot != REC_PAIRS - 1)
    def _():
        unroll = PREP_UNROLL
        n_trips = n_groups // unroll
        first_pointwise(unroll)

        def prep_body(gi, _):
            prep_groups(gi, next_pointwise(gi, unroll), unroll)
            return 0
        lax.fori_loop(0, n_trips - 1, prep_body, 0)
        prep_groups(n_trips - 1, [], unroll)

    @pl.when(slot == REC_PAIRS - 1)
    def _():
        unroll = REC_PREP_UNROLL
        n_trips = n_groups // unroll
        st_ref[...] = s0_ref[...]
        first_pointwise(unroll)

        def rec_of_trip(t):
            return [th for u in range(unroll) for th in rec_thunks(t * unroll + u)]

        prep_groups(0, next_pointwise(0, unroll), unroll)

        def prep_rec_body(gi, _):
            prep_groups(gi, interleave(rec_of_trip(gi - 1), next_pointwise(gi, unroll)), unroll)
            return 0
        lax.fori_loop(1, n_trips - 1, prep_rec_body, 0)
        prep_groups(n_trips - 1, rec_of_trip(n_trips - 2), unroll)
        for thunk in rec_of_trip(n_trips - 1):
            thunk()


def _gdn(proj, proj_m, conv_wt, row4, state0, norm_w):
    _, b, seq, _ = proj.shape
    n_pairs = GDN_HEADS // HEAD_PAIR

    def head_specs(shape, imap):
        return [pl.BlockSpec(shape, functools.partial(imap, t * GDN_HEADS + hh))
                for t in range(3) for hh in range(HEAD_PAIR)]

    src_specs = head_specs((None, None, seq, HEAD_DIM),
                           lambda off, i, j: (off + HEAD_PAIR * j, i, 0, 0))
    meta_specs = head_specs((None, META_ROWS, HEAD_DIM), lambda off, i, j: (off + HEAD_PAIR * j, 0, 0))
    tap_specs = head_specs((CONV_WIDTH, HEAD_DIM), lambda off, i, j: (0, off + HEAD_PAIR * j))
    n_qkv = 3 * HEAD_PAIR
    z_block0 = 3 * GDN_HEADS // REC_HEADS

    per_pair = lambda width, dt: pltpu.VMEM((HEAD_PAIR, seq, width), dt)
    per_rec = lambda width, dt: pltpu.VMEM((REC_HEADS, seq, width), dt)
    return pl.pallas_call(
        functools.partial(_gdn_kernel, seq=seq),
        out_shape=jax.ShapeDtypeStruct((GDN_HEADS, b, seq, HEAD_DIM), BF16),
        grid=(b, n_pairs),
        in_specs=src_specs
        + [pl.BlockSpec((REC_HEADS, None, seq, HEAD_DIM),
                        lambda i, j: (z_block0 + j // REC_PAIRS, i, 0, 0))]
        + meta_specs + tap_specs
        + [pl.BlockSpec((None, HEAD_PAIR, 1, seq),
                        lambda i, j: (i, BETA_LANE // HEAD_PAIR + j, 0, 0)),
           pl.BlockSpec((None, HEAD_PAIR, 1, seq),
                        lambda i, j: (i, DECAY_LANE // HEAD_PAIR + j, 0, 0)),
           pl.BlockSpec((REC_HEADS, HEAD_DIM, HEAD_DIM), lambda i, j: (j // REC_PAIRS, 0, 0)),
           pl.BlockSpec((1, HEAD_DIM), lambda i, j: (0, 0))],
        out_specs=pl.BlockSpec((REC_HEADS, None, seq, HEAD_DIM),
                               lambda i, j: (j // REC_PAIRS, i, 0, 0)),
        scratch_shapes=[pltpu.VMEM((max(PREP_UNROLL, REC_PREP_UNROLL) * n_qkv, HIST_ROWS + ROW_BLOCK,
                                    HEAD_DIM), F32),
                        per_pair(HEAD_DIM, BF16), per_pair(HEAD_DIM, BF16),
                        per_pair(2 * HEAD_DIM, BF16),
                        per_pair(HEAD_DIM, F32), per_pair(HEAD_DIM, F32),
                        per_rec(HEAD_DIM, BF16),
                        pltpu.VMEM((REC_HEADS, HEAD_DIM, seq), BF16),
                        pltpu.VMEM((REC_HEADS, seq // GROUP * DEC_ROWS, HEAD_DIM), F32),
                        per_rec(HEAD_DIM, F32), per_rec(HEAD_DIM, BF16), per_rec(GROUP, BF16),
                        pltpu.VMEM((REC_HEADS, HEAD_DIM, HEAD_DIM), F32),
                        per_pair(HEAD_DIM, BF16),
                        pltpu.VMEM((HEAD_PAIR, HEAD_DIM, seq), BF16),
                        pltpu.VMEM((HEAD_PAIR, seq // GROUP * DEC_ROWS, HEAD_DIM), F32)],
        compiler_params=pltpu.CompilerParams(
            dimension_semantics=("parallel", "arbitrary"), vmem_limit_bytes=VMEM_LIMIT),
        name="gdn",
    )(*([proj] * n_qkv), proj, *([proj_m] * n_qkv), *([conv_wt] * n_qkv),
      row4, row4, state0, norm_w)


def _fox_kernel(q_ref, k_ref, v_ref, g_ref, km_ref, vm_ref, colm_ref, crow_ref,
                qw_ref, kw_ref, o_ref, qt_ref, ka_ref, kam_ref, vt_ref, vtm_ref,
                m_ref, l_ref, acc_ref, *, seq):
    n_blocks = seq // ROW_BLOCK
    aug_r = lax.broadcasted_iota(jnp.int32, (LANES, ROW_BLOCK), 0)

    def key_aug(ck, valid=None):
        hi, mid, lo = _split3(ck)
        lane = lax.broadcasted_iota(jnp.int32, ck.shape, 1)
        neg_hi = -hi.astype(F32)
        if valid is not None:
            neg_hi = jnp.where(valid, neg_hi, NEG_BIG)
        blk = jnp.where(lane < 3, 1.0,
                        jnp.where(lane == 3, neg_hi,
                                  jnp.where(lane == 4, -mid.astype(F32),
                                            jnp.where(lane == 5, -lo.astype(F32), 0.0))))
        return blk.astype(BF16)

    def pro_body(i, _):
        r0 = pl.multiple_of(i * ROW_BLOCK, ROW_BLOCK)
        rows = pl.ds(r0, ROW_BLOCK)
        for hd in range(FOX_GROUP):
            qn = _rms(q_ref[hd, rows, :].astype(F32), qw_ref[...]) * (HEAD_DIM ** -0.5 * LOG2E)
            qt_ref[hd, 0:HEAD_DIM, rows] = qn.T.astype(BF16)
            hi, mid, lo = _split3(crow_ref[hd, :, rows] * LOG2E)
            aug = jnp.where(aug_r == 0, hi.astype(F32),
                            jnp.where(aug_r == 1, mid.astype(F32),
                                      jnp.where(aug_r == 2, lo.astype(F32),
                                                jnp.where(aug_r < 6, 1.0, 0.0))))
            qt_ref[hd, HEAD_DIM:2 * HEAD_DIM, rows] = aug.astype(BF16)
            ka_ref[hd, rows, 0:HEAD_DIM] = _rms(k_ref[hd, rows, :].astype(F32),
                                                kw_ref[...]).astype(BF16)
            aug_k = jnp.where(aug_r < 3, 1.0,
                              jnp.where(aug_r == 3, -hi.astype(F32),
                                        jnp.where(aug_r == 4, -mid.astype(F32),
                                                  jnp.where(aug_r == 5, -lo.astype(F32), 0.0))))
            ka_ref[hd, rows, HEAD_DIM:2 * HEAD_DIM] = aug_k.T.astype(BF16)
            vt_ref[hd, :, rows] = v_ref[hd, rows, :].astype(F32).T.astype(BF16)
        return 0
    lax.fori_loop(0, n_blocks, pro_body, 0, unroll=BLOCK_UNROLL // FOX_GROUP)

    mrow = lax.broadcasted_iota(jnp.int32, (META_ROWS, LANES), 0)
    for hd in range(FOX_GROUP):
        head = pl.program_id(1) * FOX_GROUP + hd
        kam_ref[hd, :, 0:HEAD_DIM] = _rms(km_ref[hd].astype(F32), kw_ref[...]).astype(BF16)
        ck_m = _lane_bcast(colm_ref[...], FORGET_LANE + head) * LOG2E
        kam_ref[hd, :, HEAD_DIM:2 * HEAD_DIM] = key_aug(ck_m, mrow >= META_PAD)
        vtm_ref[hd] = vm_ref[hd].astype(F32).T.astype(BF16)

    kidx = lax.broadcasted_iota(jnp.int32, (FOX_TK, FOX_TK), 0)
    qidx = lax.broadcasted_iota(jnp.int32, (FOX_TK, FOX_TK), 1)
    diag_ok = kidx <= qidx

    items = []
    for kj in range(-1, seq // FOX_TK):
        lane0 = max(kj, 0) * FOX_TK
        for p0 in range(lane0, seq, FOX_PIECE):
            for hd in range(FOX_GROUP):
                items.append((hd, kj, slice(p0, min(p0 + FOX_PIECE, seq)), p0 == lane0))

    def scores(item):
        hd, kj, ln, leads = item
        k_aug = kam_ref[hd] if kj < 0 else ka_ref[hd, kj * FOX_TK:(kj + 1) * FOX_TK, :]
        s = _dot(k_aug, qt_ref[hd, :, ln])
        if kj >= 0 and leads:
            masked = jnp.where(diag_ok, s[:, :FOX_TK], NEG_BIG)
            s = jnp.concatenate([masked, s[:, FOX_TK:]], axis=1) if s.shape[1] > FOX_TK else masked
        return s

    def softmax_stats(item, s):
        hd, kj, ln, _ = item
        top = jnp.max(s, axis=0, keepdims=True)
        if kj < 0:
            m_new, alpha = top, None
        else:
            m_old = m_ref[hd, :, ln]
            m_new = jnp.maximum(m_old, top)
            alpha = jnp.exp2(m_old - m_new)
        p = jnp.exp2(s - m_new)
        psum = jnp.sum(p, axis=0, keepdims=True)
        m_ref[hd, :, ln] = m_new
        l_ref[hd, :, ln] = psum if kj < 0 else alpha * l_ref[hd, :, ln] + psum
        return p.astype(BF16), alpha

    def values(item, p, alpha):
        hd, kj, ln, _ = item
        v_t = vtm_ref[hd] if kj < 0 else vt_ref[hd, :, kj * FOX_TK:(kj + 1) * FOX_TK]
        pv = _dot(v_t, p)
        acc_ref[hd, :, ln] = pv if kj < 0 else alpha * acc_ref[hd, :, ln] + pv

    s_cur = scores(items[0])
    p_cur = None
    for t in range(len(items) + 1):
        s_next = scores(items[t + 1]) if t + 1 < len(items) else None
        p_next = softmax_stats(items[t], s_cur) if t < len(items) else None
        if p_cur is not None:
            values(items[t - 1], *p_cur)
        s_cur, p_cur = s_next, p_next

    def out_body(i, _):
        r0 = pl.multiple_of(i * ROW_BLOCK, ROW_BLOCK)
        rows = pl.ds(r0, ROW_BLOCK)
        for hd in range(FOX_GROUP):
            out_t = acc_ref[hd, :, rows] * (1.0 / l_ref[hd, :, rows])
            o_ref[hd, rows, :] = (out_t.T * _silu(g_ref[hd, rows, :].astype(F32))).astype(BF16)
        return 0
    lax.fori_loop(0, n_blocks, out_body, 0, unroll=BLOCK_UNROLL // FOX_GROUP)


def _fox(proj, proj_m, col_m, row4, q_w, k_w):
    _, b, seq, _ = proj.shape
    base = 4 * GDN_HEADS // FOX_GROUP
    hb = FOX_HEADS // FOX_GROUP

    def head_block(off):
        return pl.BlockSpec((FOX_GROUP, None, seq, HEAD_DIM),
                            lambda i, j, off=off: (off + j, i, 0, 0))

    def meta_block(off):
        return pl.BlockSpec((FOX_GROUP, META_ROWS, HEAD_DIM), lambda i, j, off=off: (off + j, 0, 0))

    per_head = lambda shape, dt: pltpu.VMEM((FOX_GROUP,) + shape, dt)
    return pl.pallas_call(
        functools.partial(_fox_kernel, seq=seq),
        out_shape=jax.ShapeDtypeStruct((FOX_HEADS, b, seq, HEAD_DIM), BF16),
        grid=(b, FOX_HEADS // FOX_GROUP),
        in_specs=[head_block(base), head_block(base + hb), head_block(base + 2 * hb),
                  head_block(base + 3 * hb),
                  meta_block(base + hb), meta_block(base + 2 * hb),
                  pl.BlockSpec((META_ROWS, LANES), lambda i, j: (0, 0)),
                  pl.BlockSpec((None, FOX_GROUP, 1, seq),
                               lambda i, j: (i, FORGET_LANE // FOX_GROUP + j, 0, 0)),
                  pl.BlockSpec((1, HEAD_DIM), lambda i, j: (0, 0)),
                  pl.BlockSpec((1, HEAD_DIM), lambda i, j: (0, 0))],
        out_specs=pl.BlockSpec((FOX_GROUP, None, seq, HEAD_DIM), lambda i, j: (j, i, 0, 0)),
        scratch_shapes=[per_head((2 * HEAD_DIM, seq), BF16),
                        per_head((seq, 2 * HEAD_DIM), BF16),
                        per_head((META_ROWS, 2 * HEAD_DIM), BF16),
                        per_head((HEAD_DIM, seq), BF16),
                        per_head((HEAD_DIM, META_ROWS), BF16),
                        per_head((1, seq), F32), per_head((1, seq), F32),
                        per_head((HEAD_DIM, seq), F32)],
        compiler_params=pltpu.CompilerParams(
            dimension_semantics=("parallel", "arbitrary"), vmem_limit_bytes=VMEM_LIMIT),
        name="fox",
    )(proj, proj, proj, proj, proj_m, proj_m, col_m, row4, q_w, k_w)


def _out_proj_kernel(mg_ref, mf_ref, wg_ref, wf_ref, pw_ref, x_ref, o_ref):
    for r in range(o_ref.shape[0] // OUT_PROJ_CHUNK):
        rows = slice(r * OUT_PROJ_CHUNK, (r + 1) * OUT_PROJ_CHUNK)

        def rows_of(m_ref):
            return jnp.concatenate([m_ref[h, rows, :] for h in range(m_ref.shape[0])], axis=1)
        out = _dot(rows_of(mg_ref), wg_ref[...]) + _dot(rows_of(mf_ref), wf_ref[...])
        o_ref[rows, :] = x_ref[rows, :] + _rms(out, pw_ref[...])


def _out_proj(mg, mf, w_g, w_f, post_w, x2d, tm):
    m, d = x2d.shape
    return pl.pallas_call(
        _out_proj_kernel,
        out_shape=jax.ShapeDtypeStruct((m, d), F32),
        grid=(m // tm,),
        in_specs=[pl.BlockSpec((GDN_HEADS, tm, HEAD_DIM), lambda i: (0, i, 0)),
                  pl.BlockSpec((FOX_HEADS, tm, HEAD_DIM), lambda i: (0, i, 0)),
                  pl.BlockSpec((GDN_WIDTH, d), lambda i: (0, 0), pipeline_mode=pl.Buffered(1)),
                  pl.BlockSpec((FOX_WIDTH, d), lambda i: (0, 0), pipeline_mode=pl.Buffered(1)),
                  pl.BlockSpec((1, d), lambda i: (0, 0)),
                  pl.BlockSpec((tm, d), lambda i: (i, 0))],
        out_specs=pl.BlockSpec((tm, d), lambda i: (i, 0)),
        compiler_params=pltpu.CompilerParams(
            dimension_semantics=("parallel",), vmem_limit_bytes=VMEM_LIMIT),
        name="out_proj",
    )(mg, mf, w_g, w_f, post_w, x2d)


def _tile(total, want):
    t = min(total, want)
    while total % t:
        t //= 2
    return t


def _layer(x, meta_pad, pre_w, w_in, conv_w, a_log, dt_bias, gdn_norm_w, fox_q_w, fox_k_w,
           fox_f_bias, w_out, post_w):
    b, seq, d = x.shape
    assert seq % (GROUP * PREP_UNROLL) == 0 and seq % FOX_TK == 0 and seq % ROW_BLOCK == 0
    gw, fw = GDN_WIDTH, FOX_WIDTH
    o_gb = 4 * gw
    o_f = o_gb + 2 * GDN_HEADS
    o_ff = o_f + 4 * fw
    w_t = w_in.T
    w_gate = jnp.concatenate(
        [w_t[o_gb:o_f], w_t[o_ff:],
         jnp.zeros((GATE_WIDTH - 2 * GDN_HEADS - FOX_HEADS, d), w_in.dtype)], axis=0)
    pre_w2 = pre_w[None]
    w_main, proj_m, gate_m = _w_prep(w_t, meta_pad, pre_w2, w_gate, MAIN_WIDTH, o_gb, o_f - o_gb)
    zpad = jnp.zeros((GATE_WIDTH - FORGET_LANE - FOX_HEADS,), F32)
    add_vec = jnp.concatenate([jnp.zeros((GDN_HEADS,), F32), dt_bias, fox_f_bias, zpad])[None]
    alog_vec = jnp.concatenate([jnp.zeros((GDN_HEADS,), F32), a_log,
                                jnp.zeros((FOX_HEADS,), F32), zpad])[None]

    x2d = x.reshape(b * seq, d)
    proj, gate = _in_proj(x2d, pre_w2, w_main, w_gate, _tile(b * seq, IN_PROJ_TM), IN_PROJ_TN)
    proj = proj.reshape(MAIN_WIDTH // HEAD_DIM, b, seq, HEAD_DIM)

    row4 = _gate_rows(gate.reshape(b, seq, GATE_WIDTH), add_vec, alog_vec)
    col_m, row_m = _gate_meta(gate_m, add_vec, alog_vec)
    row_m4 = row_m.reshape(1, GATE_ROWS, 1, META_ROWS)

    conv_wt = conv_w.T
    state0 = _gdn_state0(proj_m, conv_wt, col_m, row_m4)
    o_gdn = _gdn(proj, proj_m, conv_wt, row4, state0, gdn_norm_w[None])
    o_fox = _fox(proj, proj_m, col_m, row4, fox_q_w[None], fox_k_w[None])

    w_out_b = w_out.astype(BF16)
    out = _out_proj(o_gdn.reshape(GDN_HEADS, b * seq, HEAD_DIM),
                    o_fox.reshape(FOX_HEADS, b * seq, HEAD_DIM),
                    w_out_b[:gw], w_out_b[gw:], post_w[None], x2d, _tile(b * seq, OUT_PROJ_TM))
    return out.reshape(b, seq, d)


def kernel(x, meta_tokens, pre_norm_w, w_in, conv_w, a_log, dt_bias, gdn_norm_w, fox_q_norm_w,
           fox_k_norm_w, fox_f_bias, w_out, post_norm_w):
    assert pre_norm_w.shape[0] == 1, "single-layer stack"
    meta_pad = jnp.concatenate(
        [jnp.zeros((META_PAD, x.shape[-1]), x.dtype), meta_tokens.astype(x.dtype)], axis=0)
    return _layer(x, meta_pad, pre_norm_w[0], w_in[0], conv_w[0], a_log[0], dt_bias[0],
                  gdn_norm_w[0], fox_q_norm_w[0], fox_k_norm_w[0], fox_f_bias[0], w_out[0],
                  post_norm_w[0])
```

```python
import functools
import math

import jax
import jax.numpy as jnp
from jax import lax
from jax.experimental import pallas as pl
from jax.experimental.pallas import tpu as pltpu

N_META = 16
HEAD_DIM = 128
GDN_HEADS = 8
FOX_HEADS = 8
GDN_WIDTH = GDN_HEADS * HEAD_DIM
FOX_WIDTH = FOX_HEADS * HEAD_DIM
CONV_WIDTH = 4
CHUNK = 64
EPS = 1e-6

LANES = 128
SUBLANES = 8
MXU_DIM = 256
MAIN_WIDTH = 4 * GDN_WIDTH + 4 * FOX_WIDTH
GATE_WIDTH = LANES
BETA_LANE, DECAY_LANE, FORGET_LANE = 0, GDN_HEADS, 2 * GDN_HEADS
GATE_ROWS = 32
META_ROWS = CHUNK
META_PAD = META_ROWS - N_META
GROUP = MXU_DIM
ROW_BLOCK = 256
HIST_ROWS = 2 * SUBLANES
IN_PROJ_TM, IN_PROJ_TN = 1024, 2048
W_PREP_ROWS = 512
OUT_PROJ_TM = 1024
OUT_PROJ_CHUNK = 512
HEAD_PAIR = 2
REC_PAIRS = 2
REC_HEADS = REC_PAIRS * HEAD_PAIR
PREP_UNROLL = 2
REC_PREP_UNROLL = 1
DEC_ROWS = SUBLANES
BLOCK_UNROLL = 8
FOX_GROUP = 4
FOX_TK = MXU_DIM
FOX_PIECE = 2 * MXU_DIM
VMEM_LIMIT = 56 * 1024 * 1024

F32 = jnp.float32
BF16 = jnp.bfloat16
NEG_BIG = -1e30
LOG2E = math.log2(math.e)


def _dot(a, b):
    return jnp.dot(a, b, preferred_element_type=F32)


def _dot_nt(a, b):
    return lax.dot_general(a, b, (((1,), (1,)), ((), ())), preferred_element_type=F32)


def _split3(x):
    hi = x.astype(BF16)
    r1 = x - hi.astype(F32)
    mid = r1.astype(BF16)
    lo = (r1 - mid.astype(F32)).astype(BF16)
    return hi, mid, lo


def _dot_exact_rhs01(parts, m):
    return _dot(parts[0], m) + _dot(parts[1], m) + _dot(parts[2], m)


def _dot_exact_lhs01(m, parts):
    return _dot(m, parts[0]) + _dot(m, parts[1]) + _dot(m, parts[2])


def _lane_bcast(col_tile, lane):
    sel = (lax.broadcasted_iota(jnp.int32, (LANES, LANES), 0) == lane).astype(BF16)
    return _dot_exact_rhs01(_split3(col_tile), sel)


def _chunk_of(idx):
    return jnp.right_shift(idx, CHUNK.bit_length() - 1)


def _rms(x, w):
    return x * lax.rsqrt(jnp.mean(x * x, axis=-1, keepdims=True) + EPS) * w


def _silu(x):
    half = 0.5 * x
    return half + half * jnp.tanh(half)


def _softplus(x):
    return jnp.maximum(x, 0.0) + jnp.log1p(jnp.exp(-jnp.abs(x)))


def _w_prep_kernel(w_hbm, meta_ref, nw_ref, wg_ref, o_ref, om_ref, ogm_ref, buf, sem, xm_ref,
                   *, rows, shift_from, shift):
    r = pl.program_id(0)

    def fetch(step, slot):
        start = step * rows
        src = pl.multiple_of(start + jnp.where(start >= shift_from, shift, 0), SUBLANES)
        return pltpu.make_async_copy(w_hbm.at[pl.ds(src, rows), :], buf.at[slot], sem.at[slot])

    @pl.when(r == 0)
    def _():
        fetch(0, 0).start()
        xm = _rms(meta_ref[...], nw_ref[...]).astype(BF16)
        xm_ref[...] = xm
        ogm_ref[...] = _dot_nt(xm, wg_ref[...].astype(BF16))

    @pl.when(r + 1 < pl.num_programs(0))
    def _():
        fetch(r + 1, (r + 1) % 2).start()

    fetch(r, r % 2).wait()
    w_bf = buf[r % 2].astype(BF16)
    o_ref[...] = w_bf
    res = _dot_nt(xm_ref[...], w_bf)
    for hd in range(om_ref.shape[0]):
        om_ref[hd] = res[:, hd * HEAD_DIM:(hd + 1) * HEAD_DIM].astype(om_ref.dtype)


def _w_prep(w_t, meta_pad, norm_w, w_gate, n_rows, shift_from, shift):
    d = w_t.shape[1]
    rows = W_PREP_ROWS
    return pl.pallas_call(
        functools.partial(_w_prep_kernel, rows=rows, shift_from=shift_from, shift=shift),
        out_shape=(jax.ShapeDtypeStruct((n_rows, d), BF16),
                   jax.ShapeDtypeStruct((n_rows // HEAD_DIM, META_ROWS, HEAD_DIM), BF16),
                   jax.ShapeDtypeStruct((META_ROWS, GATE_WIDTH), F32)),
        grid=(n_rows // rows,),
        in_specs=[pl.BlockSpec(memory_space=pl.ANY),
                  pl.BlockSpec((META_ROWS, d), lambda r: (0, 0)),
                  pl.BlockSpec((1, d), lambda r: (0, 0)),
                  pl.BlockSpec((GATE_WIDTH, d), lambda r: (0, 0))],
        out_specs=(pl.BlockSpec((rows, d), lambda r: (r, 0)),
                   pl.BlockSpec((rows // HEAD_DIM, META_ROWS, HEAD_DIM), lambda r: (r, 0, 0)),
                   pl.BlockSpec((META_ROWS, GATE_WIDTH), lambda r: (0, 0))),
        scratch_shapes=[pltpu.VMEM((2, rows, d), F32), pltpu.SemaphoreType.DMA((2,)),
                        pltpu.VMEM((META_ROWS, d), BF16)],
        compiler_params=pltpu.CompilerParams(
            dimension_semantics=("arbitrary",), vmem_limit_bytes=VMEM_LIMIT),
        name="w_prep",
    )(w_t, meta_pad, norm_w, w_gate)


def _in_proj_kernel(x_ref, nw_ref, w_ref, wg_ref, o_ref, og_ref, xn_ref):
    @pl.when(pl.program_id(1) == 0)
    def _():
        xn = _rms(x_ref[...], nw_ref[...]).astype(BF16)
        xn_ref[...] = xn
        og_ref[...] = _dot_nt(xn, wg_ref[...].astype(BF16))

    res = _dot_nt(xn_ref[...], w_ref[...])
    for hd in range(o_ref.shape[0]):
        o_ref[hd] = res[:, hd * HEAD_DIM:(hd + 1) * HEAD_DIM].astype(o_ref.dtype)


def _in_proj(x2d, norm_w, w_main, w_gate, tm, tn):
    m, d = x2d.shape
    n = w_main.shape[0]
    return pl.pallas_call(
        _in_proj_kernel,
        out_shape=(jax.ShapeDtypeStruct((n // HEAD_DIM, m, HEAD_DIM), BF16),
                   jax.ShapeDtypeStruct((m, GATE_WIDTH), F32)),
        grid=(m // tm, n // tn),
        in_specs=[pl.BlockSpec((tm, d), lambda i, j: (i, 0)),
                  pl.BlockSpec((1, d), lambda i, j: (0, 0)),
                  pl.BlockSpec((tn, d), lambda i, j: (j, 0)),
                  pl.BlockSpec((GATE_WIDTH, d), lambda i, j: (0, 0))],
        out_specs=(pl.BlockSpec((tn // HEAD_DIM, tm, HEAD_DIM), lambda i, j: (j, i, 0)),
                   pl.BlockSpec((tm, GATE_WIDTH), lambda i, j: (i, 0))),
        scratch_shapes=[pltpu.VMEM((tm, d), BF16)],
        compiler_params=pltpu.CompilerParams(
            dimension_semantics=("parallel", "arbitrary"), vmem_limit_bytes=VMEM_LIMIT),
        name="in_proj",
    )(x2d, norm_w, w_main, w_gate)


def _gate_meta_kernel(t_ref, add_ref, alog_ref, col_ref, row_ref):
    lane = lax.broadcasted_iota(jnp.int32, (META_ROWS, LANES), 1)
    row = lax.broadcasted_iota(jnp.int32, (META_ROWS, LANES), 0)
    ri = lax.broadcasted_iota(jnp.int32, (META_ROWS, META_ROWS), 0)
    ci = lax.broadcasted_iota(jnp.int32, (META_ROWS, META_ROWS), 1)
    tri = (ci <= ri).astype(BF16)
    is_beta = lane < DECAY_LANE
    is_decay = (lane >= DECAY_LANE) & (lane < FORGET_LANE)
    is_forget = (lane >= FORGET_LANE) & (lane < FORGET_LANE + FOX_HEADS)
    real = row >= META_PAD

    t = t_ref[...]
    ta = t + add_ref[...]
    beta = jnp.where(real, 1.0 / (1.0 + jnp.exp(-t)), 0.0)
    g = -jnp.exp(alog_ref[...]) * _softplus(ta)
    logf = -_softplus(-ta)
    val = jnp.where(real, jnp.where(is_decay, g, jnp.where(is_forget, logf, 0.0)), 0.0)
    cum = _dot_exact_lhs01(tri, _split3(val))
    to_come = cum - cum[META_ROWS - 1:META_ROWS, :]
    res = jnp.where(is_beta, beta, jnp.where(is_decay, cum, to_come))
    col_ref[...] = res
    row_ref[...] = res.T[:GATE_ROWS, :]


def _gate_rows_kernel(t_ref, add_ref, alog_ref, row_ref, *, rows):
    blk = ROW_BLOCK
    row = lax.broadcasted_iota(jnp.int32, (GATE_ROWS, blk), 0)
    si = lax.broadcasted_iota(jnp.int32, (blk, blk), 0)
    ti = lax.broadcasted_iota(jnp.int32, (blk, blk), 1)
    tri = (si <= ti).astype(BF16)
    tri_chunk = ((si <= ti) & (_chunk_of(si) == _chunk_of(ti))).astype(BF16)
    is_beta = row < DECAY_LANE
    is_decay = (row >= DECAY_LANE) & (row < FORGET_LANE)
    is_forget = (row >= FORGET_LANE) & (row < FORGET_LANE + FOX_HEADS)
    add = jnp.concatenate([add_ref[...]] * (blk // LANES), axis=1)
    neg_rate = -jnp.exp(jnp.concatenate([alog_ref[...]] * (blk // LANES), axis=1))

    carry = jnp.zeros((GATE_ROWS, 1), F32)
    for r in range(rows // blk):
        t = t_ref[r * blk:(r + 1) * blk, :].T[:GATE_ROWS, :]
        ta = t + add
        beta = 1.0 / (1.0 + jnp.exp(-t))
        val = jnp.where(is_decay, neg_rate * _softplus(ta), jnp.where(is_forget, -_softplus(-ta), 0.0))
        parts = _split3(val)
        cum_chunk = _dot_exact_rhs01(parts, tri_chunk)
        cum_all = _dot_exact_rhs01(parts, tri) + carry
        carry = cum_all[:, blk - 1:blk]
        row_ref[:, 0, r * blk:(r + 1) * blk] = jnp.where(is_beta, beta,
                                                         jnp.where(is_decay, cum_chunk, cum_all))


def _gate_rows(gate3d, add_vec, alog_vec):
    b, rows, _ = gate3d.shape
    per_row = lambda v: jnp.broadcast_to(v[0, :GATE_ROWS, None], (GATE_ROWS, LANES))
    return pl.pallas_call(
        functools.partial(_gate_rows_kernel, rows=rows),
        out_shape=jax.ShapeDtypeStruct((b, GATE_ROWS, 1, rows), F32),
        grid=(b,),
        in_specs=[pl.BlockSpec((None, rows, LANES), lambda i: (i, 0, 0)),
                  pl.BlockSpec((GATE_ROWS, LANES), lambda i: (0, 0)),
                  pl.BlockSpec((GATE_ROWS, LANES), lambda i: (0, 0))],
        out_specs=pl.BlockSpec((None, GATE_ROWS, 1, rows), lambda i: (i, 0, 0, 0)),
        compiler_params=pltpu.CompilerParams(
            dimension_semantics=("parallel",), vmem_limit_bytes=VMEM_LIMIT),
        name="gate_rows",
    )(gate3d, per_row(add_vec), per_row(alog_vec))


def _gate_meta(gate_m, add_vec, alog_vec):
    return pl.pallas_call(
        _gate_meta_kernel,
        out_shape=(jax.ShapeDtypeStruct((META_ROWS, LANES), F32),
                   jax.ShapeDtypeStruct((GATE_ROWS, META_ROWS), F32)),
        grid=(1,),
        in_specs=[pl.BlockSpec((META_ROWS, LANES), lambda i: (0, 0)),
                  pl.BlockSpec((1, LANES), lambda i: (0, 0)),
                  pl.BlockSpec((1, LANES), lambda i: (0, 0))],
        out_specs=(pl.BlockSpec((META_ROWS, LANES), lambda i: (0, 0)),
                   pl.BlockSpec((GATE_ROWS, META_ROWS), lambda i: (0, 0))),
        compiler_params=pltpu.CompilerParams(
            dimension_semantics=("arbitrary",), vmem_limit_bytes=VMEM_LIMIT),
        name="gate_meta",
    )(gate_m, add_vec, alog_vec)


def _conv_silu(load_rows, w):
    y = load_rows(0) * w[0:1, :]
    for j in range(1, CONV_WIDTH):
        y = y + load_rows(j) * w[j:j + 1, :]
    return _silu(y)


def _l2norm(x):
    return x * lax.rsqrt(jnp.sum(x * x, axis=-1, keepdims=True) + EPS)


def _gdn_pointwise(q, k, v, beta_b, g_b):
    r = q.shape[0]
    q = q * (lax.rsqrt(jnp.sum(q * q, axis=-1, keepdims=True) + EPS) * (HEAD_DIM ** -0.5))
    k = _l2norm(k)
    g3 = g_b.reshape(r // CHUNK, CHUNK, LANES)
    g_last = jnp.broadcast_to(g3[:, CHUNK - 1:CHUNK, :], g3.shape).reshape(r, LANES)
    e_g = jnp.exp(g_b)
    q_dec = q * e_g
    k_dec = k * jnp.exp(g_last - g_b)
    y = jnp.concatenate([v * beta_b, k * (beta_b * e_g)], axis=1)
    return q, k, q_dec, k_dec, y, jnp.exp(g_last)


def _gdn_groups(probs, fillers=()):
    r = probs[0][0].shape[0]
    ri = lax.broadcasted_iota(jnp.int32, (r, r), 0)
    ci = lax.broadcasted_iota(jnp.int32, (r, r), 1)
    same = _chunk_of(ri) == _chunk_of(ci)
    causal = same & (ci <= ri)
    strict = same & (ci < ri)

    def widen(t):
        return jnp.concatenate([t] * (r // LANES), axis=1) if r >= LANES else t[:, :r]

    fillers = list(fillers)
    n_ticks = 2 * CHUNK.bit_length() - 3
    per_tick = -(-len(fillers) // n_ticks)

    def tick():
        for thunk in fillers[:per_tick]:
            thunk()
        del fillers[:per_tick]

    kks = [_dot_nt(k, k) for _, k, _, _, _, _ in probs]
    qks = [_dot_nt(q, k) for q, k, _, _, _, _ in probs]
    tick()
    dmats = [jnp.where(causal, jnp.exp(jnp.where(causal, widen(g_b) - g_row, 0.0)), 0.0)
             for _, _, _, _, g_b, g_row in probs]
    xs = [jnp.where(strict, widen(p[3]) * kk * d, 0.0).astype(BF16)
          for p, kk, d in zip(probs, kks, dmats)]
    a_qks = [qk * d for qk, d in zip(qks, dmats)]
    eye = (ri == ci).astype(BF16)
    zs = [_dot(eye - x, p[2]) for p, x in zip(probs, xs)]
    span = 2
    while span < CHUNK:
        xs = [_dot(x, x).astype(BF16) for x in xs]
        tick()
        zs = [z + _dot(x, z.astype(BF16)) for x, z in zip(xs, zs)]
        tick()
        span *= 2
    assert not fillers
    return list(zip(zs, a_qks))


def _gdn_state0_kernel(km_ref, vm_ref, wk_ref, wv_ref, colm_ref, growm_ref, s_ref, pad_ref):
    heads = range(GDN_HEADS)
    probs, kd_t = [], []
    for h in heads:
        ls = slice(h * HEAD_DIM, (h + 1) * HEAD_DIM)
        beta_m = _lane_bcast(colm_ref[...], BETA_LANE + h)
        g_m = _lane_bcast(colm_ref[...], DECAY_LANE + h)
        conv = []
        for t, (src, w_ref) in enumerate(((km_ref, wk_ref), (vm_ref, wv_ref))):
            win = pad_ref.at[2 * h + t]
            win[0:SUBLANES, :] = jnp.zeros((SUBLANES, LANES), F32)
            win[SUBLANES:SUBLANES + META_ROWS, :] = src[h].astype(F32)
            conv.append(_conv_silu(
                lambda j, win=win: win[SUBLANES - (CONV_WIDTH - 1) + j:
                                       SUBLANES - (CONV_WIDTH - 1) + j + META_ROWS, :],
                w_ref[:, ls]))
        _, k_m, _, kd_m, y_m, _ = _gdn_pointwise(conv[0], conv[0], conv[1], beta_m, g_m)
        k_bf = k_m.astype(BF16)
        probs.append((k_bf, k_bf, y_m.astype(BF16), beta_m, g_m, growm_ref[h]))
        kd_t.append(kd_m.T.astype(BF16))
    for h, (uw_m, _) in zip(heads, _gdn_groups(probs)):
        s_ref[h] = _dot(kd_t[h], uw_m[:, :HEAD_DIM].astype(BF16))


def _gdn_state0(proj_m, conv_wt, col_m, row_m4):
    heads_block = lambda blk: pl.BlockSpec((GDN_HEADS, META_ROWS, HEAD_DIM),
                                           lambda j, blk=blk: (blk, 0, 0))
    taps_block = lambda blk: pl.BlockSpec((CONV_WIDTH, GDN_WIDTH), lambda j, blk=blk: (0, blk))
    return pl.pallas_call(
        _gdn_state0_kernel,
        out_shape=jax.ShapeDtypeStruct((GDN_HEADS, HEAD_DIM, HEAD_DIM), F32),
        grid=(1,),
        in_specs=[heads_block(1), heads_block(2), taps_block(1), taps_block(2),
                  pl.BlockSpec((META_ROWS, LANES), lambda j: (0, 0)),
                  pl.BlockSpec((None, GDN_HEADS, 1, META_ROWS),
                               lambda j: (0, DECAY_LANE // GDN_HEADS, 0, 0))],
        out_specs=pl.BlockSpec((GDN_HEADS, HEAD_DIM, HEAD_DIM), lambda j: (0, 0, 0)),
        scratch_shapes=[pltpu.VMEM((2 * GDN_HEADS, META_ROWS + SUBLANES, HEAD_DIM), F32)],
        compiler_params=pltpu.CompilerParams(
            dimension_semantics=("arbitrary",), vmem_limit_bytes=VMEM_LIMIT),
        name="gdn_state0",
    )(proj_m, proj_m, conv_wt, conv_wt, col_m, row_m4)


def _gdn_kernel(*refs, seq):
    n_qkv = 3 * HEAD_PAIR
    src_refs = [refs[t * HEAD_PAIR:(t + 1) * HEAD_PAIR] for t in range(3)]
    z_ref = refs[n_qkv]
    meta_refs = [refs[n_qkv + 1 + t * HEAD_PAIR:n_qkv + 1 + (t + 1) * HEAD_PAIR] for t in range(3)]
    tap_refs = [refs[2 * n_qkv + 1 + t * HEAD_PAIR:2 * n_qkv + 1 + (t + 1) * HEAD_PAIR]
                for t in range(3)]
    (brow_ref, grow_ref, s0_ref, nw_ref, o_ref, pad_ref, qs_ref, ks_ref, y_ref, bb_ref, gb_ref,
     qd_ref, kdt_ref, dec_ref, u_ref, w_ref, aqk_ref, st_ref,
     qd_stage, kdt_stage, dec_stage) = refs[3 * n_qkv + 1:]
    pair = pl.program_id(1)
    slot = pair % REC_PAIRS
    n_blocks = seq // ROW_BLOCK
    n_groups = seq // GROUP
    cpg = GROUP // CHUNK
    hist = CONV_WIDTH - 1

    def pointwise_head(i, bank, hh):
        r0 = pl.multiple_of(i * ROW_BLOCK, ROW_BLOCK)
        rows = pl.ds(r0, ROW_BLOCK)
        convs = []
        for t in range(3):
            win = pad_ref.at[(bank * 3 + t) * HEAD_PAIR + hh]
            if isinstance(i, int) and i == 0:
                past = meta_refs[t][hh][META_ROWS - HIST_ROWS:META_ROWS, :]
            else:
                past = src_refs[t][hh][pl.ds(pl.multiple_of(r0 - HIST_ROWS, HIST_ROWS), HIST_ROWS), :]
            win[0:HIST_ROWS, :] = past.astype(F32)
            win[HIST_ROWS:HIST_ROWS + ROW_BLOCK, :] = src_refs[t][hh][rows, :].astype(F32)
            convs.append(_conv_silu(
                lambda j, win=win: win[HIST_ROWS - hist + j:HIST_ROWS - hist + j + ROW_BLOCK, :],
                tap_refs[t][hh][...]))
        beta_b = jnp.broadcast_to(brow_ref[hh, :, rows], (LANES, ROW_BLOCK)).T
        g_b = jnp.broadcast_to(grow_ref[hh, :, rows], (LANES, ROW_BLOCK)).T
        q, k, q_dec, k_dec, y, dec = _gdn_pointwise(convs[0], convs[1], convs[2], beta_b, g_b)
        qs_ref[hh, rows, :] = q.astype(BF16)
        ks_ref[hh, rows, :] = k.astype(BF16)
        y_ref[hh, rows, :] = y.astype(BF16)
        bb_ref[hh, rows, :] = beta_b
        gb_ref[hh, rows, :] = g_b
        qd_stage[hh, rows, :] = q_dec.astype(BF16)
        kdt_stage[hh, :, rows] = k_dec.T.astype(BF16)
        n_chunks = ROW_BLOCK // CHUNK
        dec_rows = [dec[c * CHUNK:c * CHUNK + 1, :] for c in range(n_chunks)]
        dec_rows.append(jnp.zeros((DEC_ROWS - n_chunks, LANES), F32))
        dec_stage[hh, pl.ds(pl.multiple_of(i * DEC_ROWS, DEC_ROWS), DEC_ROWS), :] = (
            jnp.concatenate(dec_rows, axis=0))

    def pointwise_thunks(i, bank):
        return [functools.partial(pointwise_head, i, bank, hh) for hh in range(HEAD_PAIR)]

    def first_pointwise(unroll):
        for i in range(unroll):
            for thunk in pointwise_thunks(i, i):
                thunk()

    def prep_groups(gi, fillers, unroll):
        groups = [gi * unroll + u for u in range(unroll) for _ in range(HEAD_PAIR)]
        keys = [(hh, pl.ds(pl.multiple_of((gi * unroll + u) * GROUP, GROUP), GROUP))
                for u in range(unroll) for hh in range(HEAD_PAIR)]
        probs = [(qs_ref[hh, rows, :], ks_ref[hh, rows, :], y_ref[hh, rows, :],
                  bb_ref[hh, rows, :], gb_ref[hh, rows, :], grow_ref[hh, :, rows])
                 for hh, rows in keys]
        for g, (hh, rows), (uw, a_qk) in zip(groups, keys, _gdn_groups(probs, fillers)):
            hs = slot * HEAD_PAIR + hh
            u_ref[hs, rows, :] = uw[:, :HEAD_DIM]
            w_ref[hs, rows, :] = uw[:, HEAD_DIM:].astype(BF16)
            aqk_ref[hs, rows, :] = a_qk.astype(BF16)
            drows = pl.ds(pl.multiple_of(g * DEC_ROWS, DEC_ROWS), DEC_ROWS)
            qd_ref[hs, rows, :] = qd_stage[hh, rows, :]
            kdt_ref[hs, :, rows] = kdt_stage[hh, :, rows]
            dec_ref[hs, drows, :] = dec_stage[hh, drows, :]

    def next_pointwise(gi, unroll):
        return [th for u in range(unroll) for th in pointwise_thunks((gi + 1) * unroll + u, u)]

    def rec_thunks(gi):
        r0 = pl.multiple_of(gi * GROUP, GROUP)
        rows = pl.ds(r0, GROUP)
        heads = range(REC_HEADS)
        outs = [[] for _ in heads]

        held = {}

        def read_state(c):
            crow = pl.ds(r0 + c * CHUNK, CHUNK)
            states = [st_ref[h] for h in heads]
            s_bf = [s.astype(BF16) for s in states]
            ws = [_dot(jnp.concatenate([w_ref[h, crow, :], qd_ref[h, crow, :]], axis=0), s_bf[h])
                  for h in heads]
            held[c] = (states, ws)

        def write_state(c):
            crow = pl.ds(r0 + c * CHUNK, CHUNK)
            states, ws = held.pop(c)
            v_new = [u_ref[h, crow, :] - ws[h][:CHUNK, :] for h in heads]
            zero = lambda n: jnp.zeros((n * CHUNK, HEAD_DIM), BF16)
            v_pad = [jnp.concatenate(([zero(c)] if c else []) + [v.astype(BF16)]
                                     + ([zero(cpg - 1 - c)] if c < cpg - 1 else []), axis=0)
                     for v in v_new]
            upd = [_dot(jnp.concatenate([aqk_ref[h, crow, :], kdt_ref[h, :, rows]], axis=0),
                        v_pad[h]) for h in heads]
            for h in heads:
                dec = dec_ref[h, pl.ds(gi * DEC_ROWS + c, 1), :]
                st_ref[h] = states[h] * dec + upd[h][CHUNK:, :]
                outs[h].append(ws[h][CHUNK:, :] + upd[h][:CHUNK, :])
            if c == cpg - 1:
                for h in heads:
                    o = _rms(jnp.concatenate(outs[h], axis=0), nw_ref[...])
                    o_ref[h, rows, :] = (o * _silu(z_ref[h, rows, :].astype(F32))).astype(BF16)

        return [functools.partial(half, c) for c in range(cpg) for half in (read_state, write_state)]

    def interleave(a, b):
        if not a or not b:
            return list(a) + list(b)
        out, j = [], 0
        for i, th in enumerate(a):
            out.append(th)
            while j < len(b) and (j + 1) * len(a) <= (i + 1) * len(b):
                out.append(b[j])
                j += 1
        return out + list(b[j:])

    @pl.when(slot != REC_PAIRS - 1)
    def _():
        unroll = PREP_UNROLL
        n_trips = n_groups // unroll
        first_pointwise(unroll)

        def prep_body(gi, _):
            prep_groups(gi, next_pointwise(gi, unroll), unroll)
            return 0
        lax.fori_loop(0, n_trips - 1, prep_body, 0)
        prep_groups(n_trips - 1, [], unroll)

    @pl.when(slot == REC_PAIRS - 1)
    def _():
        unroll = REC_PREP_UNROLL
        n_trips = n_groups // unroll
        st_ref[...] = s0_ref[...]
        first_pointwise(unroll)

        def rec_of_trip(t):
            return [th for u in range(unroll) for th in rec_thunks(t * unroll + u)]

        prep_groups(0, next_pointwise(0, unroll), unroll)

        def prep_rec_body(gi, _):
            prep_groups(gi, interleave(rec_of_trip(gi - 1), next_pointwise(gi, unroll)), unroll)
            return 0
        lax.fori_loop(1, n_trips - 1, prep_rec_body, 0)
        prep_groups(n_trips - 1, rec_of_trip(n_trips - 2), unroll)
        for thunk in rec_of_trip(n_trips - 1):
            thunk()


def _gdn(proj, proj_m, conv_wt, row4, state0, norm_w):
    _, b, seq, _ = proj.shape
    n_pairs = GDN_HEADS // HEAD_PAIR

    def head_specs(shape, imap):
        return [pl.BlockSpec(shape, functools.partial(imap, t * GDN_HEADS + hh))
                for t in range(3) for hh in range(HEAD_PAIR)]

    src_specs = head_specs((None, None, seq, HEAD_DIM),
                           lambda off, i, j: (off + HEAD_PAIR * j, i, 0, 0))
    meta_specs = head_specs((None, META_ROWS, HEAD_DIM), lambda off, i, j: (off + HEAD_PAIR * j, 0, 0))
    tap_specs = head_specs((CONV_WIDTH, HEAD_DIM), lambda off, i, j: (0, off + HEAD_PAIR * j))
    n_qkv = 3 * HEAD_PAIR
    z_block0 = 3 * GDN_HEADS // REC_HEADS

    per_pair = lambda width, dt: pltpu.VMEM((HEAD_PAIR, seq, width), dt)
    per_rec = lambda width, dt: pltpu.VMEM((REC_HEADS, seq, width), dt)
    return pl.pallas_call(
        functools.partial(_gdn_kernel, seq=seq),
        out_shape=jax.ShapeDtypeStruct((GDN_HEADS, b, seq, HEAD_DIM), BF16),
        grid=(b, n_pairs),
        in_specs=src_specs
        + [pl.BlockSpec((REC_HEADS, None, seq, HEAD_DIM),
                        lambda i, j: (z_block0 + j // REC_PAIRS, i, 0, 0))]
        + meta_specs + tap_specs
        + [pl.BlockSpec((None, HEAD_PAIR, 1, seq),
                        lambda i, j: (i, BETA_LANE // HEAD_PAIR + j, 0, 0)),
           pl.BlockSpec((None, HEAD_PAIR, 1, seq),
                        lambda i, j: (i, DECAY_LANE // HEAD_PAIR + j, 0, 0)),
           pl.BlockSpec((REC_HEADS, HEAD_DIM, HEAD_DIM), lambda i, j: (j // REC_PAIRS, 0, 0)),
           pl.BlockSpec((1, HEAD_DIM), lambda i, j: (0, 0))],
        out_specs=pl.BlockSpec((REC_HEADS, None, seq, HEAD_DIM),
                               lambda i, j: (j // REC_PAIRS, i, 0, 0)),
        scratch_shapes=[pltpu.VMEM((max(PREP_UNROLL, REC_PREP_UNROLL) * n_qkv, HIST_ROWS + ROW_BLOCK,
                                    HEAD_DIM), F32),
                        per_pair(HEAD_DIM, BF16), per_pair(HEAD_DIM, BF16),
                        per_pair(2 * HEAD_DIM, BF16),
                        per_pair(HEAD_DIM, F32), per_pair(HEAD_DIM, F32),
                        per_rec(HEAD_DIM, BF16),
                        pltpu.VMEM((REC_HEADS, HEAD_DIM, seq), BF16),
                        pltpu.VMEM((REC_HEADS, seq // GROUP * DEC_ROWS, HEAD_DIM), F32),
                        per_rec(HEAD_DIM, F32), per_rec(HEAD_DIM, BF16), per_rec(GROUP, BF16),
                        pltpu.VMEM((REC_HEADS, HEAD_DIM, HEAD_DIM), F32),
                        per_pair(HEAD_DIM, BF16),
                        pltpu.VMEM((HEAD_PAIR, HEAD_DIM, seq), BF16),
                        pltpu.VMEM((HEAD_PAIR, seq // GROUP * DEC_ROWS, HEAD_DIM), F32)],
        compiler_params=pltpu.CompilerParams(
            dimension_semantics=("parallel", "arbitrary"), vmem_limit_bytes=VMEM_LIMIT),
        name="gdn",
    )(*([proj] * n_qkv), proj, *([proj_m] * n_qkv), *([conv_wt] * n_qkv),
      row4, row4, state0, norm_w)


def _fox_kernel(q_ref, k_ref, v_ref, g_ref, km_ref, vm_ref, colm_ref, crow_ref,
                qw_ref, kw_ref, o_ref, qt_ref, ka_ref, kam_ref, vt_ref, vtm_ref,
                m_ref, l_ref, acc_ref, *, seq):
    n_blocks = seq // ROW_BLOCK
    aug_r = lax.broadcasted_iota(jnp.int32, (LANES, ROW_BLOCK), 0)

    def key_aug(ck, valid=None):
        hi, mid, lo = _split3(ck)
        lane = lax.broadcasted_iota(jnp.int32, ck.shape, 1)
        neg_hi = -hi.astype(F32)
        if valid is not None:
            neg_hi = jnp.where(valid, neg_hi, NEG_BIG)
        blk = jnp.where(lane < 3, 1.0,
                        jnp.where(lane == 3, neg_hi,
                                  jnp.where(lane == 4, -mid.astype(F32),
                                            jnp.where(lane == 5, -lo.astype(F32), 0.0))))
        return blk.astype(BF16)

    def pro_body(i, _):
        r0 = pl.multiple_of(i * ROW_BLOCK, ROW_BLOCK)
        rows = pl.ds(r0, ROW_BLOCK)
        for hd in range(FOX_GROUP):
            qn = _rms(q_ref[hd, rows, :].astype(F32), qw_ref[...]) * (HEAD_DIM ** -0.5 * LOG2E)
            qt_ref[hd, 0:HEAD_DIM, rows] = qn.T.astype(BF16)
            hi, mid, lo = _split3(crow_ref[hd, :, rows] * LOG2E)
            aug = jnp.where(aug_r == 0, hi.astype(F32),
                            jnp.where(aug_r == 1, mid.astype(F32),
                                      jnp.where(aug_r == 2, lo.astype(F32),
                                                jnp.where(aug_r < 6, 1.0, 0.0))))
            qt_ref[hd, HEAD_DIM:2 * HEAD_DIM, rows] = aug.astype(BF16)
            ka_ref[hd, rows, 0:HEAD_DIM] = _rms(k_ref[hd, rows, :].astype(F32),
                                                kw_ref[...]).astype(BF16)
            aug_k = jnp.where(aug_r < 3, 1.0,
                              jnp.where(aug_r == 3, -hi.astype(F32),
                                        jnp.where(aug_r == 4, -mid.astype(F32),
                                                  jnp.where(aug_r == 5, -lo.astype(F32), 0.0))))
            ka_ref[hd, rows, HEAD_DIM:2 * HEAD_DIM] = aug_k.T.astype(BF16)
            vt_ref[hd, :, rows] = v_ref[hd, rows, :].astype(F32).T.astype(BF16)
        return 0
    lax.fori_loop(0, n_blocks, pro_body, 0, unroll=BLOCK_UNROLL // FOX_GROUP)

    mrow = lax.broadcasted_iota(jnp.int32, (META_ROWS, LANES), 0)
    for hd in range(FOX_GROUP):
        head = pl.program_id(1) * FOX_GROUP + hd
        kam_ref[hd, :, 0:HEAD_DIM] = _rms(km_ref[hd].astype(F32), kw_ref[...]).astype(BF16)
        ck_m = _lane_bcast(colm_ref[...], FORGET_LANE + head) * LOG2E
        kam_ref[hd, :, HEAD_DIM:2 * HEAD_DIM] = key_aug(ck_m, mrow >= META_PAD)
        vtm_ref[hd] = vm_ref[hd].astype(F32).T.astype(BF16)

    kidx = lax.broadcasted_iota(jnp.int32, (FOX_TK, FOX_TK), 0)
    qidx = lax.broadcasted_iota(jnp.int32, (FOX_TK, FOX_TK), 1)
    diag_ok = kidx <= qidx

    items = []
    for kj in range(-1, seq // FOX_TK):
        lane0 = max(kj, 0) * FOX_TK
        for p0 in range(lane0, seq, FOX_PIECE):
            for hd in range(FOX_GROUP):
                items.append((hd, kj, slice(p0, min(p0 + FOX_PIECE, seq)), p0 == lane0))

    def scores(item):
        hd, kj, ln, leads = item
        k_aug = kam_ref[hd] if kj < 0 else ka_ref[hd, kj * FOX_TK:(kj + 1) * FOX_TK, :]
        s = _dot(k_aug, qt_ref[hd, :, ln])
        if kj >= 0 and leads:
            masked = jnp.where(diag_ok, s[:, :FOX_TK], NEG_BIG)
            s = jnp.concatenate([masked, s[:, FOX_TK:]], axis=1) if s.shape[1] > FOX_TK else masked
        return s

    def softmax_stats(item, s):
        hd, kj, ln, _ = item
        top = jnp.max(s, axis=0, keepdims=True)
        if kj < 0:
            m_new, alpha = top, None
        else:
            m_old = m_ref[hd, :, ln]
            m_new = jnp.maximum(m_old, top)
            alpha = jnp.exp2(m_old - m_new)
        p = jnp.exp2(s - m_new)
        psum = jnp.sum(p, axis=0, keepdims=True)
        m_ref[hd, :, ln] = m_new
        l_ref[hd, :, ln] = psum if kj < 0 else alpha * l_ref[hd, :, ln] + psum
        return p.astype(BF16), alpha

    def values(item, p, alpha):
        hd, kj, ln, _ = item
        v_t = vtm_ref[hd] if kj < 0 else vt_ref[hd, :, kj * FOX_TK:(kj + 1) * FOX_TK]
        pv = _dot(v_t, p)
        acc_ref[hd, :, ln] = pv if kj < 0 else alpha * acc_ref[hd, :, ln] + pv

    s_cur = scores(items[0])
    p_cur = None
    for t in range(len(items) + 1):
        s_next = scores(items[t + 1]) if t + 1 < len(items) else None
        p_next = softmax_stats(items[t], s_cur) if t < len(items) else None
        if p_cur is not None:
            values(items[t - 1], *p_cur)
        s_cur, p_cur = s_next, p_next

    def out_body(i, _):
        r0 = pl.multiple_of(i * ROW_BLOCK, ROW_BLOCK)
        rows = pl.ds(r0, ROW_BLOCK)
        for hd in range(FOX_GROUP):
            out_t = acc_ref[hd, :, rows] * (1.0 / l_ref[hd, :, rows])
            o_ref[hd, rows, :] = (out_t.T * _silu(g_ref[hd, rows, :].astype(F32))).astype(BF16)
        return 0
    lax.fori_loop(0, n_blocks, out_body, 0, unroll=BLOCK_UNROLL // FOX_GROUP)


def _fox(proj, proj_m, col_m, row4, q_w, k_w):
    _, b, seq, _ = proj.shape
    base = 4 * GDN_HEADS // FOX_GROUP
    hb = FOX_HEADS // FOX_GROUP

    def head_block(off):
        return pl.BlockSpec((FOX_GROUP, None, seq, HEAD_DIM),
                            lambda i, j, off=off: (off + j, i, 0, 0))

    def meta_block(off):
        return pl.BlockSpec((FOX_GROUP, META_ROWS, HEAD_DIM), lambda i, j, off=off: (off + j, 0, 0))

    per_head = lambda shape, dt: pltpu.VMEM((FOX_GROUP,) + shape, dt)
    return pl.pallas_call(
        functools.partial(_fox_kernel, seq=seq),
        out_shape=jax.ShapeDtypeStruct((FOX_HEADS, b, seq, HEAD_DIM), BF16),
        grid=(b, FOX_HEADS // FOX_GROUP),
        in_specs=[head_block(base), head_block(base + hb), head_block(base + 2 * hb),
                  head_block(base + 3 * hb),
                  meta_block(base + hb), meta_block(base + 2 * hb),
                  pl.BlockSpec((META_ROWS, LANES), lambda i, j: (0, 0)),
                  pl.BlockSpec((None, FOX_GROUP, 1, seq),
                               lambda i, j: (i, FORGET_LANE // FOX_GROUP + j, 0, 0)),
                  pl.BlockSpec((1, HEAD_DIM), lambda i, j: (0, 0)),
                  pl.BlockSpec((1, HEAD_DIM), lambda i, j: (0, 0))],
        out_specs=pl.BlockSpec((FOX_GROUP, None, seq, HEAD_DIM), lambda i, j: (j, i, 0, 0)),
        scratch_shapes=[per_head((2 * HEAD_DIM, seq), BF16),
                        per_head((seq, 2 * HEAD_DIM), BF16),
                        per_head((META_ROWS, 2 * HEAD_DIM), BF16),
                        per_head((HEAD_DIM, seq), BF16),
                        per_head((HEAD_DIM, META_ROWS), BF16),
                        per_head((1, seq), F32), per_head((1, seq), F32),
                        per_head((HEAD_DIM, seq), F32)],
        compiler_params=pltpu.CompilerParams(
            dimension_semantics=("parallel", "arbitrary"), vmem_limit_bytes=VMEM_LIMIT),
        name="fox",
    )(proj, proj, proj, proj, proj_m, proj_m, col_m, row4, q_w, k_w)


def _out_proj_kernel(mg_ref, mf_ref, wg_ref, wf_ref, pw_ref, x_ref, o_ref):
    for r in range(o_ref.shape[0] // OUT_PROJ_CHUNK):
        rows = slice(r * OUT_PROJ_CHUNK, (r + 1) * OUT_PROJ_CHUNK)

        def rows_of(m_ref):
            return jnp.concatenate([m_ref[h, rows, :] for h in range(m_ref.shape[0])], axis=1)
        out = _dot(rows_of(mg_ref), wg_ref[...]) + _dot(rows_of(mf_ref), wf_ref[...])
        o_ref[rows, :] = x_ref[rows, :] + _rms(out, pw_ref[...])


def _out_proj(mg, mf, w_g, w_f, post_w, x2d, tm):
    m, d = x2d.shape
    return pl.pallas_call(
        _out_proj_kernel,
        out_shape=jax.ShapeDtypeStruct((m, d), F32),
        grid=(m // tm,),
        in_specs=[pl.BlockSpec((GDN_HEADS, tm, HEAD_DIM), lambda i: (0, i, 0)),
                  pl.BlockSpec((FOX_HEADS, tm, HEAD_DIM), lambda i: (0, i, 0)),
                  pl.BlockSpec((GDN_WIDTH, d), lambda i: (0, 0), pipeline_mode=pl.Buffered(1)),
                  pl.BlockSpec((FOX_WIDTH, d), lambda i: (0, 0), pipeline_mode=pl.Buffered(1)),
                  pl.BlockSpec((1, d), lambda i: (0, 0)),
                  pl.BlockSpec((tm, d), lambda i: (i, 0))],
        out_specs=pl.BlockSpec((tm, d), lambda i: (i, 0)),
        compiler_params=pltpu.CompilerParams(
            dimension_semantics=("parallel",), vmem_limit_bytes=VMEM_LIMIT),
        name="out_proj",
    )(mg, mf, w_g, w_f, post_w, x2d)


def _tile(total, want):
    t = min(total, want)
    while total % t:
        t //= 2
    return t


def _layer(x, meta_pad, pre_w, w_in, conv_w, a_log, dt_bias, gdn_norm_w, fox_q_w, fox_k_w,
           fox_f_bias, w_out, post_w):
    b, seq, d = x.shape
    assert seq % (GROUP * PREP_UNROLL) == 0 and seq % FOX_TK == 0 and seq % ROW_BLOCK == 0
    gw, fw = GDN_WIDTH, FOX_WIDTH
    o_gb = 4 * gw
    o_f = o_gb + 2 * GDN_HEADS
    o_ff = o_f + 4 * fw
    w_t = w_in.T
    w_gate = jnp.concatenate(
        [w_t[o_gb:o_f], w_t[o_ff:],
         jnp.zeros((GATE_WIDTH - 2 * GDN_HEADS - FOX_HEADS, d), w_in.dtype)], axis=0)
    pre_w2 = pre_w[None]
    w_main, proj_m, gate_m = _w_prep(w_t, meta_pad, pre_w2, w_gate, MAIN_WIDTH, o_gb, o_f - o_gb)
    zpad = jnp.zeros((GATE_WIDTH - FORGET_LANE - FOX_HEADS,), F32)
    add_vec = jnp.concatenate([jnp.zeros((GDN_HEADS,), F32), dt_bias, fox_f_bias, zpad])[None]
    alog_vec = jnp.concatenate([jnp.zeros((GDN_HEADS,), F32), a_log,
                                jnp.zeros((FOX_HEADS,), F32), zpad])[None]

    x2d = x.reshape(b * seq, d)
    proj, gate = _in_proj(x2d, pre_w2, w_main, w_gate, _tile(b * seq, IN_PROJ_TM), IN_PROJ_TN)
    proj = proj.reshape(MAIN_WIDTH // HEAD_DIM, b, seq, HEAD_DIM)

    row4 = _gate_rows(gate.reshape(b, seq, GATE_WIDTH), add_vec, alog_vec)
    col_m, row_m = _gate_meta(gate_m, add_vec, alog_vec)
    row_m4 = row_m.reshape(1, GATE_ROWS, 1, META_ROWS)

    conv_wt = conv_w.T
    state0 = _gdn_state0(proj_m, conv_wt, col_m, row_m4)
    o_gdn = _gdn(proj, proj_m, conv_wt, row4, state0, gdn_norm_w[None])
    o_fox = _fox(proj, proj_m, col_m, row4, fox_q_w[None], fox_k_w[None])

    w_out_b = w_out.astype(BF16)
    out = _out_proj(o_gdn.reshape(GDN_HEADS, b * seq, HEAD_DIM),
                    o_fox.reshape(FOX_HEADS, b * seq, HEAD_DIM),
                    w_out_b[:gw], w_out_b[gw:], post_w[None], x2d, _tile(b * seq, OUT_PROJ_TM))
    return out.reshape(b, seq, d)


def kernel(x, meta_tokens, pre_norm_w, w_in, conv_w, a_log, dt_bias, gdn_norm_w, fox_q_norm_w,
           fox_k_norm_w, fox_f_bias, w_out, post_norm_w):
    assert pre_norm_w.shape[0] == 1, "single-layer stack"
    meta_pad = jnp.concatenate(
        [jnp.zeros((META_PAD, x.shape[-1]), x.dtype), meta_tokens.astype(x.dtype)], axis=0)
    return _layer(x, meta_pad, pre_norm_w[0], w_in[0], conv_w[0], a_log[0], dt_bias[0],
                  gdn_norm_w[0], fox_q_norm_w[0], fox_k_norm_w[0], fox_f_bias[0], w_out[0],
                  post_norm_w[0])
```

```python
import functools
import math

import jax
import jax.numpy as jnp
from jax import lax
from jax.experimental import pallas as pl
from jax.experimental.pallas import tpu as pltpu

N_META = 16
HEAD_DIM = 128
GDN_HEADS = 8
FOX_HEADS = 8
GDN_WIDTH = GDN_HEADS * HEAD_DIM
FOX_WIDTH = FOX_HEADS * HEAD_DIM
CONV_WIDTH = 4
CHUNK = 64
EPS = 1e-6

LANES = 128
SUBLANES = 8
MXU_DIM = 256
MAIN_WIDTH = 4 * GDN_WIDTH + 4 * FOX_WIDTH
GATE_WIDTH = LANES
BETA_LANE, DECAY_LANE, FORGET_LANE = 0, GDN_HEADS, 2 * GDN_HEADS
GATE_ROWS = 32
META_ROWS = CHUNK
META_PAD = META_ROWS - N_META
GROUP = MXU_DIM
ROW_BLOCK = 256
HIST_ROWS = 2 * SUBLANES
IN_PROJ_TM, IN_PROJ_TN = 1024, 2048
W_PREP_ROWS = 512
OUT_PROJ_TM = 1024
OUT_PROJ_CHUNK = 512
HEAD_PAIR = 2
REC_PAIRS = 2
REC_HEADS = REC_PAIRS * HEAD_PAIR
PREP_UNROLL = 2
REC_PREP_UNROLL = 1
DEC_ROWS = SUBLANES
BLOCK_UNROLL = 16
FOX_GROUP = 4
FOX_TK = MXU_DIM
FOX_PIECE = 2 * MXU_DIM
VMEM_LIMIT = 56 * 1024 * 1024

F32 = jnp.float32
BF16 = jnp.bfloat16
NEG_BIG = -1e30
LOG2E = math.log2(math.e)


def _dot(a, b):
    return jnp.dot(a, b, preferred_element_type=F32)


def _dot_nt(a, b):
    return lax.dot_general(a, b, (((1,), (1,)), ((), ())), preferred_element_type=F32)


def _split3(x):
    hi = x.astype(BF16)
    r1 = x - hi.astype(F32)
    mid = r1.astype(BF16)
    lo = (r1 - mid.astype(F32)).astype(BF16)
    return hi, mid, lo


def _dot_exact_rhs01(parts, m):
    return _dot(parts[0], m) + _dot(parts[1], m) + _dot(parts[2], m)


def _dot_exact_lhs01(m, parts):
    return _dot(m, parts[0]) + _dot(m, parts[1]) + _dot(m, parts[2])


def _lane_bcast(col_tile, lane):
    sel = (lax.broadcasted_iota(jnp.int32, (LANES, LANES), 0) == lane).astype(BF16)
    return _dot_exact_rhs01(_split3(col_tile), sel)


def _chunk_of(idx):
    return jnp.right_shift(idx, CHUNK.bit_length() - 1)


def _rms(x, w):
    return x * lax.rsqrt(jnp.mean(x * x, axis=-1, keepdims=True) + EPS) * w


def _silu(x):
    half = 0.5 * x
    return half + half * jnp.tanh(half)


def _softplus(x):
    return jnp.maximum(x, 0.0) + jnp.log1p(jnp.exp(-jnp.abs(x)))


def _w_prep_kernel(w_hbm, meta_ref, nw_ref, wg_ref, o_ref, om_ref, ogm_ref, buf, sem, xm_ref,
                   *, rows, shift_from, shift):
    r = pl.program_id(0)

    def fetch(step, slot):
        start = step * rows
        src = pl.multiple_of(start + jnp.where(start >= shift_from, shift, 0), SUBLANES)
        return pltpu.make_async_copy(w_hbm.at[pl.ds(src, rows), :], buf.at[slot], sem.at[slot])

    @pl.when(r == 0)
    def _():
        fetch(0, 0).start()
        xm = _rms(meta_ref[...], nw_ref[...]).astype(BF16)
        xm_ref[...] = xm
        ogm_ref[...] = _dot_nt(xm, wg_ref[...].astype(BF16))

    @pl.when(r + 1 < pl.num_programs(0))
    def _():
        fetch(r + 1, (r + 1) % 2).start()

    fetch(r, r % 2).wait()
    w_bf = buf[r % 2].astype(BF16)
    o_ref[...] = w_bf
    res = _dot_nt(xm_ref[...], w_bf)
    for hd in range(om_ref.shape[0]):
        om_ref[hd] = res[:, hd * HEAD_DIM:(hd + 1) * HEAD_DIM].astype(om_ref.dtype)


def _w_prep(w_t, meta_pad, norm_w, w_gate, n_rows, shift_from, shift):
    d = w_t.shape[1]
    rows = W_PREP_ROWS
    return pl.pallas_call(
        functools.partial(_w_prep_kernel, rows=rows, shift_from=shift_from, shift=shift),
        out_shape=(jax.ShapeDtypeStruct((n_rows, d), BF16),
                   jax.ShapeDtypeStruct((n_rows // HEAD_DIM, META_ROWS, HEAD_DIM), BF16),
                   jax.ShapeDtypeStruct((META_ROWS, GATE_WIDTH), F32)),
        grid=(n_rows // rows,),
        in_specs=[pl.BlockSpec(memory_space=pl.ANY),
                  pl.BlockSpec((META_ROWS, d), lambda r: (0, 0)),
                  pl.BlockSpec((1, d), lambda r: (0, 0)),
                  pl.BlockSpec((GATE_WIDTH, d), lambda r: (0, 0))],
        out_specs=(pl.BlockSpec((rows, d), lambda r: (r, 0)),
                   pl.BlockSpec((rows // HEAD_DIM, META_ROWS, HEAD_DIM), lambda r: (r, 0, 0)),
                   pl.BlockSpec((META_ROWS, GATE_WIDTH), lambda r: (0, 0))),
        scratch_shapes=[pltpu.VMEM((2, rows, d), F32), pltpu.SemaphoreType.DMA((2,)),
                        pltpu.VMEM((META_ROWS, d), BF16)],
        compiler_params=pltpu.CompilerParams(
            dimension_semantics=("arbitrary",), vmem_limit_bytes=VMEM_LIMIT),
        name="w_prep",
    )(w_t, meta_pad, norm_w, w_gate)


def _in_proj_kernel(x_ref, nw_ref, w_ref, wg_ref, o_ref, og_ref, xn_ref):
    @pl.when(pl.program_id(1) == 0)
    def _():
        xn = _rms(x_ref[...], nw_ref[...]).astype(BF16)
        xn_ref[...] = xn
        og_ref[...] = _dot_nt(xn, wg_ref[...].astype(BF16))

    res = _dot_nt(xn_ref[...], w_ref[...])
    for hd in range(o_ref.shape[0]):
        o_ref[hd] = res[:, hd * HEAD_DIM:(hd + 1) * HEAD_DIM].astype(o_ref.dtype)


def _in_proj(x2d, norm_w, w_main, w_gate, tm, tn):
    m, d = x2d.shape
    n = w_main.shape[0]
    return pl.pallas_call(
        _in_proj_kernel,
        out_shape=(jax.ShapeDtypeStruct((n // HEAD_DIM, m, HEAD_DIM), BF16),
                   jax.ShapeDtypeStruct((m, GATE_WIDTH), F32)),
        grid=(m // tm, n // tn),
        in_specs=[pl.BlockSpec((tm, d), lambda i, j: (i, 0)),
                  pl.BlockSpec((1, d), lambda i, j: (0, 0)),
                  pl.BlockSpec((tn, d), lambda i, j: (j, 0)),
                  pl.BlockSpec((GATE_WIDTH, d), lambda i, j: (0, 0))],
        out_specs=(pl.BlockSpec((tn // HEAD_DIM, tm, HEAD_DIM), lambda i, j: (j, i, 0)),
                   pl.BlockSpec((tm, GATE_WIDTH), lambda i, j: (i, 0))),
        scratch_shapes=[pltpu.VMEM((tm, d), BF16)],
        compiler_params=pltpu.CompilerParams(
            dimension_semantics=("parallel", "arbitrary"), vmem_limit_bytes=VMEM_LIMIT),
        name="in_proj",
    )(x2d, norm_w, w_main, w_gate)


def _gate_meta_kernel(t_ref, add_ref, alog_ref, col_ref, row_ref):
    lane = lax.broadcasted_iota(jnp.int32, (META_ROWS, LANES), 1)
    row = lax.broadcasted_iota(jnp.int32, (META_ROWS, LANES), 0)
    ri = lax.broadcasted_iota(jnp.int32, (META_ROWS, META_ROWS), 0)
    ci = lax.broadcasted_iota(jnp.int32, (META_ROWS, META_ROWS), 1)
    tri = (ci <= ri).astype(BF16)
    is_beta = lane < DECAY_LANE
    is_decay = (lane >= DECAY_LANE) & (lane < FORGET_LANE)
    is_forget = (lane >= FORGET_LANE) & (lane < FORGET_LANE + FOX_HEADS)
    real = row >= META_PAD

    t = t_ref[...]
    ta = t + add_ref[...]
    beta = jnp.where(real, 1.0 / (1.0 + jnp.exp(-t)), 0.0)
    g = -jnp.exp(alog_ref[...]) * _softplus(ta)
    logf = -_softplus(-ta)
    val = jnp.where(real, jnp.where(is_decay, g, jnp.where(is_forget, logf, 0.0)), 0.0)
    cum = _dot_exact_lhs01(tri, _split3(val))
    to_come = cum - cum[META_ROWS - 1:META_ROWS, :]
    res = jnp.where(is_beta, beta, jnp.where(is_decay, cum, to_come))
    col_ref[...] = res
    row_ref[...] = res.T[:GATE_ROWS, :]


def _gate_rows_kernel(t_ref, add_ref, alog_ref, row_ref, *, rows):
    blk = ROW_BLOCK
    row = lax.broadcasted_iota(jnp.int32, (GATE_ROWS, blk), 0)
    si = lax.broadcasted_iota(jnp.int32, (blk, blk), 0)
    ti = lax.broadcasted_iota(jnp.int32, (blk, blk), 1)
    tri = (si <= ti).astype(BF16)
    tri_chunk = ((si <= ti) & (_chunk_of(si) == _chunk_of(ti))).astype(BF16)
    is_beta = row < DECAY_LANE
    is_decay = (row >= DECAY_LANE) & (row < FORGET_LANE)
    is_forget = (row >= FORGET_LANE) & (row < FORGET_LANE + FOX_HEADS)
    add = jnp.concatenate([add_ref[...]] * (blk // LANES), axis=1)
    neg_rate = -jnp.exp(jnp.concatenate([alog_ref[...]] * (blk // LANES), axis=1))

    carry = jnp.zeros((GATE_ROWS, 1), F32)
    for r in range(rows // blk):
        t = t_ref[r * blk:(r + 1) * blk, :].T[:GATE_ROWS, :]
        ta = t + add
        beta = 1.0 / (1.0 + jnp.exp(-t))
        val = jnp.where(is_decay, neg_rate * _softplus(ta), jnp.where(is_forget, -_softplus(-ta), 0.0))
        parts = _split3(val)
        cum_chunk = _dot_exact_rhs01(parts, tri_chunk)
        cum_all = _dot_exact_rhs01(parts, tri) + carry
        carry = cum_all[:, blk - 1:blk]
        row_ref[:, 0, r * blk:(r + 1) * blk] = jnp.where(is_beta, beta,
                                                         jnp.where(is_decay, cum_chunk, cum_all))


def _gate_rows(gate3d, add_vec, alog_vec):
    b, rows, _ = gate3d.shape
    per_row = lambda v: jnp.broadcast_to(v[0, :GATE_ROWS, None], (GATE_ROWS, LANES))
    return pl.pallas_call(
        functools.partial(_gate_rows_kernel, rows=rows),
        out_shape=jax.ShapeDtypeStruct((b, GATE_ROWS, 1, rows), F32),
        grid=(b,),
        in_specs=[pl.BlockSpec((None, rows, LANES), lambda i: (i, 0, 0)),
                  pl.BlockSpec((GATE_ROWS, LANES), lambda i: (0, 0)),
                  pl.BlockSpec((GATE_ROWS, LANES), lambda i: (0, 0))],
        out_specs=pl.BlockSpec((None, GATE_ROWS, 1, rows), lambda i: (i, 0, 0, 0)),
        compiler_params=pltpu.CompilerParams(
            dimension_semantics=("parallel",), vmem_limit_bytes=VMEM_LIMIT),
        name="gate_rows",
    )(gate3d, per_row(add_vec), per_row(alog_vec))


def _gate_meta(gate_m, add_vec, alog_vec):
    return pl.pallas_call(
        _gate_meta_kernel,
        out_shape=(jax.ShapeDtypeStruct((META_ROWS, LANES), F32),
                   jax.ShapeDtypeStruct((GATE_ROWS, META_ROWS), F32)),
        grid=(1,),
        in_specs=[pl.BlockSpec((META_ROWS, LANES), lambda i: (0, 0)),
                  pl.BlockSpec((1, LANES), lambda i: (0, 0)),
                  pl.BlockSpec((1, LANES), lambda i: (0, 0))],
        out_specs=(pl.BlockSpec((META_ROWS, LANES), lambda i: (0, 0)),
                   pl.BlockSpec((GATE_ROWS, META_ROWS), lambda i: (0, 0))),
        compiler_params=pltpu.CompilerParams(
            dimension_semantics=("arbitrary",), vmem_limit_bytes=VMEM_LIMIT),
        name="gate_meta",
    )(gate_m, add_vec, alog_vec)


def _conv_silu(load_rows, w):
    y = load_rows(0) * w[0:1, :]
    for j in range(1, CONV_WIDTH):
        y = y + load_rows(j) * w[j:j + 1, :]
    return _silu(y)


def _l2norm(x):
    return x * lax.rsqrt(jnp.sum(x * x, axis=-1, keepdims=True) + EPS)


def _gdn_pointwise(q, k, v, beta_b, g_b):
    r = q.shape[0]
    q = q * (lax.rsqrt(jnp.sum(q * q, axis=-1, keepdims=True) + EPS) * (HEAD_DIM ** -0.5))
    k = _l2norm(k)
    g3 = g_b.reshape(r // CHUNK, CHUNK, LANES)
    g_last = jnp.broadcast_to(g3[:, CHUNK - 1:CHUNK, :], g3.shape).reshape(r, LANES)
    e_g = jnp.exp(g_b)
    q_dec = q * e_g
    k_dec = k * jnp.exp(g_last - g_b)
    y = jnp.concatenate([v * beta_b, k * (beta_b * e_g)], axis=1)
    return q, k, q_dec, k_dec, y, jnp.exp(g_last)


def _gdn_groups(probs, fillers=()):
    r = probs[0][0].shape[0]
    ri = lax.broadcasted_iota(jnp.int32, (r, r), 0)
    ci = lax.broadcasted_iota(jnp.int32, (r, r), 1)
    same = _chunk_of(ri) == _chunk_of(ci)
    causal = same & (ci <= ri)
    strict = same & (ci < ri)

    def widen(t):
        return jnp.concatenate([t] * (r // LANES), axis=1) if r >= LANES else t[:, :r]

    fillers = list(fillers)
    n_ticks = 2 * CHUNK.bit_length() - 3
    per_tick = -(-len(fillers) // n_ticks)

    def tick():
        for thunk in fillers[:per_tick]:
            thunk()
        del fillers[:per_tick]

    kks = [_dot_nt(k, k) for _, k, _, _, _, _ in probs]
    qks = [_dot_nt(q, k) for q, k, _, _, _, _ in probs]
    tick()
    dmats = [jnp.where(causal, jnp.exp(jnp.where(causal, widen(g_b) - g_row, 0.0)), 0.0)
             for _, _, _, _, g_b, g_row in probs]
    xs = [jnp.where(strict, widen(p[3]) * kk * d, 0.0).astype(BF16)
          for p, kk, d in zip(probs, kks, dmats)]
    a_qks = [qk * d for qk, d in zip(qks, dmats)]
    eye = (ri == ci).astype(BF16)
    zs = [_dot(eye - x, p[2]) for p, x in zip(probs, xs)]
    span = 2
    while span < CHUNK:
        xs = [_dot(x, x).astype(BF16) for x in xs]
        tick()
        zs = [z + _dot(x, z.astype(BF16)) for x, z in zip(xs, zs)]
        tick()
        span *= 2
    assert not fillers
    return list(zip(zs, a_qks))


def _gdn_state0_kernel(km_ref, vm_ref, wk_ref, wv_ref, colm_ref, growm_ref, s_ref, pad_ref):
    heads = range(GDN_HEADS)
    probs, kd_t = [], []
    for h in heads:
        ls = slice(h * HEAD_DIM, (h + 1) * HEAD_DIM)
        beta_m = _lane_bcast(colm_ref[...], BETA_LANE + h)
        g_m = _lane_bcast(colm_ref[...], DECAY_LANE + h)
        conv = []
        for t, (src, w_ref) in enumerate(((km_ref, wk_ref), (vm_ref, wv_ref))):
            win = pad_ref.at[2 * h + t]
            win[0:SUBLANES, :] = jnp.zeros((SUBLANES, LANES), F32)
            win[SUBLANES:SUBLANES + META_ROWS, :] = src[h].astype(F32)
            conv.append(_conv_silu(
                lambda j, win=win: win[SUBLANES - (CONV_WIDTH - 1) + j:
                                       SUBLANES - (CONV_WIDTH - 1) + j + META_ROWS, :],
                w_ref[:, ls]))
        _, k_m, _, kd_m, y_m, _ = _gdn_pointwise(conv[0], conv[0], conv[1], beta_m, g_m)
        k_bf = k_m.astype(BF16)
        probs.append((k_bf, k_bf, y_m.astype(BF16), beta_m, g_m, growm_ref[h]))
        kd_t.append(kd_m.T.astype(BF16))
    for h, (uw_m, _) in zip(heads, _gdn_groups(probs)):
        s_ref[h] = _dot(kd_t[h], uw_m[:, :HEAD_DIM].astype(BF16))


def _gdn_state0(proj_m, conv_wt, col_m, row_m4):
    heads_block = lambda blk: pl.BlockSpec((GDN_HEADS, META_ROWS, HEAD_DIM),
                                           lambda j, blk=blk: (blk, 0, 0))
    taps_block = lambda blk: pl.BlockSpec((CONV_WIDTH, GDN_WIDTH), lambda j, blk=blk: (0, blk))
    return pl.pallas_call(
        _gdn_state0_kernel,
        out_shape=jax.ShapeDtypeStruct((GDN_HEADS, HEAD_DIM, HEAD_DIM), F32),
        grid=(1,),
        in_specs=[heads_block(1), heads_block(2), taps_block(1), taps_block(2),
                  pl.BlockSpec((META_ROWS, LANES), lambda j: (0, 0)),
                  pl.BlockSpec((None, GDN_HEADS, 1, META_ROWS),
                               lambda j: (0, DECAY_LANE // GDN_HEADS, 0, 0))],
        out_specs=pl.BlockSpec((GDN_HEADS, HEAD_DIM, HEAD_DIM), lambda j: (0, 0, 0)),
        scratch_shapes=[pltpu.VMEM((2 * GDN_HEADS, META_ROWS + SUBLANES, HEAD_DIM), F32)],
        compiler_params=pltpu.CompilerParams(
            dimension_semantics=("arbitrary",), vmem_limit_bytes=VMEM_LIMIT),
        name="gdn_state0",
    )(proj_m, proj_m, conv_wt, conv_wt, col_m, row_m4)


def _gdn_kernel(*refs, seq):
    n_qkv = 3 * HEAD_PAIR
    src_refs = [refs[t * HEAD_PAIR:(t + 1) * HEAD_PAIR] for t in range(3)]
    z_ref = refs[n_qkv]
    meta_refs = [refs[n_qkv + 1 + t * HEAD_PAIR:n_qkv + 1 + (t + 1) * HEAD_PAIR] for t in range(3)]
    tap_refs = [refs[2 * n_qkv + 1 + t * HEAD_PAIR:2 * n_qkv + 1 + (t + 1) * HEAD_PAIR]
                for t in range(3)]
    (brow_ref, grow_ref, s0_ref, nw_ref, o_ref, pad_ref, qs_ref, ks_ref, y_ref, bb_ref, gb_ref,
     qd_ref, kdt_ref, dec_ref, u_ref, w_ref, aqk_ref, st_ref,
     qd_stage, kdt_stage, dec_stage) = refs[3 * n_qkv + 1:]
    pair = pl.program_id(1)
    slot = pair % REC_PAIRS
    n_blocks = seq // ROW_BLOCK
    n_groups = seq // GROUP
    cpg = GROUP // CHUNK
    hist = CONV_WIDTH - 1

    def pointwise_head(i, bank, hh):
        r0 = pl.multiple_of(i * ROW_BLOCK, ROW_BLOCK)
        rows = pl.ds(r0, ROW_BLOCK)
        convs = []
        for t in range(3):
            win = pad_ref.at[(bank * 3 + t) * HEAD_PAIR + hh]
            if isinstance(i, int) and i == 0:
                past = meta_refs[t][hh][META_ROWS - HIST_ROWS:META_ROWS, :]
            else:
                past = src_refs[t][hh][pl.ds(pl.multiple_of(r0 - HIST_ROWS, HIST_ROWS), HIST_ROWS), :]
            win[0:HIST_ROWS, :] = past.astype(F32)
            win[HIST_ROWS:HIST_ROWS + ROW_BLOCK, :] = src_refs[t][hh][rows, :].astype(F32)
            convs.append(_conv_silu(
                lambda j, win=win: win[HIST_ROWS - hist + j:HIST_ROWS - hist + j + ROW_BLOCK, :],
                tap_refs[t][hh][...]))
        beta_b = jnp.broadcast_to(brow_ref[hh, :, rows], (LANES, ROW_BLOCK)).T
        g_b = jnp.broadcast_to(grow_ref[hh, :, rows], (LANES, ROW_BLOCK)).T
        q, k, q_dec, k_dec, y, dec = _gdn_pointwise(convs[0], convs[1], convs[2], beta_b, g_b)
        qs_ref[hh, rows, :] = q.astype(BF16)
        ks_ref[hh, rows, :] = k.astype(BF16)
        y_ref[hh, rows, :] = y.astype(BF16)
        bb_ref[hh, rows, :] = beta_b
        gb_ref[hh, rows, :] = g_b
        qd_stage[hh, rows, :] = q_dec.astype(BF16)
        kdt_stage[hh, :, rows] = k_dec.T.astype(BF16)
        n_chunks = ROW_BLOCK // CHUNK
        dec_rows = [dec[c * CHUNK:c * CHUNK + 1, :] for c in range(n_chunks)]
        dec_rows.append(jnp.zeros((DEC_ROWS - n_chunks, LANES), F32))
        dec_stage[hh, pl.ds(pl.multiple_of(i * DEC_ROWS, DEC_ROWS), DEC_ROWS), :] = (
            jnp.concatenate(dec_rows, axis=0))

    def pointwise_thunks(i, bank):
        return [functools.partial(pointwise_head, i, bank, hh) for hh in range(HEAD_PAIR)]

    def first_pointwise(unroll):
        for i in range(unroll):
            for thunk in pointwise_thunks(i, i):
                thunk()

    def prep_groups(gi, fillers, unroll):
        groups = [gi * unroll + u for u in range(unroll) for _ in range(HEAD_PAIR)]
        keys = [(hh, pl.ds(pl.multiple_of((gi * unroll + u) * GROUP, GROUP), GROUP))
                for u in range(unroll) for hh in range(HEAD_PAIR)]
        probs = [(qs_ref[hh, rows, :], ks_ref[hh, rows, :], y_ref[hh, rows, :],
                  bb_ref[hh, rows, :], gb_ref[hh, rows, :], grow_ref[hh, :, rows])
                 for hh, rows in keys]
        for g, (hh, rows), (uw, a_qk) in zip(groups, keys, _gdn_groups(probs, fillers)):
            hs = slot * HEAD_PAIR + hh
            u_ref[hs, rows, :] = uw[:, :HEAD_DIM]
            w_ref[hs, rows, :] = uw[:, HEAD_DIM:].astype(BF16)
            aqk_ref[hs, rows, :] = a_qk.astype(BF16)
            drows = pl.ds(pl.multiple_of(g * DEC_ROWS, DEC_ROWS), DEC_ROWS)
            qd_ref[hs, rows, :] = qd_stage[hh, rows, :]
            kdt_ref[hs, :, rows] = kdt_stage[hh, :, rows]
            dec_ref[hs, drows, :] = dec_stage[hh, drows, :]

    def next_pointwise(gi, unroll):
        return [th for u in range(unroll) for th in pointwise_thunks((gi + 1) * unroll + u, u)]

    def rec_thunks(gi):
        r0 = pl.multiple_of(gi * GROUP, GROUP)
        rows = pl.ds(r0, GROUP)
        heads = range(REC_HEADS)
        outs = [[] for _ in heads]

        held = {}

        def read_state(c):
            crow = pl.ds(r0 + c * CHUNK, CHUNK)
            states = [st_ref[h] for h in heads]
            s_bf = [s.astype(BF16) for s in states]
            ws = [_dot(jnp.concatenate([w_ref[h, crow, :], qd_ref[h, crow, :]], axis=0), s_bf[h])
                  for h in heads]
            held[c] = (states, ws)

        def write_state(c):
            crow = pl.ds(r0 + c * CHUNK, CHUNK)
            states, ws = held.pop(c)
            v_new = [u_ref[h, crow, :] - ws[h][:CHUNK, :] for h in heads]
            zero = lambda n: jnp.zeros((n * CHUNK, HEAD_DIM), BF16)
            v_pad = [jnp.concatenate(([zero(c)] if c else []) + [v.astype(BF16)]
                                     + ([zero(cpg - 1 - c)] if c < cpg - 1 else []), axis=0)
                     for v in v_new]
            upd = [_dot(jnp.concatenate([aqk_ref[h, crow, :], kdt_ref[h, :, rows]], axis=0),
                        v_pad[h]) for h in heads]
            for h in heads:
                dec = dec_ref[h, pl.ds(gi * DEC_ROWS + c, 1), :]
                st_ref[h] = states[h] * dec + upd[h][CHUNK:, :]
                outs[h].append(ws[h][CHUNK:, :] + upd[h][:CHUNK, :])
            if c == cpg - 1:
                for h in heads:
                    o = _rms(jnp.concatenate(outs[h], axis=0), nw_ref[...])
                    o_ref[h, rows, :] = (o * _silu(z_ref[h, rows, :].astype(F32))).astype(BF16)

        return [functools.partial(half, c) for c in range(cpg) for half in (read_state, write_state)]

    def interleave(a, b):
        if not a or not b:
            return list(a) + list(b)
        out, j = [], 0
        for i, th in enumerate(a):
            out.append(th)
            while j < len(b) and (j + 1) * len(a) <= (i + 1) * len(b):
                out.append(b[j])
                j += 1
        return out + list(b[j:])

    @pl.when(slot != REC_PAIRS - 1)
    def _():
        unroll = PREP_UNROLL
        n_trips = n_groups // unroll
        first_pointwise(unroll)

        def prep_body(gi, _):
            prep_groups(gi, next_pointwise(gi, unroll), unroll)
            return 0
        lax.fori_loop(0, n_trips - 1, prep_body, 0)
        prep_groups(n_trips - 1, [], unroll)

    @pl.when(slot == REC_PAIRS - 1)
    def _():
        unroll = REC_PREP_UNROLL
        n_trips = n_groups // unroll
        st_ref[...] = s0_ref[...]
        first_pointwise(unroll)

        def rec_of_trip(t):
            return [th for u in range(unroll) for th in rec_thunks(t * unroll + u)]

        prep_groups(0, next_pointwise(0, unroll), unroll)

        def prep_rec_body(gi, _):
            prep_groups(gi, interleave(rec_of_trip(gi - 1), next_pointwise(gi, unroll)), unroll)
            return 0
        lax.fori_loop(1, n_trips - 1, prep_rec_body, 0)
        prep_groups(n_trips - 1, rec_of_trip(n_trips - 2), unroll)
        for thunk in rec_of_trip(n_trips - 1):
            thunk()


def _gdn(proj, proj_m, conv_wt, row4, state0, norm_w):
    _, b, seq, _ = proj.shape
    n_pairs = GDN_HEADS // HEAD_PAIR

    def head_specs(shape, imap):
        return [pl.BlockSpec(shape, functools.partial(imap, t * GDN_HEADS + hh))
                for t in range(3) for hh in range(HEAD_PAIR)]

    src_specs = head_specs((None, None, seq, HEAD_DIM),
                           lambda off, i, j: (off + HEAD_PAIR * j, i, 0, 0))
    meta_specs = head_specs((None, META_ROWS, HEAD_DIM), lambda off, i, j: (off + HEAD_PAIR * j, 0, 0))
    tap_specs = head_specs((CONV_WIDTH, HEAD_DIM), lambda off, i, j: (0, off + HEAD_PAIR * j))
    n_qkv = 3 * HEAD_PAIR
    z_block0 = 3 * GDN_HEADS // REC_HEADS

    per_pair = lambda width, dt: pltpu.VMEM((HEAD_PAIR, seq, width), dt)
    per_rec = lambda width, dt: pltpu.VMEM((REC_HEADS, seq, width), dt)
    return pl.pallas_call(
        functools.partial(_gdn_kernel, seq=seq),
        out_shape=jax.ShapeDtypeStruct((GDN_HEADS, b, seq, HEAD_DIM), BF16),
        grid=(b, n_pairs),
        in_specs=src_specs
        + [pl.BlockSpec((REC_HEADS, None, seq, HEAD_DIM),
                        lambda i, j: (z_block0 + j // REC_PAIRS, i, 0, 0))]
        + meta_specs + tap_specs
        + [pl.BlockSpec((None, HEAD_PAIR, 1, seq),
                        lambda i, j: (i, BETA_LANE // HEAD_PAIR + j, 0, 0)),
           pl.BlockSpec((None, HEAD_PAIR, 1, seq),
                        lambda i, j: (i, DECAY_LANE // HEAD_PAIR + j, 0, 0)),
           pl.BlockSpec((REC_HEADS, HEAD_DIM, HEAD_DIM), lambda i, j: (j // REC_PAIRS, 0, 0)),
           pl.BlockSpec((1, HEAD_DIM), lambda i, j: (0, 0))],
        out_specs=pl.BlockSpec((REC_HEADS, None, seq, HEAD_DIM),
                               lambda i, j: (j // REC_PAIRS, i, 0, 0)),
        scratch_shapes=[pltpu.VMEM((max(PREP_UNROLL, REC_PREP_UNROLL) * n_qkv, HIST_ROWS + ROW_BLOCK,
                                    HEAD_DIM), F32),
                        per_pair(HEAD_DIM, BF16), per_pair(HEAD_DIM, BF16),
                        per_pair(2 * HEAD_DIM, BF16),
                        per_pair(HEAD_DIM, F32), per_pair(HEAD_DIM, F32),
                        per_rec(HEAD_DIM, BF16),
                        pltpu.VMEM((REC_HEADS, HEAD_DIM, seq), BF16),
                        pltpu.VMEM((REC_HEADS, seq // GROUP * DEC_ROWS, HEAD_DIM), F32),
                        per_rec(HEAD_DIM, F32), per_rec(HEAD_DIM, BF16), per_rec(GROUP, BF16),
                        pltpu.VMEM((REC_HEADS, HEAD_DIM, HEAD_DIM), F32),
                        per_pair(HEAD_DIM, BF16),
                        pltpu.VMEM((HEAD_PAIR, HEAD_DIM, seq), BF16),
                        pltpu.VMEM((HEAD_PAIR, seq // GROUP * DEC_ROWS, HEAD_DIM), F32)],
        compiler_params=pltpu.CompilerParams(
            dimension_semantics=("parallel", "arbitrary"), vmem_limit_bytes=VMEM_LIMIT),
        name="gdn",
    )(*([proj] * n_qkv), proj, *([proj_m] * n_qkv), *([conv_wt] * n_qkv),
      row4, row4, state0, norm_w)


def _fox_kernel(q_ref, k_ref, v_ref, g_ref, km_ref, vm_ref, colm_ref, crow_ref,
                qw_ref, kw_ref, o_ref, qt_ref, ka_ref, kam_ref, vt_ref, vtm_ref,
                m_ref, l_ref, acc_ref, *, seq):
    n_blocks = seq // ROW_BLOCK
    aug_r = lax.broadcasted_iota(jnp.int32, (LANES, ROW_BLOCK), 0)

    def key_aug(ck, valid=None):
        hi, mid, lo = _split3(ck)
        lane = lax.broadcasted_iota(jnp.int32, ck.shape, 1)
        neg_hi = -hi.astype(F32)
        if valid is not None:
            neg_hi = jnp.where(valid, neg_hi, NEG_BIG)
        blk = jnp.where(lane < 3, 1.0,
                        jnp.where(lane == 3, neg_hi,
                                  jnp.where(lane == 4, -mid.astype(F32),
                                            jnp.where(lane == 5, -lo.astype(F32), 0.0))))
        return blk.astype(BF16)

    def pro_body(i, _):
        r0 = pl.multiple_of(i * ROW_BLOCK, ROW_BLOCK)
        rows = pl.ds(r0, ROW_BLOCK)
        for hd in range(FOX_GROUP):
            qn = _rms(q_ref[hd, rows, :].astype(F32), qw_ref[...]) * (HEAD_DIM ** -0.5 * LOG2E)
            qt_ref[hd, 0:HEAD_DIM, rows] = qn.T.astype(BF16)
            hi, mid, lo = _split3(crow_ref[hd, :, rows] * LOG2E)
            aug = jnp.where(aug_r == 0, hi.astype(F32),
                            jnp.where(aug_r == 1, mid.astype(F32),
                                      jnp.where(aug_r == 2, lo.astype(F32),
                                                jnp.where(aug_r < 6, 1.0, 0.0))))
            qt_ref[hd, HEAD_DIM:2 * HEAD_DIM, rows] = aug.astype(BF16)
            ka_ref[hd, rows, 0:HEAD_DIM] = _rms(k_ref[hd, rows, :].astype(F32),
                                                kw_ref[...]).astype(BF16)
            aug_k = jnp.where(aug_r < 3, 1.0,
                              jnp.where(aug_r == 3, -hi.astype(F32),
                                        jnp.where(aug_r == 4, -mid.astype(F32),
                                                  jnp.where(aug_r == 5, -lo.astype(F32), 0.0))))
            ka_ref[hd, rows, HEAD_DIM:2 * HEAD_DIM] = aug_k.T.astype(BF16)
            vt_ref[hd, :, rows] = v_ref[hd, rows, :].astype(F32).T.astype(BF16)
        return 0
    lax.fori_loop(0, n_blocks, pro_body, 0, unroll=BLOCK_UNROLL // FOX_GROUP)

    mrow = lax.broadcasted_iota(jnp.int32, (META_ROWS, LANES), 0)
    for hd in range(FOX_GROUP):
        head = pl.program_id(1) * FOX_GROUP + hd
        kam_ref[hd, :, 0:HEAD_DIM] = _rms(km_ref[hd].astype(F32), kw_ref[...]).astype(BF16)
        ck_m = _lane_bcast(colm_ref[...], FORGET_LANE + head) * LOG2E
        kam_ref[hd, :, HEAD_DIM:2 * HEAD_DIM] = key_aug(ck_m, mrow >= META_PAD)
        vtm_ref[hd] = vm_ref[hd].astype(F32).T.astype(BF16)

    kidx = lax.broadcasted_iota(jnp.int32, (FOX_TK, FOX_TK), 0)
    qidx = lax.broadcasted_iota(jnp.int32, (FOX_TK, FOX_TK), 1)
    diag_ok = kidx <= qidx

    items = []
    for kj in range(-1, seq // FOX_TK):
        lane0 = max(kj, 0) * FOX_TK
        for p0 in range(lane0, seq, FOX_PIECE):
            for hd in range(FOX_GROUP):
                items.append((hd, kj, slice(p0, min(p0 + FOX_PIECE, seq)), p0 == lane0))

    def scores(item):
        hd, kj, ln, leads = item
        k_aug = kam_ref[hd] if kj < 0 else ka_ref[hd, kj * FOX_TK:(kj + 1) * FOX_TK, :]
        s = _dot(k_aug, qt_ref[hd, :, ln])
        if kj >= 0 and leads:
            masked = jnp.where(diag_ok, s[:, :FOX_TK], NEG_BIG)
            s = jnp.concatenate([masked, s[:, FOX_TK:]], axis=1) if s.shape[1] > FOX_TK else masked
        return s

    def softmax_stats(item, s):
        hd, kj, ln, _ = item
        top = jnp.max(s, axis=0, keepdims=True)
        if kj < 0:
            m_new, alpha = top, None
        else:
            m_old = m_ref[hd, :, ln]
            m_new = jnp.maximum(m_old, top)
            alpha = jnp.exp2(m_old - m_new)
        p = jnp.exp2(s - m_new)
        psum = jnp.sum(p, axis=0, keepdims=True)
        m_ref[hd, :, ln] = m_new
        l_ref[hd, :, ln] = psum if kj < 0 else alpha * l_ref[hd, :, ln] + psum
        return p.astype(BF16), alpha

    def values(item, p, alpha):
        hd, kj, ln, _ = item
        v_t = vtm_ref[hd] if kj < 0 else vt_ref[hd, :, kj * FOX_TK:(kj + 1) * FOX_TK]
        pv = _dot(v_t, p)
        acc_ref[hd, :, ln] = pv if kj < 0 else alpha * acc_ref[hd, :, ln] + pv

    s_cur = scores(items[0])
    p_cur = None
    for t in range(len(items) + 1):
        s_next = scores(items[t + 1]) if t + 1 < len(items) else None
        p_next = softmax_stats(items[t], s_cur) if t < len(items) else None
        if p_cur is not None:
            values(items[t - 1], *p_cur)
        s_cur, p_cur = s_next, p_next

    def out_body(i, _):
        r0 = pl.multiple_of(i * ROW_BLOCK, ROW_BLOCK)
        rows = pl.ds(r0, ROW_BLOCK)
        for hd in range(FOX_GROUP):
            out_t = acc_ref[hd, :, rows] * (1.0 / l_ref[hd, :, rows])
            o_ref[hd, rows, :] = (out_t.T * _silu(g_ref[hd, rows, :].astype(F32))).astype(BF16)
        return 0
    lax.fori_loop(0, n_blocks, out_body, 0, unroll=BLOCK_UNROLL // FOX_GROUP)


def _fox(proj, proj_m, col_m, row4, q_w, k_w):
    _, b, seq, _ = proj.shape
    base = 4 * GDN_HEADS // FOX_GROUP
    hb = FOX_HEADS // FOX_GROUP

    def head_block(off):
        return pl.BlockSpec((FOX_GROUP, None, seq, HEAD_DIM),
                            lambda i, j, off=off: (off + j, i, 0, 0))

    def meta_block(off):
        return pl.BlockSpec((FOX_GROUP, META_ROWS, HEAD_DIM), lambda i, j, off=off: (off + j, 0, 0))

    per_head = lambda shape, dt: pltpu.VMEM((FOX_GROUP,) + shape, dt)
    return pl.pallas_call(
        functools.partial(_fox_kernel, seq=seq),
        out_shape=jax.ShapeDtypeStruct((FOX_HEADS, b, seq, HEAD_DIM), BF16),
        grid=(b, FOX_HEADS // FOX_GROUP),
        in_specs=[head_block(base), head_block(base + hb), head_block(base + 2 * hb),
                  head_block(base + 3 * hb),
                  meta_block(base + hb), meta_block(base + 2 * hb),
                  pl.BlockSpec((META_ROWS, LANES), lambda i, j: (0, 0)),
                  pl.BlockSpec((None, FOX_GROUP, 1, seq),
                               lambda i, j: (i, FORGET_LANE // FOX_GROUP + j, 0, 0)),
                  pl.BlockSpec((1, HEAD_DIM), lambda i, j: (0, 0)),
                  pl.BlockSpec((1, HEAD_DIM), lambda i, j: (0, 0))],
        out_specs=pl.BlockSpec((FOX_GROUP, None, seq, HEAD_DIM), lambda i, j: (j, i, 0, 0)),
        scratch_shapes=[per_head((2 * HEAD_DIM, seq), BF16),
                        per_head((seq, 2 * HEAD_DIM), BF16),
                        per_head((META_ROWS, 2 * HEAD_DIM), BF16),
                        per_head((HEAD_DIM, seq), BF16),
                        per_head((HEAD_DIM, META_ROWS), BF16),
                        per_head((1, seq), F32), per_head((1, seq), F32),
                        per_head((HEAD_DIM, seq), F32)],
        compiler_params=pltpu.CompilerParams(
            dimension_semantics=("parallel", "arbitrary"), vmem_limit_bytes=VMEM_LIMIT),
        name="fox",
    )(proj, proj, proj, proj, proj_m, proj_m, col_m, row4, q_w, k_w)


def _out_proj_kernel(mg_ref, mf_ref, wg_ref, wf_ref, pw_ref, x_ref, o_ref):
    for r in range(o_ref.shape[0] // OUT_PROJ_CHUNK):
        rows = slice(r * OUT_PROJ_CHUNK, (r + 1) * OUT_PROJ_CHUNK)

        def rows_of(m_ref):
            return jnp.concatenate([m_ref[h, rows, :] for h in range(m_ref.shape[0])], axis=1)
        out = _dot(rows_of(mg_ref), wg_ref[...]) + _dot(rows_of(mf_ref), wf_ref[...])
        o_ref[rows, :] = x_ref[rows, :] + _rms(out, pw_ref[...])


def _out_proj(mg, mf, w_g, w_f, post_w, x2d, tm):
    m, d = x2d.shape
    return pl.pallas_call(
        _out_proj_kernel,
        out_shape=jax.ShapeDtypeStruct((m, d), F32),
        grid=(m // tm,),
        in_specs=[pl.BlockSpec((GDN_HEADS, tm, HEAD_DIM), lambda i: (0, i, 0)),
                  pl.BlockSpec((FOX_HEADS, tm, HEAD_DIM), lambda i: (0, i, 0)),
                  pl.BlockSpec((GDN_WIDTH, d), lambda i: (0, 0), pipeline_mode=pl.Buffered(1)),
                  pl.BlockSpec((FOX_WIDTH, d), lambda i: (0, 0), pipeline_mode=pl.Buffered(1)),
                  pl.BlockSpec((1, d), lambda i: (0, 0)),
                  pl.BlockSpec((tm, d), lambda i: (i, 0))],
        out_specs=pl.BlockSpec((tm, d), lambda i: (i, 0)),
        compiler_params=pltpu.CompilerParams(
            dimension_semantics=("parallel",), vmem_limit_bytes=VMEM_LIMIT),
        name="out_proj",
    )(mg, mf, w_g, w_f, post_w, x2d)


def _tile(total, want):
    t = min(total, want)
    while total % t:
        t //= 2
    return t


def _layer(x, meta_pad, pre_w, w_in, conv_w, a_log, dt_bias, gdn_norm_w, fox_q_w, fox_k_w,
           fox_f_bias, w_out, post_w):
    b, seq, d = x.shape
    assert seq % (GROUP * PREP_UNROLL) == 0 and seq % FOX_TK == 0 and seq % ROW_BLOCK == 0
    gw, fw = GDN_WIDTH, FOX_WIDTH
    o_gb = 4 * gw
    o_f = o_gb + 2 * GDN_HEADS
    o_ff = o_f + 4 * fw
    w_t = w_in.T
    w_gate = jnp.concatenate(
        [w_t[o_gb:o_f], w_t[o_ff:],
         jnp.zeros((GATE_WIDTH - 2 * GDN_HEADS - FOX_HEADS, d), w_in.dtype)], axis=0)
    pre_w2 = pre_w[None]
    w_main, proj_m, gate_m = _w_prep(w_t, meta_pad, pre_w2, w_gate, MAIN_WIDTH, o_gb, o_f - o_gb)
    zpad = jnp.zeros((GATE_WIDTH - FORGET_LANE - FOX_HEADS,), F32)
    add_vec = jnp.concatenate([jnp.zeros((GDN_HEADS,), F32), dt_bias, fox_f_bias, zpad])[None]
    alog_vec = jnp.concatenate([jnp.zeros((GDN_HEADS,), F32), a_log,
                                jnp.zeros((FOX_HEADS,), F32), zpad])[None]

    x2d = x.reshape(b * seq, d)
    proj, gate = _in_proj(x2d, pre_w2, w_main, w_gate, _tile(b * seq, IN_PROJ_TM), IN_PROJ_TN)
    proj = proj.reshape(MAIN_WIDTH // HEAD_DIM, b, seq, HEAD_DIM)

    row4 = _gate_rows(gate.reshape(b, seq, GATE_WIDTH), add_vec, alog_vec)
    col_m, row_m = _gate_meta(gate_m, add_vec, alog_vec)
    row_m4 = row_m.reshape(1, GATE_ROWS, 1, META_ROWS)

    conv_wt = conv_w.T
    state0 = _gdn_state0(proj_m, conv_wt, col_m, row_m4)
    o_gdn = _gdn(proj, proj_m, conv_wt, row4, state0, gdn_norm_w[None])
    o_fox = _fox(proj, proj_m, col_m, row4, fox_q_w[None], fox_k_w[None])

    w_out_b = w_out.astype(BF16)
    out = _out_proj(o_gdn.reshape(GDN_HEADS, b * seq, HEAD_DIM),
                    o_fox.reshape(FOX_HEADS, b * seq, HEAD_DIM),
                    w_out_b[:gw], w_out_b[gw:], post_w[None], x2d, _tile(b * seq, OUT_PROJ_TM))
    return out.reshape(b, seq, d)


def kernel(x, meta_tokens, pre_norm_w, w_in, conv_w, a_log, dt_bias, gdn_norm_w, fox_q_norm_w,
           fox_k_norm_w, fox_f_bias, w_out, post_norm_w):
    assert pre_norm_w.shape[0] == 1, "single-layer stack"
    meta_pad = jnp.concatenate(
        [jnp.zeros((META_PAD, x.shape[-1]), x.dtype), meta_tokens.astype(x.dtype)], axis=0)
    return _layer(x, meta_pad, pre_norm_w[0], w_in[0], conv_w[0], a_log[0], dt_bias[0],
                  gdn_norm_w[0], fox_q_norm_w[0], fox_k_norm_w[0], fox_f_bias[0], w_out[0],
                  post_norm_w[0])
```

```python
import functools
import math

import jax
import jax.numpy as jnp
from jax import lax
from jax.experimental import pallas as pl
from jax.experimental.pallas import tpu as pltpu

N_META = 16
HEAD_DIM = 128
GDN_HEADS = 8
FOX_HEADS = 8
GDN_WIDTH = GDN_HEADS * HEAD_DIM
FOX_WIDTH = FOX_HEADS * HEAD_DIM
CONV_WIDTH = 4
CHUNK = 64
EPS = 1e-6

LANES = 128
SUBLANES = 8
MXU_DIM = 256
MAIN_WIDTH = 4 * GDN_WIDTH + 4 * FOX_WIDTH
GATE_WIDTH = LANES
BETA_LANE, DECAY_LANE, FORGET_LANE = 0, GDN_HEADS, 2 * GDN_HEADS
GATE_ROWS = 32
META_ROWS = CHUNK
META_PAD = META_ROWS - N_META
GROUP = MXU_DIM
ROW_BLOCK = 256
HIST_ROWS = 2 * SUBLANES
IN_PROJ_TM, IN_PROJ_TN = 1024, 2048
W_PREP_ROWS = 1024
OUT_PROJ_TM = 1024
OUT_PROJ_CHUNK = 512
HEAD_PAIR = 2
REC_PAIRS = 2
REC_HEADS = REC_PAIRS * HEAD_PAIR
PREP_UNROLL = 2
REC_PREP_UNROLL = 1
DEC_ROWS = SUBLANES
BLOCK_UNROLL = 16
FOX_GROUP = 4
FOX_TK = MXU_DIM
FOX_PIECE = 2 * MXU_DIM
VMEM_LIMIT = 56 * 1024 * 1024

F32 = jnp.float32
BF16 = jnp.bfloat16
NEG_BIG = -1e30
LOG2E = math.log2(math.e)


def _dot(a, b):
    return jnp.dot(a, b, preferred_element_type=F32)


def _dot_nt(a, b):
    return lax.dot_general(a, b, (((1,), (1,)), ((), ())), preferred_element_type=F32)


def _split3(x):
    hi = x.astype(BF16)
    r1 = x - hi.astype(F32)
    mid = r1.astype(BF16)
    lo = (r1 - mid.astype(F32)).astype(BF16)
    return hi, mid, lo


def _dot_exact_rhs01(parts, m):
    return _dot(parts[0], m) + _dot(parts[1], m) + _dot(parts[2], m)


def _dot_exact_lhs01(m, parts):
    return _dot(m, parts[0]) + _dot(m, parts[1]) + _dot(m, parts[2])


def _lane_bcast(col_tile, lane):
    sel = (lax.broadcasted_iota(jnp.int32, (LANES, LANES), 0) == lane).astype(BF16)
    return _dot_exact_rhs01(_split3(col_tile), sel)


def _chunk_of(idx):
    return jnp.right_shift(idx, CHUNK.bit_length() - 1)


def _rms(x, w):
    return x * lax.rsqrt(jnp.mean(x * x, axis=-1, keepdims=True) + EPS) * w


def _silu(x):
    half = 0.5 * x
    return half + half * jnp.tanh(half)


def _softplus(x):
    return jnp.maximum(x, 0.0) + jnp.log1p(jnp.exp(-jnp.abs(x)))


def _w_prep_kernel(w_hbm, meta_ref, nw_ref, wg_ref, o_ref, om_ref, ogm_ref, buf, sem, xm_ref,
                   *, rows, shift_from, shift):
    r = pl.program_id(0)

    def fetch(step, slot):
        start = step * rows
        src = pl.multiple_of(start + jnp.where(start >= shift_from, shift, 0), SUBLANES)
        return pltpu.make_async_copy(w_hbm.at[pl.ds(src, rows), :], buf.at[slot], sem.at[slot])

    @pl.when(r == 0)
    def _():
        fetch(0, 0).start()
        xm = _rms(meta_ref[...], nw_ref[...]).astype(BF16)
        xm_ref[...] = xm
        ogm_ref[...] = _dot_nt(xm, wg_ref[...].astype(BF16))

    @pl.when(r + 1 < pl.num_programs(0))
    def _():
        fetch(r + 1, (r + 1) % 2).start()

    fetch(r, r % 2).wait()
    w_bf = buf[r % 2].astype(BF16)
    o_ref[...] = w_bf
    res = _dot_nt(xm_ref[...], w_bf)
    for hd in range(om_ref.shape[0]):
        om_ref[hd] = res[:, hd * HEAD_DIM:(hd + 1) * HEAD_DIM].astype(om_ref.dtype)


def _w_prep(w_t, meta_pad, norm_w, w_gate, n_rows, shift_from, shift):
    d = w_t.shape[1]
    rows = W_PREP_ROWS
    return pl.pallas_call(
        functools.partial(_w_prep_kernel, rows=rows, shift_from=shift_from, shift=shift),
        out_shape=(jax.ShapeDtypeStruct((n_rows, d), BF16),
                   jax.ShapeDtypeStruct((n_rows // HEAD_DIM, META_ROWS, HEAD_DIM), BF16),
                   jax.ShapeDtypeStruct((META_ROWS, GATE_WIDTH), F32)),
        grid=(n_rows // rows,),
        in_specs=[pl.BlockSpec(memory_space=pl.ANY),
                  pl.BlockSpec((META_ROWS, d), lambda r: (0, 0)),
                  pl.BlockSpec((1, d), lambda r: (0, 0)),
                  pl.BlockSpec((GATE_WIDTH, d), lambda r: (0, 0))],
        out_specs=(pl.BlockSpec((rows, d), lambda r: (r, 0)),
                   pl.BlockSpec((rows // HEAD_DIM, META_ROWS, HEAD_DIM), lambda r: (r, 0, 0)),
                   pl.BlockSpec((META_ROWS, GATE_WIDTH), lambda r: (0, 0))),
        scratch_shapes=[pltpu.VMEM((2, rows, d), F32), pltpu.SemaphoreType.DMA((2,)),
                        pltpu.VMEM((META_ROWS, d), BF16)],
        compiler_params=pltpu.CompilerParams(
            dimension_semantics=("arbitrary",), vmem_limit_bytes=VMEM_LIMIT),
        name="w_prep",
    )(w_t, meta_pad, norm_w, w_gate)


def _in_proj_kernel(x_ref, nw_ref, w_ref, wg_ref, o_ref, og_ref, xn_ref):
    @pl.when(pl.program_id(1) == 0)
    def _():
        xn = _rms(x_ref[...], nw_ref[...]).astype(BF16)
        xn_ref[...] = xn
        og_ref[...] = _dot_nt(xn, wg_ref[...].astype(BF16))

    res = _dot_nt(xn_ref[...], w_ref[...])
    for hd in range(o_ref.shape[0]):
        o_ref[hd] = res[:, hd * HEAD_DIM:(hd + 1) * HEAD_DIM].astype(o_ref.dtype)


def _in_proj(x2d, norm_w, w_main, w_gate, tm, tn):
    m, d = x2d.shape
    n = w_main.shape[0]
    return pl.pallas_call(
        _in_proj_kernel,
        out_shape=(jax.ShapeDtypeStruct((n // HEAD_DIM, m, HEAD_DIM), BF16),
                   jax.ShapeDtypeStruct((m, GATE_WIDTH), F32)),
        grid=(m // tm, n // tn),
        in_specs=[pl.BlockSpec((tm, d), lambda i, j: (i, 0)),
                  pl.BlockSpec((1, d), lambda i, j: (0, 0)),
                  pl.BlockSpec((tn, d), lambda i, j: (j, 0)),
                  pl.BlockSpec((GATE_WIDTH, d), lambda i, j: (0, 0))],
        out_specs=(pl.BlockSpec((tn // HEAD_DIM, tm, HEAD_DIM), lambda i, j: (j, i, 0)),
                   pl.BlockSpec((tm, GATE_WIDTH), lambda i, j: (i, 0))),
        scratch_shapes=[pltpu.VMEM((tm, d), BF16)],
        compiler_params=pltpu.CompilerParams(
            dimension_semantics=("parallel", "arbitrary"), vmem_limit_bytes=VMEM_LIMIT),
        name="in_proj",
    )(x2d, norm_w, w_main, w_gate)


def _gate_meta_kernel(t_ref, add_ref, alog_ref, col_ref, row_ref):
    lane = lax.broadcasted_iota(jnp.int32, (META_ROWS, LANES), 1)
    row = lax.broadcasted_iota(jnp.int32, (META_ROWS, LANES), 0)
    ri = lax.broadcasted_iota(jnp.int32, (META_ROWS, META_ROWS), 0)
    ci = lax.broadcasted_iota(jnp.int32, (META_ROWS, META_ROWS), 1)
    tri = (ci <= ri).astype(BF16)
    is_beta = lane < DECAY_LANE
    is_decay = (lane >= DECAY_LANE) & (lane < FORGET_LANE)
    is_forget = (lane >= FORGET_LANE) & (lane < FORGET_LANE + FOX_HEADS)
    real = row >= META_PAD

    t = t_ref[...]
    ta = t + add_ref[...]
    beta = jnp.where(real, 1.0 / (1.0 + jnp.exp(-t)), 0.0)
    g = -jnp.exp(alog_ref[...]) * _softplus(ta)
    logf = -_softplus(-ta)
    val = jnp.where(real, jnp.where(is_decay, g, jnp.where(is_forget, logf, 0.0)), 0.0)
    cum = _dot_exact_lhs01(tri, _split3(val))
    to_come = cum - cum[META_ROWS - 1:META_ROWS, :]
    res = jnp.where(is_beta, beta, jnp.where(is_decay, cum, to_come))
    col_ref[...] = res
    row_ref[...] = res.T[:GATE_ROWS, :]


def _gate_rows_kernel(t_ref, add_ref, alog_ref, row_ref, *, rows):
    blk = ROW_BLOCK
    row = lax.broadcasted_iota(jnp.int32, (GATE_ROWS, blk), 0)
    si = lax.broadcasted_iota(jnp.int32, (blk, blk), 0)
    ti = lax.broadcasted_iota(jnp.int32, (blk, blk), 1)
    tri = (si <= ti).astype(BF16)
    tri_chunk = ((si <= ti) & (_chunk_of(si) == _chunk_of(ti))).astype(BF16)
    is_beta = row < DECAY_LANE
    is_decay = (row >= DECAY_LANE) & (row < FORGET_LANE)
    is_forget = (row >= FORGET_LANE) & (row < FORGET_LANE + FOX_HEADS)
    add = jnp.concatenate([add_ref[...]] * (blk // LANES), axis=1)
    neg_rate = -jnp.exp(jnp.concatenate([alog_ref[...]] * (blk // LANES), axis=1))

    carry = jnp.zeros((GATE_ROWS, 1), F32)
    for r in range(rows // blk):
        t = t_ref[r * blk:(r + 1) * blk, :].T[:GATE_ROWS, :]
        ta = t + add
        beta = 1.0 / (1.0 + jnp.exp(-t))
        val = jnp.where(is_decay, neg_rate * _softplus(ta), jnp.where(is_forget, -_softplus(-ta), 0.0))
        parts = _split3(val)
        cum_chunk = _dot_exact_rhs01(parts, tri_chunk)
        cum_all = _dot_exact_rhs01(parts, tri) + carry
        carry = cum_all[:, blk - 1:blk]
        row_ref[:, 0, r * blk:(r + 1) * blk] = jnp.where(is_beta, beta,
                                                         jnp.where(is_decay, cum_chunk, cum_all))


def _gate_rows(gate3d, add_vec, alog_vec):
    b, rows, _ = gate3d.shape
    per_row = lambda v: jnp.broadcast_to(v[0, :GATE_ROWS, None], (GATE_ROWS, LANES))
    return pl.pallas_call(
        functools.partial(_gate_rows_kernel, rows=rows),
        out_shape=jax.ShapeDtypeStruct((b, GATE_ROWS, 1, rows), F32),
        grid=(b,),
        in_specs=[pl.BlockSpec((None, rows, LANES), lambda i: (i, 0, 0)),
                  pl.BlockSpec((GATE_ROWS, LANES), lambda i: (0, 0)),
                  pl.BlockSpec((GATE_ROWS, LANES), lambda i: (0, 0))],
        out_specs=pl.BlockSpec((None, GATE_ROWS, 1, rows), lambda i: (i, 0, 0, 0)),
        compiler_params=pltpu.CompilerParams(
            dimension_semantics=("parallel",), vmem_limit_bytes=VMEM_LIMIT),
        name="gate_rows",
    )(gate3d, per_row(add_vec), per_row(alog_vec))


def _gate_meta(gate_m, add_vec, alog_vec):
    return pl.pallas_call(
        _gate_meta_kernel,
        out_shape=(jax.ShapeDtypeStruct((META_ROWS, LANES), F32),
                   jax.ShapeDtypeStruct((GATE_ROWS, META_ROWS), F32)),
        grid=(1,),
        in_specs=[pl.BlockSpec((META_ROWS, LANES), lambda i: (0, 0)),
                  pl.BlockSpec((1, LANES), lambda i: (0, 0)),
                  pl.BlockSpec((1, LANES), lambda i: (0, 0))],
        out_specs=(pl.BlockSpec((META_ROWS, LANES), lambda i: (0, 0)),
                   pl.BlockSpec((GATE_ROWS, META_ROWS), lambda i: (0, 0))),
        compiler_params=pltpu.CompilerParams(
            dimension_semantics=("arbitrary",), vmem_limit_bytes=VMEM_LIMIT),
        name="gate_meta",
    )(gate_m, add_vec, alog_vec)


def _conv_silu(load_rows, w):
    y = load_rows(0) * w[0:1, :]
    for j in range(1, CONV_WIDTH):
        y = y + load_rows(j) * w[j:j + 1, :]
    return _silu(y)


def _l2norm(x):
    return x * lax.rsqrt(jnp.sum(x * x, axis=-1, keepdims=True) + EPS)


def _gdn_pointwise(q, k, v, beta_b, g_b):
    r = q.shape[0]
    q = q * (lax.rsqrt(jnp.sum(q * q, axis=-1, keepdims=True) + EPS) * (HEAD_DIM ** -0.5))
    k = _l2norm(k)
    g3 = g_b.reshape(r // CHUNK, CHUNK, LANES)
    g_last = jnp.broadcast_to(g3[:, CHUNK - 1:CHUNK, :], g3.shape).reshape(r, LANES)
    e_g = jnp.exp(g_b)
    q_dec = q * e_g
    k_dec = k * jnp.exp(g_last - g_b)
    y = jnp.concatenate([v * beta_b, k * (beta_b * e_g)], axis=1)
    return q, k, q_dec, k_dec, y, jnp.exp(g_last)


def _gdn_groups(probs, fillers=()):
    r = probs[0][0].shape[0]
    ri = lax.broadcasted_iota(jnp.int32, (r, r), 0)
    ci = lax.broadcasted_iota(jnp.int32, (r, r), 1)
    same = _chunk_of(ri) == _chunk_of(ci)
    causal = same & (ci <= ri)
    strict = same & (ci < ri)

    def widen(t):
        return jnp.concatenate([t] * (r // LANES), axis=1) if r >= LANES else t[:, :r]

    fillers = list(fillers)
    n_ticks = 2 * CHUNK.bit_length() - 3
    per_tick = -(-len(fillers) // n_ticks)

    def tick():
        for thunk in fillers[:per_tick]:
            thunk()
        del fillers[:per_tick]

    kks = [_dot_nt(k, k) for _, k, _, _, _, _ in probs]
    qks = [_dot_nt(q, k) for q, k, _, _, _, _ in probs]
    tick()
    dmats = [jnp.where(causal, jnp.exp(jnp.where(causal, widen(g_b) - g_row, 0.0)), 0.0)
             for _, _, _, _, g_b, g_row in probs]
    xs = [jnp.where(strict, widen(p[3]) * kk * d, 0.0).astype(BF16)
          for p, kk, d in zip(probs, kks, dmats)]
    a_qks = [qk * d for qk, d in zip(qks, dmats)]
    eye = (ri == ci).astype(BF16)
    zs = [_dot(eye - x, p[2]) for p, x in zip(probs, xs)]
    span = 2
    while span < CHUNK:
        xs = [_dot(x, x).astype(BF16) for x in xs]
        tick()
        zs = [z + _dot(x, z.astype(BF16)) for x, z in zip(xs, zs)]
        tick()
        span *= 2
    assert not fillers
    return list(zip(zs, a_qks))


def _gdn_state0_kernel(km_ref, vm_ref, wk_ref, wv_ref, colm_ref, growm_ref, s_ref, pad_ref):
    heads = range(GDN_HEADS)
    probs, kd_t = [], []
    for h in heads:
        ls = slice(h * HEAD_DIM, (h + 1) * HEAD_DIM)
        beta_m = _lane_bcast(colm_ref[...], BETA_LANE + h)
        g_m = _lane_bcast(colm_ref[...], DECAY_LANE + h)
        conv = []
        for t, (src, w_ref) in enumerate(((km_ref, wk_ref), (vm_ref, wv_ref))):
            win = pad_ref.at[2 * h + t]
            win[0:SUBLANES, :] = jnp.zeros((SUBLANES, LANES), F32)
            win[SUBLANES:SUBLANES + META_ROWS, :] = src[h].astype(F32)
            conv.append(_conv_silu(
                lambda j, win=win: win[SUBLANES - (CONV_WIDTH - 1) + j:
                                       SUBLANES - (CONV_WIDTH - 1) + j + META_ROWS, :],
                w_ref[:, ls]))
        _, k_m, _, kd_m, y_m, _ = _gdn_pointwise(conv[0], conv[0], conv[1], beta_m, g_m)
        k_bf = k_m.astype(BF16)
        probs.append((k_bf, k_bf, y_m.astype(BF16), beta_m, g_m, growm_ref[h]))
        kd_t.append(kd_m.T.astype(BF16))
    for h, (uw_m, _) in zip(heads, _gdn_groups(probs)):
        s_ref[h] = _dot(kd_t[h], uw_m[:, :HEAD_DIM].astype(BF16))


def _gdn_state0(proj_m, conv_wt, col_m, row_m4):
    heads_block = lambda blk: pl.BlockSpec((GDN_HEADS, META_ROWS, HEAD_DIM),
                                           lambda j, blk=blk: (blk, 0, 0))
    taps_block = lambda blk: pl.BlockSpec((CONV_WIDTH, GDN_WIDTH), lambda j, blk=blk: (0, blk))
    return pl.pallas_call(
        _gdn_state0_kernel,
        out_shape=jax.ShapeDtypeStruct((GDN_HEADS, HEAD_DIM, HEAD_DIM), F32),
        grid=(1,),
        in_specs=[heads_block(1), heads_block(2), taps_block(1), taps_block(2),
                  pl.BlockSpec((META_ROWS, LANES), lambda j: (0, 0)),
                  pl.BlockSpec((None, GDN_HEADS, 1, META_ROWS),
                               lambda j: (0, DECAY_LANE // GDN_HEADS, 0, 0))],
        out_specs=pl.BlockSpec((GDN_HEADS, HEAD_DIM, HEAD_DIM), lambda j: (0, 0, 0)),
        scratch_shapes=[pltpu.VMEM((2 * GDN_HEADS, META_ROWS + SUBLANES, HEAD_DIM), F32)],
        compiler_params=pltpu.CompilerParams(
            dimension_semantics=("arbitrary",), vmem_limit_bytes=VMEM_LIMIT),
        name="gdn_state0",
    )(proj_m, proj_m, conv_wt, conv_wt, col_m, row_m4)


def _gdn_kernel(*refs, seq):
    n_qkv = 3 * HEAD_PAIR
    src_refs = [refs[t * HEAD_PAIR:(t + 1) * HEAD_PAIR] for t in range(3)]
    z_ref = refs[n_qkv]
    meta_refs = [refs[n_qkv + 1 + t * HEAD_PAIR:n_qkv + 1 + (t + 1) * HEAD_PAIR] for t in range(3)]
    tap_refs = [refs[2 * n_qkv + 1 + t * HEAD_PAIR:2 * n_qkv + 1 + (t + 1) * HEAD_PAIR]
                for t in range(3)]
    (brow_ref, grow_ref, s0_ref, nw_ref, o_ref, pad_ref, qs_ref, ks_ref, y_ref, bb_ref, gb_ref,
     qd_ref, kdt_ref, dec_ref, u_ref, w_ref, aqk_ref, st_ref,
     qd_stage, kdt_stage, dec_stage) = refs[3 * n_qkv + 1:]
    pair = pl.program_id(1)
    slot = pair % REC_PAIRS
    n_blocks = seq // ROW_BLOCK
    n_groups = seq // GROUP
    cpg = GROUP // CHUNK
    hist = CONV_WIDTH - 1

    def pointwise_head(i, bank, hh):
        r0 = pl.multiple_of(i * ROW_BLOCK, ROW_BLOCK)
        rows = pl.ds(r0, ROW_BLOCK)
        convs = []
        for t in range(3):
            win = pad_ref.at[(bank * 3 + t) * HEAD_PAIR + hh]
            if isinstance(i, int) and i == 0:
                past = meta_refs[t][hh][META_ROWS - HIST_ROWS:META_ROWS, :]
            else:
                past = src_refs[t][hh][pl.ds(pl.multiple_of(r0 - HIST_ROWS, HIST_ROWS), HIST_ROWS), :]
            win[0:HIST_ROWS, :] = past.astype(F32)
            win[HIST_ROWS:HIST_ROWS + ROW_BLOCK, :] = src_refs[t][hh][rows, :].astype(F32)
            convs.append(_conv_silu(
                lambda j, win=win: win[HIST_ROWS - hist + j:HIST_ROWS - hist + j + ROW_BLOCK, :],
                tap_refs[t][hh][...]))
        beta_b = jnp.broadcast_to(brow_ref[hh, :, rows], (LANES, ROW_BLOCK)).T
        g_b = jnp.broadcast_to(grow_ref[hh, :, rows], (LANES, ROW_BLOCK)).T
        q, k, q_dec, k_dec, y, dec = _gdn_pointwise(convs[0], convs[1], convs[2], beta_b, g_b)
        qs_ref[hh, rows, :] = q.astype(BF16)
        ks_ref[hh, rows, :] = k.astype(BF16)
        y_ref[hh, rows, :] = y.astype(BF16)
        bb_ref[hh, rows, :] = beta_b
        gb_ref[hh, rows, :] = g_b
        qd_stage[hh, rows, :] = q_dec.astype(BF16)
        kdt_stage[hh, :, rows] = k_dec.T.astype(BF16)
        n_chunks = ROW_BLOCK // CHUNK
        dec_rows = [dec[c * CHUNK:c * CHUNK + 1, :] for c in range(n_chunks)]
        dec_rows.append(jnp.zeros((DEC_ROWS - n_chunks, LANES), F32))
        dec_stage[hh, pl.ds(pl.multiple_of(i * DEC_ROWS, DEC_ROWS), DEC_ROWS), :] = (
            jnp.concatenate(dec_rows, axis=0))

    def pointwise_thunks(i, bank):
        return [functools.partial(pointwise_head, i, bank, hh) for hh in range(HEAD_PAIR)]

    def first_pointwise(unroll):
        for i in range(unroll):
            for thunk in pointwise_thunks(i, i):
                thunk()

    def prep_groups(gi, fillers, unroll):
        groups = [gi * unroll + u for u in range(unroll) for _ in range(HEAD_PAIR)]
        keys = [(hh, pl.ds(pl.multiple_of((gi * unroll + u) * GROUP, GROUP), GROUP))
                for u in range(unroll) for hh in range(HEAD_PAIR)]
        probs = [(qs_ref[hh, rows, :], ks_ref[hh, rows, :], y_ref[hh, rows, :],
                  bb_ref[hh, rows, :], gb_ref[hh, rows, :], grow_ref[hh, :, rows])
                 for hh, rows in keys]
        for g, (hh, rows), (uw, a_qk) in zip(groups, keys, _gdn_groups(probs, fillers)):
            hs = slot * HEAD_PAIR + hh
            u_ref[hs, rows, :] = uw[:, :HEAD_DIM]
            w_ref[hs, rows, :] = uw[:, HEAD_DIM:].astype(BF16)
            aqk_ref[hs, rows, :] = a_qk.astype(BF16)
            drows = pl.ds(pl.multiple_of(g * DEC_ROWS, DEC_ROWS), DEC_ROWS)
            qd_ref[hs, rows, :] = qd_stage[hh, rows, :]
            kdt_ref[hs, :, rows] = kdt_stage[hh, :, rows]
            dec_ref[hs, drows, :] = dec_stage[hh, drows, :]

    def next_pointwise(gi, unroll):
        return [th for u in range(unroll) for th in pointwise_thunks((gi + 1) * unroll + u, u)]

    def rec_thunks(gi):
        r0 = pl.multiple_of(gi * GROUP, GROUP)
        rows = pl.ds(r0, GROUP)
        heads = range(REC_HEADS)
        outs = [[] for _ in heads]

        held = {}

        def read_state(c):
            crow = pl.ds(r0 + c * CHUNK, CHUNK)
            states = [st_ref[h] for h in heads]
            s_bf = [s.astype(BF16) for s in states]
            ws = [_dot(jnp.concatenate([w_ref[h, crow, :], qd_ref[h, crow, :]], axis=0), s_bf[h])
                  for h in heads]
            held[c] = (states, ws)

        def write_state(c):
            crow = pl.ds(r0 + c * CHUNK, CHUNK)
            states, ws = held.pop(c)
            v_new = [u_ref[h, crow, :] - ws[h][:CHUNK, :] for h in heads]
            zero = lambda n: jnp.zeros((n * CHUNK, HEAD_DIM), BF16)
            v_pad = [jnp.concatenate(([zero(c)] if c else []) + [v.astype(BF16)]
                                     + ([zero(cpg - 1 - c)] if c < cpg - 1 else []), axis=0)
                     for v in v_new]
            upd = [_dot(jnp.concatenate([aqk_ref[h, crow, :], kdt_ref[h, :, rows]], axis=0),
                        v_pad[h]) for h in heads]
            for h in heads:
                dec = dec_ref[h, pl.ds(gi * DEC_ROWS + c, 1), :]
                st_ref[h] = states[h] * dec + upd[h][CHUNK:, :]
                outs[h].append(ws[h][CHUNK:, :] + upd[h][:CHUNK, :])
            if c == cpg - 1:
                for h in heads:
                    o = _rms(jnp.concatenate(outs[h], axis=0), nw_ref[...])
                    o_ref[h, rows, :] = (o * _silu(z_ref[h, rows, :].astype(F32))).astype(BF16)

        return [functools.partial(half, c) for c in range(cpg) for half in (read_state, write_state)]

    def interleave(a, b):
        if not a or not b:
            return list(a) + list(b)
        out, j = [], 0
        for i, th in enumerate(a):
            out.append(th)
            while j < len(b) and (j + 1) * len(a) <= (i + 1) * len(b):
                out.append(b[j])
                j += 1
        return out + list(b[j:])

    @pl.when(slot != REC_PAIRS - 1)
    def _():
        unroll = PREP_UNROLL
        n_trips = n_groups // unroll
        first_pointwise(unroll)

        def prep_body(gi, _):
            prep_groups(gi, next_pointwise(gi, unroll), unroll)
            return 0
        lax.fori_loop(0, n_trips - 1, prep_body, 0)
        prep_groups(n_trips - 1, [], unroll)

    @pl.when(slot == REC_PAIRS - 1)
    def _():
        unroll = REC_PREP_UNROLL
        n_trips = n_groups // unroll
        st_ref[...] = s0_ref[...]
        first_pointwise(unroll)

        def rec_of_trip(t):
            return [th for u in range(unroll) for th in rec_thunks(t * unroll + u)]

        prep_groups(0, next_pointwise(0, unroll), unroll)

        def prep_rec_body(gi, _):
            prep_groups(gi, interleave(rec_of_trip(gi - 1), next_pointwise(gi, unroll)), unroll)
            return 0
        lax.fori_loop(1, n_trips - 1, prep_rec_body, 0)
        prep_groups(n_trips - 1, rec_of_trip(n_trips - 2), unroll)
        for thunk in rec_of_trip(n_trips - 1):
            thunk()


def _gdn(proj, proj_m, conv_wt, row4, state0, norm_w):
    _, b, seq, _ = proj.shape
    n_pairs = GDN_HEADS // HEAD_PAIR

    def head_specs(shape, imap):
        return [pl.BlockSpec(shape, functools.partial(imap, t * GDN_HEADS + hh))
                for t in range(3) for hh in range(HEAD_PAIR)]

    src_specs = head_specs((None, None, seq, HEAD_DIM),
                           lambda off, i, j: (off + HEAD_PAIR * j, i, 0, 0))
    meta_specs = head_specs((None, META_ROWS, HEAD_DIM), lambda off, i, j: (off + HEAD_PAIR * j, 0, 0))
    tap_specs = head_specs((CONV_WIDTH, HEAD_DIM), lambda off, i, j: (0, off + HEAD_PAIR * j))
    n_qkv = 3 * HEAD_PAIR
    z_block0 = 3 * GDN_HEADS // REC_HEADS

    per_pair = lambda width, dt: pltpu.VMEM((HEAD_PAIR, seq, width), dt)
    per_rec = lambda width, dt: pltpu.VMEM((REC_HEADS, seq, width), dt)
    return pl.pallas_call(
        functools.partial(_gdn_kernel, seq=seq),
        out_shape=jax.ShapeDtypeStruct((GDN_HEADS, b, seq, HEAD_DIM), BF16),
        grid=(b, n_pairs),
        in_specs=src_specs
        + [pl.BlockSpec((REC_HEADS, None, seq, HEAD_DIM),
                        lambda i, j: (z_block0 + j // REC_PAIRS, i, 0, 0))]
        + meta_specs + tap_specs
        + [pl.BlockSpec((None, HEAD_PAIR, 1, seq),
                        lambda i, j: (i, BETA_LANE // HEAD_PAIR + j, 0, 0)),
           pl.BlockSpec((None, HEAD_PAIR, 1, seq),
                        lambda i, j: (i, DECAY_LANE // HEAD_PAIR + j, 0, 0)),
           pl.BlockSpec((REC_HEADS, HEAD_DIM, HEAD_DIM), lambda i, j: (j // REC_PAIRS, 0, 0)),
           pl.BlockSpec((1, HEAD_DIM), lambda i, j: (0, 0))],
        out_specs=pl.BlockSpec((REC_HEADS, None, seq, HEAD_DIM),
                               lambda i, j: (j // REC_PAIRS, i, 0, 0)),
        scratch_shapes=[pltpu.VMEM((max(PREP_UNROLL, REC_PREP_UNROLL) * n_qkv, HIST_ROWS + ROW_BLOCK,
                                    HEAD_DIM), F32),
                        per_pair(HEAD_DIM, BF16), per_pair(HEAD_DIM, BF16),
                        per_pair(2 * HEAD_DIM, BF16),
                        per_pair(HEAD_DIM, F32), per_pair(HEAD_DIM, F32),
                        per_rec(HEAD_DIM, BF16),
                        pltpu.VMEM((REC_HEADS, HEAD_DIM, seq), BF16),
                        pltpu.VMEM((REC_HEADS, seq // GROUP * DEC_ROWS, HEAD_DIM), F32),
                        per_rec(HEAD_DIM, F32), per_rec(HEAD_DIM, BF16), per_rec(GROUP, BF16),
                        pltpu.VMEM((REC_HEADS, HEAD_DIM, HEAD_DIM), F32),
                        per_pair(HEAD_DIM, BF16),
                        pltpu.VMEM((HEAD_PAIR, HEAD_DIM, seq), BF16),
                        pltpu.VMEM((HEAD_PAIR, seq // GROUP * DEC_ROWS, HEAD_DIM), F32)],
        compiler_params=pltpu.CompilerParams(
            dimension_semantics=("parallel", "arbitrary"), vmem_limit_bytes=VMEM_LIMIT),
        name="gdn",
    )(*([proj] * n_qkv), proj, *([proj_m] * n_qkv), *([conv_wt] * n_qkv),
      row4, row4, state0, norm_w)


def _fox_kernel(q_ref, k_ref, v_ref, g_ref, km_ref, vm_ref, colm_ref, crow_ref,
                qw_ref, kw_ref, o_ref, qt_ref, ka_ref, kam_ref, vt_ref, vtm_ref,
                m_ref, l_ref, acc_ref, *, seq):
    n_blocks = seq // ROW_BLOCK
    aug_r = lax.broadcasted_iota(jnp.int32, (LANES, ROW_BLOCK), 0)

    def key_aug(ck, valid=None):
        hi, mid, lo = _split3(ck)
        lane = lax.broadcasted_iota(jnp.int32, ck.shape, 1)
        neg_hi = -hi.astype(F32)
        if valid is not None:
            neg_hi = jnp.where(valid, neg_hi, NEG_BIG)
        blk = jnp.where(lane < 3, 1.0,
                        jnp.where(lane == 3, neg_hi,
                                  jnp.where(lane == 4, -mid.astype(F32),
                                            jnp.where(lane == 5, -lo.astype(F32), 0.0))))
        return blk.astype(BF16)

    def pro_body(i, _):
        r0 = pl.multiple_of(i * ROW_BLOCK, ROW_BLOCK)
        rows = pl.ds(r0, ROW_BLOCK)
        for hd in range(FOX_GROUP):
            qn = _rms(q_ref[hd, rows, :].astype(F32), qw_ref[...]) * (HEAD_DIM ** -0.5 * LOG2E)
            qt_ref[hd, 0:HEAD_DIM, rows] = qn.T.astype(BF16)
            hi, mid, lo = _split3(crow_ref[hd, :, rows] * LOG2E)
            aug = jnp.where(aug_r == 0, hi.astype(F32),
                            jnp.where(aug_r == 1, mid.astype(F32),
                                      jnp.where(aug_r == 2, lo.astype(F32),
                                                jnp.where(aug_r < 6, 1.0, 0.0))))
            qt_ref[hd, HEAD_DIM:2 * HEAD_DIM, rows] = aug.astype(BF16)
            ka_ref[hd, rows, 0:HEAD_DIM] = _rms(k_ref[hd, rows, :].astype(F32),
                                                kw_ref[...]).astype(BF16)
            aug_k = jnp.where(aug_r < 3, 1.0,
                              jnp.where(aug_r == 3, -hi.astype(F32),
                                        jnp.where(aug_r == 4, -mid.astype(F32),
                                                  jnp.where(aug_r == 5, -lo.astype(F32), 0.0))))
            ka_ref[hd, rows, HEAD_DIM:2 * HEAD_DIM] = aug_k.T.astype(BF16)
            vt_ref[hd, :, rows] = v_ref[hd, rows, :].astype(F32).T.astype(BF16)
        return 0
    lax.fori_loop(0, n_blocks, pro_body, 0, unroll=BLOCK_UNROLL // FOX_GROUP)

    mrow = lax.broadcasted_iota(jnp.int32, (META_ROWS, LANES), 0)
    for hd in range(FOX_GROUP):
        head = pl.program_id(1) * FOX_GROUP + hd
        kam_ref[hd, :, 0:HEAD_DIM] = _rms(km_ref[hd].astype(F32), kw_ref[...]).astype(BF16)
        ck_m = _lane_bcast(colm_ref[...], FORGET_LANE + head) * LOG2E
        kam_ref[hd, :, HEAD_DIM:2 * HEAD_DIM] = key_aug(ck_m, mrow >= META_PAD)
        vtm_ref[hd] = vm_ref[hd].astype(F32).T.astype(BF16)

    kidx = lax.broadcasted_iota(jnp.int32, (FOX_TK, FOX_TK), 0)
    qidx = lax.broadcasted_iota(jnp.int32, (FOX_TK, FOX_TK), 1)
    diag_ok = kidx <= qidx

    items = []
    for kj in range(-1, seq // FOX_TK):
        lane0 = max(kj, 0) * FOX_TK
        for p0 in range(lane0, seq, FOX_PIECE):
            for hd in range(FOX_GROUP):
                items.append((hd, kj, slice(p0, min(p0 + FOX_PIECE, seq)), p0 == lane0))

    def scores(item):
        hd, kj, ln, leads = item
        k_aug = kam_ref[hd] if kj < 0 else ka_ref[hd, kj * FOX_TK:(kj + 1) * FOX_TK, :]
        s = _dot(k_aug, qt_ref[hd, :, ln])
        if kj >= 0 and leads:
            masked = jnp.where(diag_ok, s[:, :FOX_TK], NEG_BIG)
            s = jnp.concatenate([masked, s[:, FOX_TK:]], axis=1) if s.shape[1] > FOX_TK else masked
        return s

    def softmax_stats(item, s):
        hd, kj, ln, _ = item
        top = jnp.max(s, axis=0, keepdims=True)
        if kj < 0:
            m_new, alpha = top, None
        else:
            m_old = m_ref[hd, :, ln]
            m_new = jnp.maximum(m_old, top)
            alpha = jnp.exp2(m_old - m_new)
        p = jnp.exp2(s - m_new)
        psum = jnp.sum(p, axis=0, keepdims=True)
        m_ref[hd, :, ln] = m_new
        l_ref[hd, :, ln] = psum if kj < 0 else alpha * l_ref[hd, :, ln] + psum
        return p.astype(BF16), alpha

    def values(item, p, alpha):
        hd, kj, ln, _ = item
        v_t = vtm_ref[hd] if kj < 0 else vt_ref[hd, :, kj * FOX_TK:(kj + 1) * FOX_TK]
        pv = _dot(v_t, p)
        acc_ref[hd, :, ln] = pv if kj < 0 else alpha * acc_ref[hd, :, ln] + pv

    s_cur = scores(items[0])
    p_cur = None
    for t in range(len(items) + 1):
        s_next = scores(items[t + 1]) if t + 1 < len(items) else None
        p_next = softmax_stats(items[t], s_cur) if t < len(items) else None
        if p_cur is not None:
            values(items[t - 1], *p_cur)
        s_cur, p_cur = s_next, p_next

    def out_body(i, _):
        r0 = pl.multiple_of(i * ROW_BLOCK, ROW_BLOCK)
        rows = pl.ds(r0, ROW_BLOCK)
        for hd in range(FOX_GROUP):
            out_t = acc_ref[hd, :, rows] * (1.0 / l_ref[hd, :, rows])
            o_ref[hd, rows, :] = (out_t.T * _silu(g_ref[hd, rows, :].astype(F32))).astype(BF16)
        return 0
    lax.fori_loop(0, n_blocks, out_body, 0, unroll=BLOCK_UNROLL // FOX_GROUP)


def _fox(proj, proj_m, col_m, row4, q_w, k_w):
    _, b, seq, _ = proj.shape
    base = 4 * GDN_HEADS // FOX_GROUP
    hb = FOX_HEADS // FOX_GROUP

    def head_block(off):
        return pl.BlockSpec((FOX_GROUP, None, seq, HEAD_DIM),
                            lambda i, j, off=off: (off + j, i, 0, 0))

    def meta_block(off):
        return pl.BlockSpec((FOX_GROUP, META_ROWS, HEAD_DIM), lambda i, j, off=off: (off + j, 0, 0))

    per_head = lambda shape, dt: pltpu.VMEM((FOX_GROUP,) + shape, dt)
    return pl.pallas_call(
        functools.partial(_fox_kernel, seq=seq),
        out_shape=jax.ShapeDtypeStruct((FOX_HEADS, b, seq, HEAD_DIM), BF16),
        grid=(b, FOX_HEADS // FOX_GROUP),
        in_specs=[head_block(base), head_block(base + hb), head_block(base + 2 * hb),
                  head_block(base + 3 * hb),
                  meta_block(base + hb), meta_block(base + 2 * hb),
                  pl.BlockSpec((META_ROWS, LANES), lambda i, j: (0, 0)),
                  pl.BlockSpec((None, FOX_GROUP, 1, seq),
                               lambda i, j: (i, FORGET_LANE // FOX_GROUP + j, 0, 0)),
                  pl.BlockSpec((1, HEAD_DIM), lambda i, j: (0, 0)),
                  pl.BlockSpec((1, HEAD_DIM), lambda i, j: (0, 0))],
        out_specs=pl.BlockSpec((FOX_GROUP, None, seq, HEAD_DIM), lambda i, j: (j, i, 0, 0)),
        scratch_shapes=[per_head((2 * HEAD_DIM, seq), BF16),
                        per_head((seq, 2 * HEAD_DIM), BF16),
                        per_head((META_ROWS, 2 * HEAD_DIM), BF16),
                        per_head((HEAD_DIM, seq), BF16),
                        per_head((HEAD_DIM, META_ROWS), BF16),
                        per_head((1, seq), F32), per_head((1, seq), F32),
                        per_head((HEAD_DIM, seq), F32)],
        compiler_params=pltpu.CompilerParams(
            dimension_semantics=("parallel", "arbitrary"), vmem_limit_bytes=VMEM_LIMIT),
        name="fox",
    )(proj, proj, proj, proj, proj_m, proj_m, col_m, row4, q_w, k_w)


def _out_proj_kernel(mg_ref, mf_ref, wg_ref, wf_ref, pw_ref, x_ref, o_ref):
    for r in range(o_ref.shape[0] // OUT_PROJ_CHUNK):
        rows = slice(r * OUT_PROJ_CHUNK, (r + 1) * OUT_PROJ_CHUNK)

        def rows_of(m_ref):
            return jnp.concatenate([m_ref[h, rows, :] for h in range(m_ref.shape[0])], axis=1)
        out = _dot(rows_of(mg_ref), wg_ref[...]) + _dot(rows_of(mf_ref), wf_ref[...])
        o_ref[rows, :] = x_ref[rows, :] + _rms(out, pw_ref[...])


def _out_proj(mg, mf, w_g, w_f, post_w, x2d, tm):
    m, d = x2d.shape
    return pl.pallas_call(
        _out_proj_kernel,
        out_shape=jax.ShapeDtypeStruct((m, d), F32),
        grid=(m // tm,),
        in_specs=[pl.BlockSpec((GDN_HEADS, tm, HEAD_DIM), lambda i: (0, i, 0)),
                  pl.BlockSpec((FOX_HEADS, tm, HEAD_DIM), lambda i: (0, i, 0)),
                  pl.BlockSpec((GDN_WIDTH, d), lambda i: (0, 0), pipeline_mode=pl.Buffered(1)),
                  pl.BlockSpec((FOX_WIDTH, d), lambda i: (0, 0), pipeline_mode=pl.Buffered(1)),
                  pl.BlockSpec((1, d), lambda i: (0, 0)),
                  pl.BlockSpec((tm, d), lambda i: (i, 0))],
        out_specs=pl.BlockSpec((tm, d), lambda i: (i, 0)),
        compiler_params=pltpu.CompilerParams(
            dimension_semantics=("parallel",), vmem_limit_bytes=VMEM_LIMIT),
        name="out_proj",
    )(mg, mf, w_g, w_f, post_w, x2d)


def _tile(total, want):
    t = min(total, want)
    while total % t:
        t //= 2
    return t


def _layer(x, meta_pad, pre_w, w_in, conv_w, a_log, dt_bias, gdn_norm_w, fox_q_w, fox_k_w,
           fox_f_bias, w_out, post_w):
    b, seq, d = x.shape
    assert seq % (GROUP * PREP_UNROLL) == 0 and seq % FOX_TK == 0 and seq % ROW_BLOCK == 0
    gw, fw = GDN_WIDTH, FOX_WIDTH
    o_gb = 4 * gw
    o_f = o_gb + 2 * GDN_HEADS
    o_ff = o_f + 4 * fw
    w_t = w_in.T
    w_gate = jnp.concatenate(
        [w_t[o_gb:o_f], w_t[o_ff:],
         jnp.zeros((GATE_WIDTH - 2 * GDN_HEADS - FOX_HEADS, d), w_in.dtype)], axis=0)
    pre_w2 = pre_w[None]
    w_main, proj_m, gate_m = _w_prep(w_t, meta_pad, pre_w2, w_gate, MAIN_WIDTH, o_gb, o_f - o_gb)
    zpad = jnp.zeros((GATE_WIDTH - FORGET_LANE - FOX_HEADS,), F32)
    add_vec = jnp.concatenate([jnp.zeros((GDN_HEADS,), F32), dt_bias, fox_f_bias, zpad])[None]
    alog_vec = jnp.concatenate([jnp.zeros((GDN_HEADS,), F32), a_log,
                                jnp.zeros((FOX_HEADS,), F32), zpad])[None]

    x2d = x.reshape(b * seq, d)
    proj, gate = _in_proj(x2d, pre_w2, w_main, w_gate, _tile(b * seq, IN_PROJ_TM), IN_PROJ_TN)
    proj = proj.reshape(MAIN_WIDTH // HEAD_DIM, b, seq, HEAD_DIM)

    row4 = _gate_rows(gate.reshape(b, seq, GATE_WIDTH), add_vec, alog_vec)
    col_m, row_m = _gate_meta(gate_m, add_vec, alog_vec)
    row_m4 = row_m.reshape(1, GATE_ROWS, 1, META_ROWS)

    conv_wt = conv_w.T
    state0 = _gdn_state0(proj_m, conv_wt, col_m, row_m4)
    o_gdn = _gdn(proj, proj_m, conv_wt, row4, state0, gdn_norm_w[None])
    o_fox = _fox(proj, proj_m, col_m, row4, fox_q_w[None], fox_k_w[None])

    w_out_b = w_out.astype(BF16)
    out = _out_proj(o_gdn.reshape(GDN_HEADS, b * seq, HEAD_DIM),
                    o_fox.reshape(FOX_HEADS, b * seq, HEAD_DIM),
                    w_out_b[:gw], w_out_b[gw:], post_w[None], x2d, _tile(b * seq, OUT_PROJ_TM))
    return out.reshape(b, seq, d)


def kernel(x, meta_tokens, pre_norm_w, w_in, conv_w, a_log, dt_bias, gdn_norm_w, fox_q_norm_w,
           fox_k_norm_w, fox_f_bias, w_out, post_norm_w):
    assert pre_norm_w.shape[0] == 1, "single-layer stack"
    meta_pad = jnp.concatenate(
        [jnp.zeros((META_PAD, x.shape[-1]), x.dtype), meta_tokens.astype(x.dtype)], axis=0)
    return _layer(x, meta_pad, pre_norm_w[0], w_in[0], conv_w[0], a_log[0], dt_bias[0],
                  gdn_norm_w[0], fox_q_norm_w[0], fox_k_norm_w[0], fox_f_bias[0], w_out[0],
                  post_norm_w[0])
```
